```python
import math
import jax
import jax.numpy as jnp
from jax import lax
import numpy as np

D_MODEL = 2048
BATCH = 1
SEQ = 16384
DEPTH = 2

CTX_LEN = 256
GRID_W = 64
HEAD_DIM = 128
BRANCH_WIDTH = 1024
N_BRANCH = 3
N_MOD = 6
RMS_EPS = 1e-6
NEG_INF = -1e30

HY_WIDTH = BRANCH_WIDTH
HY_GROUPS = HY_WIDTH // HEAD_DIM
HY_ORDER = 2
HY_DIRS = 2
HY_SHORT_WIDTH = 3
HY_BANDS = 16
HY_EMB = 1 + 2 * HY_BANDS
HY_FILTER_HIDDEN = 64
HY_DECAY_TARGET = 1e-2
HY_DECAY_SHORT_PCT = 0.3
HY_DECAY_LONG_PCT = 1.5
HY_DECAY_SHIFT = 0.05

WA_HEADS = BRANCH_WIDTH // HEAD_DIM
WA_KV_HEADS = 2
WA_WINDOW = 128
WA_BLOCK = 128

NA_HEADS = BRANCH_WIDTH // HEAD_DIM
NA_WIN_ROWS = 8
NA_WIN_COLS = 16

ROPE_BASE = 10000.0

N_EXPERTS = 16
EC_CAPACITY_FACTOR = 2
D_EXPERT = 1024

OFF_HY = 0
OFF_WA_Q = OFF_HY + 3 * HY_WIDTH
OFF_WA_KV = OFF_WA_Q + WA_HEADS * HEAD_DIM
OFF_NA_Q = OFF_WA_KV + 2 * WA_KV_HEADS * HEAD_DIM
OFF_NA_KV = OFF_NA_Q + NA_HEADS * HEAD_DIM
OFF_GATE = OFF_NA_KV + 2 * NA_HEADS * HEAD_DIM
N_IN = OFF_GATE + N_BRANCH * D_MODEL

kernel_name = 'hybrid_hyena_window_natten_ec_moe_dit'


def rms_norm(x, g):
    xf = x.astype(jnp.float32)
    y = xf * lax.rsqrt(jnp.mean(xf * xf, axis=-1, keepdims=True) + RMS_EPS)
    return (y * g.astype(jnp.float32)).astype(x.dtype)


def modulate(h, shift, scale):
    return h * (1 + scale) + shift


def split_heads(z, n_heads):
    return z.reshape(z.shape[:-1] + (n_heads, HEAD_DIM))


def keys_values(z, n_heads, k_gain):
    k, v = jnp.split(z, 2, axis=-1)
    return rms_norm(split_heads(k, n_heads), k_gain), split_heads(v, n_heads)


def axial_rope_tables(L):
    t = jnp.arange(L, dtype=jnp.int32)
    row = (t // GRID_W).astype(jnp.float32)
    col = (t % GRID_W).astype(jnp.float32)
    nf = HEAD_DIM // 4
    inv = ROPE_BASE ** (-jnp.arange(nf, dtype=jnp.float32) / nf)
    ang = jnp.stack([row[:, None] * inv, col[:, None] * inv], axis=1)
    return jnp.cos(ang), jnp.sin(ang)


def apply_axial_rope(x, cos, sin):
    nf = HEAD_DIM // 4
    xr = x.reshape(x.shape[:-1] + (2, 2, nf))
    x1, x2 = xr[..., 0, :], xr[..., 1, :]
    c = cos[None, :, None].astype(x.dtype)
    s = sin[None, :, None].astype(x.dtype)
    out = jnp.stack([x1 * c - x2 * s, x2 * c + x1 * s], axis=-2)
    return out.reshape(x.shape)


def short_conv3(z, w, b):
    L = z.shape[1]
    zp = jnp.pad(z, ((0, 0), (1, 1), (0, 0)))
    return zp[:, :L] * w[0] + zp[:, 1:L + 1] * w[1] + zp[:, 2:] * w[2] + b


def hyena_filter_spectra(L, fw1, fb1, fw2, fb2, fw3, freq, decay):
    f32 = jnp.float32
    t = jnp.linspace(0.0, 1.0, L, dtype=f32)[:, None]
    pos = jnp.arange(L, dtype=f32)[:, None]
    bands = jnp.linspace(1e-4, HY_BANDS - 1, HY_BANDS, dtype=f32)[None, :]
    ang = (2.0 * math.pi / L) * bands * pos
    feats = jnp.concatenate([t, jnp.cos(ang), -jnp.sin(ang)], axis=-1)
    fr = freq.astype(f32)
    h = jnp.sin(fr[0] * (feats @ fw1.astype(f32) + fb1.astype(f32)))
    h = jnp.sin(fr[1] * (h @ fw2.astype(f32) + fb2.astype(f32)))
    h = (h @ fw3.astype(f32)).reshape(L, HY_DIRS, HY_ORDER, HY_WIDTH)
    rate = jnp.abs(decay.astype(f32)).reshape(HY_DIRS, HY_ORDER, HY_WIDTH)
    h = h * (jnp.exp(-t[:, :, None, None] * rate) + HY_DECAY_SHIFT)
    hf, hb = h[:, 0], h[:, 1]
    k = jnp.concatenate([hf[:1] + hb[:1], hf[1:], jnp.zeros_like(hf[:1]), hb[1:][::-1]], axis=0)
    k = k / jnp.sum(jnp.abs(k), axis=0, keepdims=True)
    return jnp.fft.rfft(k, axis=0)


def bidir_long_conv(u, k_spec, d_skip):
    L = u.shape[1]
    uf = jnp.fft.rfft(u.astype(jnp.float32), n=2 * L, axis=1)
    y = jnp.fft.irfft(uf * k_spec[None], n=2 * L, axis=1)[:, :L]
    return (y + u.astype(jnp.float32) * d_skip.astype(jnp.float32)).astype(u.dtype)


def hyena_branch(z, p):
    L = z.shape[1]
    z = short_conv3(z, p['hy_conv_w'], p['hy_conv_b'])
    v, x1, x2 = jnp.split(z, 3, axis=-1)
    k_spec = hyena_filter_spectra(L, p['hy_fw1'], p['hy_fb1'], p['hy_fw2'], p['hy_fb2'],
                                  p['hy_fw3'], p['hy_freq'], p['hy_decay'])
    y = x1 * bidir_long_conv(v, k_spec[:, 0], p['hy_d'][0])
    return x2 * bidir_long_conv(y, k_spec[:, 1], p['hy_d'][1])


def windowed_gqa_latent(q, k, v, kc, vc, sink):
    B, L, H, Dh = q.shape
    kvh = k.shape[2]
    g = H // kvh
    nb = L // WA_BLOCK
    qb = q.reshape(B, nb, WA_BLOCK, kvh, g, Dh)

    def band(t):
        tp = jnp.pad(t, ((0, 0), (WA_BLOCK, WA_BLOCK), (0, 0), (0, 0)))
        tp = tp.reshape(B, nb + 2, WA_BLOCK, kvh, Dh)
        return jnp.concatenate([tp[:, :-2], tp[:, 1:-1], tp[:, 2:]], axis=2)

    kb, vb = band(k), band(v)
    scale = HEAD_DIM ** -0.5
    blk = jnp.arange(nb)[:, None] * WA_BLOCK
    qpos = (blk + jnp.arange(WA_BLOCK)[None])[:, :, None]
    kpos = (blk - WA_BLOCK + jnp.arange(3 * WA_BLOCK)[None])[:, None, :]
    valid = (kpos >= 0) & (kpos < L) & (jnp.abs(qpos - kpos) <= WA_WINDOW)
    s_loc = jnp.einsum('bnqkgd,bnskd->bnkgqs', qb, kb).astype(jnp.float32) * scale
    s_loc = jnp.where(valid[None, :, None, None], s_loc, NEG_INF)
    s_ctx = jnp.einsum('bnqkgd,bckd->bnkgqc', qb, kc).astype(jnp.float32) * scale
    s_sink = jnp.broadcast_to(sink.astype(jnp.float32).reshape(1, 1, kvh, g, 1, 1), s_loc.shape[:-1] + (1,))
    p = jax.nn.softmax(jnp.concatenate([s_loc, s_ctx, s_sink], axis=-1), axis=-1).astype(v.dtype)
    nl = 3 * WA_BLOCK
    nc = kc.shape[1]
    o = (jnp.einsum('bnkgqs,bnskd->bnqkgd', p[..., :nl], vb)
         + jnp.einsum('bnkgqc,bckd->bnqkgd', p[..., nl:nl + nc], vc))
    return o.reshape(B, L, H * Dh)


def neighbourhood_attention_latent(q, k, v, kc, vc, rpb):
    B, L, H, Dh = q.shape
    rows = L // GRID_W
    wr = min(NA_WIN_ROWS, rows)
    qg = q.reshape(B, rows, GRID_W, H, Dh)
    kg = k.reshape(B, rows, GRID_W, H, Dh)
    vg = v.reshape(B, rows, GRID_W, H, Dh)
    col = jnp.arange(GRID_W)
    cstart = jnp.clip(col - NA_WIN_COLS // 2, 0, GRID_W - NA_WIN_COLS)
    col_in = (col[None, :] >= cstart[:, None]) & (col[None, :] < cstart[:, None] + NA_WIN_COLS)
    dc_idx = jnp.clip(col[None, :] - col[:, None] + NA_WIN_COLS - 1, 0, 2 * NA_WIN_COLS - 2)
    scale = HEAD_DIM ** -0.5
    n_loc = wr * GRID_W

    def row_block(r):
        r0 = jnp.clip(r - NA_WIN_ROWS // 2, 0, rows - wr)
        q_r = lax.dynamic_index_in_dim(qg, r, axis=1, keepdims=False)
        k_r = lax.dynamic_slice_in_dim(kg, r0, wr, axis=1)
        v_r = lax.dynamic_slice_in_dim(vg, r0, wr, axis=1).reshape(B, n_loc, H, Dh)
        dr_idx = r0 + jnp.arange(wr) - r + NA_WIN_ROWS - 1
        bias = rpb[:, dr_idx[None, :, None], dc_idx[:, None, :]].astype(jnp.float32)
        s_loc = jnp.einsum('bqhd,bwkhd->bhqwk', q_r, k_r).astype(jnp.float32) * scale + bias
        s_loc = jnp.where(col_in[:, None, :], s_loc, NEG_INF).reshape(B, H, GRID_W, n_loc)
        s_ctx = jnp.einsum('bqhd,bchd->bhqc', q_r, kc).astype(jnp.float32) * scale
        p = jax.nn.softmax(jnp.concatenate([s_loc, s_ctx], axis=-1), axis=-1).astype(v.dtype)
        return (jnp.einsum('bhqs,bshd->bqhd', p[..., :n_loc], v_r)
                + jnp.einsum('bhqc,bchd->bqhd', p[..., n_loc:], vc))

    out = lax.map(row_block, jnp.arange(rows))
    return jnp.moveaxis(out, 0, 1).reshape(B, L, H * Dh)


def context_attention(q, k, v, sink):
    B, Lc, H, Dh = q.shape
    kvh = k.shape[2]
    g = H // kvh
    qg = q.reshape(B, Lc, kvh, g, Dh)
    s = jnp.einsum('bqkgd,bckd->bkgqc', qg, k).astype(jnp.float32) * (HEAD_DIM ** -0.5)
    if sink is not None:
        s_sink = jnp.broadcast_to(sink.astype(jnp.float32).reshape(1, kvh, g, 1, 1), s.shape[:-1] + (1,))
        s = jnp.concatenate([s, s_sink], axis=-1)
    p = jax.nn.softmax(s, axis=-1)[..., :Lc].astype(v.dtype)
    return jnp.einsum('bkgqc,bckd->bqkgd', p, v).reshape(B, Lc, H * Dh)


def merge_branches(y_hy, y_wa, y_na, z_gate, w_branch, w_out):
    g = jax.nn.sigmoid(z_gate.astype(jnp.float32)).astype(z_gate.dtype)
    g_hy, g_wa, g_na = jnp.split(g, N_BRANCH, axis=-1)
    m = g_hy * (y_hy @ w_branch[0]) + g_wa * (y_wa @ w_branch[1]) + g_na * (y_na @ w_branch[2])
    return m @ w_out


def expert_choice_ffn(h, w_router, w_gate, w_up, w_down):
    B, N, D = h.shape
    cap = EC_CAPACITY_FACTOR * N // N_EXPERTS

    def one_set(hs):
        aff = jax.nn.softmax((hs @ w_router).astype(jnp.float32), axis=-1)
        gsel, idx = lax.top_k(aff.T, cap)
        xe = hs[idx]
        a = jnp.einsum('ecd,edf->ecf', xe, w_gate)
        u = jnp.einsum('ecd,edf->ecf', xe, w_up)
        ye = jnp.einsum('ecf,efd->ecd', jax.nn.silu(a) * u, w_down) * gsel[..., None].astype(hs.dtype)
        return jnp.zeros_like(hs).at[idx.reshape(-1)].add(ye.reshape(-1, D))

    return jax.vmap(one_set)(h)


def trunk_layer(x, ctx, c, c_ctx, p, update_ctx):
    L = x.shape[1]
    mx = jnp.split((jax.nn.silu(c) @ p['w_mod'] + p['b_mod'])[:, None, :], N_MOD, axis=-1)
    mc = jnp.split(jax.nn.silu(c_ctx) @ p['w_mod'] + p['b_mod'], N_MOD, axis=-1)
    w_in = p['w_in']
    hx = modulate(rms_norm(x, p['norm1_g']), mx[0], mx[1])
    hc = modulate(rms_norm(ctx, p['norm1_g']), mc[0], mc[1])

    if update_ctx:
        zc = hc @ w_in
        zc_wa_kv = zc[..., OFF_WA_KV:OFF_NA_Q]
        zc_na_kv = zc[..., OFF_NA_KV:OFF_GATE]
    else:
        zc_wa_kv = hc @ w_in[:, OFF_WA_KV:OFF_NA_Q]
        zc_na_kv = hc @ w_in[:, OFF_NA_KV:OFF_GATE]
    kc_wa, vc_wa = keys_values(zc_wa_kv, WA_KV_HEADS, p['wa_k_norm'])
    kc_na, vc_na = keys_values(zc_na_kv, NA_HEADS, p['na_k_norm'])

    zx = hx @ w_in
    cos, sin = axial_rope_tables(L)
    y_hy = hyena_branch(zx[..., OFF_HY:OFF_WA_Q], p)
    q_wa = apply_axial_rope(rms_norm(split_heads(zx[..., OFF_WA_Q:OFF_WA_KV], WA_HEADS), p['wa_q_norm']), cos, sin)
    k_wa, v_wa = keys_values(zx[..., OFF_WA_KV:OFF_NA_Q], WA_KV_HEADS, p['wa_k_norm'])
    k_wa = apply_axial_rope(k_wa, cos, sin)
    y_wa = windowed_gqa_latent(q_wa, k_wa, v_wa, kc_wa, vc_wa, p['wa_sink'])
    q_na = rms_norm(split_heads(zx[..., OFF_NA_Q:OFF_NA_KV], NA_HEADS), p['na_q_norm'])
    k_na, v_na = keys_values(zx[..., OFF_NA_KV:OFF_GATE], NA_HEADS, p['na_k_norm'])
    y_na = neighbourhood_attention_latent(q_na, k_na, v_na, kc_na, vc_na, p['na_rpb'])
    x = x + mx[2] * merge_branches(y_hy, y_wa, y_na, zx[..., OFF_GATE:], p['w_branch'], p['w_out'])
    h2 = modulate(rms_norm(x, p['norm2_g']), mx[3], mx[4])
    x = x + mx[5] * expert_choice_ffn(h2, p['w_router'], p['w_gate'], p['w_up'], p['w_down'])

    if update_ctx:
        yc_hy = hyena_branch(zc[..., OFF_HY:OFF_WA_Q], p)
        qc_wa = rms_norm(split_heads(zc[..., OFF_WA_Q:OFF_WA_KV], WA_HEADS), p['wa_q_norm'])
        yc_wa = context_attention(qc_wa, kc_wa, vc_wa, p['wa_sink'])
        qc_na = rms_norm(split_heads(zc[..., OFF_NA_Q:OFF_NA_KV], NA_HEADS), p['na_q_norm'])
        yc_na = context_attention(qc_na, kc_na, vc_na, None)
        ctx = ctx + mc[2] * merge_branches(yc_hy, yc_wa, yc_na, zc[..., OFF_GATE:], p['w_branch'], p['w_out'])
        hc2 = modulate(rms_norm(ctx, p['norm2_g']), mc[3], mc[4])
        ctx = ctx + mc[5] * expert_choice_ffn(hc2, p['w_router'], p['w_gate'], p['w_up'], p['w_down'])
    return x, ctx


def setup_inputs(seed: int = 0) -> dict:
    key = jax.random.key(seed)
    ks = iter(jax.random.split(key, 40))

    def nrm(shape, scale):
        return jax.random.normal(next(ks), shape, jnp.float32) * scale

    n_filt = HY_DIRS * HY_ORDER * HY_WIDTH
    base_decay = jnp.linspace(math.log(HY_DECAY_TARGET) / HY_DECAY_SHORT_PCT,
                              math.log(HY_DECAY_TARGET) / HY_DECAY_LONG_PCT, HY_WIDTH, dtype=jnp.float32)
    base_decay = jnp.broadcast_to(base_decay, (DEPTH, HY_DIRS, HY_ORDER, HY_WIDTH)).reshape(DEPTH, n_filt)
    return {
        'x': nrm((BATCH, SEQ, D_MODEL), 1.0),
        'c': nrm((BATCH, D_MODEL), 1.0),
        'ctx': nrm((BATCH, CTX_LEN, D_MODEL), 1.0),
        'c_ctx': nrm((D_MODEL,), 1.0),
        'w_mod': nrm((DEPTH, D_MODEL, N_MOD * D_MODEL), D_MODEL ** -0.5),
        'b_mod': nrm((DEPTH, N_MOD * D_MODEL), 0.02),
        'norm1_g': 1.0 + nrm((DEPTH, D_MODEL), 0.02),
        'w_in': nrm((DEPTH, D_MODEL, N_IN), D_MODEL ** -0.5),
        'hy_conv_w': nrm((DEPTH, HY_SHORT_WIDTH, 3 * HY_WIDTH), 0.5),
        'hy_conv_b': nrm((DEPTH, 3 * HY_WIDTH), 0.02),
        'hy_fw1': nrm((DEPTH, HY_EMB, HY_FILTER_HIDDEN), HY_EMB ** -0.5),
        'hy_fb1': nrm((DEPTH, HY_FILTER_HIDDEN), 0.1),
        'hy_fw2': nrm((DEPTH, HY_FILTER_HIDDEN, HY_FILTER_HIDDEN), HY_FILTER_HIDDEN ** -0.5),
        'hy_fb2': nrm((DEPTH, HY_FILTER_HIDDEN), 0.1),
        'hy_fw3': nrm((DEPTH, HY_FILTER_HIDDEN, n_filt), HY_FILTER_HIDDEN ** -0.5),
        'hy_freq': 1.0 + nrm((DEPTH, 2, HY_FILTER_HIDDEN), 0.1),
        'hy_decay': base_decay * (1.0 + nrm((DEPTH, n_filt), 0.05)),
        'hy_d': nrm((DEPTH, HY_ORDER, HY_WIDTH), 0.5),
        'wa_q_norm': 1.0 + nrm((DEPTH, HEAD_DIM), 0.02),
        'wa_k_norm': 1.0 + nrm((DEPTH, HEAD_DIM), 0.02),
        'wa_sink': nrm((DEPTH, WA_HEADS), 0.5),
        'na_q_norm': 1.0 + nrm((DEPTH, HEAD_DIM), 0.02),
        'na_k_norm': 1.0 + nrm((DEPTH, HEAD_DIM), 0.02),
        'na_rpb': nrm((DEPTH, NA_HEADS, 2 * NA_WIN_ROWS - 1, 2 * NA_WIN_COLS - 1), 0.2),
        'w_branch': nrm((DEPTH, N_BRANCH, BRANCH_WIDTH, D_MODEL), BRANCH_WIDTH ** -0.5),
        'w_out': nrm((DEPTH, D_MODEL, D_MODEL), D_MODEL ** -0.5),
        'norm2_g': 1.0 + nrm((DEPTH, D_MODEL), 0.02),
        'w_router': nrm((DEPTH, D_MODEL, N_EXPERTS), D_MODEL ** -0.5),
        'w_gate': nrm((DEPTH, N_EXPERTS, D_MODEL, D_EXPERT), D_MODEL ** -0.5),
        'w_up': nrm((DEPTH, N_EXPERTS, D_MODEL, D_EXPERT), D_MODEL ** -0.5),
        'w_down': nrm((DEPTH, N_EXPERTS, D_EXPERT, D_MODEL), D_EXPERT ** -0.5),
    }


def reference(x, c, ctx, c_ctx, w_mod, b_mod, norm1_g, w_in, hy_conv_w, hy_conv_b, hy_fw1, hy_fb1,
              hy_fw2, hy_fb2, hy_fw3, hy_freq, hy_decay, hy_d, wa_q_norm, wa_k_norm, wa_sink,
              na_q_norm, na_k_norm, na_rpb, w_branch, w_out, norm2_g, w_router, w_gate, w_up, w_down):
    for l in range(DEPTH):
        p = {
            'w_mod': w_mod[l], 'b_mod': b_mod[l], 'norm1_g': norm1_g[l], 'w_in': w_in[l],
            'hy_conv_w': hy_conv_w[l], 'hy_conv_b': hy_conv_b[l], 'hy_fw1': hy_fw1[l], 'hy_fb1': hy_fb1[l],
            'hy_fw2': hy_fw2[l], 'hy_fb2': hy_fb2[l], 'hy_fw3': hy_fw3[l], 'hy_freq': hy_freq[l],
            'hy_decay': hy_decay[l], 'hy_d': hy_d[l], 'wa_q_norm': wa_q_norm[l], 'wa_k_norm': wa_k_norm[l],
            'wa_sink': wa_sink[l], 'na_q_norm': na_q_norm[l], 'na_k_norm': na_k_norm[l], 'na_rpb': na_rpb[l],
            'w_branch': w_branch[l], 'w_out': w_out[l], 'norm2_g': norm2_g[l], 'w_router': w_router[l],
            'w_gate': w_gate[l], 'w_up': w_up[l], 'w_down': w_down[l],
        }
        x, ctx = trunk_layer(x, ctx, c, c_ctx, p, l < DEPTH - 1)
    return x
```

```python
import math
from functools import partial

import jax
import jax.numpy as jnp
from jax import lax
from jax.experimental import pallas as pl
from jax.experimental.pallas import tpu as pltpu

D_MODEL = 2048
SEQ = 16384
DEPTH = 2
CTX_LEN = 256
GRID_W = 64
HEAD_DIM = 128
BRANCH_WIDTH = 1024
N_BRANCH = 3
N_MOD = 6
RMS_EPS = 1e-6
NEG_INF = -1e30

HY_WIDTH = BRANCH_WIDTH
HY_ORDER = 2
HY_DIRS = 2
HY_BANDS = 16
HY_DECAY_SHIFT = 0.05

WA_HEADS = BRANCH_WIDTH // HEAD_DIM
WA_KV_HEADS = 2
WA_WINDOW = 128
WA_BLOCK = 128

NA_HEADS = BRANCH_WIDTH // HEAD_DIM
NA_WIN_ROWS = 8
NA_WIN_COLS = 16

ROPE_BASE = 10000.0

N_EXPERTS = 16
EC_CAPACITY_FACTOR = 2
D_EXPERT = 1024

OFF_HY = 0
OFF_WA_Q = OFF_HY + 3 * HY_WIDTH
OFF_WA_KV = OFF_WA_Q + WA_HEADS * HEAD_DIM
OFF_NA_Q = OFF_WA_KV + 2 * WA_KV_HEADS * HEAD_DIM
OFF_NA_KV = OFF_NA_Q + NA_HEADS * HEAD_DIM
OFF_GATE = OFF_NA_KV + 2 * NA_HEADS * HEAD_DIM
N_IN = OFF_GATE + N_BRANCH * D_MODEL

VMEM_LIMIT_BYTES = 56 * 1024 * 1024


def _mm_kernel(a_ref, b_ref, o_ref):
    o_ref[...] = jnp.dot(a_ref[...].astype(jnp.bfloat16), b_ref[...].astype(jnp.bfloat16),
                         preferred_element_type=jnp.float32).astype(o_ref.dtype)


def _pick(n, pref):
    for t in pref:
        if n % t == 0:
            return t
    return n


def matmul(a, b, out_dtype=jnp.float32):
    M, K = a.shape
    _, N = b.shape
    tm = _pick(M, (512, 256, 128, 64, 32, 16, 8))
    tn = _pick(N, (1024, 512, 256, 128))
    return pl.pallas_call(
        _mm_kernel,
        grid=(N // tn, M // tm),
        in_specs=[pl.BlockSpec((tm, K), lambda j, i: (i, 0)),
                  pl.BlockSpec((K, tn), lambda j, i: (0, j))],
        out_specs=pl.BlockSpec((tm, tn), lambda j, i: (i, j)),
        out_shape=jax.ShapeDtypeStruct((M, N), out_dtype),
        compiler_params=pltpu.CompilerParams(
            dimension_semantics=("parallel", "parallel"), vmem_limit_bytes=VMEM_LIMIT_BYTES),
        name="matmul",
    )(a, b)


def _bmm_kernel(a_ref, b_ref, o_ref):
    o_ref[0] = jnp.dot(a_ref[0].astype(jnp.bfloat16), b_ref[0].astype(jnp.bfloat16),
                       preferred_element_type=jnp.float32).astype(o_ref.dtype)


def batched_matmul(a, b, out_dtype=jnp.float32):
    E, M, K = a.shape
    _, _, N = b.shape
    tm = _pick(M, (512, 256, 128, 64, 32, 16, 8))
    tn = _pick(N, (1024, 512, 256, 128))
    return pl.pallas_call(
        _bmm_kernel,
        grid=(E, N // tn, M // tm),
        in_specs=[pl.BlockSpec((1, tm, K), lambda e, j, i: (e, i, 0)),
                  pl.BlockSpec((1, K, tn), lambda e, j, i: (e, 0, j))],
        out_specs=pl.BlockSpec((1, tm, tn), lambda e, j, i: (e, i, j)),
        out_shape=jax.ShapeDtypeStruct((E, M, N), out_dtype),
        compiler_params=pltpu.CompilerParams(
            dimension_semantics=("parallel", "parallel", "parallel"), vmem_limit_bytes=VMEM_LIMIT_BYTES),
        name="batched_matmul",
    )(a, b)


def rms_norm(x, g):
    xf = x.astype(jnp.float32)
    y = xf * lax.rsqrt(jnp.mean(xf * xf, axis=-1, keepdims=True) + RMS_EPS)
    return (y * g.astype(jnp.float32)).astype(x.dtype)


def modulate(h, shift, scale):
    return h * (1 + scale) + shift


def split_heads(z, n_heads):
    return z.reshape(z.shape[:-1] + (n_heads, HEAD_DIM))


def keys_values(z, n_heads, k_gain):
    k, v = jnp.split(z, 2, axis=-1)
    return rms_norm(split_heads(k, n_heads), k_gain), split_heads(v, n_heads)


def axial_rope_tables(L):
    t = jnp.arange(L, dtype=jnp.int32)
    row = (t // GRID_W).astype(jnp.float32)
    col = (t % GRID_W).astype(jnp.float32)
    nf = HEAD_DIM // 4
    inv = ROPE_BASE ** (-jnp.arange(nf, dtype=jnp.float32) / nf)
    ang = jnp.stack([row[:, None] * inv, col[:, None] * inv], axis=1)
    return jnp.cos(ang), jnp.sin(ang)


def apply_axial_rope(x, cos, sin):
    nf = HEAD_DIM // 4
    xr = x.reshape(x.shape[:-1] + (2, 2, nf))
    x1, x2 = xr[..., 0, :], xr[..., 1, :]
    c = cos[None, :, None].astype(x.dtype)
    s = sin[None, :, None].astype(x.dtype)
    out = jnp.stack([x1 * c - x2 * s, x2 * c + x1 * s], axis=-2)
    return out.reshape(x.shape)


def short_conv3(z, w, b):
    L = z.shape[1]
    zp = jnp.pad(z, ((0, 0), (1, 1), (0, 0)))
    return zp[:, :L] * w[0] + zp[:, 1:L + 1] * w[1] + zp[:, 2:] * w[2] + b


def hyena_filter_spectra(L, fw1, fb1, fw2, fb2, fw3, freq, decay):
    f32 = jnp.float32
    t = jnp.linspace(0.0, 1.0, L, dtype=f32)[:, None]
    pos = jnp.arange(L, dtype=f32)[:, None]
    bands = jnp.linspace(1e-4, HY_BANDS - 1, HY_BANDS, dtype=f32)[None, :]
    ang = (2.0 * math.pi / L) * bands * pos
    feats = jnp.concatenate([t, jnp.cos(ang), -jnp.sin(ang)], axis=-1)
    fr = freq.astype(f32)
    h = jnp.sin(fr[0] * (feats @ fw1.astype(f32) + fb1.astype(f32)))
    h = jnp.sin(fr[1] * (h @ fw2.astype(f32) + fb2.astype(f32)))
    h = (h @ fw3.astype(f32)).reshape(L, HY_DIRS, HY_ORDER, HY_WIDTH)
    rate = jnp.abs(decay.astype(f32)).reshape(HY_DIRS, HY_ORDER, HY_WIDTH)
    h = h * (jnp.exp(-t[:, :, None, None] * rate) + HY_DECAY_SHIFT)
    hf, hb = h[:, 0], h[:, 1]
    k = jnp.concatenate([hf[:1] + hb[:1], hf[1:], jnp.zeros_like(hf[:1]), hb[1:][::-1]], axis=0)
    k = k / jnp.sum(jnp.abs(k), axis=0, keepdims=True)
    return jnp.fft.rfft(k, axis=0)


def bidir_long_conv(u, k_spec, d_skip):
    L = u.shape[1]
    uf = jnp.fft.rfft(u.astype(jnp.float32), n=2 * L, axis=1)
    y = jnp.fft.irfft(uf * k_spec[None], n=2 * L, axis=1)[:, :L]
    return (y + u.astype(jnp.float32) * d_skip.astype(jnp.float32)).astype(u.dtype)


def hyena_branch(z, p):
    L = z.shape[1]
    z = short_conv3(z, p['hy_conv_w'], p['hy_conv_b'])
    v, x1, x2 = jnp.split(z, 3, axis=-1)
    k_spec = hyena_filter_spectra(L, p['hy_fw1'], p['hy_fb1'], p['hy_fw2'], p['hy_fb2'],
                                  p['hy_fw3'], p['hy_freq'], p['hy_decay'])
    y = x1 * bidir_long_conv(v, k_spec[:, 0], p['hy_d'][0])
    return x2 * bidir_long_conv(y, k_spec[:, 1], p['hy_d'][1])


def windowed_gqa_latent(q, k, v, kc, vc, sink):
    B, L, H, Dh = q.shape
    kvh = k.shape[2]
    g = H // kvh
    nb = L // WA_BLOCK
    qb = q.reshape(B, nb, WA_BLOCK, kvh, g, Dh)

    def band(t):
        tp = jnp.pad(t, ((0, 0), (WA_BLOCK, WA_BLOCK), (0, 0), (0, 0)))
        tp = tp.reshape(B, nb + 2, WA_BLOCK, kvh, Dh)
        return jnp.concatenate([tp[:, :-2], tp[:, 1:-1], tp[:, 2:]], axis=2)

    kb, vb = band(k), band(v)
    scale = HEAD_DIM ** -0.5
    blk = jnp.arange(nb)[:, None] * WA_BLOCK
    qpos = (blk + jnp.arange(WA_BLOCK)[None])[:, :, None]
    kpos = (blk - WA_BLOCK + jnp.arange(3 * WA_BLOCK)[None])[:, None, :]
    valid = (kpos >= 0) & (kpos < L) & (jnp.abs(qpos - kpos) <= WA_WINDOW)
    s_loc = jnp.einsum('bnqkgd,bnskd->bnkgqs', qb, kb).astype(jnp.float32) * scale
    s_loc = jnp.where(valid[None, :, None, None], s_loc, NEG_INF)
    s_ctx = jnp.einsum('bnqkgd,bckd->bnkgqc', qb, kc).astype(jnp.float32) * scale
    s_sink = jnp.broadcast_to(sink.astype(jnp.float32).reshape(1, 1, kvh, g, 1, 1), s_loc.shape[:-1] + (1,))
    p = jax.nn.softmax(jnp.concatenate([s_loc, s_ctx, s_sink], axis=-1), axis=-1).astype(v.dtype)
    nl = 3 * WA_BLOCK
    nc = kc.shape[1]
    o = (jnp.einsum('bnkgqs,bnskd->bnqkgd', p[..., :nl], vb)
         + jnp.einsum('bnkgqc,bckd->bnqkgd', p[..., nl:nl + nc], vc))
    return o.reshape(B, L, H * Dh)


def neighbourhood_attention_latent(q, k, v, kc, vc, rpb):
    B, L, H, Dh = q.shape
    rows = L // GRID_W
    wr = min(NA_WIN_ROWS, rows)
    qg = q.reshape(B, rows, GRID_W, H, Dh)
    kg = k.reshape(B, rows, GRID_W, H, Dh)
    vg = v.reshape(B, rows, GRID_W, H, Dh)
    col = jnp.arange(GRID_W)
    cstart = jnp.clip(col - NA_WIN_COLS // 2, 0, GRID_W - NA_WIN_COLS)
    col_in = (col[None, :] >= cstart[:, None]) & (col[None, :] < cstart[:, None] + NA_WIN_COLS)
    dc_idx = jnp.clip(col[None, :] - col[:, None] + NA_WIN_COLS - 1, 0, 2 * NA_WIN_COLS - 2)
    scale = HEAD_DIM ** -0.5
    n_loc = wr * GRID_W

    def row_block(r):
        r0 = jnp.clip(r - NA_WIN_ROWS // 2, 0, rows - wr)
        q_r = lax.dynamic_index_in_dim(qg, r, axis=1, keepdims=False)
        k_r = lax.dynamic_slice_in_dim(kg, r0, wr, axis=1)
        v_r = lax.dynamic_slice_in_dim(vg, r0, wr, axis=1).reshape(B, n_loc, H, Dh)
        dr_idx = r0 + jnp.arange(wr) - r + NA_WIN_ROWS - 1
        bias = rpb[:, dr_idx[None, :, None], dc_idx[:, None, :]].astype(jnp.float32)
        s_loc = jnp.einsum('bqhd,bwkhd->bhqwk', q_r, k_r).astype(jnp.float32) * scale + bias
        s_loc = jnp.where(col_in[:, None, :], s_loc, NEG_INF).reshape(B, H, GRID_W, n_loc)
        s_ctx = jnp.einsum('bqhd,bchd->bhqc', q_r, kc).astype(jnp.float32) * scale
        p = jax.nn.softmax(jnp.concatenate([s_loc, s_ctx], axis=-1), axis=-1).astype(v.dtype)
        return (jnp.einsum('bhqs,bshd->bqhd', p[..., :n_loc], v_r)
                + jnp.einsum('bhqc,bchd->bqhd', p[..., n_loc:], vc))

    out = lax.map(row_block, jnp.arange(rows))
    return jnp.moveaxis(out, 0, 1).reshape(B, L, H * Dh)


def context_attention(q, k, v, sink):
    B, Lc, H, Dh = q.shape
    kvh = k.shape[2]
    g = H // kvh
    qg = q.reshape(B, Lc, kvh, g, Dh)
    s = jnp.einsum('bqkgd,bckd->bkgqc', qg, k).astype(jnp.float32) * (HEAD_DIM ** -0.5)
    if sink is not None:
        s_sink = jnp.broadcast_to(sink.astype(jnp.float32).reshape(1, kvh, g, 1, 1), s.shape[:-1] + (1,))
        s = jnp.concatenate([s, s_sink], axis=-1)
    p = jax.nn.softmax(s, axis=-1)[..., :Lc].astype(v.dtype)
    return jnp.einsum('bkgqc,bckd->bqkgd', p, v).reshape(B, Lc, H * Dh)


def merge_branches(y_hy, y_wa, y_na, z_gate, w_branch, w_out):
    g = jax.nn.sigmoid(z_gate.astype(jnp.float32)).astype(z_gate.dtype)
    g_hy, g_wa, g_na = jnp.split(g, N_BRANCH, axis=-1)
    m = (g_hy * matmul(y_hy[0], w_branch[0])[None] + g_wa * matmul(y_wa[0], w_branch[1])[None]
         + g_na * matmul(y_na[0], w_branch[2])[None])
    return matmul(m[0], w_out)[None]


def expert_choice_ffn(h, w_router, w_gate, w_up, w_down):
    B, N, D = h.shape
    cap = EC_CAPACITY_FACTOR * N // N_EXPERTS
    hs = h[0]
    aff = jax.nn.softmax(jnp.dot(hs, w_router, precision=lax.Precision.HIGHEST).astype(jnp.float32), axis=-1)
    gsel, idx = lax.top_k(aff.T, cap)
    xe = hs[idx]
    a = batched_matmul(xe, w_gate)
    u = batched_matmul(xe, w_up)
    ye = batched_matmul(jax.nn.silu(a) * u, w_down) * gsel[..., None].astype(hs.dtype)
    return jnp.zeros_like(hs).at[idx.reshape(-1)].add(ye.reshape(-1, D))[None]


def trunk_layer(x, ctx, c, c_ctx, p, update_ctx):
    L = x.shape[1]
    mx = jnp.split((jax.nn.silu(c) @ p['w_mod'] + p['b_mod'])[:, None, :], N_MOD, axis=-1)
    mc = jnp.split(jax.nn.silu(c_ctx) @ p['w_mod'] + p['b_mod'], N_MOD, axis=-1)
    w_in = p['w_in']
    hx = modulate(rms_norm(x, p['norm1_g']), mx[0], mx[1])
    hc = modulate(rms_norm(ctx, p['norm1_g']), mc[0], mc[1])

    if update_ctx:
        zc = matmul(hc[0], w_in)[None]
        zc_wa_kv = zc[..., OFF_WA_KV:OFF_NA_Q]
        zc_na_kv = zc[..., OFF_NA_KV:OFF_GATE]
    else:
        zc_wa_kv = matmul(hc[0], w_in[:, OFF_WA_KV:OFF_NA_Q])[None]
        zc_na_kv = matmul(hc[0], w_in[:, OFF_NA_KV:OFF_GATE])[None]
    kc_wa, vc_wa = keys_values(zc_wa_kv, WA_KV_HEADS, p['wa_k_norm'])
    kc_na, vc_na = keys_values(zc_na_kv, NA_HEADS, p['na_k_norm'])

    zx = matmul(hx[0], w_in)[None]
    cos, sin = axial_rope_tables(L)
    y_hy = hyena_branch(zx[..., OFF_HY:OFF_WA_Q], p)
    q_wa = apply_axial_rope(rms_norm(split_heads(zx[..., OFF_WA_Q:OFF_WA_KV], WA_HEADS), p['wa_q_norm']), cos, sin)
    k_wa, v_wa = keys_values(zx[..., OFF_WA_KV:OFF_NA_Q], WA_KV_HEADS, p['wa_k_norm'])
    k_wa = apply_axial_rope(k_wa, cos, sin)
    y_wa = windowed_gqa_latent(q_wa, k_wa, v_wa, kc_wa, vc_wa, p['wa_sink'])
    q_na = rms_norm(split_heads(zx[..., OFF_NA_Q:OFF_NA_KV], NA_HEADS), p['na_q_norm'])
    k_na, v_na = keys_values(zx[..., OFF_NA_KV:OFF_GATE], NA_HEADS, p['na_k_norm'])
    y_na = neighbourhood_attention_latent(q_na, k_na, v_na, kc_na, vc_na, p['na_rpb'])
    x = x + mx[2] * merge_branches(y_hy, y_wa, y_na, zx[..., OFF_GATE:], p['w_branch'], p['w_out'])
    h2 = modulate(rms_norm(x, p['norm2_g']), mx[3], mx[4])
    x = x + mx[5] * expert_choice_ffn(h2, p['w_router'], p['w_gate'], p['w_up'], p['w_down'])

    if update_ctx:
        yc_hy = hyena_branch(zc[..., OFF_HY:OFF_WA_Q], p)
        qc_wa = rms_norm(split_heads(zc[..., OFF_WA_Q:OFF_WA_KV], WA_HEADS), p['wa_q_norm'])
        yc_wa = context_attention(qc_wa, kc_wa, vc_wa, p['wa_sink'])
        qc_na = rms_norm(split_heads(zc[..., OFF_NA_Q:OFF_NA_KV], NA_HEADS), p['na_q_norm'])
        yc_na = context_attention(qc_na, kc_na, vc_na, None)
        ctx = ctx + mc[2] * merge_branches(yc_hy, yc_wa, yc_na, zc[..., OFF_GATE:], p['w_branch'], p['w_out'])
        hc2 = modulate(rms_norm(ctx, p['norm2_g']), mc[3], mc[4])
        ctx = ctx + mc[5] * expert_choice_ffn(hc2, p['w_router'], p['w_gate'], p['w_up'], p['w_down'])
    return x, ctx


def kernel(x, c, ctx, c_ctx, w_mod, b_mod, norm1_g, w_in, hy_conv_w, hy_conv_b, hy_fw1, hy_fb1, hy_fw2, hy_fb2, hy_fw3, hy_freq, hy_decay, hy_d, wa_q_norm, wa_k_norm, wa_sink, na_q_norm, na_k_norm, na_rpb, w_branch, w_out, norm2_g, w_router, w_gate, w_up, w_down):
    for l in range(DEPTH):
        p = {
            'w_mod': w_mod[l], 'b_mod': b_mod[l], 'norm1_g': norm1_g[l], 'w_in': w_in[l],
            'hy_conv_w': hy_conv_w[l], 'hy_conv_b': hy_conv_b[l], 'hy_fw1': hy_fw1[l], 'hy_fb1': hy_fb1[l],
            'hy_fw2': hy_fw2[l], 'hy_fb2': hy_fb2[l], 'hy_fw3': hy_fw3[l], 'hy_freq': hy_freq[l],
            'hy_decay': hy_decay[l], 'hy_d': hy_d[l], 'wa_q_norm': wa_q_norm[l], 'wa_k_norm': wa_k_norm[l],
            'wa_sink': wa_sink[l], 'na_q_norm': na_q_norm[l], 'na_k_norm': na_k_norm[l], 'na_rpb': na_rpb[l],
            'w_branch': w_branch[l], 'w_out': w_out[l], 'norm2_g': norm2_g[l], 'w_router': w_router[l],
            'w_gate': w_gate[l], 'w_up': w_up[l], 'w_down': w_down[l],
        }
        x, ctx = trunk_layer(x, ctx, c, c_ctx, p, l < DEPTH - 1)
    return x
```

```python
import math
from functools import partial

import jax
import jax.numpy as jnp
from jax import lax
from jax.experimental import pallas as pl
from jax.experimental.pallas import tpu as pltpu

D_MODEL = 2048
SEQ = 16384
DEPTH = 2
CTX_LEN = 256
GRID_W = 64
HEAD_DIM = 128
BRANCH_WIDTH = 1024
N_BRANCH = 3
N_MOD = 6
RMS_EPS = 1e-6
NEG_INF = -1e30

HY_WIDTH = BRANCH_WIDTH
HY_ORDER = 2
HY_DIRS = 2
HY_BANDS = 16
HY_DECAY_SHIFT = 0.05

WA_HEADS = BRANCH_WIDTH // HEAD_DIM
WA_KV_HEADS = 2
WA_WINDOW = 128
WA_BLOCK = 128

NA_HEADS = BRANCH_WIDTH // HEAD_DIM
NA_WIN_ROWS = 8
NA_WIN_COLS = 16

ROPE_BASE = 10000.0

N_EXPERTS = 16
EC_CAPACITY_FACTOR = 2
D_EXPERT = 1024

OFF_HY = 0
OFF_WA_Q = OFF_HY + 3 * HY_WIDTH
OFF_WA_KV = OFF_WA_Q + WA_HEADS * HEAD_DIM
OFF_NA_Q = OFF_WA_KV + 2 * WA_KV_HEADS * HEAD_DIM
OFF_NA_KV = OFF_NA_Q + NA_HEADS * HEAD_DIM
OFF_GATE = OFF_NA_KV + 2 * NA_HEADS * HEAD_DIM
N_IN = OFF_GATE + N_BRANCH * D_MODEL

VMEM_LIMIT_BYTES = 56 * 1024 * 1024


def _mm_kernel(a_ref, b_ref, o_ref):
    o_ref[...] = jnp.dot(a_ref[...].astype(jnp.bfloat16), b_ref[...].astype(jnp.bfloat16),
                         preferred_element_type=jnp.float32).astype(o_ref.dtype)


def _pick(n, pref):
    for t in pref:
        if n % t == 0:
            return t
    return n


def matmul(a, b, out_dtype=jnp.float32):
    M, K = a.shape
    _, N = b.shape
    tm = _pick(M, (512, 256, 128, 64, 32, 16, 8))
    tn = _pick(N, (1024, 512, 256, 128))
    return pl.pallas_call(
        _mm_kernel,
        grid=(N // tn, M // tm),
        in_specs=[pl.BlockSpec((tm, K), lambda j, i: (i, 0)),
                  pl.BlockSpec((K, tn), lambda j, i: (0, j))],
        out_specs=pl.BlockSpec((tm, tn), lambda j, i: (i, j)),
        out_shape=jax.ShapeDtypeStruct((M, N), out_dtype),
        compiler_params=pltpu.CompilerParams(
            dimension_semantics=("parallel", "parallel"), vmem_limit_bytes=VMEM_LIMIT_BYTES),
        name="matmul",
    )(a, b)


def _bmm_kernel(a_ref, b_ref, o_ref):
    o_ref[0] = jnp.dot(a_ref[0].astype(jnp.bfloat16), b_ref[0].astype(jnp.bfloat16),
                       preferred_element_type=jnp.float32).astype(o_ref.dtype)


def batched_matmul(a, b, out_dtype=jnp.float32):
    E, M, K = a.shape
    _, _, N = b.shape
    tm = _pick(M, (512, 256, 128, 64, 32, 16, 8))
    tn = _pick(N, (1024, 512, 256, 128))
    return pl.pallas_call(
        _bmm_kernel,
        grid=(E, N // tn, M // tm),
        in_specs=[pl.BlockSpec((1, tm, K), lambda e, j, i: (e, i, 0)),
                  pl.BlockSpec((1, K, tn), lambda e, j, i: (e, 0, j))],
        out_specs=pl.BlockSpec((1, tm, tn), lambda e, j, i: (e, i, j)),
        out_shape=jax.ShapeDtypeStruct((E, M, N), out_dtype),
        compiler_params=pltpu.CompilerParams(
            dimension_semantics=("parallel", "parallel", "parallel"), vmem_limit_bytes=VMEM_LIMIT_BYTES),
        name="batched_matmul",
    )(a, b)


def _norm_mod_kernel(x_ref, g_ref, shift_ref, scale_ref, o_ref):
    x = x_ref[...]
    y = x * lax.rsqrt(jnp.mean(x * x, axis=-1, keepdims=True) + RMS_EPS) * g_ref[...]
    o_ref[...] = (y * (1.0 + scale_ref[...]) + shift_ref[...]).astype(o_ref.dtype)


def norm_modulate(x, g, shift, scale, out_dtype=jnp.bfloat16):
    M, D = x.shape
    tm = _pick(M, (512, 256, 128, 64, 32, 16, 8))
    vec = pl.BlockSpec((1, D), lambda i: (0, 0))
    return pl.pallas_call(
        _norm_mod_kernel,
        grid=(M // tm,),
        in_specs=[pl.BlockSpec((tm, D), lambda i: (i, 0)), vec, vec, vec],
        out_specs=pl.BlockSpec((tm, D), lambda i: (i, 0)),
        out_shape=jax.ShapeDtypeStruct((M, D), out_dtype),
        compiler_params=pltpu.CompilerParams(
            dimension_semantics=("parallel",), vmem_limit_bytes=VMEM_LIMIT_BYTES),
        name="norm_modulate",
    )(x, g.reshape(1, D), shift.reshape(1, D), scale.reshape(1, D))


def _swap_halves(x):
    lane = lax.broadcasted_iota(jnp.int32, x.shape, 1)
    return jnp.where((lane % 64) < 32, pltpu.roll(x, 96, 1), pltpu.roll(x, 32, 1))


def _proj_kernel(*refs, mode, post_scale, rope):
    if mode == "headnorm":
        if rope:
            a_ref, w_ref, gain_ref, cos_ref, sin_ref, o_ref = refs
        else:
            a_ref, w_ref, gain_ref, o_ref = refs
    else:
        a_ref, w_ref, o_ref = refs
    acc = jnp.dot(a_ref[...], w_ref[...], preferred_element_type=jnp.float32)
    if mode == "plain":
        o_ref[...] = acc.astype(o_ref.dtype)
    elif mode == "sigmoid":
        o_ref[...] = jax.nn.sigmoid(acc).astype(o_ref.dtype)
    else:
        gain = gain_ref[...] * post_scale
        for h in range(acc.shape[1] // HEAD_DIM):
            xh = acc[:, h * HEAD_DIM:(h + 1) * HEAD_DIM]
            y = xh * lax.rsqrt(jnp.mean(xh * xh, axis=-1, keepdims=True) + RMS_EPS) * gain
            if rope:
                y = y * cos_ref[...] + _swap_halves(y) * sin_ref[...]
            o_ref[:, h * HEAD_DIM:(h + 1) * HEAD_DIM] = y.astype(o_ref.dtype)


def project(a, w, mode="plain", out_dtype=jnp.float32, gain=None, post_scale=1.0, rope=None):
    M, K = a.shape
    _, N = w.shape
    tm = _pick(M, (1024, 512, 256, 128, 64, 32, 16, 8))
    tn = _pick(N, (1024, 512, 256, 128))
    in_specs = [pl.BlockSpec((tm, K), lambda j, i: (i, 0)),
                pl.BlockSpec((K, tn), lambda j, i: (0, j))]
    args = [a, w]
    if mode == "headnorm":
        in_specs.append(pl.BlockSpec((1, HEAD_DIM), lambda j, i: (0, 0)))
        args.append(gain.reshape(1, HEAD_DIM).astype(jnp.float32))
        if rope is not None:
            in_specs += [pl.BlockSpec((tm, HEAD_DIM), lambda j, i: (i, 0))] * 2
            args += list(rope)
    return pl.pallas_call(
        partial(_proj_kernel, mode=mode, post_scale=post_scale, rope=rope is not None),
        grid=(N // tn, M // tm),
        in_specs=in_specs,
        out_specs=pl.BlockSpec((tm, tn), lambda j, i: (i, j)),
        out_shape=jax.ShapeDtypeStruct((M, N), out_dtype),
        compiler_params=pltpu.CompilerParams(
            dimension_semantics=("parallel", "parallel"), vmem_limit_bytes=VMEM_LIMIT_BYTES),
        name="project_" + mode,
    )(*args)


def rope_lane_tables(L):
    t = jnp.arange(L, dtype=jnp.int32)
    row = (t // GRID_W).astype(jnp.float32)
    col = (t % GRID_W).astype(jnp.float32)
    nf = HEAD_DIM // 4
    inv = ROPE_BASE ** (-jnp.arange(nf, dtype=jnp.float32) / nf)
    ar, ac = row[:, None] * inv, col[:, None] * inv
    cos = jnp.concatenate([jnp.cos(ar), jnp.cos(ar), jnp.cos(ac), jnp.cos(ac)], axis=-1)
    sin = jnp.concatenate([-jnp.sin(ar), jnp.sin(ar), -jnp.sin(ac), jnp.sin(ac)], axis=-1)
    return cos, sin


_NT = (((1,), (1,)), ((), ()))


def _wa_kernel(sink_ref, q_ref, k_ref, v_ref, kc_ref, vc_ref, o_ref, *, tq, seq):
    g = pl.program_id(0)
    i = pl.program_id(1)
    nwin = tq + 2 * WA_WINDOW
    ws = jnp.clip(i * tq - WA_WINDOW, 0, seq - nwin)
    start = pl.multiple_of(ws, WA_WINDOW)
    kwin = k_ref[pl.ds(start, nwin), :]
    vwin = v_ref[pl.ds(start, nwin), :]
    qpos = i * tq + lax.broadcasted_iota(jnp.int32, (tq, nwin), 0)
    kpos = ws + lax.broadcasted_iota(jnp.int32, (tq, nwin), 1)
    valid = jnp.abs(qpos - kpos) <= WA_WINDOW
    group = WA_HEADS // WA_KV_HEADS
    for hh in range(group):
        q = q_ref[:, hh * HEAD_DIM:(hh + 1) * HEAD_DIM]
        s = jnp.where(valid, lax.dot_general(q, kwin, _NT, preferred_element_type=jnp.float32), NEG_INF)
        sc = lax.dot_general(q, kc_ref[...], _NT, preferred_element_type=jnp.float32)
        sk = sink_ref[g * group + hh]
        m = jnp.maximum(jnp.maximum(jnp.max(s, axis=-1, keepdims=True), jnp.max(sc, axis=-1, keepdims=True)), sk)
        p = jnp.exp(s - m)
        pc = jnp.exp(sc - m)
        denom = jnp.sum(p, axis=-1, keepdims=True) + jnp.sum(pc, axis=-1, keepdims=True) + jnp.exp(sk - m)
        o = (jnp.dot(p.astype(vwin.dtype), vwin, preferred_element_type=jnp.float32)
             + jnp.dot(pc.astype(vwin.dtype), vc_ref[...], preferred_element_type=jnp.float32))
        o_ref[:, hh * HEAD_DIM:(hh + 1) * HEAD_DIM] = (o / denom).astype(o_ref.dtype)


def windowed_attention(q, k, v, kc, vc, sink, tq=256):
    L = q.shape[0]
    Lc = kc.shape[0]
    gw = (WA_HEADS // WA_KV_HEADS) * HEAD_DIM
    slab = pl.BlockSpec((L, HEAD_DIM), lambda g, i, s: (0, g))
    cslab = pl.BlockSpec((Lc, HEAD_DIM), lambda g, i, s: (0, g))
    return pl.pallas_call(
        partial(_wa_kernel, tq=tq, seq=L),
        grid_spec=pltpu.PrefetchScalarGridSpec(
            num_scalar_prefetch=1,
            grid=(WA_KV_HEADS, L // tq),
            in_specs=[pl.BlockSpec((tq, gw), lambda g, i, s: (i, g)), slab, slab, cslab, cslab],
            out_specs=pl.BlockSpec((tq, gw), lambda g, i, s: (i, g)),
        ),
        out_shape=jax.ShapeDtypeStruct((L, WA_HEADS * HEAD_DIM), jnp.bfloat16),
        compiler_params=pltpu.CompilerParams(
            dimension_semantics=("parallel", "parallel"), vmem_limit_bytes=VMEM_LIMIT_BYTES),
        name="windowed_attention",
    )(sink.astype(jnp.float32), q, k, v, kc, vc)


NA_ROW_BLOCK = 4
NA_KEY_ROWS = NA_ROW_BLOCK + NA_WIN_ROWS - 1


def _na_kernel(q_ref, k_ref, v_ref, kc_ref, vc_ref, bias_ref, o_ref, *, rows):
    i = pl.program_id(1)
    ws = jnp.clip(i * NA_ROW_BLOCK - NA_WIN_ROWS // 2, 0, rows - NA_KEY_ROWS)
    start = pl.multiple_of(ws * GRID_W, GRID_W)
    nk = NA_KEY_ROWS * GRID_W
    kwin = k_ref[pl.ds(start, nk), :]
    vwin = v_ref[pl.ds(start, nk), :]
    q = q_ref[...]
    s = lax.dot_general(q, kwin, _NT, preferred_element_type=jnp.float32) + bias_ref[0, 0]
    sc = lax.dot_general(q, kc_ref[...], _NT, preferred_element_type=jnp.float32)
    m = jnp.maximum(jnp.max(s, axis=-1, keepdims=True), jnp.max(sc, axis=-1, keepdims=True))
    p = jnp.exp(s - m)
    pc = jnp.exp(sc - m)
    denom = jnp.sum(p, axis=-1, keepdims=True) + jnp.sum(pc, axis=-1, keepdims=True)
    o = (jnp.dot(p.astype(vwin.dtype), vwin, preferred_element_type=jnp.float32)
         + jnp.dot(pc.astype(vwin.dtype), vc_ref[...], preferred_element_type=jnp.float32))
    o_ref[...] = (o / denom).astype(o_ref.dtype)


def na_bias_tiles(rpb, rows):
    col = jnp.arange(GRID_W)
    cstart = jnp.clip(col - NA_WIN_COLS // 2, 0, GRID_W - NA_WIN_COLS)
    col_in = (col[None, :] >= cstart[:, None]) & (col[None, :] < cstart[:, None] + NA_WIN_COLS)
    dc_idx = jnp.clip(col[None, :] - col[:, None] + NA_WIN_COLS - 1, 0, 2 * NA_WIN_COLS - 2)
    tiles = []
    for blk in (0, 1, rows // NA_ROW_BLOCK - 1):
        r = blk * NA_ROW_BLOCK
        ws = min(max(r - NA_WIN_ROWS // 2, 0), rows - NA_KEY_ROWS)
        qr = r + jnp.arange(NA_ROW_BLOCK)
        kr = ws + jnp.arange(NA_KEY_ROWS)
        r0 = jnp.clip(qr - NA_WIN_ROWS // 2, 0, rows - NA_WIN_ROWS)
        row_in = (kr[None, :] >= r0[:, None]) & (kr[None, :] < r0[:, None] + NA_WIN_ROWS)
        dr_idx = jnp.clip(kr[None, :] - qr[:, None] + NA_WIN_ROWS - 1, 0, 2 * NA_WIN_ROWS - 2)
        b = rpb[:, dr_idx[:, None, :, None], dc_idx[None, :, None, :]].astype(jnp.float32)
        ok = row_in[:, None, :, None] & col_in[None, :, None, :]
        b = jnp.where(ok[None], b, NEG_INF)
        tiles.append(b.reshape(rpb.shape[0], NA_ROW_BLOCK * GRID_W, NA_KEY_ROWS * GRID_W))
    return jnp.stack(tiles)


def neighbourhood_attention(q, k, v, kc, vc, rpb):
    L = q.shape[0]
    Lc = kc.shape[0]
    rows = L // GRID_W
    nblk = rows // NA_ROW_BLOCK
    tq = NA_ROW_BLOCK * GRID_W
    nk = NA_KEY_ROWS * GRID_W
    bias = na_bias_tiles(rpb, rows)
    slab = pl.BlockSpec((L, HEAD_DIM), lambda h, i: (0, h))
    cslab = pl.BlockSpec((Lc, HEAD_DIM), lambda h, i: (0, h))
    variant = lambda h, i: (jnp.where(i == 0, 0, jnp.where(i == nblk - 1, 2, 1)), h, 0, 0)
    return pl.pallas_call(
        partial(_na_kernel, rows=rows),
        grid=(NA_HEADS, nblk),
        in_specs=[pl.BlockSpec((tq, HEAD_DIM), lambda h, i: (i, h)), slab, slab, cslab, cslab,
                  pl.BlockSpec((1, 1, tq, nk), variant)],
        out_specs=pl.BlockSpec((tq, HEAD_DIM), lambda h, i: (i, h)),
        out_shape=jax.ShapeDtypeStruct((L, NA_HEADS * HEAD_DIM), jnp.bfloat16),
        compiler_params=pltpu.CompilerParams(
            dimension_semantics=("parallel", "parallel"), vmem_limit_bytes=VMEM_LIMIT_BYTES),
        name="neighbourhood_attention",
    )(q, k, v, kc, vc, bias)


def rms_norm(x, g):
    xf = x.astype(jnp.float32)
    y = xf * lax.rsqrt(jnp.mean(xf * xf, axis=-1, keepdims=True) + RMS_EPS)
    return (y * g.astype(jnp.float32)).astype(x.dtype)


def modulate(h, shift, scale):
    return h * (1 + scale) + shift


def split_heads(z, n_heads):
    return z.reshape(z.shape[:-1] + (n_heads, HEAD_DIM))


def keys_values(z, n_heads, k_gain):
    k, v = jnp.split(z, 2, axis=-1)
    return rms_norm(split_heads(k, n_heads), k_gain), split_heads(v, n_heads)


def axial_rope_tables(L):
    t = jnp.arange(L, dtype=jnp.int32)
    row = (t // GRID_W).astype(jnp.float32)
    col = (t % GRID_W).astype(jnp.float32)
    nf = HEAD_DIM // 4
    inv = ROPE_BASE ** (-jnp.arange(nf, dtype=jnp.float32) / nf)
    ang = jnp.stack([row[:, None] * inv, col[:, None] * inv], axis=1)
    return jnp.cos(ang), jnp.sin(ang)


def apply_axial_rope(x, cos, sin):
    nf = HEAD_DIM // 4
    xr = x.reshape(x.shape[:-1] + (2, 2, nf))
    x1, x2 = xr[..., 0, :], xr[..., 1, :]
    c = cos[None, :, None].astype(x.dtype)
    s = sin[None, :, None].astype(x.dtype)
    out = jnp.stack([x1 * c - x2 * s, x2 * c + x1 * s], axis=-2)
    return out.reshape(x.shape)


def short_conv3(z, w, b):
    L = z.shape[1]
    zp = jnp.pad(z, ((0, 0), (1, 1), (0, 0)))
    return zp[:, :L] * w[0] + zp[:, 1:L + 1] * w[1] + zp[:, 2:] * w[2] + b


def hyena_filter_spectra(L, fw1, fb1, fw2, fb2, fw3, freq, decay):
    f32 = jnp.float32
    t = jnp.linspace(0.0, 1.0, L, dtype=f32)[:, None]
    pos = jnp.arange(L, dtype=f32)[:, None]
    bands = jnp.linspace(1e-4, HY_BANDS - 1, HY_BANDS, dtype=f32)[None, :]
    ang = (2.0 * math.pi / L) * bands * pos
    feats = jnp.concatenate([t, jnp.cos(ang), -jnp.sin(ang)], axis=-1)
    fr = freq.astype(f32)
    h = jnp.sin(fr[0] * (feats @ fw1.astype(f32) + fb1.astype(f32)))
    h = jnp.sin(fr[1] * (h @ fw2.astype(f32) + fb2.astype(f32)))
    h = (h @ fw3.astype(f32)).reshape(L, HY_DIRS, HY_ORDER, HY_WIDTH)
    rate = jnp.abs(decay.astype(f32)).reshape(HY_DIRS, HY_ORDER, HY_WIDTH)
    h = h * (jnp.exp(-t[:, :, None, None] * rate) + HY_DECAY_SHIFT)
    hf, hb = h[:, 0], h[:, 1]
    k = jnp.concatenate([hf[:1] + hb[:1], hf[1:], jnp.zeros_like(hf[:1]), hb[1:][::-1]], axis=0)
    k = k / jnp.sum(jnp.abs(k), axis=0, keepdims=True)
    return jnp.fft.rfft(k, axis=0)


def bidir_long_conv(u, k_spec, d_skip):
    L = u.shape[1]
    uf = jnp.fft.rfft(u.astype(jnp.float32), n=2 * L, axis=1)
    y = jnp.fft.irfft(uf * k_spec[None], n=2 * L, axis=1)[:, :L]
    return (y + u.astype(jnp.float32) * d_skip.astype(jnp.float32)).astype(u.dtype)


def hyena_branch(z, p):
    L = z.shape[1]
    z = short_conv3(z, p['hy_conv_w'], p['hy_conv_b'])
    v, x1, x2 = jnp.split(z, 3, axis=-1)
    k_spec = hyena_filter_spectra(L, p['hy_fw1'], p['hy_fb1'], p['hy_fw2'], p['hy_fb2'],
                                  p['hy_fw3'], p['hy_freq'], p['hy_decay'])
    y = x1 * bidir_long_conv(v, k_spec[:, 0], p['hy_d'][0])
    return x2 * bidir_long_conv(y, k_spec[:, 1], p['hy_d'][1])


def windowed_gqa_latent(q, k, v, kc, vc, sink):
    B, L, H, Dh = q.shape
    kvh = k.shape[2]
    g = H // kvh
    nb = L // WA_BLOCK
    qb = q.reshape(B, nb, WA_BLOCK, kvh, g, Dh)

    def band(t):
        tp = jnp.pad(t, ((0, 0), (WA_BLOCK, WA_BLOCK), (0, 0), (0, 0)))
        tp = tp.reshape(B, nb + 2, WA_BLOCK, kvh, Dh)
        return jnp.concatenate([tp[:, :-2], tp[:, 1:-1], tp[:, 2:]], axis=2)

    kb, vb = band(k), band(v)
    scale = HEAD_DIM ** -0.5
    blk = jnp.arange(nb)[:, None] * WA_BLOCK
    qpos = (blk + jnp.arange(WA_BLOCK)[None])[:, :, None]
    kpos = (blk - WA_BLOCK + jnp.arange(3 * WA_BLOCK)[None])[:, None, :]
    valid = (kpos >= 0) & (kpos < L) & (jnp.abs(qpos - kpos) <= WA_WINDOW)
    s_loc = jnp.einsum('bnqkgd,bnskd->bnkgqs', qb, kb).astype(jnp.float32) * scale
    s_loc = jnp.where(valid[None, :, None, None], s_loc, NEG_INF)
    s_ctx = jnp.einsum('bnqkgd,bckd->bnkgqc', qb, kc).astype(jnp.float32) * scale
    s_sink = jnp.broadcast_to(sink.astype(jnp.float32).reshape(1, 1, kvh, g, 1, 1), s_loc.shape[:-1] + (1,))
    p = jax.nn.softmax(jnp.concatenate([s_loc, s_ctx, s_sink], axis=-1), axis=-1).astype(v.dtype)
    nl = 3 * WA_BLOCK
    nc = kc.shape[1]
    o = (jnp.einsum('bnkgqs,bnskd->bnqkgd', p[..., :nl], vb)
         + jnp.einsum('bnkgqc,bckd->bnqkgd', p[..., nl:nl + nc], vc))
    return o.reshape(B, L, H * Dh)


def neighbourhood_attention_latent(q, k, v, kc, vc, rpb):
    B, L, H, Dh = q.shape
    rows = L // GRID_W
    wr = min(NA_WIN_ROWS, rows)
    qg = q.reshape(B, rows, GRID_W, H, Dh)
    kg = k.reshape(B, rows, GRID_W, H, Dh)
    vg = v.reshape(B, rows, GRID_W, H, Dh)
    col = jnp.arange(GRID_W)
    cstart = jnp.clip(col - NA_WIN_COLS // 2, 0, GRID_W - NA_WIN_COLS)
    col_in = (col[None, :] >= cstart[:, None]) & (col[None, :] < cstart[:, None] + NA_WIN_COLS)
    dc_idx = jnp.clip(col[None, :] - col[:, None] + NA_WIN_COLS - 1, 0, 2 * NA_WIN_COLS - 2)
    scale = HEAD_DIM ** -0.5
    n_loc = wr * GRID_W

    def row_block(r):
        r0 = jnp.clip(r - NA_WIN_ROWS // 2, 0, rows - wr)
        q_r = lax.dynamic_index_in_dim(qg, r, axis=1, keepdims=False)
        k_r = lax.dynamic_slice_in_dim(kg, r0, wr, axis=1)
        v_r = lax.dynamic_slice_in_dim(vg, r0, wr, axis=1).reshape(B, n_loc, H, Dh)
        dr_idx = r0 + jnp.arange(wr) - r + NA_WIN_ROWS - 1
        bias = rpb[:, dr_idx[None, :, None], dc_idx[:, None, :]].astype(jnp.float32)
        s_loc = jnp.einsum('bqhd,bwkhd->bhqwk', q_r, k_r).astype(jnp.float32) * scale + bias
        s_loc = jnp.where(col_in[:, None, :], s_loc, NEG_INF).reshape(B, H, GRID_W, n_loc)
        s_ctx = jnp.einsum('bqhd,bchd->bhqc', q_r, kc).astype(jnp.float32) * scale
        p = jax.nn.softmax(jnp.concatenate([s_loc, s_ctx], axis=-1), axis=-1).astype(v.dtype)
        return (jnp.einsum('bhqs,bshd->bqhd', p[..., :n_loc], v_r)
                + jnp.einsum('bhqc,bchd->bqhd', p[..., n_loc:], vc))

    out = lax.map(row_block, jnp.arange(rows))
    return jnp.moveaxis(out, 0, 1).reshape(B, L, H * Dh)


def context_attention(q, k, v, sink):
    B, Lc, H, Dh = q.shape
    kvh = k.shape[2]
    g = H // kvh
    qg = q.reshape(B, Lc, kvh, g, Dh)
    s = jnp.einsum('bqkgd,bckd->bkgqc', qg, k).astype(jnp.float32) * (HEAD_DIM ** -0.5)
    if sink is not None:
        s_sink = jnp.broadcast_to(sink.astype(jnp.float32).reshape(1, kvh, g, 1, 1), s.shape[:-1] + (1,))
        s = jnp.concatenate([s, s_sink], axis=-1)
    p = jax.nn.softmax(s, axis=-1)[..., :Lc].astype(v.dtype)
    return jnp.einsum('bkgqc,bckd->bqkgd', p, v).reshape(B, Lc, H * Dh)


def merge_branches(y_hy, y_wa, y_na, z_gate, w_branch, w_out):
    g = jax.nn.sigmoid(z_gate.astype(jnp.float32)).astype(z_gate.dtype)
    g_hy, g_wa, g_na = jnp.split(g, N_BRANCH, axis=-1)
    m = (g_hy * matmul(y_hy[0], w_branch[0])[None] + g_wa * matmul(y_wa[0], w_branch[1])[None]
         + g_na * matmul(y_na[0], w_branch[2])[None])
    return matmul(m[0], w_out)[None]


def merge_latent(y_hy, y_wa, y_na, g, w_branch, w_out):
    g_hy, g_wa, g_na = jnp.split(g, N_BRANCH, axis=-1)
    m = (g_hy * matmul(y_hy[0], w_branch[0])[None] + g_wa * matmul(y_wa[0], w_branch[1])[None]
         + g_na * matmul(y_na[0], w_branch[2])[None])
    return matmul(m[0], w_out)[None]


def expert_choice_ffn(h, w_router, w_gate, w_up, w_down):
    B, N, D = h.shape
    cap = EC_CAPACITY_FACTOR * N // N_EXPERTS
    hs = h[0]
    aff = jax.nn.softmax(jnp.dot(hs, w_router, precision=lax.Precision.HIGHEST).astype(jnp.float32), axis=-1)
    gsel, idx = lax.top_k(aff.T, cap)
    xe = hs[idx]
    a = batched_matmul(xe, w_gate)
    u = batched_matmul(xe, w_up)
    ye = batched_matmul(jax.nn.silu(a) * u, w_down) * gsel[..., None].astype(hs.dtype)
    return jnp.zeros_like(hs).at[idx.reshape(-1)].add(ye.reshape(-1, D))[None]


def trunk_layer(x, ctx, c, c_ctx, p, update_ctx):
    L = x.shape[1]
    mx = jnp.split((jax.nn.silu(c) @ p['w_mod'] + p['b_mod'])[:, None, :], N_MOD, axis=-1)
    mc = jnp.split(jax.nn.silu(c_ctx) @ p['w_mod'] + p['b_mod'], N_MOD, axis=-1)
    w_in = p['w_in']
    hc = modulate(rms_norm(ctx, p['norm1_g']), mc[0], mc[1])

    if update_ctx:
        zc = matmul(hc[0], w_in)[None]
        zc_wa_kv = zc[..., OFF_WA_KV:OFF_NA_Q]
        zc_na_kv = zc[..., OFF_NA_KV:OFF_GATE]
    else:
        zc_wa_kv = matmul(hc[0], w_in[:, OFF_WA_KV:OFF_NA_Q])[None]
        zc_na_kv = matmul(hc[0], w_in[:, OFF_NA_KV:OFF_GATE])[None]
    kc_wa, vc_wa = keys_values(zc_wa_kv, WA_KV_HEADS, p['wa_k_norm'])
    kc_na, vc_na = keys_values(zc_na_kv, NA_HEADS, p['na_k_norm'])

    bf = jnp.bfloat16
    hxb = norm_modulate(x[0], p['norm1_g'], mx[0][0, 0], mx[1][0, 0])
    wb = w_in.astype(bf)
    rope = rope_lane_tables(L)
    qk_scale = HEAD_DIM ** -0.5
    z_hy = project(hxb, wb[:, OFF_HY:OFF_WA_Q])
    y_hy = hyena_branch(z_hy[None], p)
    q_wa = project(hxb, wb[:, OFF_WA_Q:OFF_WA_KV], "headnorm", bf, p['wa_q_norm'], qk_scale, rope)
    nkv = WA_KV_HEADS * HEAD_DIM
    k_wa = project(hxb, wb[:, OFF_WA_KV:OFF_WA_KV + nkv], "headnorm", bf, p['wa_k_norm'], 1.0, rope)
    v_wa = project(hxb, wb[:, OFF_WA_KV + nkv:OFF_NA_Q], "plain", bf)
    y_wa = windowed_attention(q_wa, k_wa, v_wa, kc_wa[0].reshape(-1, nkv).astype(bf),
                              vc_wa[0].reshape(-1, nkv).astype(bf), p['wa_sink'])[None]
    q_na = project(hxb, wb[:, OFF_NA_Q:OFF_NA_KV], "headnorm", bf, p['na_q_norm'], qk_scale)
    nna = NA_HEADS * HEAD_DIM
    k_na = project(hxb, wb[:, OFF_NA_KV:OFF_NA_KV + nna], "headnorm", bf, p['na_k_norm'])
    v_na = project(hxb, wb[:, OFF_NA_KV + nna:OFF_GATE], "plain", bf)
    y_na = neighbourhood_attention(q_na, k_na, v_na, kc_na[0].reshape(-1, nna).astype(bf),
                                   vc_na[0].reshape(-1, nna).astype(bf), p['na_rpb'])[None]
    gates = project(hxb, wb[:, OFF_GATE:], "sigmoid")[None]
    x = x + mx[2] * merge_latent(y_hy, y_wa, y_na, gates, p['w_branch'], p['w_out'])
    h2 = modulate(rms_norm(x, p['norm2_g']), mx[3], mx[4])
    x = x + mx[5] * expert_choice_ffn(h2, p['w_router'], p['w_gate'], p['w_up'], p['w_down'])

    if update_ctx:
        yc_hy = hyena_branch(zc[..., OFF_HY:OFF_WA_Q], p)
        qc_wa = rms_norm(split_heads(zc[..., OFF_WA_Q:OFF_WA_KV], WA_HEADS), p['wa_q_norm'])
        yc_wa = context_attention(qc_wa, kc_wa, vc_wa, p['wa_sink'])
        qc_na = rms_norm(split_heads(zc[..., OFF_NA_Q:OFF_NA_KV], NA_HEADS), p['na_q_norm'])
        yc_na = context_attention(qc_na, kc_na, vc_na, None)
        ctx = ctx + mc[2] * merge_branches(yc_hy, yc_wa, yc_na, zc[..., OFF_GATE:], p['w_branch'], p['w_out'])
        hc2 = modulate(rms_norm(ctx, p['norm2_g']), mc[3], mc[4])
        ctx = ctx + mc[5] * expert_choice_ffn(hc2, p['w_router'], p['w_gate'], p['w_up'], p['w_down'])
    return x, ctx


def kernel(x, c, ctx, c_ctx, w_mod, b_mod, norm1_g, w_in, hy_conv_w, hy_conv_b, hy_fw1, hy_fb1, hy_fw2, hy_fb2, hy_fw3, hy_freq, hy_decay, hy_d, wa_q_norm, wa_k_norm, wa_sink, na_q_norm, na_k_norm, na_rpb, w_branch, w_out, norm2_g, w_router, w_gate, w_up, w_down):
    for l in range(DEPTH):
        p = {
            'w_mod': w_mod[l], 'b_mod': b_mod[l], 'norm1_g': norm1_g[l], 'w_in': w_in[l],
            'hy_conv_w': hy_conv_w[l], 'hy_conv_b': hy_conv_b[l], 'hy_fw1': hy_fw1[l], 'hy_fb1': hy_fb1[l],
            'hy_fw2': hy_fw2[l], 'hy_fb2': hy_fb2[l], 'hy_fw3': hy_fw3[l], 'hy_freq': hy_freq[l],
            'hy_decay': hy_decay[l], 'hy_d': hy_d[l], 'wa_q_norm': wa_q_norm[l], 'wa_k_norm': wa_k_norm[l],
            'wa_sink': wa_sink[l], 'na_q_norm': na_q_norm[l], 'na_k_norm': na_k_norm[l], 'na_rpb': na_rpb[l],
            'w_branch': w_branch[l], 'w_out': w_out[l], 'norm2_g': norm2_g[l], 'w_router': w_router[l],
            'w_gate': w_gate[l], 'w_up': w_up[l], 'w_down': w_down[l],
        }
        x, ctx = trunk_layer(x, ctx, c, c_ctx, p, l < DEPTH - 1)
    return x
```

```python
import math
from functools import partial

import jax
import jax.numpy as jnp
from jax import lax
from jax.experimental import pallas as pl
from jax.experimental.pallas import tpu as pltpu

D_MODEL = 2048
SEQ = 16384
DEPTH = 2
CTX_LEN = 256
GRID_W = 64
HEAD_DIM = 128
BRANCH_WIDTH = 1024
N_BRANCH = 3
N_MOD = 6
RMS_EPS = 1e-6
NEG_INF = -1e30

HY_WIDTH = BRANCH_WIDTH
HY_ORDER = 2
HY_DIRS = 2
HY_BANDS = 16
HY_DECAY_SHIFT = 0.05

WA_HEADS = BRANCH_WIDTH // HEAD_DIM
WA_KV_HEADS = 2
WA_WINDOW = 128
WA_BLOCK = 128

NA_HEADS = BRANCH_WIDTH // HEAD_DIM
NA_WIN_ROWS = 8
NA_WIN_COLS = 16

ROPE_BASE = 10000.0

N_EXPERTS = 16
EC_CAPACITY_FACTOR = 2
D_EXPERT = 1024

OFF_HY = 0
OFF_WA_Q = OFF_HY + 3 * HY_WIDTH
OFF_WA_KV = OFF_WA_Q + WA_HEADS * HEAD_DIM
OFF_NA_Q = OFF_WA_KV + 2 * WA_KV_HEADS * HEAD_DIM
OFF_NA_KV = OFF_NA_Q + NA_HEADS * HEAD_DIM
OFF_GATE = OFF_NA_KV + 2 * NA_HEADS * HEAD_DIM
N_IN = OFF_GATE + N_BRANCH * D_MODEL

VMEM_LIMIT_BYTES = 56 * 1024 * 1024


def _mm_kernel(a_ref, b_ref, o_ref):
    o_ref[...] = jnp.dot(a_ref[...].astype(jnp.bfloat16), b_ref[...].astype(jnp.bfloat16),
                         preferred_element_type=jnp.float32).astype(o_ref.dtype)


def _pick(n, pref):
    for t in pref:
        if n % t == 0:
            return t
    return n


def matmul(a, b, out_dtype=jnp.float32, tm=None, tn=None):
    M, K = a.shape
    _, N = b.shape
    tm = tm or _pick(M, (512, 256, 128, 64, 32, 16, 8))
    tn = tn or _pick(N, (1024, 512, 256, 128))
    return pl.pallas_call(
        _mm_kernel,
        grid=(N // tn, M // tm),
        in_specs=[pl.BlockSpec((tm, K), lambda j, i: (i, 0)),
                  pl.BlockSpec((K, tn), lambda j, i: (0, j))],
        out_specs=pl.BlockSpec((tm, tn), lambda j, i: (i, j)),
        out_shape=jax.ShapeDtypeStruct((M, N), out_dtype),
        compiler_params=pltpu.CompilerParams(
            dimension_semantics=("parallel", "parallel"), vmem_limit_bytes=VMEM_LIMIT_BYTES),
        name="matmul",
    )(a, b)


def _bmm_kernel(a_ref, b_ref, o_ref):
    o_ref[0] = jnp.dot(a_ref[0].astype(jnp.bfloat16), b_ref[0].astype(jnp.bfloat16),
                       preferred_element_type=jnp.float32).astype(o_ref.dtype)


def batched_matmul(a, b, out_dtype=jnp.float32):
    E, M, K = a.shape
    _, _, N = b.shape
    tm = _pick(M, (512, 256, 128, 64, 32, 16, 8))
    tn = _pick(N, (1024, 512, 256, 128))
    return pl.pallas_call(
        _bmm_kernel,
        grid=(E, N // tn, M // tm),
        in_specs=[pl.BlockSpec((1, tm, K), lambda e, j, i: (e, i, 0)),
                  pl.BlockSpec((1, K, tn), lambda e, j, i: (e, 0, j))],
        out_specs=pl.BlockSpec((1, tm, tn), lambda e, j, i: (e, i, j)),
        out_shape=jax.ShapeDtypeStruct((E, M, N), out_dtype),
        compiler_params=pltpu.CompilerParams(
            dimension_semantics=("parallel", "parallel", "parallel"), vmem_limit_bytes=VMEM_LIMIT_BYTES),
        name="batched_matmul",
    )(a, b)


def _norm_mod_kernel(x_ref, g_ref, shift_ref, scale_ref, o_ref):
    x = x_ref[...]
    y = x * lax.rsqrt(jnp.mean(x * x, axis=-1, keepdims=True) + RMS_EPS) * g_ref[...]
    o_ref[...] = (y * (1.0 + scale_ref[...]) + shift_ref[...]).astype(o_ref.dtype)


def norm_modulate(x, g, shift, scale, out_dtype=jnp.bfloat16):
    M, D = x.shape
    tm = _pick(M, (512, 256, 128, 64, 32, 16, 8))
    vec = pl.BlockSpec((1, D), lambda i: (0, 0))
    return pl.pallas_call(
        _norm_mod_kernel,
        grid=(M // tm,),
        in_specs=[pl.BlockSpec((tm, D), lambda i: (i, 0)), vec, vec, vec],
        out_specs=pl.BlockSpec((tm, D), lambda i: (i, 0)),
        out_shape=jax.ShapeDtypeStruct((M, D), out_dtype),
        compiler_params=pltpu.CompilerParams(
            dimension_semantics=("parallel",), vmem_limit_bytes=VMEM_LIMIT_BYTES),
        name="norm_modulate",
    )(x, g.reshape(1, D), shift.reshape(1, D), scale.reshape(1, D))


def _swap_halves(x):
    lane = lax.broadcasted_iota(jnp.int32, x.shape, 1)
    return jnp.where((lane % 64) < 32, pltpu.roll(x, 96, 1), pltpu.roll(x, 32, 1))


def _proj_kernel(*refs, mode, post_scale, rope):
    if mode == "headnorm":
        if rope:
            a_ref, w_ref, gain_ref, cos_ref, sin_ref, o_ref = refs
        else:
            a_ref, w_ref, gain_ref, o_ref = refs
    else:
        a_ref, w_ref, o_ref = refs
    acc = jnp.dot(a_ref[...], w_ref[...], preferred_element_type=jnp.float32)
    if mode == "plain":
        o_ref[...] = acc.astype(o_ref.dtype)
    elif mode == "sigmoid":
        o_ref[...] = jax.nn.sigmoid(acc).astype(o_ref.dtype)
    else:
        gain = gain_ref[...] * post_scale
        for h in range(acc.shape[1] // HEAD_DIM):
            xh = acc[:, h * HEAD_DIM:(h + 1) * HEAD_DIM]
            y = xh * lax.rsqrt(jnp.mean(xh * xh, axis=-1, keepdims=True) + RMS_EPS) * gain
            if rope:
                y = y * cos_ref[...] + _swap_halves(y) * sin_ref[...]
            o_ref[:, h * HEAD_DIM:(h + 1) * HEAD_DIM] = y.astype(o_ref.dtype)


def project(a, w, mode="plain", out_dtype=jnp.float32, gain=None, post_scale=1.0, rope=None):
    M, K = a.shape
    _, N = w.shape
    tm = _pick(M, (1024, 512, 256, 128, 64, 32, 16, 8))
    tn = _pick(N, (1024, 512, 256, 128))
    in_specs = [pl.BlockSpec((tm, K), lambda j, i: (i, 0)),
                pl.BlockSpec((K, tn), lambda j, i: (0, j))]
    args = [a, w]
    if mode == "headnorm":
        in_specs.append(pl.BlockSpec((1, HEAD_DIM), lambda j, i: (0, 0)))
        args.append(gain.reshape(1, HEAD_DIM).astype(jnp.float32))
        if rope is not None:
            in_specs += [pl.BlockSpec((tm, HEAD_DIM), lambda j, i: (i, 0))] * 2
            args += list(rope)
    return pl.pallas_call(
        partial(_proj_kernel, mode=mode, post_scale=post_scale, rope=rope is not None),
        grid=(N // tn, M // tm),
        in_specs=in_specs,
        out_specs=pl.BlockSpec((tm, tn), lambda j, i: (i, j)),
        out_shape=jax.ShapeDtypeStruct((M, N), out_dtype),
        compiler_params=pltpu.CompilerParams(
            dimension_semantics=("parallel", "parallel"), vmem_limit_bytes=VMEM_LIMIT_BYTES),
        name="project_" + mode,
    )(*args)


def rope_lane_tables(L):
    t = jnp.arange(L, dtype=jnp.int32)
    row = (t // GRID_W).astype(jnp.float32)
    col = (t % GRID_W).astype(jnp.float32)
    nf = HEAD_DIM // 4
    inv = ROPE_BASE ** (-jnp.arange(nf, dtype=jnp.float32) / nf)
    ar, ac = row[:, None] * inv, col[:, None] * inv
    cos = jnp.concatenate([jnp.cos(ar), jnp.cos(ar), jnp.cos(ac), jnp.cos(ac)], axis=-1)
    sin = jnp.concatenate([-jnp.sin(ar), jnp.sin(ar), -jnp.sin(ac), jnp.sin(ac)], axis=-1)
    return cos, sin


_NT = (((1,), (1,)), ((), ()))


def _wa_kernel(sink_ref, q_ref, k_ref, v_ref, kc_ref, vc_ref, o_ref, *, tq, seq):
    g = pl.program_id(0)
    i = pl.program_id(1)
    nwin = tq + 2 * WA_WINDOW
    ws = jnp.clip(i * tq - WA_WINDOW, 0, seq - nwin)
    start = pl.multiple_of(ws, WA_WINDOW)
    kwin = k_ref[pl.ds(start, nwin), :]
    vwin = v_ref[pl.ds(start, nwin), :]
    qpos = i * tq + lax.broadcasted_iota(jnp.int32, (tq, nwin), 0)
    kpos = ws + lax.broadcasted_iota(jnp.int32, (tq, nwin), 1)
    valid = jnp.abs(qpos - kpos) <= WA_WINDOW
    group = WA_HEADS // WA_KV_HEADS
    for hh in range(group):
        q = q_ref[:, hh * HEAD_DIM:(hh + 1) * HEAD_DIM]
        s = jnp.where(valid, lax.dot_general(q, kwin, _NT, preferred_element_type=jnp.float32), NEG_INF)
        sc = lax.dot_general(q, kc_ref[...], _NT, preferred_element_type=jnp.float32)
        sk = sink_ref[g * group + hh]
        m = jnp.maximum(jnp.maximum(jnp.max(s, axis=-1, keepdims=True), jnp.max(sc, axis=-1, keepdims=True)), sk)
        p = jnp.exp(s - m)
        pc = jnp.exp(sc - m)
        denom = jnp.sum(p, axis=-1, keepdims=True) + jnp.sum(pc, axis=-1, keepdims=True) + jnp.exp(sk - m)
        o = (jnp.dot(p.astype(vwin.dtype), vwin, preferred_element_type=jnp.float32)
             + jnp.dot(pc.astype(vwin.dtype), vc_ref[...], preferred_element_type=jnp.float32))
        o_ref[:, hh * HEAD_DIM:(hh + 1) * HEAD_DIM] = (o / denom).astype(o_ref.dtype)


def windowed_attention(q, k, v, kc, vc, sink, tq=256):
    L = q.shape[0]
    Lc = kc.shape[0]
    gw = (WA_HEADS // WA_KV_HEADS) * HEAD_DIM
    slab = pl.BlockSpec((L, HEAD_DIM), lambda g, i, s: (0, g))
    cslab = pl.BlockSpec((Lc, HEAD_DIM), lambda g, i, s: (0, g))
    return pl.pallas_call(
        partial(_wa_kernel, tq=tq, seq=L),
        grid_spec=pltpu.PrefetchScalarGridSpec(
            num_scalar_prefetch=1,
            grid=(WA_KV_HEADS, L // tq),
            in_specs=[pl.BlockSpec((tq, gw), lambda g, i, s: (i, g)), slab, slab, cslab, cslab],
            out_specs=pl.BlockSpec((tq, gw), lambda g, i, s: (i, g)),
        ),
        out_shape=jax.ShapeDtypeStruct((L, WA_HEADS * HEAD_DIM), jnp.bfloat16),
        compiler_params=pltpu.CompilerParams(
            dimension_semantics=("parallel", "parallel"), vmem_limit_bytes=VMEM_LIMIT_BYTES),
        name="windowed_attention",
    )(sink.astype(jnp.float32), q, k, v, kc, vc)


NA_ROW_BLOCK = 4
NA_KEY_ROWS = NA_ROW_BLOCK + NA_WIN_ROWS - 1


def _na_kernel(q_ref, k_ref, v_ref, kc_ref, vc_ref, bias_ref, o_ref, *, rows):
    i = pl.program_id(1)
    ws = jnp.clip(i * NA_ROW_BLOCK - NA_WIN_ROWS // 2, 0, rows - NA_KEY_ROWS)
    start = pl.multiple_of(ws * GRID_W, GRID_W)
    nk = NA_KEY_ROWS * GRID_W
    kwin = k_ref[pl.ds(start, nk), :]
    vwin = v_ref[pl.ds(start, nk), :]
    q = q_ref[...]
    s = lax.dot_general(q, kwin, _NT, preferred_element_type=jnp.float32) + bias_ref[0, 0]
    sc = lax.dot_general(q, kc_ref[...], _NT, preferred_element_type=jnp.float32)
    m = jnp.maximum(jnp.max(s, axis=-1, keepdims=True), jnp.max(sc, axis=-1, keepdims=True))
    p = jnp.exp(s - m)
    pc = jnp.exp(sc - m)
    denom = jnp.sum(p, axis=-1, keepdims=True) + jnp.sum(pc, axis=-1, keepdims=True)
    o = (jnp.dot(p.astype(vwin.dtype), vwin, preferred_element_type=jnp.float32)
         + jnp.dot(pc.astype(vwin.dtype), vc_ref[...], preferred_element_type=jnp.float32))
    o_ref[...] = (o / denom).astype(o_ref.dtype)


def na_bias_tiles(rpb, rows):
    col = jnp.arange(GRID_W)
    cstart = jnp.clip(col - NA_WIN_COLS // 2, 0, GRID_W - NA_WIN_COLS)
    col_in = (col[None, :] >= cstart[:, None]) & (col[None, :] < cstart[:, None] + NA_WIN_COLS)
    dc_idx = jnp.clip(col[None, :] - col[:, None] + NA_WIN_COLS - 1, 0, 2 * NA_WIN_COLS - 2)
    exact = lax.Precision.HIGHEST
    by_col = jnp.einsum('hrd,qkd->hrqk', rpb.astype(jnp.float32),
                        jax.nn.one_hot(dc_idx, 2 * NA_WIN_COLS - 1, dtype=jnp.float32), precision=exact)
    tiles = []
    for blk in (0, 1, rows // NA_ROW_BLOCK - 1):
        r = blk * NA_ROW_BLOCK
        ws = min(max(r - NA_WIN_ROWS // 2, 0), rows - NA_KEY_ROWS)
        qr = r + jnp.arange(NA_ROW_BLOCK)
        kr = ws + jnp.arange(NA_KEY_ROWS)
        r0 = jnp.clip(qr - NA_WIN_ROWS // 2, 0, rows - NA_WIN_ROWS)
        row_in = (kr[None, :] >= r0[:, None]) & (kr[None, :] < r0[:, None] + NA_WIN_ROWS)
        dr_idx = jnp.clip(kr[None, :] - qr[:, None] + NA_WIN_ROWS - 1, 0, 2 * NA_WIN_ROWS - 2)
        b = jnp.einsum('hrqk,abr->haqbk', by_col,
                       jax.nn.one_hot(dr_idx, 2 * NA_WIN_ROWS - 1, dtype=jnp.float32), precision=exact)
        ok = row_in[:, None, :, None] & col_in[None, :, None, :]
        b = jnp.where(ok[None], b, NEG_INF)
        tiles.append(b.reshape(rpb.shape[0], NA_ROW_BLOCK * GRID_W, NA_KEY_ROWS * GRID_W))
    return jnp.stack(tiles)


def neighbourhood_attention(q, k, v, kc, vc, rpb):
    L = q.shape[0]
    Lc = kc.shape[0]
    rows = L // GRID_W
    nblk = rows // NA_ROW_BLOCK
    tq = NA_ROW_BLOCK * GRID_W
    nk = NA_KEY_ROWS * GRID_W
    bias = na_bias_tiles(rpb, rows)
    slab = pl.BlockSpec((L, HEAD_DIM), lambda h, i: (0, h))
    cslab = pl.BlockSpec((Lc, HEAD_DIM), lambda h, i: (0, h))
    variant = lambda h, i: (jnp.where(i == 0, 0, jnp.where(i == nblk - 1, 2, 1)), h, 0, 0)
    return pl.pallas_call(
        partial(_na_kernel, rows=rows),
        grid=(NA_HEADS, nblk),
        in_specs=[pl.BlockSpec((tq, HEAD_DIM), lambda h, i: (i, h)), slab, slab, cslab, cslab,
                  pl.BlockSpec((1, 1, tq, nk), variant)],
        out_specs=pl.BlockSpec((tq, HEAD_DIM), lambda h, i: (i, h)),
        out_shape=jax.ShapeDtypeStruct((L, NA_HEADS * HEAD_DIM), jnp.bfloat16),
        compiler_params=pltpu.CompilerParams(
            dimension_semantics=("parallel", "parallel"), vmem_limit_bytes=VMEM_LIMIT_BYTES),
        name="neighbourhood_attention",
    )(q, k, v, kc, vc, bias)


DFT_N1 = 128
DFT_K1 = DFT_N1 // 2 + 1
DFT_K1_PAD = 72


def dft_constants(L):
    N = 2 * L
    N2 = N // DFT_N1
    f32, bf = jnp.float32, jnp.bfloat16
    k1 = jnp.arange(DFT_K1, dtype=jnp.int32)
    n1 = jnp.arange(DFT_N1, dtype=jnp.int32)
    th = ((k1[:, None] * n1[None, :]) % DFT_N1).astype(f32) * (2.0 * math.pi / DFT_N1)
    pad = ((0, DFT_K1_PAD - DFT_K1), (0, 0))
    rows_fwd = jnp.concatenate([jnp.pad(jnp.cos(th), pad), jnp.pad(-jnp.sin(th), pad)], axis=0)
    ck = jnp.where((k1 == 0) | (k1 == DFT_N1 // 2), 1.0, 2.0)[:, None]
    half = DFT_N1 // 2
    rows_inv = jnp.concatenate([jnp.pad(ck * jnp.cos(th[:, :half]), pad),
                                jnp.pad(-ck * jnp.sin(th[:, :half]), pad)], axis=0).T
    n2 = jnp.arange(N2, dtype=jnp.int32)
    kk = k1[:, None] + DFT_N1 * n2[None, :]
    ang = ((kk[:, :, None] * n2[None, None, :]) % N).astype(f32) * (2.0 * math.pi / N)
    gr, gi = jnp.cos(ang), -jnp.sin(ang)
    mid_fwd = jnp.concatenate([jnp.concatenate([gr, -gi], axis=2),
                               jnp.concatenate([gi, gr], axis=2)], axis=1)
    return {
        "rows_fwd": rows_fwd.astype(bf), "rows_fwd_half": rows_fwd[:, :half].astype(bf),
        "rows_inv": rows_inv.astype(bf),
        "mid_fwd": mid_fwd.astype(bf), "mid_inv": jnp.swapaxes(mid_fwd, 1, 2).astype(bf),
    }


def _spectral_fwd_kernel(a_ref, g_ref, o_ref):
    n2 = a_ref.shape[2]
    x = (jnp.dot(g_ref[0, :, :n2], a_ref[0, 0], preferred_element_type=jnp.float32)
         + jnp.dot(g_ref[0, :, n2:], a_ref[1, 0], preferred_element_type=jnp.float32))
    o_ref[0, 0] = x[:n2]
    o_ref[1, 0] = x[n2:]


def spectral_fwd(a, mid_fwd, cb=512):
    _, _, n2, C = a.shape
    cb = min(cb, C)
    return pl.pallas_call(
        _spectral_fwd_kernel,
        grid=(DFT_K1, C // cb),
        in_specs=[pl.BlockSpec((2, 1, n2, cb), lambda k, c: (0, k, 0, c)),
                  pl.BlockSpec((1, 2 * n2, 2 * n2), lambda k, c: (k, 0, 0))],
        out_specs=pl.BlockSpec((2, 1, n2, cb), lambda k, c: (0, k, 0, c)),
        out_shape=jax.ShapeDtypeStruct((2, DFT_K1, n2, C), jnp.float32),
        compiler_params=pltpu.CompilerParams(
            dimension_semantics=("parallel", "parallel"), vmem_limit_bytes=VMEM_LIMIT_BYTES),
        name="spectral_fwd",
    )(a, mid_fwd)


def _spectral_mid_kernel(a_ref, gf_ref, gi_ref, ks_ref, o_ref):
    n2 = a_ref.shape[2]
    k1 = pl.program_id(0)

    @pl.when(k1 < DFT_K1)
    def _():
        x = (jnp.dot(gf_ref[0, :, :n2], a_ref[0, 0], preferred_element_type=jnp.float32)
             + jnp.dot(gf_ref[0, :, n2:], a_ref[1, 0], preferred_element_type=jnp.float32))
        xr, xi = x[:n2], x[n2:]
        kr, ki = ks_ref[0, 0], ks_ref[1, 0]
        yr = (xr * kr - xi * ki).astype(jnp.bfloat16)
        yi = (xr * ki + xi * kr).astype(jnp.bfloat16)
        b = (jnp.dot(gi_ref[0, :, :n2], yr, preferred_element_type=jnp.float32)
             + jnp.dot(gi_ref[0, :, n2:], yi, preferred_element_type=jnp.float32))
        o_ref[0, 0] = b[:n2].astype(o_ref.dtype)
        o_ref[1, 0] = b[n2:].astype(o_ref.dtype)

    @pl.when(k1 >= DFT_K1)
    def _():
        o_ref[...] = jnp.zeros_like(o_ref)


def spectral_mid(a, consts, kspec, col_off, cb=512):
    _, _, n2, C = a.shape
    cb = min(cb, C)
    off = col_off // cb
    kc = lambda k: jnp.minimum(k, DFT_K1 - 1)
    return pl.pallas_call(
        _spectral_mid_kernel,
        grid=(DFT_K1_PAD, C // cb),
        in_specs=[pl.BlockSpec((2, 1, n2, cb), lambda k, c: (0, k, 0, c)),
                  pl.BlockSpec((1, 2 * n2, 2 * n2), lambda k, c: (kc(k), 0, 0)),
                  pl.BlockSpec((1, 2 * n2, 2 * n2), lambda k, c: (kc(k), 0, 0)),
                  pl.BlockSpec((2, 1, n2, cb), lambda k, c: (0, kc(k), 0, off + c))],
        out_specs=pl.BlockSpec((2, 1, n2, cb), lambda k, c: (0, k, 0, c)),
        out_shape=jax.ShapeDtypeStruct(a.shape, jnp.bfloat16),
        compiler_params=pltpu.CompilerParams(
            dimension_semantics=("parallel", "parallel"), vmem_limit_bytes=VMEM_LIMIT_BYTES),
        name="spectral_mid",
    )(a, consts["mid_fwd"], consts["mid_inv"], kspec)


def _inverse_rows_kernel(f_ref, b_ref, u_ref, gate_ref, scale_ref, d_ref, o_ref):
    y = jnp.dot(f_ref[...], b_ref[...], preferred_element_type=jnp.float32)
    o_ref[...] = (gate_ref[...] * (y * scale_ref[...] + u_ref[...] * d_ref[...])).astype(o_ref.dtype)


def inverse_rows(rows_inv, b, u, gate, scale, d_skip, out_dtype):
    nr, cols = u.shape
    C = scale.shape[0]
    tc = min(cols, 8 * C)
    rep = tc // C
    vec = pl.BlockSpec((1, tc), lambda j: (0, 0))
    blk = pl.BlockSpec((nr, tc), lambda j: (0, j))
    return pl.pallas_call(
        _inverse_rows_kernel,
        grid=(cols // tc,),
        in_specs=[pl.BlockSpec(rows_inv.shape, lambda j: (0, 0)),
                  pl.BlockSpec((b.shape[0], tc), lambda j: (0, j)), blk, blk, vec, vec],
        out_specs=blk,
        out_shape=jax.ShapeDtypeStruct((nr, cols), out_dtype),
        compiler_params=pltpu.CompilerParams(
            dimension_semantics=("parallel",), vmem_limit_bytes=VMEM_LIMIT_BYTES),
        name="inverse_rows",
    )(rows_inv, b, u, gate, jnp.tile(scale, rep).reshape(1, tc), jnp.tile(d_skip, rep).reshape(1, tc))


def _conv3_kernel(z_ref, prev_ref, next_ref, w_ref, b_ref, o_ref):
    i = pl.program_id(0)
    z = z_ref[...]
    tm = z.shape[0]
    row = lax.broadcasted_iota(jnp.int32, z.shape, 0)
    prev_row = jnp.where(i > 0, prev_ref[7:8, :], 0.0)
    next_row = jnp.where(i < pl.num_programs(0) - 1, next_ref[0:1, :], 0.0)
    zp = jnp.where(row == 0, prev_row, pltpu.roll(z, 1, 0))
    zn = jnp.where(row == tm - 1, next_row, pltpu.roll(z, tm - 1, 0))
    o_ref[...] = zp * w_ref[0:1, :] + z * w_ref[1:2, :] + zn * w_ref[2:3, :] + b_ref[...]


def short_conv3_part(z, w, b, part, width):
    L = z.shape[0]
    tm = _pick(L, (512, 256, 128, 64, 32, 16, 8))
    tc = _pick(width, (1024, 512, 256, 128))
    off = part * width // tc
    halo = 8
    return pl.pallas_call(
        _conv3_kernel,
        grid=(L // tm, width // tc),
        in_specs=[pl.BlockSpec((tm, tc), lambda i, j: (i, off + j)),
                  pl.BlockSpec((halo, tc), lambda i, j: (jnp.maximum(i * (tm // halo) - 1, 0), off + j)),
                  pl.BlockSpec((halo, tc), lambda i, j: (jnp.minimum((i + 1) * (tm // halo), L // halo - 1), off + j)),
                  pl.BlockSpec((3, tc), lambda i, j: (0, off + j)),
                  pl.BlockSpec((1, tc), lambda i, j: (0, off + j))],
        out_specs=pl.BlockSpec((tm, tc), lambda i, j: (i, j)),
        out_shape=jax.ShapeDtypeStruct((L, width), jnp.float32),
        compiler_params=pltpu.CompilerParams(
            dimension_semantics=("parallel", "parallel"), vmem_limit_bytes=VMEM_LIMIT_BYTES),
        name="short_conv3",
    )(z, z, z, w, b.reshape(1, -1))


FILTER_PAD = 128


def _filter_kernel(feats_ref, feats0_ref, fw1_ref, fb1_ref, fw2_ref, fb2_ref, fr_ref, fw3_ref, fw3b_ref,
                   rate_ref, rateb_ref, kf_ref, asum_ref, *, zero_tile):
    i = pl.program_id(0)
    exact = lax.Precision.HIGHEST

    def mlp(feats, fw3, rate):
        h = jnp.sin(fr_ref[0:1, :] * (jnp.dot(feats, fw1_ref[...], precision=exact,
                                              preferred_element_type=jnp.float32) + fb1_ref[...]))
        h = jnp.sin(fr_ref[1:2, :] * (jnp.dot(h, fw2_ref[...], precision=exact,
                                              preferred_element_type=jnp.float32) + fb2_ref[...]))
        h = jnp.dot(h, fw3, precision=exact, preferred_element_type=jnp.float32)
        return h * (jnp.exp(-feats[:, 0:1] * rate) + HY_DECAY_SHIFT)

    k = mlp(feats_ref[...], fw3_ref[0], rate_ref[0])
    lag0_back = mlp(feats0_ref[...], fw3b_ref[0], rateb_ref[0])[0:1, :]
    row = lax.broadcasted_iota(jnp.int32, k.shape, 0)
    k = k + jnp.where((row == 0) & (i == 0), lag0_back, 0.0)
    k = jnp.where((row == 0) & (i == zero_tile), 0.0, k)
    kf_ref[...] = k.astype(kf_ref.dtype)

    @pl.when(i == 0)
    def _():
        asum_ref[...] = jnp.zeros_like(asum_ref)

    asum_ref[...] += jnp.sum(jnp.abs(k), axis=0, keepdims=True)


def hyena_filter(L, p):
    f32 = jnp.float32
    N = 2 * L
    C2 = HY_ORDER * HY_WIDTH
    n = jnp.arange(N, dtype=jnp.int32)
    pos = jnp.where(n < L, n, N - n).astype(f32)[:, None]
    bands = jnp.linspace(1e-4, HY_BANDS - 1, HY_BANDS, dtype=f32)[None, :]
    ang = (2.0 * math.pi / L) * bands * pos
    feats = jnp.concatenate([pos / (L - 1), jnp.cos(ang), -jnp.sin(ang)], axis=-1)
    feats = jnp.pad(feats, ((0, 0), (0, FILTER_PAD - feats.shape[1])))
    feats0 = jnp.broadcast_to(feats[0:1], (8, FILTER_PAD))
    hid = p['hy_fw2'].shape[0]
    ph = FILTER_PAD - hid
    fw1 = jnp.pad(p['hy_fw1'].astype(f32), ((0, FILTER_PAD - p['hy_fw1'].shape[0]), (0, ph)))
    fb1 = jnp.pad(p['hy_fb1'].astype(f32), (0, ph)).reshape(1, -1)
    fw2 = jnp.pad(p['hy_fw2'].astype(f32), ((0, ph), (0, ph)))
    fb2 = jnp.pad(p['hy_fb2'].astype(f32), (0, ph)).reshape(1, -1)
    fr = jnp.pad(p['hy_freq'].astype(f32), ((0, 0), (0, ph)))
    fw3 = jnp.pad(p['hy_fw3'].astype(f32), ((0, ph), (0, 0))).reshape(FILTER_PAD, HY_DIRS, C2).transpose(1, 0, 2)
    rate = jnp.abs(p['hy_decay'].astype(f32)).reshape(HY_DIRS, 1, C2)
    tr = _pick(L, (512, 256, 128, 64, 32, 16, 8))
    half_tiles = L // tr
    full = lambda shape: pl.BlockSpec(shape, lambda i: (0,) * len(shape))
    by_dir = lambda shape: pl.BlockSpec(shape, lambda i: (i // half_tiles, 0, 0))
    back = lambda shape: pl.BlockSpec(shape, lambda i: (1, 0, 0))
    return pl.pallas_call(
        partial(_filter_kernel, zero_tile=half_tiles),
        grid=(N // tr,),
        in_specs=[pl.BlockSpec((tr, FILTER_PAD), lambda i: (i, 0)), full((8, FILTER_PAD)),
                  full((FILTER_PAD, FILTER_PAD)), full((1, FILTER_PAD)),
                  full((FILTER_PAD, FILTER_PAD)), full((1, FILTER_PAD)), full((2, FILTER_PAD)),
                  by_dir((1, FILTER_PAD, C2)), back((1, FILTER_PAD, C2)),
                  by_dir((1, 1, C2)), back((1, 1, C2))],
        out_specs=[pl.BlockSpec((tr, C2), lambda i: (i, 0)), pl.BlockSpec((1, C2), lambda i: (0, 0))],
        out_shape=[jax.ShapeDtypeStruct((N, C2), jnp.bfloat16), jax.ShapeDtypeStruct((1, C2), f32)],
        compiler_params=pltpu.CompilerParams(
            dimension_semantics=("arbitrary",), vmem_limit_bytes=VMEM_LIMIT_BYTES),
        name="hyena_filter",
    )(feats, feats0, fw1, fb1, fw2, fb2, fr, fw3, fw3, rate, rate)


def hyena_latent(z, p, consts):
    L = z.shape[0]
    C = HY_WIDTH
    N = 2 * L
    n2 = N // DFT_N1
    half = DFT_N1 // 2
    v, x1, x2 = (short_conv3_part(z, p['hy_conv_w'], p['hy_conv_b'], i, C) for i in range(3))
    kf, asum = hyena_filter(L, p)
    col_tile = 8 * C
    ka = matmul(consts["rows_fwd"], kf.reshape(DFT_N1, n2 * HY_ORDER * C), jnp.bfloat16,
                tm=2 * DFT_K1_PAD, tn=min(col_tile, n2 * HY_ORDER * C))
    kspec = spectral_fwd(ka.reshape(2, DFT_K1_PAD, n2, HY_ORDER * C), consts["mid_fwd"])
    scale = 1.0 / (N * asum[0])

    def long_conv(u, gate, order, out_dtype):
        ua = matmul(consts["rows_fwd_half"], u.reshape(half, n2 * C), jnp.bfloat16,
                    tm=2 * DFT_K1_PAD, tn=min(col_tile, n2 * C))
        bm = spectral_mid(ua.reshape(2, DFT_K1_PAD, n2, C), consts, kspec, order * C)
        y = inverse_rows(consts["rows_inv"], bm.reshape(2 * DFT_K1_PAD, n2 * C), u.reshape(half, n2 * C),
                         gate.reshape(half, n2 * C), scale[order * C:(order + 1) * C],
                         p['hy_d'][order].astype(jnp.float32), out_dtype)
        return y.reshape(L, C)

    y1 = long_conv(v, x1, 0, jnp.float32)
    return long_conv(y1, x2, 1, jnp.bfloat16)


def rms_norm(x, g):
    xf = x.astype(jnp.float32)
    y = xf * lax.rsqrt(jnp.mean(xf * xf, axis=-1, keepdims=True) + RMS_EPS)
    return (y * g.astype(jnp.float32)).astype(x.dtype)


def modulate(h, shift, scale):
    return h * (1 + scale) + shift


def split_heads(z, n_heads):
    return z.reshape(z.shape[:-1] + (n_heads, HEAD_DIM))


def keys_values(z, n_heads, k_gain):
    k, v = jnp.split(z, 2, axis=-1)
    return rms_norm(split_heads(k, n_heads), k_gain), split_heads(v, n_heads)


def axial_rope_tables(L):
    t = jnp.arange(L, dtype=jnp.int32)
    row = (t // GRID_W).astype(jnp.float32)
    col = (t % GRID_W).astype(jnp.float32)
    nf = HEAD_DIM // 4
    inv = ROPE_BASE ** (-jnp.arange(nf, dtype=jnp.float32) / nf)
    ang = jnp.stack([row[:, None] * inv, col[:, None] * inv], axis=1)
    return jnp.cos(ang), jnp.sin(ang)


def apply_axial_rope(x, cos, sin):
    nf = HEAD_DIM // 4
    xr = x.reshape(x.shape[:-1] + (2, 2, nf))
    x1, x2 = xr[..., 0, :], xr[..., 1, :]
    c = cos[None, :, None].astype(x.dtype)
    s = sin[None, :, None].astype(x.dtype)
    out = jnp.stack([x1 * c - x2 * s, x2 * c + x1 * s], axis=-2)
    return out.reshape(x.shape)


def short_conv3(z, w, b):
    L = z.shape[1]
    zp = jnp.pad(z, ((0, 0), (1, 1), (0, 0)))
    return zp[:, :L] * w[0] + zp[:, 1:L + 1] * w[1] + zp[:, 2:] * w[2] + b


def hyena_filter_spectra(L, fw1, fb1, fw2, fb2, fw3, freq, decay):
    f32 = jnp.float32
    t = jnp.linspace(0.0, 1.0, L, dtype=f32)[:, None]
    pos = jnp.arange(L, dtype=f32)[:, None]
    bands = jnp.linspace(1e-4, HY_BANDS - 1, HY_BANDS, dtype=f32)[None, :]
    ang = (2.0 * math.pi / L) * bands * pos
    feats = jnp.concatenate([t, jnp.cos(ang), -jnp.sin(ang)], axis=-1)
    fr = freq.astype(f32)
    h = jnp.sin(fr[0] * (feats @ fw1.astype(f32) + fb1.astype(f32)))
    h = jnp.sin(fr[1] * (h @ fw2.astype(f32) + fb2.astype(f32)))
    h = (h @ fw3.astype(f32)).reshape(L, HY_DIRS, HY_ORDER, HY_WIDTH)
    rate = jnp.abs(decay.astype(f32)).reshape(HY_DIRS, HY_ORDER, HY_WIDTH)
    h = h * (jnp.exp(-t[:, :, None, None] * rate) + HY_DECAY_SHIFT)
    hf, hb = h[:, 0], h[:, 1]
    k = jnp.concatenate([hf[:1] + hb[:1], hf[1:], jnp.zeros_like(hf[:1]), hb[1:][::-1]], axis=0)
    k = k / jnp.sum(jnp.abs(k), axis=0, keepdims=True)
    return jnp.fft.rfft(k, axis=0)


def bidir_long_conv(u, k_spec, d_skip):
    L = u.shape[1]
    uf = jnp.fft.rfft(u.astype(jnp.float32), n=2 * L, axis=1)
    y = jnp.fft.irfft(uf * k_spec[None], n=2 * L, axis=1)[:, :L]
    return (y + u.astype(jnp.float32) * d_skip.astype(jnp.float32)).astype(u.dtype)


def hyena_branch(z, p):
    L = z.shape[1]
    z = short_conv3(z, p['hy_conv_w'], p['hy_conv_b'])
    v, x1, x2 = jnp.split(z, 3, axis=-1)
    k_spec = hyena_filter_spectra(L, p['hy_fw1'], p['hy_fb1'], p['hy_fw2'], p['hy_fb2'],
                                  p['hy_fw3'], p['hy_freq'], p['hy_decay'])
    y = x1 * bidir_long_conv(v, k_spec[:, 0], p['hy_d'][0])
    return x2 * bidir_long_conv(y, k_spec[:, 1], p['hy_d'][1])


def windowed_gqa_latent(q, k, v, kc, vc, sink):
    B, L, H, Dh = q.shape
    kvh = k.shape[2]
    g = H // kvh
    nb = L // WA_BLOCK
    qb = q.reshape(B, nb, WA_BLOCK, kvh, g, Dh)

    def band(t):
        tp = jnp.pad(t, ((0, 0), (WA_BLOCK, WA_BLOCK), (0, 0), (0, 0)))
        tp = tp.reshape(B, nb + 2, WA_BLOCK, kvh, Dh)
        return jnp.concatenate([tp[:, :-2], tp[:, 1:-1], tp[:, 2:]], axis=2)

    kb, vb = band(k), band(v)
    scale = HEAD_DIM ** -0.5
    blk = jnp.arange(nb)[:, None] * WA_BLOCK
    qpos = (blk + jnp.arange(WA_BLOCK)[None])[:, :, None]
    kpos = (blk - WA_BLOCK + jnp.arange(3 * WA_BLOCK)[None])[:, None, :]
    valid = (kpos >= 0) & (kpos < L) & (jnp.abs(qpos - kpos) <= WA_WINDOW)
    s_loc = jnp.einsum('bnqkgd,bnskd->bnkgqs', qb, kb).astype(jnp.float32) * scale
    s_loc = jnp.where(valid[None, :, None, None], s_loc, NEG_INF)
    s_ctx = jnp.einsum('bnqkgd,bckd->bnkgqc', qb, kc).astype(jnp.float32) * scale
    s_sink = jnp.broadcast_to(sink.astype(jnp.float32).reshape(1, 1, kvh, g, 1, 1), s_loc.shape[:-1] + (1,))
    p = jax.nn.softmax(jnp.concatenate([s_loc, s_ctx, s_sink], axis=-1), axis=-1).astype(v.dtype)
    nl = 3 * WA_BLOCK
    nc = kc.shape[1]
    o = (jnp.einsum('bnkgqs,bnskd->bnqkgd', p[..., :nl], vb)
         + jnp.einsum('bnkgqc,bckd->bnqkgd', p[..., nl:nl + nc], vc))
    return o.reshape(B, L, H * Dh)


def neighbourhood_attention_latent(q, k, v, kc, vc, rpb):
    B, L, H, Dh = q.shape
    rows = L // GRID_W
    wr = min(NA_WIN_ROWS, rows)
    qg = q.reshape(B, rows, GRID_W, H, Dh)
    kg = k.reshape(B, rows, GRID_W, H, Dh)
    vg = v.reshape(B, rows, GRID_W, H, Dh)
    col = jnp.arange(GRID_W)
    cstart = jnp.clip(col - NA_WIN_COLS // 2, 0, GRID_W - NA_WIN_COLS)
    col_in = (col[None, :] >= cstart[:, None]) & (col[None, :] < cstart[:, None] + NA_WIN_COLS)
    dc_idx = jnp.clip(col[None, :] - col[:, None] + NA_WIN_COLS - 1, 0, 2 * NA_WIN_COLS - 2)
    scale = HEAD_DIM ** -0.5
    n_loc = wr * GRID_W

    def row_block(r):
        r0 = jnp.clip(r - NA_WIN_ROWS // 2, 0, rows - wr)
        q_r = lax.dynamic_index_in_dim(qg, r, axis=1, keepdims=False)
        k_r = lax.dynamic_slice_in_dim(kg, r0, wr, axis=1)
        v_r = lax.dynamic_slice_in_dim(vg, r0, wr, axis=1).reshape(B, n_loc, H, Dh)
        dr_idx = r0 + jnp.arange(wr) - r + NA_WIN_ROWS - 1
        bias = rpb[:, dr_idx[None, :, None], dc_idx[:, None, :]].astype(jnp.float32)
        s_loc = jnp.einsum('bqhd,bwkhd->bhqwk', q_r, k_r).astype(jnp.float32) * scale + bias
        s_loc = jnp.where(col_in[:, None, :], s_loc, NEG_INF).reshape(B, H, GRID_W, n_loc)
        s_ctx = jnp.einsum('bqhd,bchd->bhqc', q_r, kc).astype(jnp.float32) * scale
        p = jax.nn.softmax(jnp.concatenate([s_loc, s_ctx], axis=-1), axis=-1).astype(v.dtype)
        return (jnp.einsum('bhqs,bshd->bqhd', p[..., :n_loc], v_r)
                + jnp.einsum('bhqc,bchd->bqhd', p[..., n_loc:], vc))

    out = lax.map(row_block, jnp.arange(rows))
    return jnp.moveaxis(out, 0, 1).reshape(B, L, H * Dh)


def context_attention(q, k, v, sink):
    B, Lc, H, Dh = q.shape
    kvh = k.shape[2]
    g = H // kvh
    qg = q.reshape(B, Lc, kvh, g, Dh)
    s = jnp.einsum('bqkgd,bckd->bkgqc', qg, k).astype(jnp.float32) * (HEAD_DIM ** -0.5)
    if sink is not None:
        s_sink = jnp.broadcast_to(sink.astype(jnp.float32).reshape(1, kvh, g, 1, 1), s.shape[:-1] + (1,))
        s = jnp.concatenate([s, s_sink], axis=-1)
    p = jax.nn.softmax(s, axis=-1)[..., :Lc].astype(v.dtype)
    return jnp.einsum('bkgqc,bckd->bqkgd', p, v).reshape(B, Lc, H * Dh)


def merge_branches(y_hy, y_wa, y_na, z_gate, w_branch, w_out):
    g = jax.nn.sigmoid(z_gate.astype(jnp.float32)).astype(z_gate.dtype)
    g_hy, g_wa, g_na = jnp.split(g, N_BRANCH, axis=-1)
    m = (g_hy * matmul(y_hy[0], w_branch[0])[None] + g_wa * matmul(y_wa[0], w_branch[1])[None]
         + g_na * matmul(y_na[0], w_branch[2])[None])
    return matmul(m[0], w_out)[None]


def _resident(shape, index_map):
    return pl.BlockSpec(shape, index_map, pipeline_mode=pl.Buffered(1))


def _merge_kernel(y0_ref, y1_ref, y2_ref, g_ref, wb_ref, o_ref):
    d = o_ref.shape[1]
    m = None
    for b, y_ref in enumerate((y0_ref, y1_ref, y2_ref)):
        t = g_ref[:, b * d:(b + 1) * d] * jnp.dot(y_ref[...], wb_ref[b], preferred_element_type=jnp.float32)
        m = t if m is None else m + t
    o_ref[...] = m.astype(o_ref.dtype)


def merge_gated(y_hy, y_wa, y_na, gates, w_branch):
    M, wbr = y_hy.shape
    D = w_branch.shape[2]
    tm = _pick(M, (256, 128, 64, 32, 16, 8))
    yspec = pl.BlockSpec((tm, wbr), lambda i: (i, 0))
    return pl.pallas_call(
        _merge_kernel,
        grid=(M // tm,),
        in_specs=[yspec, yspec, yspec, pl.BlockSpec((tm, N_BRANCH * D), lambda i: (i, 0)),
                  _resident((N_BRANCH, wbr, D), lambda i: (0, 0, 0))],
        out_specs=pl.BlockSpec((tm, D), lambda i: (i, 0)),
        out_shape=jax.ShapeDtypeStruct((M, D), jnp.bfloat16),
        compiler_params=pltpu.CompilerParams(
            dimension_semantics=("parallel",), vmem_limit_bytes=VMEM_LIMIT_BYTES),
        name="merge_gated",
    )(y_hy, y_wa, y_na, gates, w_branch)


ROUTER_PAD = 128


def _out_proj_kernel(m_ref, w_ref, x_ref, gate_ref, g2_ref, shift_ref, scale_ref, wr_ref, x_out, h_out, lg_out):
    x = x_ref[...] + gate_ref[...] * jnp.dot(m_ref[...], w_ref[...], preferred_element_type=jnp.float32)
    x_out[...] = x
    h = x * lax.rsqrt(jnp.mean(x * x, axis=-1, keepdims=True) + RMS_EPS) * g2_ref[...]
    h = h * (1.0 + scale_ref[...]) + shift_ref[...]
    h_out[...] = h.astype(h_out.dtype)
    lg_out[...] = jnp.dot(h, wr_ref[...], precision=lax.Precision.HIGHEST, preferred_element_type=jnp.float32)


def out_project(m, w_out, x, gate, norm_g, shift, scale, w_router):
    M, D = x.shape
    tm = _pick(M, (256, 128, 64, 32, 16, 8))
    row = pl.BlockSpec((tm, D), lambda i: (i, 0))
    vec = pl.BlockSpec((1, D), lambda i: (0, 0))
    wr = jnp.pad(w_router.astype(jnp.float32), ((0, 0), (0, ROUTER_PAD - w_router.shape[1])))
    v2 = lambda a: a.reshape(1, D).astype(jnp.float32)
    return pl.pallas_call(
        _out_proj_kernel,
        grid=(M // tm,),
        in_specs=[row, _resident((D, D), lambda i: (0, 0)), row, vec, vec, vec, vec,
                  _resident((D, ROUTER_PAD), lambda i: (0, 0))],
        out_specs=[row, row, pl.BlockSpec((tm, ROUTER_PAD), lambda i: (i, 0))],
        out_shape=[jax.ShapeDtypeStruct((M, D), jnp.float32), jax.ShapeDtypeStruct((M, D), jnp.bfloat16),
                   jax.ShapeDtypeStruct((M, ROUTER_PAD), jnp.float32)],
        compiler_params=pltpu.CompilerParams(
            dimension_semantics=("parallel",), vmem_limit_bytes=VMEM_LIMIT_BYTES),
        name="out_project",
    )(m, w_out, x, v2(gate), v2(norm_g), v2(shift), v2(scale), wr)


def _gate_up_kernel(x_ref, wg_ref, wu_ref, o_ref, wg_bf, wu_bf):
    @pl.when(pl.program_id(2) == 0)
    def _():
        wg_bf[...] = wg_ref[0].astype(jnp.bfloat16)
        wu_bf[...] = wu_ref[0].astype(jnp.bfloat16)

    x = x_ref[0]
    a = jnp.dot(x, wg_bf[...], preferred_element_type=jnp.float32)
    u = jnp.dot(x, wu_bf[...], preferred_element_type=jnp.float32)
    o_ref[0] = (a * jax.nn.sigmoid(a) * u).astype(o_ref.dtype)


def _down_kernel(h_ref, wd_ref, gsel_ref, gate_ref, o_ref, wd_bf):
    @pl.when(pl.program_id(1) == 0)
    def _():
        wd_bf[...] = wd_ref[0].astype(jnp.bfloat16)

    y = jnp.dot(h_ref[0], wd_bf[...], preferred_element_type=jnp.float32)
    o_ref[0] = y * gsel_ref[0] * gate_ref[...]


def expert_ffn(xe, w_gate, w_up, w_down, gsel, out_gate):
    E, cap, D = xe.shape
    F = w_gate.shape[2]
    tm = _pick(cap, (512, 256, 128, 64, 32, 16, 8))
    tn = _pick(F, (512, 256, 128))
    seq = pltpu.CompilerParams(dimension_semantics=("parallel", "parallel", "arbitrary"),
                               vmem_limit_bytes=VMEM_LIMIT_BYTES)
    h = pl.pallas_call(
        _gate_up_kernel,
        grid=(E, F // tn, cap // tm),
        in_specs=[pl.BlockSpec((1, tm, D), lambda e, j, i: (e, i, 0)),
                  pl.BlockSpec((1, D, tn), lambda e, j, i: (e, 0, j)),
                  pl.BlockSpec((1, D, tn), lambda e, j, i: (e, 0, j))],
        out_specs=pl.BlockSpec((1, tm, tn), lambda e, j, i: (e, i, j)),
        out_shape=jax.ShapeDtypeStruct((E, cap, F), jnp.bfloat16),
        scratch_shapes=[pltpu.VMEM((D, tn), jnp.bfloat16), pltpu.VMEM((D, tn), jnp.bfloat16)],
        compiler_params=seq,
        name="expert_gate_up",
    )(xe, w_gate, w_up)
    return pl.pallas_call(
        _down_kernel,
        grid=(E, cap // tm),
        in_specs=[pl.BlockSpec((1, tm, F), lambda e, i: (e, i, 0)),
                  pl.BlockSpec((1, F, D), lambda e, i: (e, 0, 0)),
                  pl.BlockSpec((1, tm, 1), lambda e, i: (e, i, 0)),
                  pl.BlockSpec((1, D), lambda e, i: (0, 0))],
        out_specs=pl.BlockSpec((1, tm, D), lambda e, i: (e, i, 0)),
        out_shape=jax.ShapeDtypeStruct((E, cap, D), jnp.float32),
        scratch_shapes=[pltpu.VMEM((F, D), jnp.bfloat16)],
        compiler_params=pltpu.CompilerParams(dimension_semantics=("parallel", "arbitrary"),
                                             vmem_limit_bytes=VMEM_LIMIT_BYTES),
        name="expert_down",
    )(h, w_down, gsel.reshape(E, cap, 1).astype(jnp.float32), out_gate.reshape(1, D).astype(jnp.float32))


def expert_choice_latent(x1, h2, logits, out_gate, w_gate, w_up, w_down):
    N, D = x1.shape
    cap = EC_CAPACITY_FACTOR * N // N_EXPERTS
    aff = jax.nn.softmax(logits[:, :N_EXPERTS], axis=-1)
    gsel, idx = lax.top_k(aff.T, cap)
    ye = expert_ffn(h2[idx], w_gate, w_up, w_down, gsel, out_gate)
    return x1.at[idx.reshape(-1)].add(ye.reshape(-1, D))


def expert_choice_ffn(h, w_router, w_gate, w_up, w_down):
    B, N, D = h.shape
    cap = EC_CAPACITY_FACTOR * N // N_EXPERTS
    hs = h[0]
    aff = jax.nn.softmax(jnp.dot(hs, w_router, precision=lax.Precision.HIGHEST).astype(jnp.float32), axis=-1)
    gsel, idx = lax.top_k(aff.T, cap)
    xe = hs[idx]
    a = batched_matmul(xe, w_gate)
    u = batched_matmul(xe, w_up)
    ye = batched_matmul(jax.nn.silu(a) * u, w_down) * gsel[..., None].astype(hs.dtype)
    return jnp.zeros_like(hs).at[idx.reshape(-1)].add(ye.reshape(-1, D))[None]


def trunk_layer(x, ctx, c, c_ctx, p, update_ctx, consts):
    L = x.shape[1]
    cond = jnp.pad(jnp.concatenate([jax.nn.silu(c), jax.nn.silu(c_ctx)[None]], axis=0), ((0, 6), (0, 0)))
    mod = matmul(cond, p['w_mod']) + p['b_mod']
    mx = jnp.split(mod[0:1, None, :], N_MOD, axis=-1)
    mc = jnp.split(mod[1], N_MOD, axis=-1)
    w_in = p['w_in']
    hc = modulate(rms_norm(ctx, p['norm1_g']), mc[0], mc[1])

    if update_ctx:
        zc = matmul(hc[0], w_in)[None]
        zc_wa_kv = zc[..., OFF_WA_KV:OFF_NA_Q]
        zc_na_kv = zc[..., OFF_NA_KV:OFF_GATE]
    else:
        zc_wa_kv = matmul(hc[0], w_in[:, OFF_WA_KV:OFF_NA_Q])[None]
        zc_na_kv = matmul(hc[0], w_in[:, OFF_NA_KV:OFF_GATE])[None]
    kc_wa, vc_wa = keys_values(zc_wa_kv, WA_KV_HEADS, p['wa_k_norm'])
    kc_na, vc_na = keys_values(zc_na_kv, NA_HEADS, p['na_k_norm'])

    bf = jnp.bfloat16
    hxb = norm_modulate(x[0], p['norm1_g'], mx[0][0, 0], mx[1][0, 0])
    wb = w_in.astype(bf)
    rope = rope_lane_tables(L)
    qk_scale = HEAD_DIM ** -0.5
    z_hy = project(hxb, wb[:, OFF_HY:OFF_WA_Q])
    y_hy = hyena_latent(z_hy, p, consts)[None]
    q_wa = project(hxb, wb[:, OFF_WA_Q:OFF_WA_KV], "headnorm", bf, p['wa_q_norm'], qk_scale, rope)
    nkv = WA_KV_HEADS * HEAD_DIM
    k_wa = project(hxb, wb[:, OFF_WA_KV:OFF_WA_KV + nkv], "headnorm", bf, p['wa_k_norm'], 1.0, rope)
    v_wa = project(hxb, wb[:, OFF_WA_KV + nkv:OFF_NA_Q], "plain", bf)
    y_wa = windowed_attention(q_wa, k_wa, v_wa, kc_wa[0].reshape(-1, nkv).astype(bf),
                              vc_wa[0].reshape(-1, nkv).astype(bf), p['wa_sink'])[None]
    q_na = project(hxb, wb[:, OFF_NA_Q:OFF_NA_KV], "headnorm", bf, p['na_q_norm'], qk_scale)
    nna = NA_HEADS * HEAD_DIM
    k_na = project(hxb, wb[:, OFF_NA_KV:OFF_NA_KV + nna], "headnorm", bf, p['na_k_norm'])
    v_na = project(hxb, wb[:, OFF_NA_KV + nna:OFF_GATE], "plain", bf)
    y_na = neighbourhood_attention(q_na, k_na, v_na, kc_na[0].reshape(-1, nna).astype(bf),
                                   vc_na[0].reshape(-1, nna).astype(bf), p['na_rpb'])[None]
    gates = project(hxb, wb[:, OFF_GATE:], "sigmoid")
    m = merge_gated(y_hy[0], y_wa[0], y_na[0], gates, p['w_branch'].astype(bf))
    x1, h2, logits = out_project(m, p['w_out'].astype(bf), x[0], mx[2][0, 0], p['norm2_g'],
                                 mx[3][0, 0], mx[4][0, 0], p['w_router'])
    x = expert_choice_latent(x1, h2, logits, mx[5][0, 0], p['w_gate'], p['w_up'], p['w_down'])[None]

    if update_ctx:
        yc_hy = hyena_branch(zc[..., OFF_HY:OFF_WA_Q], p)
        qc_wa = rms_norm(split_heads(zc[..., OFF_WA_Q:OFF_WA_KV], WA_HEADS), p['wa_q_norm'])
        yc_wa = context_attention(qc_wa, kc_wa, vc_wa, p['wa_sink'])
        qc_na = rms_norm(split_heads(zc[..., OFF_NA_Q:OFF_NA_KV], NA_HEADS), p['na_q_norm'])
        yc_na = context_attention(qc_na, kc_na, vc_na, None)
        ctx = ctx + mc[2] * merge_branches(yc_hy, yc_wa, yc_na, zc[..., OFF_GATE:], p['w_branch'], p['w_out'])
        hc2 = modulate(rms_norm(ctx, p['norm2_g']), mc[3], mc[4])
        ctx = ctx + mc[5] * expert_choice_ffn(hc2, p['w_router'], p['w_gate'], p['w_up'], p['w_down'])
    return x, ctx


def kernel(x, c, ctx, c_ctx, w_mod, b_mod, norm1_g, w_in, hy_conv_w, hy_conv_b, hy_fw1, hy_fb1, hy_fw2, hy_fb2, hy_fw3, hy_freq, hy_decay, hy_d, wa_q_norm, wa_k_norm, wa_sink, na_q_norm, na_k_norm, na_rpb, w_branch, w_out, norm2_g, w_router, w_gate, w_up, w_down):
    consts = dft_constants(x.shape[1])
    for l in range(DEPTH):
        p = {
            'w_mod': w_mod[l], 'b_mod': b_mod[l], 'norm1_g': norm1_g[l], 'w_in': w_in[l],
            'hy_conv_w': hy_conv_w[l], 'hy_conv_b': hy_conv_b[l], 'hy_fw1': hy_fw1[l], 'hy_fb1': hy_fb1[l],
            'hy_fw2': hy_fw2[l], 'hy_fb2': hy_fb2[l], 'hy_fw3': hy_fw3[l], 'hy_freq': hy_freq[l],
            'hy_decay': hy_decay[l], 'hy_d': hy_d[l], 'wa_q_norm': wa_q_norm[l], 'wa_k_norm': wa_k_norm[l],
            'wa_sink': wa_sink[l], 'na_q_norm': na_q_norm[l], 'na_k_norm': na_k_norm[l], 'na_rpb': na_rpb[l],
            'w_branch': w_branch[l], 'w_out': w_out[l], 'norm2_g': norm2_g[l], 'w_router': w_router[l],
            'w_gate': w_gate[l], 'w_up': w_up[l], 'w_down': w_down[l],
        }
        x, ctx = trunk_layer(x, ctx, c, c_ctx, p, l < DEPTH - 1, consts)
    return x
```

```python
import math
from functools import partial

import jax
import jax.numpy as jnp
from jax import lax
from jax.experimental import pallas as pl
from jax.experimental.pallas import tpu as pltpu

D_MODEL = 2048
SEQ = 16384
DEPTH = 2
CTX_LEN = 256
GRID_W = 64
HEAD_DIM = 128
BRANCH_WIDTH = 1024
N_BRANCH = 3
N_MOD = 6
RMS_EPS = 1e-6
NEG_INF = -1e30

HY_WIDTH = BRANCH_WIDTH
HY_ORDER = 2
HY_DIRS = 2
HY_BANDS = 16
HY_DECAY_SHIFT = 0.05

WA_HEADS = BRANCH_WIDTH // HEAD_DIM
WA_KV_HEADS = 2
WA_WINDOW = 128
WA_BLOCK = 128

NA_HEADS = BRANCH_WIDTH // HEAD_DIM
NA_WIN_ROWS = 8
NA_WIN_COLS = 16

ROPE_BASE = 10000.0

N_EXPERTS = 16
EC_CAPACITY_FACTOR = 2
D_EXPERT = 1024

OFF_HY = 0
OFF_WA_Q = OFF_HY + 3 * HY_WIDTH
OFF_WA_KV = OFF_WA_Q + WA_HEADS * HEAD_DIM
OFF_NA_Q = OFF_WA_KV + 2 * WA_KV_HEADS * HEAD_DIM
OFF_NA_KV = OFF_NA_Q + NA_HEADS * HEAD_DIM
OFF_GATE = OFF_NA_KV + 2 * NA_HEADS * HEAD_DIM
N_IN = OFF_GATE + N_BRANCH * D_MODEL

VMEM_LIMIT_BYTES = 56 * 1024 * 1024


def _mm_kernel(a_ref, b_ref, o_ref):
    o_ref[...] = jnp.dot(a_ref[...].astype(jnp.bfloat16), b_ref[...].astype(jnp.bfloat16),
                         preferred_element_type=jnp.float32).astype(o_ref.dtype)


def _pick(n, pref):
    for t in pref:
        if n % t == 0:
            return t
    return n


def matmul(a, b, out_dtype=jnp.float32, tm=None, tn=None):
    M, K = a.shape
    _, N = b.shape
    tm = tm or _pick(M, (512, 256, 128, 64, 32, 16, 8))
    tn = tn or _pick(N, (1024, 512, 256, 128))
    return pl.pallas_call(
        _mm_kernel,
        grid=(N // tn, M // tm),
        in_specs=[pl.BlockSpec((tm, K), lambda j, i: (i, 0)),
                  pl.BlockSpec((K, tn), lambda j, i: (0, j))],
        out_specs=pl.BlockSpec((tm, tn), lambda j, i: (i, j)),
        out_shape=jax.ShapeDtypeStruct((M, N), out_dtype),
        compiler_params=pltpu.CompilerParams(
            dimension_semantics=("parallel", "parallel"), vmem_limit_bytes=VMEM_LIMIT_BYTES),
        name="matmul",
    )(a, b)


def _bmm_kernel(a_ref, b_ref, o_ref):
    o_ref[0] = jnp.dot(a_ref[0].astype(jnp.bfloat16), b_ref[0].astype(jnp.bfloat16),
                       preferred_element_type=jnp.float32).astype(o_ref.dtype)


def batched_matmul(a, b, out_dtype=jnp.float32):
    E, M, K = a.shape
    _, _, N = b.shape
    tm = _pick(M, (512, 256, 128, 64, 32, 16, 8))
    tn = _pick(N, (1024, 512, 256, 128))
    return pl.pallas_call(
        _bmm_kernel,
        grid=(E, N // tn, M // tm),
        in_specs=[pl.BlockSpec((1, tm, K), lambda e, j, i: (e, i, 0)),
                  pl.BlockSpec((1, K, tn), lambda e, j, i: (e, 0, j))],
        out_specs=pl.BlockSpec((1, tm, tn), lambda e, j, i: (e, i, j)),
        out_shape=jax.ShapeDtypeStruct((E, M, N), out_dtype),
        compiler_params=pltpu.CompilerParams(
            dimension_semantics=("parallel", "parallel", "parallel"), vmem_limit_bytes=VMEM_LIMIT_BYTES),
        name="batched_matmul",
    )(a, b)


def _norm_mod_kernel(x_ref, g_ref, shift_ref, scale_ref, o_ref):
    x = x_ref[...]
    y = x * lax.rsqrt(jnp.mean(x * x, axis=-1, keepdims=True) + RMS_EPS) * g_ref[...]
    o_ref[...] = (y * (1.0 + scale_ref[...]) + shift_ref[...]).astype(o_ref.dtype)


def norm_modulate(x, g, shift, scale, out_dtype=jnp.bfloat16):
    M, D = x.shape
    tm = _pick(M, (512, 256, 128, 64, 32, 16, 8))
    vec = pl.BlockSpec((1, D), lambda i: (0, 0))
    return pl.pallas_call(
        _norm_mod_kernel,
        grid=(M // tm,),
        in_specs=[pl.BlockSpec((tm, D), lambda i: (i, 0)), vec, vec, vec],
        out_specs=pl.BlockSpec((tm, D), lambda i: (i, 0)),
        out_shape=jax.ShapeDtypeStruct((M, D), out_dtype),
        compiler_params=pltpu.CompilerParams(
            dimension_semantics=("parallel",), vmem_limit_bytes=VMEM_LIMIT_BYTES),
        name="norm_modulate",
    )(x, g.reshape(1, D), shift.reshape(1, D), scale.reshape(1, D))


def _swap_halves(x):
    lane = lax.broadcasted_iota(jnp.int32, x.shape, 1)
    return jnp.where((lane % 64) < 32, pltpu.roll(x, 96, 1), pltpu.roll(x, 32, 1))


def _proj_kernel(*refs, mode, post_scale, rope):
    if mode == "headnorm":
        if rope:
            a_ref, w_ref, gain_ref, cos_ref, sin_ref, o_ref, w_bf = refs
        else:
            a_ref, w_ref, gain_ref, o_ref, w_bf = refs
    else:
        a_ref, w_ref, o_ref, w_bf = refs

    @pl.when(pl.program_id(1) == 0)
    def _():
        w_bf[...] = w_ref[...].astype(jnp.bfloat16)

    acc = jnp.dot(a_ref[...], w_bf[...], preferred_element_type=jnp.float32)
    if mode == "plain":
        o_ref[...] = acc.astype(o_ref.dtype)
    elif mode == "sigmoid":
        o_ref[...] = jax.nn.sigmoid(acc).astype(o_ref.dtype)
    else:
        gain = gain_ref[...] * post_scale
        for h in range(acc.shape[1] // HEAD_DIM):
            xh = acc[:, h * HEAD_DIM:(h + 1) * HEAD_DIM]
            y = xh * lax.rsqrt(jnp.mean(xh * xh, axis=-1, keepdims=True) + RMS_EPS) * gain
            if rope:
                y = y * cos_ref[...] + _swap_halves(y) * sin_ref[...]
            o_ref[:, h * HEAD_DIM:(h + 1) * HEAD_DIM] = y.astype(o_ref.dtype)


def project(a, w, col_off, n_cols, mode="plain", out_dtype=jnp.float32, gain=None, post_scale=1.0, rope=None):
    M, K = a.shape
    tm = _pick(M, (1024, 512, 256, 128, 64, 32, 16, 8))
    tn = next(t for t in (1024, 768, 512, 256, 128) if n_cols % t == 0 and col_off % t == 0)
    off = col_off // tn
    in_specs = [pl.BlockSpec((tm, K), lambda j, i: (i, 0)),
                pl.BlockSpec((K, tn), lambda j, i: (0, off + j))]
    args = [a, w]
    if mode == "headnorm":
        in_specs.append(pl.BlockSpec((1, HEAD_DIM), lambda j, i: (0, 0)))
        args.append(gain.reshape(1, HEAD_DIM).astype(jnp.float32))
        if rope is not None:
            in_specs += [pl.BlockSpec((tm, HEAD_DIM), lambda j, i: (i, 0))] * 2
            args += list(rope)
    return pl.pallas_call(
        partial(_proj_kernel, mode=mode, post_scale=post_scale, rope=rope is not None),
        grid=(n_cols // tn, M // tm),
        in_specs=in_specs,
        out_specs=pl.BlockSpec((tm, tn), lambda j, i: (i, j)),
        out_shape=jax.ShapeDtypeStruct((M, n_cols), out_dtype),
        scratch_shapes=[pltpu.VMEM((K, tn), jnp.bfloat16)],
        compiler_params=pltpu.CompilerParams(
            dimension_semantics=("parallel", "arbitrary"), vmem_limit_bytes=VMEM_LIMIT_BYTES),
        name="project_" + mode,
    )(*args)


def rope_lane_tables(L):
    t = jnp.arange(L, dtype=jnp.int32)
    row = (t // GRID_W).astype(jnp.float32)
    col = (t % GRID_W).astype(jnp.float32)
    nf = HEAD_DIM // 4
    inv = ROPE_BASE ** (-jnp.arange(nf, dtype=jnp.float32) / nf)
    ar, ac = row[:, None] * inv, col[:, None] * inv
    cos = jnp.concatenate([jnp.cos(ar), jnp.cos(ar), jnp.cos(ac), jnp.cos(ac)], axis=-1)
    sin = jnp.concatenate([-jnp.sin(ar), jnp.sin(ar), -jnp.sin(ac), jnp.sin(ac)], axis=-1)
    return cos, sin


_NT = (((1,), (1,)), ((), ()))


def _wa_kernel(sink_ref, q_ref, k_ref, v_ref, kc_ref, vc_ref, o_ref, *, tq, seq):
    g = pl.program_id(0)
    i = pl.program_id(1)
    nwin = tq + 2 * WA_WINDOW
    ws = jnp.clip(i * tq - WA_WINDOW, 0, seq - nwin)
    start = pl.multiple_of(ws, WA_WINDOW)
    kwin = k_ref[pl.ds(start, nwin), :]
    vwin = v_ref[pl.ds(start, nwin), :]
    qpos = i * tq + lax.broadcasted_iota(jnp.int32, (tq, nwin), 0)
    kpos = ws + lax.broadcasted_iota(jnp.int32, (tq, nwin), 1)
    valid = jnp.abs(qpos - kpos) <= WA_WINDOW
    group = WA_HEADS // WA_KV_HEADS
    for hh in range(group):
        q = q_ref[:, hh * HEAD_DIM:(hh + 1) * HEAD_DIM]
        s = jnp.where(valid, lax.dot_general(q, kwin, _NT, preferred_element_type=jnp.float32), NEG_INF)
        sc = lax.dot_general(q, kc_ref[...], _NT, preferred_element_type=jnp.float32)
        sk = sink_ref[g * group + hh]
        m = jnp.maximum(jnp.maximum(jnp.max(s, axis=-1, keepdims=True), jnp.max(sc, axis=-1, keepdims=True)), sk)
        p = jnp.exp(s - m)
        pc = jnp.exp(sc - m)
        denom = jnp.sum(p, axis=-1, keepdims=True) + jnp.sum(pc, axis=-1, keepdims=True) + jnp.exp(sk - m)
        o = (jnp.dot(p.astype(vwin.dtype), vwin, preferred_element_type=jnp.float32)
             + jnp.dot(pc.astype(vwin.dtype), vc_ref[...], preferred_element_type=jnp.float32))
        o_ref[:, hh * HEAD_DIM:(hh + 1) * HEAD_DIM] = (o / denom).astype(o_ref.dtype)


def windowed_attention(q, k, v, kc, vc, sink, tq=256):
    L = q.shape[0]
    Lc = kc.shape[0]
    gw = (WA_HEADS // WA_KV_HEADS) * HEAD_DIM
    slab = pl.BlockSpec((L, HEAD_DIM), lambda g, i, s: (0, g))
    cslab = pl.BlockSpec((Lc, HEAD_DIM), lambda g, i, s: (0, g))
    return pl.pallas_call(
        partial(_wa_kernel, tq=tq, seq=L),
        grid_spec=pltpu.PrefetchScalarGridSpec(
            num_scalar_prefetch=1,
            grid=(WA_KV_HEADS, L // tq),
            in_specs=[pl.BlockSpec((tq, gw), lambda g, i, s: (i, g)), slab, slab, cslab, cslab],
            out_specs=pl.BlockSpec((tq, gw), lambda g, i, s: (i, g)),
        ),
        out_shape=jax.ShapeDtypeStruct((L, WA_HEADS * HEAD_DIM), jnp.bfloat16),
        compiler_params=pltpu.CompilerParams(
            dimension_semantics=("parallel", "parallel"), vmem_limit_bytes=VMEM_LIMIT_BYTES),
        name="windowed_attention",
    )(sink.astype(jnp.float32), q, k, v, kc, vc)


NA_ROW_BLOCK = 4
NA_KEY_ROWS = NA_ROW_BLOCK + NA_WIN_ROWS - 1


def _na_kernel(q_ref, k_ref, v_ref, kc_ref, vc_ref, bias_ref, o_ref, *, rows):
    i = pl.program_id(1)
    ws = jnp.clip(i * NA_ROW_BLOCK - NA_WIN_ROWS // 2, 0, rows - NA_KEY_ROWS)
    start = pl.multiple_of(ws * GRID_W, GRID_W)
    nk = NA_KEY_ROWS * GRID_W
    kwin = k_ref[pl.ds(start, nk), :]
    vwin = v_ref[pl.ds(start, nk), :]
    q = q_ref[...]
    s = lax.dot_general(q, kwin, _NT, preferred_element_type=jnp.float32) + bias_ref[0, 0]
    sc = lax.dot_general(q, kc_ref[...], _NT, preferred_element_type=jnp.float32)
    m = jnp.maximum(jnp.max(s, axis=-1, keepdims=True), jnp.max(sc, axis=-1, keepdims=True))
    p = jnp.exp(s - m)
    pc = jnp.exp(sc - m)
    denom = jnp.sum(p, axis=-1, keepdims=True) + jnp.sum(pc, axis=-1, keepdims=True)
    o = (jnp.dot(p.astype(vwin.dtype), vwin, preferred_element_type=jnp.float32)
         + jnp.dot(pc.astype(vwin.dtype), vc_ref[...], preferred_element_type=jnp.float32))
    o_ref[...] = (o / denom).astype(o_ref.dtype)


def na_bias_tiles(rpb, rows):
    col = jnp.arange(GRID_W)
    cstart = jnp.clip(col - NA_WIN_COLS // 2, 0, GRID_W - NA_WIN_COLS)
    col_in = (col[None, :] >= cstart[:, None]) & (col[None, :] < cstart[:, None] + NA_WIN_COLS)
    dc_idx = jnp.clip(col[None, :] - col[:, None] + NA_WIN_COLS - 1, 0, 2 * NA_WIN_COLS - 2)
    exact = lax.Precision.HIGHEST
    by_col = jnp.einsum('hrd,qkd->hrqk', rpb.astype(jnp.float32),
                        jax.nn.one_hot(dc_idx, 2 * NA_WIN_COLS - 1, dtype=jnp.float32), precision=exact)
    tiles = []
    for blk in (0, 1, rows // NA_ROW_BLOCK - 1):
        r = blk * NA_ROW_BLOCK
        ws = min(max(r - NA_WIN_ROWS // 2, 0), rows - NA_KEY_ROWS)
        qr = r + jnp.arange(NA_ROW_BLOCK)
        kr = ws + jnp.arange(NA_KEY_ROWS)
        r0 = jnp.clip(qr - NA_WIN_ROWS // 2, 0, rows - NA_WIN_ROWS)
        row_in = (kr[None, :] >= r0[:, None]) & (kr[None, :] < r0[:, None] + NA_WIN_ROWS)
        dr_idx = jnp.clip(kr[None, :] - qr[:, None] + NA_WIN_ROWS - 1, 0, 2 * NA_WIN_ROWS - 2)
        b = jnp.einsum('hrqk,abr->haqbk', by_col,
                       jax.nn.one_hot(dr_idx, 2 * NA_WIN_ROWS - 1, dtype=jnp.float32), precision=exact)
        ok = row_in[:, None, :, None] & col_in[None, :, None, :]
        b = jnp.where(ok[None], b, NEG_INF)
        tiles.append(b.reshape(rpb.shape[0], NA_ROW_BLOCK * GRID_W, NA_KEY_ROWS * GRID_W))
    return jnp.stack(tiles)


def neighbourhood_attention(q, k, v, kc, vc, rpb):
    L = q.shape[0]
    Lc = kc.shape[0]
    rows = L // GRID_W
    nblk = rows // NA_ROW_BLOCK
    tq = NA_ROW_BLOCK * GRID_W
    nk = NA_KEY_ROWS * GRID_W
    bias = na_bias_tiles(rpb, rows)
    slab = pl.BlockSpec((L, HEAD_DIM), lambda h, i: (0, h))
    cslab = pl.BlockSpec((Lc, HEAD_DIM), lambda h, i: (0, h))
    variant = lambda h, i: (jnp.where(i == 0, 0, jnp.where(i == nblk - 1, 2, 1)), h, 0, 0)
    return pl.pallas_call(
        partial(_na_kernel, rows=rows),
        grid=(NA_HEADS, nblk),
        in_specs=[pl.BlockSpec((tq, HEAD_DIM), lambda h, i: (i, h)), slab, slab, cslab, cslab,
                  pl.BlockSpec((1, 1, tq, nk), variant)],
        out_specs=pl.BlockSpec((tq, HEAD_DIM), lambda h, i: (i, h)),
        out_shape=jax.ShapeDtypeStruct((L, NA_HEADS * HEAD_DIM), jnp.bfloat16),
        compiler_params=pltpu.CompilerParams(
            dimension_semantics=("parallel", "parallel"), vmem_limit_bytes=VMEM_LIMIT_BYTES),
        name="neighbourhood_attention",
    )(q, k, v, kc, vc, bias)


DFT_N1 = 128
DFT_K1 = DFT_N1 // 2 + 1
DFT_K1_PAD = 72


def dft_constants(L):
    N = 2 * L
    N2 = N // DFT_N1
    f32, bf = jnp.float32, jnp.bfloat16
    k1 = jnp.arange(DFT_K1, dtype=jnp.int32)
    n1 = jnp.arange(DFT_N1, dtype=jnp.int32)
    th = ((k1[:, None] * n1[None, :]) % DFT_N1).astype(f32) * (2.0 * math.pi / DFT_N1)
    pad = ((0, DFT_K1_PAD - DFT_K1), (0, 0))
    rows_fwd = jnp.concatenate([jnp.pad(jnp.cos(th), pad), jnp.pad(-jnp.sin(th), pad)], axis=0)
    ck = jnp.where((k1 == 0) | (k1 == DFT_N1 // 2), 1.0, 2.0)[:, None]
    half = DFT_N1 // 2
    rows_inv = jnp.concatenate([jnp.pad(ck * jnp.cos(th[:, :half]), pad),
                                jnp.pad(-ck * jnp.sin(th[:, :half]), pad)], axis=0).T
    n2 = jnp.arange(N2, dtype=jnp.int32)
    kk = k1[:, None] + DFT_N1 * n2[None, :]
    ang = ((kk[:, :, None] * n2[None, None, :]) % N).astype(f32) * (2.0 * math.pi / N)
    gr, gi = jnp.cos(ang), -jnp.sin(ang)
    mid_fwd = jnp.concatenate([jnp.concatenate([gr, -gi], axis=2),
                               jnp.concatenate([gi, gr], axis=2)], axis=1)
    return {
        "rows_fwd": rows_fwd.astype(bf), "rows_fwd_half": rows_fwd[:, :half].astype(bf),
        "rows_inv": rows_inv.astype(bf),
        "mid_fwd": mid_fwd.astype(bf), "mid_inv": jnp.swapaxes(mid_fwd, 1, 2).astype(bf),
    }


def _spectral_fwd_kernel(a_ref, g_ref, o_ref):
    n2 = a_ref.shape[2]
    x = (jnp.dot(g_ref[0, :, :n2], a_ref[0, 0], preferred_element_type=jnp.float32)
         + jnp.dot(g_ref[0, :, n2:], a_ref[1, 0], preferred_element_type=jnp.float32))
    o_ref[0, 0] = x[:n2]
    o_ref[1, 0] = x[n2:]


def spectral_fwd(a, mid_fwd, cb=512):
    _, _, n2, C = a.shape
    cb = min(cb, C)
    return pl.pallas_call(
        _spectral_fwd_kernel,
        grid=(DFT_K1, C // cb),
        in_specs=[pl.BlockSpec((2, 1, n2, cb), lambda k, c: (0, k, 0, c)),
                  pl.BlockSpec((1, 2 * n2, 2 * n2), lambda k, c: (k, 0, 0))],
        out_specs=pl.BlockSpec((2, 1, n2, cb), lambda k, c: (0, k, 0, c)),
        out_shape=jax.ShapeDtypeStruct((2, DFT_K1, n2, C), jnp.float32),
        compiler_params=pltpu.CompilerParams(
            dimension_semantics=("parallel", "parallel"), vmem_limit_bytes=VMEM_LIMIT_BYTES),
        name="spectral_fwd",
    )(a, mid_fwd)


def _spectral_mid_kernel(a_ref, gf_ref, gi_ref, ks_ref, o_ref):
    n2 = a_ref.shape[2]
    k1 = pl.program_id(0)

    @pl.when(k1 < DFT_K1)
    def _():
        x = (jnp.dot(gf_ref[0, :, :n2], a_ref[0, 0], preferred_element_type=jnp.float32)
             + jnp.dot(gf_ref[0, :, n2:], a_ref[1, 0], preferred_element_type=jnp.float32))
        xr, xi = x[:n2], x[n2:]
        kr, ki = ks_ref[0, 0], ks_ref[1, 0]
        yr = (xr * kr - xi * ki).astype(jnp.bfloat16)
        yi = (xr * ki + xi * kr).astype(jnp.bfloat16)
        b = (jnp.dot(gi_ref[0, :, :n2], yr, preferred_element_type=jnp.float32)
             + jnp.dot(gi_ref[0, :, n2:], yi, preferred_element_type=jnp.float32))
        o_ref[0, 0] = b[:n2].astype(o_ref.dtype)
        o_ref[1, 0] = b[n2:].astype(o_ref.dtype)

    @pl.when(k1 >= DFT_K1)
    def _():
        o_ref[...] = jnp.zeros_like(o_ref)


def spectral_mid(a, consts, kspec, col_off, cb=512):
    _, _, n2, C = a.shape
    cb = min(cb, C)
    off = col_off // cb
    kc = lambda k: jnp.minimum(k, DFT_K1 - 1)
    return pl.pallas_call(
        _spectral_mid_kernel,
        grid=(DFT_K1_PAD, C // cb),
        in_specs=[pl.BlockSpec((2, 1, n2, cb), lambda k, c: (0, k, 0, c)),
                  pl.BlockSpec((1, 2 * n2, 2 * n2), lambda k, c: (kc(k), 0, 0)),
                  pl.BlockSpec((1, 2 * n2, 2 * n2), lambda k, c: (kc(k), 0, 0)),
                  pl.BlockSpec((2, 1, n2, cb), lambda k, c: (0, kc(k), 0, off + c))],
        out_specs=pl.BlockSpec((2, 1, n2, cb), lambda k, c: (0, k, 0, c)),
        out_shape=jax.ShapeDtypeStruct(a.shape, jnp.bfloat16),
        compiler_params=pltpu.CompilerParams(
            dimension_semantics=("parallel", "parallel"), vmem_limit_bytes=VMEM_LIMIT_BYTES),
        name="spectral_mid",
    )(a, consts["mid_fwd"], consts["mid_inv"], kspec)


DFT_ROW_GROUP = 16
DFT_ROW_LANES = 256


def _rows_fwd_kernel(f_ref, u_ref, o_ref):
    x = pltpu.einshape("abc->bac", u_ref[...].astype(jnp.float32))
    rs = [jnp.dot(f_ref[...], x[s].astype(jnp.bfloat16), preferred_element_type=jnp.float32)
          for s in range(x.shape[0])]
    o_ref[...] = pltpu.einshape("abc->bac", jnp.stack(rs)).astype(o_ref.dtype)


def dft_rows_fwd(rows_mat, u3):
    kn, n2, C = u3.shape
    grp = min(DFT_ROW_GROUP, n2)
    tc = min(DFT_ROW_LANES, C)
    m = rows_mat.shape[0]
    return pl.pallas_call(
        _rows_fwd_kernel,
        grid=(n2 // grp, C // tc),
        in_specs=[pl.BlockSpec((m, kn), lambda j, c: (0, 0)),
                  pl.BlockSpec((kn, grp, tc), lambda j, c: (0, j, c))],
        out_specs=pl.BlockSpec((m, grp, tc), lambda j, c: (0, j, c)),
        out_shape=jax.ShapeDtypeStruct((m, n2, C), jnp.bfloat16),
        compiler_params=pltpu.CompilerParams(
            dimension_semantics=("parallel", "parallel"), vmem_limit_bytes=VMEM_LIMIT_BYTES),
        name="dft_rows_fwd",
    )(rows_mat, u3)


def _rows_inv_kernel(f_ref, b_ref, u_ref, gate_ref, scale_ref, d_ref, o_ref):
    x = pltpu.einshape("abc->bac", b_ref[...].astype(jnp.float32))
    ys = [jnp.dot(f_ref[...], x[s].astype(jnp.bfloat16), preferred_element_type=jnp.float32)
          for s in range(x.shape[0])]
    y = pltpu.einshape("abc->bac", jnp.stack(ys))
    o_ref[...] = (gate_ref[...] * (y * scale_ref[...] + u_ref[...] * d_ref[...])).astype(o_ref.dtype)


def dft_rows_inv(rows_inv, b3, u3, gate3, scale, d_skip, out_dtype):
    nr, n2, C = u3.shape
    grp = min(DFT_ROW_GROUP, n2)
    tc = min(DFT_ROW_LANES, C)
    blk = pl.BlockSpec((nr, grp, tc), lambda j, c: (0, j, c))
    vec = pl.BlockSpec((1, 1, tc), lambda j, c: (0, 0, c))
    return pl.pallas_call(
        _rows_inv_kernel,
        grid=(n2 // grp, C // tc),
        in_specs=[pl.BlockSpec(rows_inv.shape, lambda j, c: (0, 0)),
                  pl.BlockSpec((b3.shape[0], grp, tc), lambda j, c: (0, j, c)), blk, blk, vec, vec],
        out_specs=blk,
        out_shape=jax.ShapeDtypeStruct((nr, n2, C), out_dtype),
        compiler_params=pltpu.CompilerParams(
            dimension_semantics=("parallel", "parallel"), vmem_limit_bytes=VMEM_LIMIT_BYTES),
        name="dft_rows_inv",
    )(rows_inv, b3, u3, gate3, scale.reshape(1, 1, C), d_skip.reshape(1, 1, C))


def _conv3_kernel(z_ref, prev_ref, next_ref, w_ref, b_ref, o_ref):
    i = pl.program_id(0)
    z = z_ref[...]
    tm = z.shape[0]
    row = lax.broadcasted_iota(jnp.int32, z.shape, 0)
    prev_row = jnp.where(i > 0, prev_ref[7:8, :], 0.0)
    next_row = jnp.where(i < pl.num_programs(0) - 1, next_ref[0:1, :], 0.0)
    zp = jnp.where(row == 0, prev_row, pltpu.roll(z, 1, 0))
    zn = jnp.where(row == tm - 1, next_row, pltpu.roll(z, tm - 1, 0))
    o_ref[...] = zp * w_ref[0:1, :] + z * w_ref[1:2, :] + zn * w_ref[2:3, :] + b_ref[...]


def short_conv3_part(z, w, b, part, width):
    L = z.shape[0]
    tm = _pick(L, (512, 256, 128, 64, 32, 16, 8))
    tc = _pick(width, (1024, 512, 256, 128))
    off = part * width // tc
    halo = 8
    return pl.pallas_call(
        _conv3_kernel,
        grid=(L // tm, width // tc),
        in_specs=[pl.BlockSpec((tm, tc), lambda i, j: (i, off + j)),
                  pl.BlockSpec((halo, tc), lambda i, j: (jnp.maximum(i * (tm // halo) - 1, 0), off + j)),
                  pl.BlockSpec((halo, tc), lambda i, j: (jnp.minimum((i + 1) * (tm // halo), L // halo - 1), off + j)),
                  pl.BlockSpec((3, tc), lambda i, j: (0, off + j)),
                  pl.BlockSpec((1, tc), lambda i, j: (0, off + j))],
        out_specs=pl.BlockSpec((tm, tc), lambda i, j: (i, j)),
        out_shape=jax.ShapeDtypeStruct((L, width), jnp.float32),
        compiler_params=pltpu.CompilerParams(
            dimension_semantics=("parallel", "parallel"), vmem_limit_bytes=VMEM_LIMIT_BYTES),
        name="short_conv3",
    )(z, z, z, w, b.reshape(1, -1))


FILTER_PAD = 128


def _filter_kernel(feats_ref, feats0_ref, fw1_ref, fb1_ref, fw2_ref, fb2_ref, fr_ref, fw3_ref, fw3b_ref,
                   rate_ref, rateb_ref, kf_ref, asum_ref, *, zero_tile):
    i = pl.program_id(0)
    exact = lax.Precision.HIGHEST

    def mlp(feats, fw3, rate):
        h = jnp.sin(fr_ref[0:1, :] * (jnp.dot(feats, fw1_ref[...], precision=exact,
                                              preferred_element_type=jnp.float32) + fb1_ref[...]))
        h = jnp.sin(fr_ref[1:2, :] * (jnp.dot(h, fw2_ref[...], precision=exact,
                                              preferred_element_type=jnp.float32) + fb2_ref[...]))
        h = jnp.dot(h.astype(jnp.bfloat16), fw3.astype(jnp.bfloat16), preferred_element_type=jnp.float32)
        return h * (jnp.exp(-feats[:, 0:1] * rate) + HY_DECAY_SHIFT)

    k = mlp(feats_ref[...], fw3_ref[0], rate_ref[0])
    lag0_back = mlp(feats0_ref[...], fw3b_ref[0], rateb_ref[0])[0:1, :]
    row = lax.broadcasted_iota(jnp.int32, k.shape, 0)
    k = k + jnp.where((row == 0) & (i == 0), lag0_back, 0.0)
    k = jnp.where((row == 0) & (i == zero_tile), 0.0, k)
    kf_ref[...] = k.astype(kf_ref.dtype)

    @pl.when(i == 0)
    def _():
        asum_ref[...] = jnp.zeros_like(asum_ref)

    asum_ref[...] += jnp.sum(jnp.abs(k), axis=0, keepdims=True)


def hyena_filter(L, p):
    f32 = jnp.float32
    N = 2 * L
    C2 = HY_ORDER * HY_WIDTH
    n = jnp.arange(N, dtype=jnp.int32)
    pos = jnp.where(n < L, n, N - n).astype(f32)[:, None]
    bands = jnp.linspace(1e-4, HY_BANDS - 1, HY_BANDS, dtype=f32)[None, :]
    ang = (2.0 * math.pi / L) * bands * pos
    feats = jnp.concatenate([pos / (L - 1), jnp.cos(ang), -jnp.sin(ang)], axis=-1)
    feats = jnp.pad(feats, ((0, 0), (0, FILTER_PAD - feats.shape[1])))
    feats0 = jnp.broadcast_to(feats[0:1], (8, FILTER_PAD))
    hid = p['hy_fw2'].shape[0]
    ph = FILTER_PAD - hid
    fw1 = jnp.pad(p['hy_fw1'].astype(f32), ((0, FILTER_PAD - p['hy_fw1'].shape[0]), (0, ph)))
    fb1 = jnp.pad(p['hy_fb1'].astype(f32), (0, ph)).reshape(1, -1)
    fw2 = jnp.pad(p['hy_fw2'].astype(f32), ((0, ph), (0, ph)))
    fb2 = jnp.pad(p['hy_fb2'].astype(f32), (0, ph)).reshape(1, -1)
    fr = jnp.pad(p['hy_freq'].astype(f32), ((0, 0), (0, ph)))
    fw3 = jnp.pad(p['hy_fw3'].astype(f32), ((0, ph), (0, 0))).reshape(FILTER_PAD, HY_DIRS, C2).transpose(1, 0, 2)
    rate = jnp.abs(p['hy_decay'].astype(f32)).reshape(HY_DIRS, 1, C2)
    tr = _pick(L, (512, 256, 128, 64, 32, 16, 8))
    half_tiles = L // tr
    full = lambda shape: pl.BlockSpec(shape, lambda i: (0,) * len(shape))
    by_dir = lambda shape: pl.BlockSpec(shape, lambda i: (i // half_tiles, 0, 0))
    back = lambda shape: pl.BlockSpec(shape, lambda i: (1, 0, 0))
    return pl.pallas_call(
        partial(_filter_kernel, zero_tile=half_tiles),
        grid=(N // tr,),
        in_specs=[pl.BlockSpec((tr, FILTER_PAD), lambda i: (i, 0)), full((8, FILTER_PAD)),
                  full((FILTER_PAD, FILTER_PAD)), full((1, FILTER_PAD)),
                  full((FILTER_PAD, FILTER_PAD)), full((1, FILTER_PAD)), full((2, FILTER_PAD)),
                  by_dir((1, FILTER_PAD, C2)), back((1, FILTER_PAD, C2)),
                  by_dir((1, 1, C2)), back((1, 1, C2))],
        out_specs=[pl.BlockSpec((tr, C2), lambda i: (i, 0)), pl.BlockSpec((1, C2), lambda i: (0, 0))],
        out_shape=[jax.ShapeDtypeStruct((N, C2), jnp.bfloat16), jax.ShapeDtypeStruct((1, C2), f32)],
        compiler_params=pltpu.CompilerParams(
            dimension_semantics=("arbitrary",), vmem_limit_bytes=VMEM_LIMIT_BYTES),
        name="hyena_filter",
    )(feats, feats0, fw1, fb1, fw2, fb2, fr, fw3, fw3, rate, rate)


def hyena_latent(z, p, consts):
    L = z.shape[0]
    C = HY_WIDTH
    N = 2 * L
    n2 = N // DFT_N1
    half = DFT_N1 // 2
    v, x1, x2 = (short_conv3_part(z, p['hy_conv_w'], p['hy_conv_b'], i, C) for i in range(3))
    kf, asum = hyena_filter(L, p)
    ka = dft_rows_fwd(consts["rows_fwd"], kf.reshape(DFT_N1, n2, HY_ORDER * C))
    kspec = spectral_fwd(ka.reshape(2, DFT_K1_PAD, n2, HY_ORDER * C), consts["mid_fwd"])
    scale = 1.0 / (N * asum[0])

    def long_conv(u, gate, order, out_dtype):
        u3 = u.reshape(half, n2, C)
        ua = dft_rows_fwd(consts["rows_fwd_half"], u3)
        bm = spectral_mid(ua.reshape(2, DFT_K1_PAD, n2, C), consts, kspec, order * C)
        y = dft_rows_inv(consts["rows_inv"], bm.reshape(2 * DFT_K1_PAD, n2, C), u3, gate.reshape(half, n2, C),
                         scale[order * C:(order + 1) * C], p['hy_d'][order].astype(jnp.float32), out_dtype)
        return y.reshape(L, C)

    y1 = long_conv(v, x1, 0, jnp.float32)
    return long_conv(y1, x2, 1, jnp.bfloat16)


def rms_norm(x, g):
    xf = x.astype(jnp.float32)
    y = xf * lax.rsqrt(jnp.mean(xf * xf, axis=-1, keepdims=True) + RMS_EPS)
    return (y * g.astype(jnp.float32)).astype(x.dtype)


def modulate(h, shift, scale):
    return h * (1 + scale) + shift


def split_heads(z, n_heads):
    return z.reshape(z.shape[:-1] + (n_heads, HEAD_DIM))


def keys_values(z, n_heads, k_gain):
    k, v = jnp.split(z, 2, axis=-1)
    return rms_norm(split_heads(k, n_heads), k_gain), split_heads(v, n_heads)


def axial_rope_tables(L):
    t = jnp.arange(L, dtype=jnp.int32)
    row = (t // GRID_W).astype(jnp.float32)
    col = (t % GRID_W).astype(jnp.float32)
    nf = HEAD_DIM // 4
    inv = ROPE_BASE ** (-jnp.arange(nf, dtype=jnp.float32) / nf)
    ang = jnp.stack([row[:, None] * inv, col[:, None] * inv], axis=1)
    return jnp.cos(ang), jnp.sin(ang)


def apply_axial_rope(x, cos, sin):
    nf = HEAD_DIM // 4
    xr = x.reshape(x.shape[:-1] + (2, 2, nf))
    x1, x2 = xr[..., 0, :], xr[..., 1, :]
    c = cos[None, :, None].astype(x.dtype)
    s = sin[None, :, None].astype(x.dtype)
    out = jnp.stack([x1 * c - x2 * s, x2 * c + x1 * s], axis=-2)
    return out.reshape(x.shape)


def short_conv3(z, w, b):
    L = z.shape[1]
    zp = jnp.pad(z, ((0, 0), (1, 1), (0, 0)))
    return zp[:, :L] * w[0] + zp[:, 1:L + 1] * w[1] + zp[:, 2:] * w[2] + b


def hyena_filter_spectra(L, fw1, fb1, fw2, fb2, fw3, freq, decay):
    f32 = jnp.float32
    t = jnp.linspace(0.0, 1.0, L, dtype=f32)[:, None]
    pos = jnp.arange(L, dtype=f32)[:, None]
    bands = jnp.linspace(1e-4, HY_BANDS - 1, HY_BANDS, dtype=f32)[None, :]
    ang = (2.0 * math.pi / L) * bands * pos
    feats = jnp.concatenate([t, jnp.cos(ang), -jnp.sin(ang)], axis=-1)
    fr = freq.astype(f32)
    h = jnp.sin(fr[0] * (feats @ fw1.astype(f32) + fb1.astype(f32)))
    h = jnp.sin(fr[1] * (h @ fw2.astype(f32) + fb2.astype(f32)))
    h = (h @ fw3.astype(f32)).reshape(L, HY_DIRS, HY_ORDER, HY_WIDTH)
    rate = jnp.abs(decay.astype(f32)).reshape(HY_DIRS, HY_ORDER, HY_WIDTH)
    h = h * (jnp.exp(-t[:, :, None, None] * rate) + HY_DECAY_SHIFT)
    hf, hb = h[:, 0], h[:, 1]
    k = jnp.concatenate([hf[:1] + hb[:1], hf[1:], jnp.zeros_like(hf[:1]), hb[1:][::-1]], axis=0)
    k = k / jnp.sum(jnp.abs(k), axis=0, keepdims=True)
    return jnp.fft.rfft(k, axis=0)


def bidir_long_conv(u, k_spec, d_skip):
    L = u.shape[1]
    uf = jnp.fft.rfft(u.astype(jnp.float32), n=2 * L, axis=1)
    y = jnp.fft.irfft(uf * k_spec[None], n=2 * L, axis=1)[:, :L]
    return (y + u.astype(jnp.float32) * d_skip.astype(jnp.float32)).astype(u.dtype)


def hyena_branch(z, p):
    L = z.shape[1]
    z = short_conv3(z, p['hy_conv_w'], p['hy_conv_b'])
    v, x1, x2 = jnp.split(z, 3, axis=-1)
    k_spec = hyena_filter_spectra(L, p['hy_fw1'], p['hy_fb1'], p['hy_fw2'], p['hy_fb2'],
                                  p['hy_fw3'], p['hy_freq'], p['hy_decay'])
    y = x1 * bidir_long_conv(v, k_spec[:, 0], p['hy_d'][0])
    return x2 * bidir_long_conv(y, k_spec[:, 1], p['hy_d'][1])


def windowed_gqa_latent(q, k, v, kc, vc, sink):
    B, L, H, Dh = q.shape
    kvh = k.shape[2]
    g = H // kvh
    nb = L // WA_BLOCK
    qb = q.reshape(B, nb, WA_BLOCK, kvh, g, Dh)

    def band(t):
        tp = jnp.pad(t, ((0, 0), (WA_BLOCK, WA_BLOCK), (0, 0), (0, 0)))
        tp = tp.reshape(B, nb + 2, WA_BLOCK, kvh, Dh)
        return jnp.concatenate([tp[:, :-2], tp[:, 1:-1], tp[:, 2:]], axis=2)

    kb, vb = band(k), band(v)
    scale = HEAD_DIM ** -0.5
    blk = jnp.arange(nb)[:, None] * WA_BLOCK
    qpos = (blk + jnp.arange(WA_BLOCK)[None])[:, :, None]
    kpos = (blk - WA_BLOCK + jnp.arange(3 * WA_BLOCK)[None])[:, None, :]
    valid = (kpos >= 0) & (kpos < L) & (jnp.abs(qpos - kpos) <= WA_WINDOW)
    s_loc = jnp.einsum('bnqkgd,bnskd->bnkgqs', qb, kb).astype(jnp.float32) * scale
    s_loc = jnp.where(valid[None, :, None, None], s_loc, NEG_INF)
    s_ctx = jnp.einsum('bnqkgd,bckd->bnkgqc', qb, kc).astype(jnp.float32) * scale
    s_sink = jnp.broadcast_to(sink.astype(jnp.float32).reshape(1, 1, kvh, g, 1, 1), s_loc.shape[:-1] + (1,))
    p = jax.nn.softmax(jnp.concatenate([s_loc, s_ctx, s_sink], axis=-1), axis=-1).astype(v.dtype)
    nl = 3 * WA_BLOCK
    nc = kc.shape[1]
    o = (jnp.einsum('bnkgqs,bnskd->bnqkgd', p[..., :nl], vb)
         + jnp.einsum('bnkgqc,bckd->bnqkgd', p[..., nl:nl + nc], vc))
    return o.reshape(B, L, H * Dh)


def neighbourhood_attention_latent(q, k, v, kc, vc, rpb):
    B, L, H, Dh = q.shape
    rows = L // GRID_W
    wr = min(NA_WIN_ROWS, rows)
    qg = q.reshape(B, rows, GRID_W, H, Dh)
    kg = k.reshape(B, rows, GRID_W, H, Dh)
    vg = v.reshape(B, rows, GRID_W, H, Dh)
    col = jnp.arange(GRID_W)
    cstart = jnp.clip(col - NA_WIN_COLS // 2, 0, GRID_W - NA_WIN_COLS)
    col_in = (col[None, :] >= cstart[:, None]) & (col[None, :] < cstart[:, None] + NA_WIN_COLS)
    dc_idx = jnp.clip(col[None, :] - col[:, None] + NA_WIN_COLS - 1, 0, 2 * NA_WIN_COLS - 2)
    scale = HEAD_DIM ** -0.5
    n_loc = wr * GRID_W

    def row_block(r):
        r0 = jnp.clip(r - NA_WIN_ROWS // 2, 0, rows - wr)
        q_r = lax.dynamic_index_in_dim(qg, r, axis=1, keepdims=False)
        k_r = lax.dynamic_slice_in_dim(kg, r0, wr, axis=1)
        v_r = lax.dynamic_slice_in_dim(vg, r0, wr, axis=1).reshape(B, n_loc, H, Dh)
        dr_idx = r0 + jnp.arange(wr) - r + NA_WIN_ROWS - 1
        bias = rpb[:, dr_idx[None, :, None], dc_idx[:, None, :]].astype(jnp.float32)
        s_loc = jnp.einsum('bqhd,bwkhd->bhqwk', q_r, k_r).astype(jnp.float32) * scale + bias
        s_loc = jnp.where(col_in[:, None, :], s_loc, NEG_INF).reshape(B, H, GRID_W, n_loc)
        s_ctx = jnp.einsum('bqhd,bchd->bhqc', q_r, kc).astype(jnp.float32) * scale
        p = jax.nn.softmax(jnp.concatenate([s_loc, s_ctx], axis=-1), axis=-1).astype(v.dtype)
        return (jnp.einsum('bhqs,bshd->bqhd', p[..., :n_loc], v_r)
                + jnp.einsum('bhqc,bchd->bqhd', p[..., n_loc:], vc))

    out = lax.map(row_block, jnp.arange(rows))
    return jnp.moveaxis(out, 0, 1).reshape(B, L, H * Dh)


def context_attention(q, k, v, sink):
    B, Lc, H, Dh = q.shape
    kvh = k.shape[2]
    g = H // kvh
    qg = q.reshape(B, Lc, kvh, g, Dh)
    s = jnp.einsum('bqkgd,bckd->bkgqc', qg, k).astype(jnp.float32) * (HEAD_DIM ** -0.5)
    if sink is not None:
        s_sink = jnp.broadcast_to(sink.astype(jnp.float32).reshape(1, kvh, g, 1, 1), s.shape[:-1] + (1,))
        s = jnp.concatenate([s, s_sink], axis=-1)
    p = jax.nn.softmax(s, axis=-1)[..., :Lc].astype(v.dtype)
    return jnp.einsum('bkgqc,bckd->bqkgd', p, v).reshape(B, Lc, H * Dh)


def merge_branches(y_hy, y_wa, y_na, z_gate, w_branch, w_out):
    g = jax.nn.sigmoid(z_gate.astype(jnp.float32)).astype(z_gate.dtype)
    g_hy, g_wa, g_na = jnp.split(g, N_BRANCH, axis=-1)
    m = (g_hy * matmul(y_hy[0], w_branch[0])[None] + g_wa * matmul(y_wa[0], w_branch[1])[None]
         + g_na * matmul(y_na[0], w_branch[2])[None])
    return matmul(m[0], w_out)[None]


def _resident(shape, index_map):
    return pl.BlockSpec(shape, index_map, pipeline_mode=pl.Buffered(1))


def _merge_kernel(y0_ref, y1_ref, y2_ref, g_ref, wb_ref, o_ref):
    d = o_ref.shape[1]
    m = None
    for b, y_ref in enumerate((y0_ref, y1_ref, y2_ref)):
        t = g_ref[:, b * d:(b + 1) * d] * jnp.dot(y_ref[...], wb_ref[b], preferred_element_type=jnp.float32)
        m = t if m is None else m + t
    o_ref[...] = m.astype(o_ref.dtype)


def merge_gated(y_hy, y_wa, y_na, gates, w_branch):
    M, wbr = y_hy.shape
    D = w_branch.shape[2]
    tm = _pick(M, (256, 128, 64, 32, 16, 8))
    yspec = pl.BlockSpec((tm, wbr), lambda i: (i, 0))
    return pl.pallas_call(
        _merge_kernel,
        grid=(M // tm,),
        in_specs=[yspec, yspec, yspec, pl.BlockSpec((tm, N_BRANCH * D), lambda i: (i, 0)),
                  _resident((N_BRANCH, wbr, D), lambda i: (0, 0, 0))],
        out_specs=pl.BlockSpec((tm, D), lambda i: (i, 0)),
        out_shape=jax.ShapeDtypeStruct((M, D), jnp.bfloat16),
        compiler_params=pltpu.CompilerParams(
            dimension_semantics=("parallel",), vmem_limit_bytes=VMEM_LIMIT_BYTES),
        name="merge_gated",
    )(y_hy, y_wa, y_na, gates, w_branch)


ROUTER_PAD = 128


def _out_proj_kernel(m_ref, w_ref, x_ref, gate_ref, g2_ref, shift_ref, scale_ref, wr_hi_ref, wr_lo_ref,
                     x_out, h_out, lg_out):
    x = x_ref[...] + gate_ref[...] * jnp.dot(m_ref[...], w_ref[...], preferred_element_type=jnp.float32)
    x_out[...] = x
    h = x * lax.rsqrt(jnp.mean(x * x, axis=-1, keepdims=True) + RMS_EPS) * g2_ref[...]
    h = h * (1.0 + scale_ref[...]) + shift_ref[...]
    h_out[...] = h.astype(h_out.dtype)
    h_hi = h.astype(jnp.bfloat16)
    h_lo = (h - h_hi.astype(jnp.float32)).astype(jnp.bfloat16)
    lg_out[...] = (jnp.dot(h_hi, wr_hi_ref[...], preferred_element_type=jnp.float32)
                   + jnp.dot(h_lo, wr_hi_ref[...], preferred_element_type=jnp.float32)
                   + jnp.dot(h_hi, wr_lo_ref[...], preferred_element_type=jnp.float32))


def out_project(m, w_out, x, gate, norm_g, shift, scale, w_router):
    M, D = x.shape
    tm = _pick(M, (256, 128, 64, 32, 16, 8))
    row = pl.BlockSpec((tm, D), lambda i: (i, 0))
    vec = pl.BlockSpec((1, D), lambda i: (0, 0))
    wr = jnp.pad(w_router.astype(jnp.float32), ((0, 0), (0, ROUTER_PAD - w_router.shape[1])))
    wr_hi = wr.astype(jnp.bfloat16)
    wr_lo = (wr - wr_hi.astype(jnp.float32)).astype(jnp.bfloat16)
    v2 = lambda a: a.reshape(1, D).astype(jnp.float32)
    return pl.pallas_call(
        _out_proj_kernel,
        grid=(M // tm,),
        in_specs=[row, _resident((D, D), lambda i: (0, 0)), row, vec, vec, vec, vec,
                  _resident((D, ROUTER_PAD), lambda i: (0, 0)), _resident((D, ROUTER_PAD), lambda i: (0, 0))],
        out_specs=[row, row, pl.BlockSpec((tm, ROUTER_PAD), lambda i: (i, 0))],
        out_shape=[jax.ShapeDtypeStruct((M, D), jnp.float32), jax.ShapeDtypeStruct((M, D), jnp.bfloat16),
                   jax.ShapeDtypeStruct((M, ROUTER_PAD), jnp.float32)],
        compiler_params=pltpu.CompilerParams(
            dimension_semantics=("parallel",), vmem_limit_bytes=VMEM_LIMIT_BYTES),
        name="out_project",
    )(m, w_out, x, v2(gate), v2(norm_g), v2(shift), v2(scale), wr_hi, wr_lo)


def _gate_up_kernel(x_ref, wg_ref, wu_ref, o_ref, wg_bf, wu_bf):
    @pl.when(pl.program_id(2) == 0)
    def _():
        wg_bf[...] = wg_ref[0].astype(jnp.bfloat16)
        wu_bf[...] = wu_ref[0].astype(jnp.bfloat16)

    x = x_ref[0]
    a = jnp.dot(x, wg_bf[...], preferred_element_type=jnp.float32)
    u = jnp.dot(x, wu_bf[...], preferred_element_type=jnp.float32)
    o_ref[0] = (a * jax.nn.sigmoid(a) * u).astype(o_ref.dtype)


def _down_kernel(h_ref, wd_ref, gsel_ref, gate_ref, o_ref, wd_bf):
    @pl.when(pl.program_id(1) == 0)
    def _():
        wd_bf[...] = wd_ref[0].astype(jnp.bfloat16)

    y = jnp.dot(h_ref[0], wd_bf[...], preferred_element_type=jnp.float32)
    o_ref[0] = y * gsel_ref[0] * gate_ref[...]


def expert_ffn(xe, w_gate, w_up, w_down, gsel, out_gate):
    E, cap, D = xe.shape
    F = w_gate.shape[2]
    tm = _pick(cap, (512, 256, 128, 64, 32, 16, 8))
    tn = _pick(F, (512, 256, 128))
    seq = pltpu.CompilerParams(dimension_semantics=("parallel", "parallel", "arbitrary"),
                               vmem_limit_bytes=VMEM_LIMIT_BYTES)
    h = pl.pallas_call(
        _gate_up_kernel,
        grid=(E, F // tn, cap // tm),
        in_specs=[pl.BlockSpec((1, tm, D), lambda e, j, i: (e, i, 0)),
                  pl.BlockSpec((1, D, tn), lambda e, j, i: (e, 0, j)),
                  pl.BlockSpec((1, D, tn), lambda e, j, i: (e, 0, j))],
        out_specs=pl.BlockSpec((1, tm, tn), lambda e, j, i: (e, i, j)),
        out_shape=jax.ShapeDtypeStruct((E, cap, F), jnp.bfloat16),
        scratch_shapes=[pltpu.VMEM((D, tn), jnp.bfloat16), pltpu.VMEM((D, tn), jnp.bfloat16)],
        compiler_params=seq,
        name="expert_gate_up",
    )(xe, w_gate, w_up)
    return pl.pallas_call(
        _down_kernel,
        grid=(E, cap // tm),
        in_specs=[pl.BlockSpec((1, tm, F), lambda e, i: (e, i, 0)),
                  pl.BlockSpec((1, F, D), lambda e, i: (e, 0, 0)),
                  pl.BlockSpec((1, tm, 1), lambda e, i: (e, i, 0)),
                  pl.BlockSpec((1, D), lambda e, i: (0, 0))],
        out_specs=pl.BlockSpec((1, tm, D), lambda e, i: (e, i, 0)),
        out_shape=jax.ShapeDtypeStruct((E, cap, D), jnp.float32),
        scratch_shapes=[pltpu.VMEM((F, D), jnp.bfloat16)],
        compiler_params=pltpu.CompilerParams(dimension_semantics=("parallel", "arbitrary"),
                                             vmem_limit_bytes=VMEM_LIMIT_BYTES),
        name="expert_down",
    )(h, w_down, gsel.reshape(E, cap, 1).astype(jnp.float32), out_gate.reshape(1, D).astype(jnp.float32))


def expert_choice_latent(x1, h2, logits, out_gate, w_gate, w_up, w_down):
    N, D = x1.shape
    cap = EC_CAPACITY_FACTOR * N // N_EXPERTS
    aff = jax.nn.softmax(logits[:, :N_EXPERTS], axis=-1)
    gsel, idx = lax.top_k(aff.T, cap)
    ye = expert_ffn(h2[idx], w_gate, w_up, w_down, gsel, out_gate)
    return x1.at[idx.reshape(-1)].add(ye.reshape(-1, D))


def expert_choice_ffn(h, w_router, w_gate, w_up, w_down):
    B, N, D = h.shape
    cap = EC_CAPACITY_FACTOR * N // N_EXPERTS
    hs = h[0]
    aff = jax.nn.softmax(jnp.dot(hs, w_router, precision=lax.Precision.HIGHEST).astype(jnp.float32), axis=-1)
    gsel, idx = lax.top_k(aff.T, cap)
    xe = hs[idx]
    a = batched_matmul(xe, w_gate)
    u = batched_matmul(xe, w_up)
    ye = batched_matmul(jax.nn.silu(a) * u, w_down) * gsel[..., None].astype(hs.dtype)
    return jnp.zeros_like(hs).at[idx.reshape(-1)].add(ye.reshape(-1, D))[None]


def trunk_layer(x, ctx, c, c_ctx, p, update_ctx, consts):
    L = x.shape[1]
    cond = jnp.pad(jnp.concatenate([jax.nn.silu(c), jax.nn.silu(c_ctx)[None]], axis=0), ((0, 6), (0, 0)))
    mod = matmul(cond, p['w_mod']) + p['b_mod']
    mx = jnp.split(mod[0:1, None, :], N_MOD, axis=-1)
    mc = jnp.split(mod[1], N_MOD, axis=-1)
    w_in = p['w_in']
    hc = modulate(rms_norm(ctx, p['norm1_g']), mc[0], mc[1])

    if update_ctx:
        zc = matmul(hc[0], w_in)[None]
        zc_wa_kv = zc[..., OFF_WA_KV:OFF_NA_Q]
        zc_na_kv = zc[..., OFF_NA_KV:OFF_GATE]
    else:
        zc_wa_kv = matmul(hc[0], w_in[:, OFF_WA_KV:OFF_NA_Q])[None]
        zc_na_kv = matmul(hc[0], w_in[:, OFF_NA_KV:OFF_GATE])[None]
    kc_wa, vc_wa = keys_values(zc_wa_kv, WA_KV_HEADS, p['wa_k_norm'])
    kc_na, vc_na = keys_values(zc_na_kv, NA_HEADS, p['na_k_norm'])

    bf = jnp.bfloat16
    hxb = norm_modulate(x[0], p['norm1_g'], mx[0][0, 0], mx[1][0, 0])
    rope = rope_lane_tables(L)
    qk_scale = HEAD_DIM ** -0.5
    nkv = WA_KV_HEADS * HEAD_DIM
    nna = NA_HEADS * HEAD_DIM
    z_hy = project(hxb, w_in, OFF_HY, OFF_WA_Q - OFF_HY)
    y_hy = hyena_latent(z_hy, p, consts)
    q_wa = project(hxb, w_in, OFF_WA_Q, WA_HEADS * HEAD_DIM, "headnorm", bf, p['wa_q_norm'], qk_scale, rope)
    k_wa = project(hxb, w_in, OFF_WA_KV, nkv, "headnorm", bf, p['wa_k_norm'], 1.0, rope)
    v_wa = project(hxb, w_in, OFF_WA_KV + nkv, nkv, "plain", bf)
    y_wa = windowed_attention(q_wa, k_wa, v_wa, kc_wa[0].reshape(-1, nkv).astype(bf),
                              vc_wa[0].reshape(-1, nkv).astype(bf), p['wa_sink'])
    q_na = project(hxb, w_in, OFF_NA_Q, nna, "headnorm", bf, p['na_q_norm'], qk_scale)
    k_na = project(hxb, w_in, OFF_NA_KV, nna, "headnorm", bf, p['na_k_norm'])
    v_na = project(hxb, w_in, OFF_NA_KV + nna, nna, "plain", bf)
    y_na = neighbourhood_attention(q_na, k_na, v_na, kc_na[0].reshape(-1, nna).astype(bf),
                                   vc_na[0].reshape(-1, nna).astype(bf), p['na_rpb'])
    gates = project(hxb, w_in, OFF_GATE, N_BRANCH * D_MODEL, "sigmoid")
    m = merge_gated(y_hy, y_wa, y_na, gates, p['w_branch'].astype(bf))
    x1, h2, logits = out_project(m, p['w_out'].astype(bf), x[0], mx[2][0, 0], p['norm2_g'],
                                 mx[3][0, 0], mx[4][0, 0], p['w_router'])
    x = expert_choice_latent(x1, h2, logits, mx[5][0, 0], p['w_gate'], p['w_up'], p['w_down'])[None]

    if update_ctx:
        yc_hy = hyena_branch(zc[..., OFF_HY:OFF_WA_Q], p)
        qc_wa = rms_norm(split_heads(zc[..., OFF_WA_Q:OFF_WA_KV], WA_HEADS), p['wa_q_norm'])
        yc_wa = context_attention(qc_wa, kc_wa, vc_wa, p['wa_sink'])
        qc_na = rms_norm(split_heads(zc[..., OFF_NA_Q:OFF_NA_KV], NA_HEADS), p['na_q_norm'])
        yc_na = context_attention(qc_na, kc_na, vc_na, None)
        ctx = ctx + mc[2] * merge_branches(yc_hy, yc_wa, yc_na, zc[..., OFF_GATE:], p['w_branch'], p['w_out'])
        hc2 = modulate(rms_norm(ctx, p['norm2_g']), mc[3], mc[4])
        ctx = ctx + mc[5] * expert_choice_ffn(hc2, p['w_router'], p['w_gate'], p['w_up'], p['w_down'])
    return x, ctx


def kernel(x, c, ctx, c_ctx, w_mod, b_mod, norm1_g, w_in, hy_conv_w, hy_conv_b, hy_fw1, hy_fb1, hy_fw2, hy_fb2, hy_fw3, hy_freq, hy_decay, hy_d, wa_q_norm, wa_k_norm, wa_sink, na_q_norm, na_k_norm, na_rpb, w_branch, w_out, norm2_g, w_router, w_gate, w_up, w_down):
    consts = dft_constants(x.shape[1])
    for l in range(DEPTH):
        p = {
            'w_mod': w_mod[l], 'b_mod': b_mod[l], 'norm1_g': norm1_g[l], 'w_in': w_in[l],
            'hy_conv_w': hy_conv_w[l], 'hy_conv_b': hy_conv_b[l], 'hy_fw1': hy_fw1[l], 'hy_fb1': hy_fb1[l],
            'hy_fw2': hy_fw2[l], 'hy_fb2': hy_fb2[l], 'hy_fw3': hy_fw3[l], 'hy_freq': hy_freq[l],
            'hy_decay': hy_decay[l], 'hy_d': hy_d[l], 'wa_q_norm': wa_q_norm[l], 'wa_k_norm': wa_k_norm[l],
            'wa_sink': wa_sink[l], 'na_q_norm': na_q_norm[l], 'na_k_norm': na_k_norm[l], 'na_rpb': na_rpb[l],
            'w_branch': w_branch[l], 'w_out': w_out[l], 'norm2_g': norm2_g[l], 'w_router': w_router[l],
            'w_gate': w_gate[l], 'w_up': w_up[l], 'w_down': w_down[l],
        }
        x, ctx = trunk_layer(x, ctx, c, c_ctx, p, l < DEPTH - 1, consts)
    return x
```

```python
import math
from functools import partial

import jax
import jax.numpy as jnp
from jax import lax
from jax.experimental import pallas as pl
from jax.experimental.pallas import tpu as pltpu

D_MODEL = 2048
SEQ = 16384
DEPTH = 2
CTX_LEN = 256
GRID_W = 64
HEAD_DIM = 128
BRANCH_WIDTH = 1024
N_BRANCH = 3
N_MOD = 6
RMS_EPS = 1e-6
NEG_INF = -1e30

HY_WIDTH = BRANCH_WIDTH
HY_ORDER = 2
HY_DIRS = 2
HY_BANDS = 16
HY_DECAY_SHIFT = 0.05

WA_HEADS = BRANCH_WIDTH // HEAD_DIM
WA_KV_HEADS = 2
WA_WINDOW = 128
WA_BLOCK = 128

NA_HEADS = BRANCH_WIDTH // HEAD_DIM
NA_WIN_ROWS = 8
NA_WIN_COLS = 16

ROPE_BASE = 10000.0

N_EXPERTS = 16
EC_CAPACITY_FACTOR = 2
D_EXPERT = 1024

OFF_HY = 0
OFF_WA_Q = OFF_HY + 3 * HY_WIDTH
OFF_WA_KV = OFF_WA_Q + WA_HEADS * HEAD_DIM
OFF_NA_Q = OFF_WA_KV + 2 * WA_KV_HEADS * HEAD_DIM
OFF_NA_KV = OFF_NA_Q + NA_HEADS * HEAD_DIM
OFF_GATE = OFF_NA_KV + 2 * NA_HEADS * HEAD_DIM
N_IN = OFF_GATE + N_BRANCH * D_MODEL

VMEM_LIMIT_BYTES = 56 * 1024 * 1024


def _mm_kernel(a_ref, b_ref, o_ref):
    o_ref[...] = jnp.dot(a_ref[...].astype(jnp.bfloat16), b_ref[...].astype(jnp.bfloat16),
                         preferred_element_type=jnp.float32).astype(o_ref.dtype)


def _pick(n, pref):
    for t in pref:
        if n % t == 0:
            return t
    return n


def matmul(a, b, out_dtype=jnp.float32, layer=None):
    M, K = a.shape
    N = b.shape[-1]
    tm = _pick(M, (512, 256, 128, 64, 32, 16, 8))
    tn = _pick(N, (1024, 512, 256, 128))
    if layer is None:
        b_spec = pl.BlockSpec((K, tn), lambda j, i: (0, j))
    else:
        b_spec = pl.BlockSpec((None, K, tn), lambda j, i: (layer, 0, j))
    return pl.pallas_call(
        _mm_kernel,
        grid=(N // tn, M // tm),
        in_specs=[pl.BlockSpec((tm, K), lambda j, i: (i, 0)), b_spec],
        out_specs=pl.BlockSpec((tm, tn), lambda j, i: (i, j)),
        out_shape=jax.ShapeDtypeStruct((M, N), out_dtype),
        compiler_params=pltpu.CompilerParams(
            dimension_semantics=("parallel", "parallel"), vmem_limit_bytes=VMEM_LIMIT_BYTES),
        name="matmul",
    )(a, b)


def _norm_mod_kernel(x_ref, g_ref, shift_ref, scale_ref, o_ref):
    x = x_ref[...]
    y = x * lax.rsqrt(jnp.mean(x * x, axis=-1, keepdims=True) + RMS_EPS) * g_ref[...]
    o_ref[...] = (y * (1.0 + scale_ref[...]) + shift_ref[...]).astype(o_ref.dtype)


def norm_modulate(x, g, shift, scale, out_dtype=jnp.bfloat16):
    M, D = x.shape
    tm = _pick(M, (512, 256, 128, 64, 32, 16, 8))
    vec = pl.BlockSpec((1, D), lambda i: (0, 0))
    return pl.pallas_call(
        _norm_mod_kernel,
        grid=(M // tm,),
        in_specs=[pl.BlockSpec((tm, D), lambda i: (i, 0)), vec, vec, vec],
        out_specs=pl.BlockSpec((tm, D), lambda i: (i, 0)),
        out_shape=jax.ShapeDtypeStruct((M, D), out_dtype),
        compiler_params=pltpu.CompilerParams(
            dimension_semantics=("parallel",), vmem_limit_bytes=VMEM_LIMIT_BYTES),
        name="norm_modulate",
    )(x, g.reshape(1, D), shift.reshape(1, D), scale.reshape(1, D))


def _swap_halves(x):
    lane = lax.broadcasted_iota(jnp.int32, x.shape, 1)
    return jnp.where((lane % 64) < 32, pltpu.roll(x, 96, 1), pltpu.roll(x, 32, 1))


def _proj_kernel(*refs, mode, post_scale, rope):
    if mode == "headnorm":
        if rope:
            a_ref, w_ref, gain_ref, cos_ref, sin_ref, o_ref, w_bf = refs
        else:
            a_ref, w_ref, gain_ref, o_ref, w_bf = refs
    else:
        a_ref, w_ref, o_ref, w_bf = refs

    @pl.when(pl.program_id(1) == 0)
    def _():
        w_bf[...] = w_ref[...].astype(jnp.bfloat16)

    acc = jnp.dot(a_ref[...], w_bf[...], preferred_element_type=jnp.float32)
    if mode == "plain":
        o_ref[...] = acc.astype(o_ref.dtype)
    elif mode == "sigmoid":
        o_ref[...] = jax.nn.sigmoid(acc).astype(o_ref.dtype)
    else:
        gain = gain_ref[...] * post_scale
        for h in range(acc.shape[1] // HEAD_DIM):
            xh = acc[:, h * HEAD_DIM:(h + 1) * HEAD_DIM]
            y = xh * lax.rsqrt(jnp.mean(xh * xh, axis=-1, keepdims=True) + RMS_EPS) * gain
            if rope:
                y = y * cos_ref[...] + _swap_halves(y) * sin_ref[...]
            o_ref[:, h * HEAD_DIM:(h + 1) * HEAD_DIM] = y.astype(o_ref.dtype)


def project(a, w, layer, col_off, n_cols, mode="plain", out_dtype=jnp.float32, gain=None, post_scale=1.0,
            rope=None):
    M, K = a.shape
    tm = _pick(M, (1024, 512, 256, 128, 64, 32, 16, 8))
    tn = next(t for t in (1024, 768, 512, 256, 128) if n_cols % t == 0 and col_off % t == 0)
    off = col_off // tn
    in_specs = [pl.BlockSpec((tm, K), lambda j, i: (i, 0)),
                pl.BlockSpec((None, K, tn), lambda j, i: (layer, 0, off + j))]
    args = [a, w]
    if mode == "headnorm":
        in_specs.append(pl.BlockSpec((1, HEAD_DIM), lambda j, i: (0, 0)))
        args.append(gain.reshape(1, HEAD_DIM).astype(jnp.float32))
        if rope is not None:
            in_specs += [pl.BlockSpec((tm, HEAD_DIM), lambda j, i: (i, 0))] * 2
            args += list(rope)
    return pl.pallas_call(
        partial(_proj_kernel, mode=mode, post_scale=post_scale, rope=rope is not None),
        grid=(n_cols // tn, M // tm),
        in_specs=in_specs,
        out_specs=pl.BlockSpec((tm, tn), lambda j, i: (i, j)),
        out_shape=jax.ShapeDtypeStruct((M, n_cols), out_dtype),
        scratch_shapes=[pltpu.VMEM((K, tn), jnp.bfloat16)],
        compiler_params=pltpu.CompilerParams(
            dimension_semantics=("parallel", "arbitrary"), vmem_limit_bytes=VMEM_LIMIT_BYTES),
        name="project_" + mode,
    )(*args)


def rope_lane_tables(L):
    t = jnp.arange(L, dtype=jnp.int32)
    row = (t // GRID_W).astype(jnp.float32)
    col = (t % GRID_W).astype(jnp.float32)
    nf = HEAD_DIM // 4
    inv = ROPE_BASE ** (-jnp.arange(nf, dtype=jnp.float32) / nf)
    ar, ac = row[:, None] * inv, col[:, None] * inv
    cos = jnp.concatenate([jnp.cos(ar), jnp.cos(ar), jnp.cos(ac), jnp.cos(ac)], axis=-1)
    sin = jnp.concatenate([-jnp.sin(ar), jnp.sin(ar), -jnp.sin(ac), jnp.sin(ac)], axis=-1)
    return cos, sin


_NT = (((1,), (1,)), ((), ()))


def _wa_kernel(sink_ref, q_ref, k_ref, v_ref, kc_ref, vc_ref, o_ref, *, tq, seq):
    g = pl.program_id(0)
    i = pl.program_id(1)
    nwin = tq + 2 * WA_WINDOW
    ws = jnp.clip(i * tq - WA_WINDOW, 0, seq - nwin)
    start = pl.multiple_of(ws, WA_WINDOW)
    kwin = k_ref[pl.ds(start, nwin), :]
    vwin = v_ref[pl.ds(start, nwin), :]
    qpos = i * tq + lax.broadcasted_iota(jnp.int32, (tq, nwin), 0)
    kpos = ws + lax.broadcasted_iota(jnp.int32, (tq, nwin), 1)
    valid = jnp.abs(qpos - kpos) <= WA_WINDOW
    group = WA_HEADS // WA_KV_HEADS
    for hh in range(group):
        q = q_ref[:, hh * HEAD_DIM:(hh + 1) * HEAD_DIM]
        s = jnp.where(valid, lax.dot_general(q, kwin, _NT, preferred_element_type=jnp.float32), NEG_INF)
        sc = lax.dot_general(q, kc_ref[...], _NT, preferred_element_type=jnp.float32)
        sk = sink_ref[g * group + hh]
        m = jnp.maximum(jnp.maximum(jnp.max(s, axis=-1, keepdims=True), jnp.max(sc, axis=-1, keepdims=True)), sk)
        p = jnp.exp(s - m)
        pc = jnp.exp(sc - m)
        denom = jnp.sum(p, axis=-1, keepdims=True) + jnp.sum(pc, axis=-1, keepdims=True) + jnp.exp(sk - m)
        o = (jnp.dot(p.astype(vwin.dtype), vwin, preferred_element_type=jnp.float32)
             + jnp.dot(pc.astype(vwin.dtype), vc_ref[...], preferred_element_type=jnp.float32))
        o_ref[:, hh * HEAD_DIM:(hh + 1) * HEAD_DIM] = (o / denom).astype(o_ref.dtype)


def windowed_attention(q, k, v, kc, vc, sink, tq=256):
    L = q.shape[0]
    Lc = kc.shape[0]
    gw = (WA_HEADS // WA_KV_HEADS) * HEAD_DIM
    slab = pl.BlockSpec((L, HEAD_DIM), lambda g, i, s: (0, g))
    cslab = pl.BlockSpec((Lc, HEAD_DIM), lambda g, i, s: (0, g))
    return pl.pallas_call(
        partial(_wa_kernel, tq=tq, seq=L),
        grid_spec=pltpu.PrefetchScalarGridSpec(
            num_scalar_prefetch=1,
            grid=(WA_KV_HEADS, L // tq),
            in_specs=[pl.BlockSpec((tq, gw), lambda g, i, s: (i, g)), slab, slab, cslab, cslab],
            out_specs=pl.BlockSpec((tq, gw), lambda g, i, s: (i, g)),
        ),
        out_shape=jax.ShapeDtypeStruct((L, WA_HEADS * HEAD_DIM), jnp.bfloat16),
        compiler_params=pltpu.CompilerParams(
            dimension_semantics=("parallel", "parallel"), vmem_limit_bytes=VMEM_LIMIT_BYTES),
        name="windowed_attention",
    )(sink.astype(jnp.float32), q, k, v, kc, vc)


NA_ROW_BLOCK = 4
NA_KEY_ROWS = NA_ROW_BLOCK + NA_WIN_ROWS - 1


def _na_kernel(q_ref, k_ref, v_ref, kc_ref, vc_ref, bias_ref, o_ref, *, rows):
    i = pl.program_id(1)
    ws = jnp.clip(i * NA_ROW_BLOCK - NA_WIN_ROWS // 2, 0, rows - NA_KEY_ROWS)
    start = pl.multiple_of(ws * GRID_W, GRID_W)
    nk = NA_KEY_ROWS * GRID_W
    kwin = k_ref[pl.ds(start, nk), :]
    vwin = v_ref[pl.ds(start, nk), :]
    q = q_ref[...]
    s = lax.dot_general(q, kwin, _NT, preferred_element_type=jnp.float32) + bias_ref[0, 0]
    sc = lax.dot_general(q, kc_ref[...], _NT, preferred_element_type=jnp.float32)
    m = jnp.maximum(jnp.max(s, axis=-1, keepdims=True), jnp.max(sc, axis=-1, keepdims=True))
    p = jnp.exp(s - m)
    pc = jnp.exp(sc - m)
    denom = jnp.sum(p, axis=-1, keepdims=True) + jnp.sum(pc, axis=-1, keepdims=True)
    o = (jnp.dot(p.astype(vwin.dtype), vwin, preferred_element_type=jnp.float32)
         + jnp.dot(pc.astype(vwin.dtype), vc_ref[...], preferred_element_type=jnp.float32))
    o_ref[...] = (o / denom).astype(o_ref.dtype)


def na_bias_tiles(rpb, rows):
    col = jnp.arange(GRID_W)
    cstart = jnp.clip(col - NA_WIN_COLS // 2, 0, GRID_W - NA_WIN_COLS)
    col_in = (col[None, :] >= cstart[:, None]) & (col[None, :] < cstart[:, None] + NA_WIN_COLS)
    dc_idx = jnp.clip(col[None, :] - col[:, None] + NA_WIN_COLS - 1, 0, 2 * NA_WIN_COLS - 2)
    exact = lax.Precision.HIGHEST
    by_col = jnp.einsum('hrd,qkd->hrqk', rpb.astype(jnp.float32),
                        jax.nn.one_hot(dc_idx, 2 * NA_WIN_COLS - 1, dtype=jnp.float32), precision=exact)
    tiles = []
    for blk in (0, 1, rows // NA_ROW_BLOCK - 1):
        r = blk * NA_ROW_BLOCK
        ws = min(max(r - NA_WIN_ROWS // 2, 0), rows - NA_KEY_ROWS)
        qr = r + jnp.arange(NA_ROW_BLOCK)
        kr = ws + jnp.arange(NA_KEY_ROWS)
        r0 = jnp.clip(qr - NA_WIN_ROWS // 2, 0, rows - NA_WIN_ROWS)
        row_in = (kr[None, :] >= r0[:, None]) & (kr[None, :] < r0[:, None] + NA_WIN_ROWS)
        dr_idx = jnp.clip(kr[None, :] - qr[:, None] + NA_WIN_ROWS - 1, 0, 2 * NA_WIN_ROWS - 2)
        b = jnp.einsum('hrqk,abr->haqbk', by_col,
                       jax.nn.one_hot(dr_idx, 2 * NA_WIN_ROWS - 1, dtype=jnp.float32), precision=exact)
        ok = row_in[:, None, :, None] & col_in[None, :, None, :]
        b = jnp.where(ok[None], b, NEG_INF)
        tiles.append(b.reshape(rpb.shape[0], NA_ROW_BLOCK * GRID_W, NA_KEY_ROWS * GRID_W))
    return jnp.stack(tiles)


def neighbourhood_attention(q, k, v, kc, vc, rpb):
    L = q.shape[0]
    Lc = kc.shape[0]
    rows = L // GRID_W
    nblk = rows // NA_ROW_BLOCK
    tq = NA_ROW_BLOCK * GRID_W
    nk = NA_KEY_ROWS * GRID_W
    bias = na_bias_tiles(rpb, rows)
    slab = pl.BlockSpec((L, HEAD_DIM), lambda h, i: (0, h))
    cslab = pl.BlockSpec((Lc, HEAD_DIM), lambda h, i: (0, h))
    variant = lambda h, i: (jnp.where(i == 0, 0, jnp.where(i == nblk - 1, 2, 1)), h, 0, 0)
    return pl.pallas_call(
        partial(_na_kernel, rows=rows),
        grid=(NA_HEADS, nblk),
        in_specs=[pl.BlockSpec((tq, HEAD_DIM), lambda h, i: (i, h)), slab, slab, cslab, cslab,
                  pl.BlockSpec((1, 1, tq, nk), variant)],
        out_specs=pl.BlockSpec((tq, HEAD_DIM), lambda h, i: (i, h)),
        out_shape=jax.ShapeDtypeStruct((L, NA_HEADS * HEAD_DIM), jnp.bfloat16),
        compiler_params=pltpu.CompilerParams(
            dimension_semantics=("parallel", "parallel"), vmem_limit_bytes=VMEM_LIMIT_BYTES),
        name="neighbourhood_attention",
    )(q, k, v, kc, vc, bias)


DFT_N1 = 128
DFT_K1 = DFT_N1 // 2 + 1
DFT_K1_PAD = 72


def dft_constants(L):
    N = 2 * L
    N2 = N // DFT_N1
    f32, bf = jnp.float32, jnp.bfloat16
    k1 = jnp.arange(DFT_K1, dtype=jnp.int32)
    n1 = jnp.arange(DFT_N1, dtype=jnp.int32)
    th = ((k1[:, None] * n1[None, :]) % DFT_N1).astype(f32) * (2.0 * math.pi / DFT_N1)
    pad = ((0, DFT_K1_PAD - DFT_K1), (0, 0))
    rows_fwd = jnp.concatenate([jnp.pad(jnp.cos(th), pad), jnp.pad(-jnp.sin(th), pad)], axis=0)
    ck = jnp.where((k1 == 0) | (k1 == DFT_N1 // 2), 1.0, 2.0)[:, None]
    half = DFT_N1 // 2
    rows_inv = jnp.concatenate([jnp.pad(ck * jnp.cos(th[:, :half]), pad),
                                jnp.pad(-ck * jnp.sin(th[:, :half]), pad)], axis=0).T
    n2 = jnp.arange(N2, dtype=jnp.int32)
    kk = k1[:, None] + DFT_N1 * n2[None, :]
    ang = ((kk[:, :, None] * n2[None, None, :]) % N).astype(f32) * (2.0 * math.pi / N)
    gr, gi = jnp.cos(ang), -jnp.sin(ang)
    mid_fwd = jnp.concatenate([jnp.concatenate([gr, -gi], axis=2),
                               jnp.concatenate([gi, gr], axis=2)], axis=1)
    return {
        "rows_fwd": rows_fwd.astype(bf), "rows_fwd_half": rows_fwd[:, :half].astype(bf),
        "rows_inv": rows_inv.astype(bf),
        "mid_fwd": mid_fwd.astype(bf), "mid_inv": jnp.swapaxes(mid_fwd, 1, 2).astype(bf),
    }


def _spectral_fwd_kernel(a_ref, g_ref, o_ref):
    n2 = a_ref.shape[2]
    x = (jnp.dot(g_ref[0, :, :n2], a_ref[0, 0], preferred_element_type=jnp.float32)
         + jnp.dot(g_ref[0, :, n2:], a_ref[1, 0], preferred_element_type=jnp.float32))
    o_ref[0, 0] = x[:n2].astype(o_ref.dtype)
    o_ref[1, 0] = x[n2:].astype(o_ref.dtype)


def spectral_fwd(a, mid_fwd, cb=1024):
    _, _, n2, C = a.shape
    cb = min(cb, C)
    return pl.pallas_call(
        _spectral_fwd_kernel,
        grid=(DFT_K1, C // cb),
        in_specs=[pl.BlockSpec((2, 1, n2, cb), lambda k, c: (0, k, 0, c)),
                  pl.BlockSpec((1, 2 * n2, 2 * n2), lambda k, c: (k, 0, 0))],
        out_specs=pl.BlockSpec((2, 1, n2, cb), lambda k, c: (0, k, 0, c)),
        out_shape=jax.ShapeDtypeStruct((2, DFT_K1, n2, C), jnp.bfloat16),
        compiler_params=pltpu.CompilerParams(
            dimension_semantics=("parallel", "parallel"), vmem_limit_bytes=VMEM_LIMIT_BYTES),
        name="spectral_fwd",
    )(a, mid_fwd)


def _spectral_mid_kernel(a_ref, gf_ref, gi_ref, ks_ref, o_ref):
    n2 = a_ref.shape[2]
    k1 = pl.program_id(0)

    @pl.when(k1 < DFT_K1)
    def _():
        x = (jnp.dot(gf_ref[0, :, :n2], a_ref[0, 0], preferred_element_type=jnp.float32)
             + jnp.dot(gf_ref[0, :, n2:], a_ref[1, 0], preferred_element_type=jnp.float32))
        xr, xi = x[:n2], x[n2:]
        kr, ki = ks_ref[0, 0].astype(jnp.float32), ks_ref[1, 0].astype(jnp.float32)
        yr = (xr * kr - xi * ki).astype(jnp.bfloat16)
        yi = (xr * ki + xi * kr).astype(jnp.bfloat16)
        b = (jnp.dot(gi_ref[0, :, :n2], yr, preferred_element_type=jnp.float32)
             + jnp.dot(gi_ref[0, :, n2:], yi, preferred_element_type=jnp.float32))
        o_ref[0, 0] = b[:n2].astype(o_ref.dtype)
        o_ref[1, 0] = b[n2:].astype(o_ref.dtype)

    @pl.when(k1 >= DFT_K1)
    def _():
        o_ref[...] = jnp.zeros_like(o_ref)


def spectral_mid(a, consts, kspec, col_off, cb=1024):
    _, _, n2, C = a.shape
    cb = min(cb, C)
    off = col_off // cb
    kc = lambda k: jnp.minimum(k, DFT_K1 - 1)
    return pl.pallas_call(
        _spectral_mid_kernel,
        grid=(DFT_K1_PAD, C // cb),
        in_specs=[pl.BlockSpec((2, 1, n2, cb), lambda k, c: (0, k, 0, c)),
                  pl.BlockSpec((1, 2 * n2, 2 * n2), lambda k, c: (kc(k), 0, 0)),
                  pl.BlockSpec((1, 2 * n2, 2 * n2), lambda k, c: (kc(k), 0, 0)),
                  pl.BlockSpec((2, 1, n2, cb), lambda k, c: (0, kc(k), 0, off + c))],
        out_specs=pl.BlockSpec((2, 1, n2, cb), lambda k, c: (0, k, 0, c)),
        out_shape=jax.ShapeDtypeStruct(a.shape, jnp.bfloat16),
        compiler_params=pltpu.CompilerParams(
            dimension_semantics=("parallel", "parallel"), vmem_limit_bytes=VMEM_LIMIT_BYTES),
        name="spectral_mid",
    )(a, consts["mid_fwd"], consts["mid_inv"], kspec)


DFT_ROW_GROUP = 16
DFT_ROW_LANES = 256


def _rows_fwd_kernel(f_ref, u_ref, o_ref):
    x = pltpu.einshape("abc->bac", u_ref[...].astype(jnp.float32))
    rs = [jnp.dot(f_ref[...], x[s].astype(jnp.bfloat16), preferred_element_type=jnp.float32)
          for s in range(x.shape[0])]
    o_ref[...] = pltpu.einshape("abc->bac", jnp.stack(rs)).astype(o_ref.dtype)


def dft_rows_fwd(rows_mat, u3):
    kn, n2, C = u3.shape
    grp = min(DFT_ROW_GROUP, n2)
    tc = min(DFT_ROW_LANES, C)
    m = rows_mat.shape[0]
    return pl.pallas_call(
        _rows_fwd_kernel,
        grid=(n2 // grp, C // tc),
        in_specs=[pl.BlockSpec((m, kn), lambda j, c: (0, 0)),
                  pl.BlockSpec((kn, grp, tc), lambda j, c: (0, j, c))],
        out_specs=pl.BlockSpec((m, grp, tc), lambda j, c: (0, j, c)),
        out_shape=jax.ShapeDtypeStruct((m, n2, C), jnp.bfloat16),
        compiler_params=pltpu.CompilerParams(
            dimension_semantics=("parallel", "parallel"), vmem_limit_bytes=VMEM_LIMIT_BYTES),
        name="dft_rows_fwd",
    )(rows_mat, u3)


def _rows_inv_kernel(f_ref, b_ref, u_ref, gate_ref, scale_ref, d_ref, o_ref):
    x = pltpu.einshape("abc->bac", b_ref[...].astype(jnp.float32))
    ys = [jnp.dot(f_ref[...], x[s].astype(jnp.bfloat16), preferred_element_type=jnp.float32)
          for s in range(x.shape[0])]
    y = pltpu.einshape("abc->bac", jnp.stack(ys))
    o_ref[...] = (gate_ref[...] * (y * scale_ref[...] + u_ref[...] * d_ref[...])).astype(o_ref.dtype)


def dft_rows_inv(rows_inv, b3, u3, gate3, scale, d_skip, out_dtype):
    nr, n2, C = u3.shape
    grp = min(DFT_ROW_GROUP, n2)
    tc = min(DFT_ROW_LANES, C)
    blk = pl.BlockSpec((nr, grp, tc), lambda j, c: (0, j, c))
    vec = pl.BlockSpec((1, 1, tc), lambda j, c: (0, 0, c))
    return pl.pallas_call(
        _rows_inv_kernel,
        grid=(n2 // grp, C // tc),
        in_specs=[pl.BlockSpec(rows_inv.shape, lambda j, c: (0, 0)),
                  pl.BlockSpec((b3.shape[0], grp, tc), lambda j, c: (0, j, c)), blk, blk, vec, vec],
        out_specs=blk,
        out_shape=jax.ShapeDtypeStruct((nr, n2, C), out_dtype),
        compiler_params=pltpu.CompilerParams(
            dimension_semantics=("parallel", "parallel"), vmem_limit_bytes=VMEM_LIMIT_BYTES),
        name="dft_rows_inv",
    )(rows_inv, b3, u3, gate3, scale.reshape(1, 1, C), d_skip.reshape(1, 1, C))


def _conv3_kernel(z_ref, prev_ref, next_ref, w_ref, b_ref, o_ref):
    i = pl.program_id(0)
    z = z_ref[...]
    tm = z.shape[0]
    row = lax.broadcasted_iota(jnp.int32, z.shape, 0)
    prev_row = jnp.where(i > 0, prev_ref[7:8, :], 0.0)
    next_row = jnp.where(i < pl.num_programs(0) - 1, next_ref[0:1, :], 0.0)
    zp = jnp.where(row == 0, prev_row, pltpu.roll(z, 1, 0))
    zn = jnp.where(row == tm - 1, next_row, pltpu.roll(z, tm - 1, 0))
    o_ref[...] = zp * w_ref[0:1, :] + z * w_ref[1:2, :] + zn * w_ref[2:3, :] + b_ref[...]


def short_conv3_part(z, w, b, part, width):
    L = z.shape[0]
    tm = _pick(L, (512, 256, 128, 64, 32, 16, 8))
    tc = _pick(width, (1024, 512, 256, 128))
    off = part * width // tc
    halo = 8
    return pl.pallas_call(
        _conv3_kernel,
        grid=(L // tm, width // tc),
        in_specs=[pl.BlockSpec((tm, tc), lambda i, j: (i, off + j)),
                  pl.BlockSpec((halo, tc), lambda i, j: (jnp.maximum(i * (tm // halo) - 1, 0), off + j)),
                  pl.BlockSpec((halo, tc), lambda i, j: (jnp.minimum((i + 1) * (tm // halo), L // halo - 1), off + j)),
                  pl.BlockSpec((3, tc), lambda i, j: (0, off + j)),
                  pl.BlockSpec((1, tc), lambda i, j: (0, off + j))],
        out_specs=pl.BlockSpec((tm, tc), lambda i, j: (i, j)),
        out_shape=jax.ShapeDtypeStruct((L, width), jnp.float32),
        compiler_params=pltpu.CompilerParams(
            dimension_semantics=("parallel", "parallel"), vmem_limit_bytes=VMEM_LIMIT_BYTES),
        name="short_conv3",
    )(z, z, z, w, b.reshape(1, -1))


FILTER_PAD = 128


def _filter_kernel(feats_ref, feats0_ref, fw1_ref, fb1_ref, fw2_ref, fb2_ref, fr_ref, fw3_ref, fw3b_ref,
                   rate_ref, rateb_ref, kf_ref, asum_ref, *, zero_tile):
    i = pl.program_id(0)
    exact = lax.Precision.HIGHEST

    def mlp(feats, fw3, rate):
        h = jnp.sin(fr_ref[0:1, :] * (jnp.dot(feats, fw1_ref[...], precision=exact,
                                              preferred_element_type=jnp.float32) + fb1_ref[...]))
        h = jnp.sin(fr_ref[1:2, :] * (jnp.dot(h, fw2_ref[...], precision=exact,
                                              preferred_element_type=jnp.float32) + fb2_ref[...]))
        h = jnp.dot(h.astype(jnp.bfloat16), fw3.astype(jnp.bfloat16), preferred_element_type=jnp.float32)
        return h * (jnp.exp(-feats[:, 0:1] * rate) + HY_DECAY_SHIFT)

    k = mlp(feats_ref[...], fw3_ref[0], rate_ref[0])
    lag0_back = mlp(feats0_ref[...], fw3b_ref[0], rateb_ref[0])[0:1, :]
    row = lax.broadcasted_iota(jnp.int32, k.shape, 0)
    k = k + jnp.where((row == 0) & (i == 0), lag0_back, 0.0)
    k = jnp.where((row == 0) & (i == zero_tile), 0.0, k)
    kf_ref[...] = k.astype(kf_ref.dtype)

    @pl.when(i == 0)
    def _():
        asum_ref[...] = jnp.zeros_like(asum_ref)

    asum_ref[...] += jnp.sum(jnp.abs(k), axis=0, keepdims=True)


def hyena_filter(L, p):
    f32 = jnp.float32
    N = 2 * L
    C2 = HY_ORDER * HY_WIDTH
    n = jnp.arange(N, dtype=jnp.int32)
    pos = jnp.where(n < L, n, N - n).astype(f32)[:, None]
    bands = jnp.linspace(1e-4, HY_BANDS - 1, HY_BANDS, dtype=f32)[None, :]
    ang = (2.0 * math.pi / L) * bands * pos
    feats = jnp.concatenate([pos / (L - 1), jnp.cos(ang), -jnp.sin(ang)], axis=-1)
    feats = jnp.pad(feats, ((0, 0), (0, FILTER_PAD - feats.shape[1])))
    feats0 = jnp.broadcast_to(feats[0:1], (8, FILTER_PAD))
    hid = p['hy_fw2'].shape[0]
    ph = FILTER_PAD - hid
    fw1 = jnp.pad(p['hy_fw1'].astype(f32), ((0, FILTER_PAD - p['hy_fw1'].shape[0]), (0, ph)))
    fb1 = jnp.pad(p['hy_fb1'].astype(f32), (0, ph)).reshape(1, -1)
    fw2 = jnp.pad(p['hy_fw2'].astype(f32), ((0, ph), (0, ph)))
    fb2 = jnp.pad(p['hy_fb2'].astype(f32), (0, ph)).reshape(1, -1)
    fr = jnp.pad(p['hy_freq'].astype(f32), ((0, 0), (0, ph)))
    fw3 = jnp.pad(p['hy_fw3'].astype(f32), ((0, ph), (0, 0))).reshape(FILTER_PAD, HY_DIRS, C2).transpose(1, 0, 2)
    rate = jnp.abs(p['hy_decay'].astype(f32)).reshape(HY_DIRS, 1, C2)
    tr = _pick(L, (512, 256, 128, 64, 32, 16, 8))
    half_tiles = L // tr
    full = lambda shape: pl.BlockSpec(shape, lambda i: (0,) * len(shape))
    by_dir = lambda shape: pl.BlockSpec(shape, lambda i: (i // half_tiles, 0, 0))
    back = lambda shape: pl.BlockSpec(shape, lambda i: (1, 0, 0))
    return pl.pallas_call(
        partial(_filter_kernel, zero_tile=half_tiles),
        grid=(N // tr,),
        in_specs=[pl.BlockSpec((tr, FILTER_PAD), lambda i: (i, 0)), full((8, FILTER_PAD)),
                  full((FILTER_PAD, FILTER_PAD)), full((1, FILTER_PAD)),
                  full((FILTER_PAD, FILTER_PAD)), full((1, FILTER_PAD)), full((2, FILTER_PAD)),
                  by_dir((1, FILTER_PAD, C2)), back((1, FILTER_PAD, C2)),
                  by_dir((1, 1, C2)), back((1, 1, C2))],
        out_specs=[pl.BlockSpec((tr, C2), lambda i: (i, 0)), pl.BlockSpec((1, C2), lambda i: (0, 0))],
        out_shape=[jax.ShapeDtypeStruct((N, C2), jnp.bfloat16), jax.ShapeDtypeStruct((1, C2), f32)],
        compiler_params=pltpu.CompilerParams(
            dimension_semantics=("arbitrary",), vmem_limit_bytes=VMEM_LIMIT_BYTES),
        name="hyena_filter",
    )(feats, feats0, fw1, fb1, fw2, fb2, fr, fw3, fw3, rate, rate)


def hyena_latent(z, p, consts):
    L = z.shape[0]
    C = HY_WIDTH
    N = 2 * L
    n2 = N // DFT_N1
    half = DFT_N1 // 2
    v, x1, x2 = (short_conv3_part(z, p['hy_conv_w'], p['hy_conv_b'], i, C) for i in range(3))
    kf, asum = hyena_filter(L, p)
    ka = dft_rows_fwd(consts["rows_fwd"], kf.reshape(DFT_N1, n2, HY_ORDER * C))
    kspec = spectral_fwd(ka.reshape(2, DFT_K1_PAD, n2, HY_ORDER * C), consts["mid_fwd"])
    scale = 1.0 / (N * asum[0])

    def long_conv(u, gate, order, out_dtype):
        u3 = u.reshape(half, n2, C)
        ua = dft_rows_fwd(consts["rows_fwd_half"], u3)
        bm = spectral_mid(ua.reshape(2, DFT_K1_PAD, n2, C), consts, kspec, order * C)
        y = dft_rows_inv(consts["rows_inv"], bm.reshape(2 * DFT_K1_PAD, n2, C), u3, gate.reshape(half, n2, C),
                         scale[order * C:(order + 1) * C], p['hy_d'][order].astype(jnp.float32), out_dtype)
        return y.reshape(L, C)

    y1 = long_conv(v, x1, 0, jnp.float32)
    return long_conv(y1, x2, 1, jnp.bfloat16)


def rms_norm(x, g):
    xf = x.astype(jnp.float32)
    y = xf * lax.rsqrt(jnp.mean(xf * xf, axis=-1, keepdims=True) + RMS_EPS)
    return (y * g.astype(jnp.float32)).astype(x.dtype)


def modulate(h, shift, scale):
    return h * (1 + scale) + shift


def short_conv3(z, w, b):
    L = z.shape[1]
    zp = jnp.pad(z, ((0, 0), (1, 1), (0, 0)))
    return zp[:, :L] * w[0] + zp[:, 1:L + 1] * w[1] + zp[:, 2:] * w[2] + b


def hyena_filter_spectra(L, fw1, fb1, fw2, fb2, fw3, freq, decay):
    f32 = jnp.float32
    t = jnp.linspace(0.0, 1.0, L, dtype=f32)[:, None]
    pos = jnp.arange(L, dtype=f32)[:, None]
    bands = jnp.linspace(1e-4, HY_BANDS - 1, HY_BANDS, dtype=f32)[None, :]
    ang = (2.0 * math.pi / L) * bands * pos
    feats = jnp.concatenate([t, jnp.cos(ang), -jnp.sin(ang)], axis=-1)
    fr = freq.astype(f32)
    h = jnp.sin(fr[0] * (feats @ fw1.astype(f32) + fb1.astype(f32)))
    h = jnp.sin(fr[1] * (h @ fw2.astype(f32) + fb2.astype(f32)))
    h = (h @ fw3.astype(f32)).reshape(L, HY_DIRS, HY_ORDER, HY_WIDTH)
    rate = jnp.abs(decay.astype(f32)).reshape(HY_DIRS, HY_ORDER, HY_WIDTH)
    h = h * (jnp.exp(-t[:, :, None, None] * rate) + HY_DECAY_SHIFT)
    hf, hb = h[:, 0], h[:, 1]
    k = jnp.concatenate([hf[:1] + hb[:1], hf[1:], jnp.zeros_like(hf[:1]), hb[1:][::-1]], axis=0)
    k = k / jnp.sum(jnp.abs(k), axis=0, keepdims=True)
    return jnp.fft.rfft(k, axis=0)


def bidir_long_conv(u, k_spec, d_skip):
    L = u.shape[1]
    uf = jnp.fft.rfft(u.astype(jnp.float32), n=2 * L, axis=1)
    y = jnp.fft.irfft(uf * k_spec[None], n=2 * L, axis=1)[:, :L]
    return (y + u.astype(jnp.float32) * d_skip.astype(jnp.float32)).astype(u.dtype)


def hyena_branch(z, p):
    L = z.shape[1]
    z = short_conv3(z, p['hy_conv_w'], p['hy_conv_b'])
    v, x1, x2 = jnp.split(z, 3, axis=-1)
    k_spec = hyena_filter_spectra(L, p['hy_fw1'], p['hy_fb1'], p['hy_fw2'], p['hy_fb2'],
                                  p['hy_fw3'], p['hy_freq'], p['hy_decay'])
    y = x1 * bidir_long_conv(v, k_spec[:, 0], p['hy_d'][0])
    return x2 * bidir_long_conv(y, k_spec[:, 1], p['hy_d'][1])


def _resident(shape, index_map):
    return pl.BlockSpec(shape, index_map, pipeline_mode=pl.Buffered(1))


def _merge_kernel(y0_ref, y1_ref, y2_ref, g_ref, wb_ref, o_ref):
    d = o_ref.shape[1]
    m = None
    for b, y_ref in enumerate((y0_ref, y1_ref, y2_ref)):
        t = g_ref[:, b * d:(b + 1) * d] * jnp.dot(y_ref[...], wb_ref[b], preferred_element_type=jnp.float32)
        m = t if m is None else m + t
    o_ref[...] = m.astype(o_ref.dtype)


def merge_gated(y_hy, y_wa, y_na, gates, w_branch):
    M, wbr = y_hy.shape
    D = w_branch.shape[2]
    tm = _pick(M, (256, 128, 64, 32, 16, 8))
    yspec = pl.BlockSpec((tm, wbr), lambda i: (i, 0))
    return pl.pallas_call(
        _merge_kernel,
        grid=(M // tm,),
        in_specs=[yspec, yspec, yspec, pl.BlockSpec((tm, N_BRANCH * D), lambda i: (i, 0)),
                  _resident((N_BRANCH, wbr, D), lambda i: (0, 0, 0))],
        out_specs=pl.BlockSpec((tm, D), lambda i: (i, 0)),
        out_shape=jax.ShapeDtypeStruct((M, D), jnp.bfloat16),
        compiler_params=pltpu.CompilerParams(
            dimension_semantics=("parallel",), vmem_limit_bytes=VMEM_LIMIT_BYTES),
        name="merge_gated",
    )(y_hy, y_wa, y_na, gates, w_branch)


ROUTER_PAD = 128


def _out_proj_kernel(m_ref, w_ref, x_ref, gate_ref, g2_ref, shift_ref, scale_ref, wr_hi_ref, wr_lo_ref,
                     x_out, h_out, lg_out):
    x = x_ref[...] + gate_ref[...] * jnp.dot(m_ref[...], w_ref[...], preferred_element_type=jnp.float32)
    x_out[...] = x
    h = x * lax.rsqrt(jnp.mean(x * x, axis=-1, keepdims=True) + RMS_EPS) * g2_ref[...]
    h = h * (1.0 + scale_ref[...]) + shift_ref[...]
    h_out[...] = h.astype(h_out.dtype)
    h_hi = h.astype(jnp.bfloat16)
    h_lo = (h - h_hi.astype(jnp.float32)).astype(jnp.bfloat16)
    lg_out[...] = (jnp.dot(h_hi, wr_hi_ref[...], preferred_element_type=jnp.float32)
                   + jnp.dot(h_lo, wr_hi_ref[...], preferred_element_type=jnp.float32)
                   + jnp.dot(h_hi, wr_lo_ref[...], preferred_element_type=jnp.float32))


def out_project(m, w_out, x, gate, norm_g, shift, scale, w_router):
    M, D = x.shape
    tm = _pick(M, (256, 128, 64, 32, 16, 8))
    row = pl.BlockSpec((tm, D), lambda i: (i, 0))
    vec = pl.BlockSpec((1, D), lambda i: (0, 0))
    wr = jnp.pad(w_router.astype(jnp.float32), ((0, 0), (0, ROUTER_PAD - w_router.shape[1])))
    wr_hi = wr.astype(jnp.bfloat16)
    wr_lo = (wr - wr_hi.astype(jnp.float32)).astype(jnp.bfloat16)
    v2 = lambda a: a.reshape(1, D).astype(jnp.float32)
    return pl.pallas_call(
        _out_proj_kernel,
        grid=(M // tm,),
        in_specs=[row, _resident((D, D), lambda i: (0, 0)), row, vec, vec, vec, vec,
                  _resident((D, ROUTER_PAD), lambda i: (0, 0)), _resident((D, ROUTER_PAD), lambda i: (0, 0))],
        out_specs=[row, row, pl.BlockSpec((tm, ROUTER_PAD), lambda i: (i, 0))],
        out_shape=[jax.ShapeDtypeStruct((M, D), jnp.float32), jax.ShapeDtypeStruct((M, D), jnp.bfloat16),
                   jax.ShapeDtypeStruct((M, ROUTER_PAD), jnp.float32)],
        compiler_params=pltpu.CompilerParams(
            dimension_semantics=("parallel",), vmem_limit_bytes=VMEM_LIMIT_BYTES),
        name="out_project",
    )(m, w_out, x, v2(gate), v2(norm_g), v2(shift), v2(scale), wr_hi, wr_lo)


def _gate_up_kernel(x_ref, wg_ref, wu_ref, o_ref, wg_bf, wu_bf):
    @pl.when(pl.program_id(2) == 0)
    def _():
        wg_bf[...] = wg_ref[...].astype(jnp.bfloat16)
        wu_bf[...] = wu_ref[...].astype(jnp.bfloat16)

    x = x_ref[0]
    a = jnp.dot(x, wg_bf[...], preferred_element_type=jnp.float32)
    u = jnp.dot(x, wu_bf[...], preferred_element_type=jnp.float32)
    o_ref[0] = (a * jax.nn.sigmoid(a) * u).astype(o_ref.dtype)


def _down_kernel(h_ref, wd_ref, gsel_ref, gate_ref, o_ref, wd_bf):
    @pl.when(pl.program_id(1) == 0)
    def _():
        wd_bf[...] = wd_ref[...].astype(jnp.bfloat16)

    y = jnp.dot(h_ref[0], wd_bf[...], preferred_element_type=jnp.float32)
    o_ref[0] = y * gsel_ref[0] * gate_ref[...]


def expert_ffn(xe, w_gate, w_up, w_down, layer, gsel, out_gate):
    E, cap, D = xe.shape
    F = w_gate.shape[3]
    tm = _pick(cap, (512, 256, 128, 64, 32, 16, 8))
    tn = _pick(F, (512, 256, 128))
    seq = pltpu.CompilerParams(dimension_semantics=("parallel", "parallel", "arbitrary"),
                               vmem_limit_bytes=VMEM_LIMIT_BYTES)
    h = pl.pallas_call(
        _gate_up_kernel,
        grid=(E, F // tn, cap // tm),
        in_specs=[pl.BlockSpec((1, tm, D), lambda e, j, i: (e, i, 0)),
                  pl.BlockSpec((None, None, D, tn), lambda e, j, i: (layer, e, 0, j)),
                  pl.BlockSpec((None, None, D, tn), lambda e, j, i: (layer, e, 0, j))],
        out_specs=pl.BlockSpec((1, tm, tn), lambda e, j, i: (e, i, j)),
        out_shape=jax.ShapeDtypeStruct((E, cap, F), jnp.bfloat16),
        scratch_shapes=[pltpu.VMEM((D, tn), jnp.bfloat16), pltpu.VMEM((D, tn), jnp.bfloat16)],
        compiler_params=seq,
        name="expert_gate_up",
    )(xe, w_gate, w_up)
    return pl.pallas_call(
        _down_kernel,
        grid=(E, cap // tm),
        in_specs=[pl.BlockSpec((1, tm, F), lambda e, i: (e, i, 0)),
                  pl.BlockSpec((None, None, F, D), lambda e, i: (layer, e, 0, 0)),
                  pl.BlockSpec((1, tm, 1), lambda e, i: (e, i, 0)),
                  pl.BlockSpec((1, D), lambda e, i: (0, 0))],
        out_specs=pl.BlockSpec((1, tm, D), lambda e, i: (e, i, 0)),
        out_shape=jax.ShapeDtypeStruct((E, cap, D), jnp.float32),
        scratch_shapes=[pltpu.VMEM((F, D), jnp.bfloat16)],
        compiler_params=pltpu.CompilerParams(dimension_semantics=("parallel", "arbitrary"),
                                             vmem_limit_bytes=VMEM_LIMIT_BYTES),
        name="expert_down",
    )(h, w_down, gsel.reshape(E, cap, 1).astype(jnp.float32), out_gate.reshape(1, D).astype(jnp.float32))


def expert_choice_latent(x1, h2, logits, out_gate, w_gate, w_up, w_down, layer):
    N, D = x1.shape
    cap = EC_CAPACITY_FACTOR * N // N_EXPERTS
    aff = jax.nn.softmax(logits[:, :N_EXPERTS], axis=-1)
    gsel, idx = lax.top_k(aff.T, cap)
    ye = expert_ffn(h2[idx], w_gate, w_up, w_down, layer, gsel, out_gate)
    return x1.at[idx.reshape(-1)].add(ye.reshape(-1, D))


def _ctx_attn_kernel(sink_ref, q_ref, k_ref, v_ref, o_ref, *, use_sink):
    s = lax.dot_general(q_ref[...], k_ref[...], _NT, preferred_element_type=jnp.float32)
    m = jnp.max(s, axis=-1, keepdims=True)
    if use_sink:
        sk = sink_ref[pl.program_id(0)]
        m = jnp.maximum(m, sk)
    p = jnp.exp(s - m)
    denom = jnp.sum(p, axis=-1, keepdims=True)
    if use_sink:
        denom = denom + jnp.exp(sk - m)
    o = jnp.dot(p.astype(v_ref.dtype), v_ref[...], preferred_element_type=jnp.float32)
    o_ref[...] = (o / denom).astype(o_ref.dtype)


def context_attention(q, k, v, sink, n_heads, n_kv_heads):
    Lc = q.shape[0]
    group = n_heads // n_kv_heads
    use_sink = sink is not None
    sink = jnp.zeros((n_heads,), jnp.float32) if sink is None else sink.astype(jnp.float32)
    kv = pl.BlockSpec((Lc, HEAD_DIM), lambda h, s: (0, h // group))
    return pl.pallas_call(
        partial(_ctx_attn_kernel, use_sink=use_sink),
        grid_spec=pltpu.PrefetchScalarGridSpec(
            num_scalar_prefetch=1,
            grid=(n_heads,),
            in_specs=[pl.BlockSpec((Lc, HEAD_DIM), lambda h, s: (0, h)), kv, kv],
            out_specs=pl.BlockSpec((Lc, HEAD_DIM), lambda h, s: (0, h)),
        ),
        out_shape=jax.ShapeDtypeStruct((Lc, n_heads * HEAD_DIM), jnp.bfloat16),
        compiler_params=pltpu.CompilerParams(
            dimension_semantics=("parallel",), vmem_limit_bytes=VMEM_LIMIT_BYTES),
        name="context_attention",
    )(sink, q, k, v)


def trunk_layer(x, ctx, c, c_ctx, p, stacked, layer, update_ctx, consts):
    L = x.shape[0]
    bf = jnp.bfloat16
    w_in = stacked['w_in']
    cond = jnp.pad(jnp.concatenate([jax.nn.silu(c), jax.nn.silu(c_ctx)[None]], axis=0), ((0, 6), (0, 0)))
    mod = matmul(cond, stacked['w_mod'], layer=layer) + p['b_mod']
    mx = jnp.split(mod[0], N_MOD)
    mc = jnp.split(mod[1], N_MOD)
    qk_scale = HEAD_DIM ** -0.5
    nwq = WA_HEADS * HEAD_DIM
    nkv = WA_KV_HEADS * HEAD_DIM
    nna = NA_HEADS * HEAD_DIM
    proj = lambda a, *args: project(a, w_in, layer, *args)

    def mix_and_ffn(res, y_hy, y_wa, y_na, gates, m_vec):
        m = merge_gated(y_hy, y_wa, y_na, gates, p['w_branch'].astype(bf))
        r1, h2, logits = out_project(m, p['w_out'].astype(bf), res, m_vec[2], p['norm2_g'], m_vec[3], m_vec[4],
                                     p['w_router'])
        return expert_choice_latent(r1, h2, logits, m_vec[5], stacked['w_gate'], stacked['w_up'],
                                    stacked['w_down'], layer)

    hcb = norm_modulate(ctx, p['norm1_g'], mc[0], mc[1])
    kc_wa = proj(hcb, OFF_WA_KV, nkv, "headnorm", bf, p['wa_k_norm'])
    vc_wa = proj(hcb, OFF_WA_KV + nkv, nkv, "plain", bf)
    kc_na = proj(hcb, OFF_NA_KV, nna, "headnorm", bf, p['na_k_norm'])
    vc_na = proj(hcb, OFF_NA_KV + nna, nna, "plain", bf)

    hxb = norm_modulate(x, p['norm1_g'], mx[0], mx[1])
    rope = rope_lane_tables(L)
    y_hy = hyena_latent(proj(hxb, OFF_HY, OFF_WA_Q - OFF_HY), p, consts)
    q_wa = proj(hxb, OFF_WA_Q, nwq, "headnorm", bf, p['wa_q_norm'], qk_scale, rope)
    k_wa = proj(hxb, OFF_WA_KV, nkv, "headnorm", bf, p['wa_k_norm'], 1.0, rope)
    v_wa = proj(hxb, OFF_WA_KV + nkv, nkv, "plain", bf)
    y_wa = windowed_attention(q_wa, k_wa, v_wa, kc_wa, vc_wa, p['wa_sink'])
    q_na = proj(hxb, OFF_NA_Q, nna, "headnorm", bf, p['na_q_norm'], qk_scale)
    k_na = proj(hxb, OFF_NA_KV, nna, "headnorm", bf, p['na_k_norm'])
    v_na = proj(hxb, OFF_NA_KV + nna, nna, "plain", bf)
    y_na = neighbourhood_attention(q_na, k_na, v_na, kc_na, vc_na, p['na_rpb'])
    gates = proj(hxb, OFF_GATE, N_BRANCH * D_MODEL, "sigmoid")
    x = mix_and_ffn(x, y_hy, y_wa, y_na, gates, mx)

    if update_ctx:
        yc_hy = hyena_branch(proj(hcb, OFF_HY, OFF_WA_Q - OFF_HY)[None], p)[0].astype(bf)
        qc_wa = proj(hcb, OFF_WA_Q, nwq, "headnorm", bf, p['wa_q_norm'], qk_scale)
        yc_wa = context_attention(qc_wa, kc_wa, vc_wa, p['wa_sink'], WA_HEADS, WA_KV_HEADS)
        qc_na = proj(hcb, OFF_NA_Q, nna, "headnorm", bf, p['na_q_norm'], qk_scale)
        yc_na = context_attention(qc_na, kc_na, vc_na, None, NA_HEADS, NA_HEADS)
        gates_c = proj(hcb, OFF_GATE, N_BRANCH * D_MODEL, "sigmoid")
        ctx = mix_and_ffn(ctx, yc_hy, yc_wa, yc_na, gates_c, mc)
    return x, ctx


def kernel(x, c, ctx, c_ctx, w_mod, b_mod, norm1_g, w_in, hy_conv_w, hy_conv_b, hy_fw1, hy_fb1, hy_fw2, hy_fb2, hy_fw3, hy_freq, hy_decay, hy_d, wa_q_norm, wa_k_norm, wa_sink, na_q_norm, na_k_norm, na_rpb, w_branch, w_out, norm2_g, w_router, w_gate, w_up, w_down):
    consts = dft_constants(x.shape[1])
    stacked = {'w_mod': w_mod, 'w_in': w_in, 'w_gate': w_gate, 'w_up': w_up, 'w_down': w_down}
    xs, cs = x[0], ctx[0]
    for l in range(DEPTH):
        p = {
            'b_mod': b_mod[l], 'norm1_g': norm1_g[l],
            'hy_conv_w': hy_conv_w[l], 'hy_conv_b': hy_conv_b[l], 'hy_fw1': hy_fw1[l], 'hy_fb1': hy_fb1[l],
            'hy_fw2': hy_fw2[l], 'hy_fb2': hy_fb2[l], 'hy_fw3': hy_fw3[l], 'hy_freq': hy_freq[l],
            'hy_decay': hy_decay[l], 'hy_d': hy_d[l], 'wa_q_norm': wa_q_norm[l], 'wa_k_norm': wa_k_norm[l],
            'wa_sink': wa_sink[l], 'na_q_norm': na_q_norm[l], 'na_k_norm': na_k_norm[l], 'na_rpb': na_rpb[l],
            'w_branch': w_branch[l], 'w_out': w_out[l], 'norm2_g': norm2_g[l], 'w_router': w_router[l],
        }
        xs, cs = trunk_layer(xs, cs, c, c_ctx, p, stacked, l, l < DEPTH - 1, consts)
    return xs[None]
```

```python
import math
from functools import partial

import jax
import jax.numpy as jnp
from jax import lax
from jax.experimental import pallas as pl
from jax.experimental.pallas import tpu as pltpu

D_MODEL = 2048
SEQ = 16384
DEPTH = 2
CTX_LEN = 256
GRID_W = 64
HEAD_DIM = 128
BRANCH_WIDTH = 1024
N_BRANCH = 3
N_MOD = 6
RMS_EPS = 1e-6
NEG_INF = -1e30

HY_WIDTH = BRANCH_WIDTH
HY_ORDER = 2
HY_DIRS = 2
HY_BANDS = 16
HY_DECAY_SHIFT = 0.05

WA_HEADS = BRANCH_WIDTH // HEAD_DIM
WA_KV_HEADS = 2
WA_WINDOW = 128
WA_BLOCK = 128

NA_HEADS = BRANCH_WIDTH // HEAD_DIM
NA_WIN_ROWS = 8
NA_WIN_COLS = 16

ROPE_BASE = 10000.0

N_EXPERTS = 16
EC_CAPACITY_FACTOR = 2
D_EXPERT = 1024

OFF_HY = 0
OFF_WA_Q = OFF_HY + 3 * HY_WIDTH
OFF_WA_KV = OFF_WA_Q + WA_HEADS * HEAD_DIM
OFF_NA_Q = OFF_WA_KV + 2 * WA_KV_HEADS * HEAD_DIM
OFF_NA_KV = OFF_NA_Q + NA_HEADS * HEAD_DIM
OFF_GATE = OFF_NA_KV + 2 * NA_HEADS * HEAD_DIM
N_IN = OFF_GATE + N_BRANCH * D_MODEL

VMEM_LIMIT_BYTES = 56 * 1024 * 1024


def _mm_kernel(a_ref, b_ref, o_ref):
    o_ref[...] = jnp.dot(a_ref[...].astype(jnp.bfloat16), b_ref[...].astype(jnp.bfloat16),
                         preferred_element_type=jnp.float32).astype(o_ref.dtype)


def _pick(n, pref):
    for t in pref:
        if n % t == 0:
            return t
    return n


def matmul(a, b, out_dtype=jnp.float32, layer=None):
    M, K = a.shape
    N = b.shape[-1]
    tm = _pick(M, (512, 256, 128, 64, 32, 16, 8))
    tn = _pick(N, (1024, 512, 256, 128))
    if layer is None:
        b_spec = pl.BlockSpec((K, tn), lambda j, i: (0, j))
    else:
        b_spec = pl.BlockSpec((None, K, tn), lambda j, i: (layer, 0, j))
    return pl.pallas_call(
        _mm_kernel,
        grid=(N // tn, M // tm),
        in_specs=[pl.BlockSpec((tm, K), lambda j, i: (i, 0)), b_spec],
        out_specs=pl.BlockSpec((tm, tn), lambda j, i: (i, j)),
        out_shape=jax.ShapeDtypeStruct((M, N), out_dtype),
        compiler_params=pltpu.CompilerParams(
            dimension_semantics=("parallel", "parallel"), vmem_limit_bytes=VMEM_LIMIT_BYTES),
        name="matmul",
    )(a, b)


def _norm_mod_kernel(x_ref, g_ref, shift_ref, scale_ref, o_ref):
    x = x_ref[...]
    y = x * lax.rsqrt(jnp.mean(x * x, axis=-1, keepdims=True) + RMS_EPS) * g_ref[...]
    o_ref[...] = (y * (1.0 + scale_ref[...]) + shift_ref[...]).astype(o_ref.dtype)


def norm_modulate(x, g, shift, scale, out_dtype=jnp.bfloat16):
    M, D = x.shape
    tm = _pick(M, (512, 256, 128, 64, 32, 16, 8))
    vec = pl.BlockSpec((1, D), lambda i: (0, 0))
    return pl.pallas_call(
        _norm_mod_kernel,
        grid=(M // tm,),
        in_specs=[pl.BlockSpec((tm, D), lambda i: (i, 0)), vec, vec, vec],
        out_specs=pl.BlockSpec((tm, D), lambda i: (i, 0)),
        out_shape=jax.ShapeDtypeStruct((M, D), out_dtype),
        compiler_params=pltpu.CompilerParams(
            dimension_semantics=("parallel",), vmem_limit_bytes=VMEM_LIMIT_BYTES),
        name="norm_modulate",
    )(x, g.reshape(1, D), shift.reshape(1, D), scale.reshape(1, D))


def _swap_halves(x):
    lane = lax.broadcasted_iota(jnp.int32, x.shape, 1)
    return jnp.where((lane % 64) < 32, pltpu.roll(x, 96, 1), pltpu.roll(x, 32, 1))


def _proj_kernel(*refs, mode, post_scale, rope):
    if mode == "headnorm":
        if rope:
            a_ref, w_ref, gain_ref, cos_ref, sin_ref, o_ref, w_bf = refs
        else:
            a_ref, w_ref, gain_ref, o_ref, w_bf = refs
    else:
        a_ref, w_ref, o_ref, w_bf = refs

    @pl.when(pl.program_id(1) == 0)
    def _():
        w_bf[...] = w_ref[...].astype(jnp.bfloat16)

    acc = jnp.dot(a_ref[...], w_bf[...], preferred_element_type=jnp.float32)
    if mode == "plain":
        o_ref[...] = acc.astype(o_ref.dtype)
    elif mode == "sigmoid":
        o_ref[...] = jax.nn.sigmoid(acc).astype(o_ref.dtype)
    else:
        gain = gain_ref[...] * post_scale
        for h in range(acc.shape[1] // HEAD_DIM):
            xh = acc[:, h * HEAD_DIM:(h + 1) * HEAD_DIM]
            y = xh * lax.rsqrt(jnp.mean(xh * xh, axis=-1, keepdims=True) + RMS_EPS) * gain
            if rope:
                y = y * cos_ref[...] + _swap_halves(y) * sin_ref[...]
            o_ref[:, h * HEAD_DIM:(h + 1) * HEAD_DIM] = y.astype(o_ref.dtype)


def project(a, w, layer, col_off, n_cols, mode="plain", out_dtype=jnp.float32, gain=None, post_scale=1.0,
            rope=None):
    M, K = a.shape
    tm = _pick(M, (1024, 512, 256, 128, 64, 32, 16, 8))
    tn = next(t for t in (1024, 768, 512, 256, 128) if n_cols % t == 0 and col_off % t == 0)
    off = col_off // tn
    in_specs = [pl.BlockSpec((tm, K), lambda j, i: (i, 0)),
                pl.BlockSpec((None, K, tn), lambda j, i: (layer, 0, off + j))]
    args = [a, w]
    if mode == "headnorm":
        in_specs.append(pl.BlockSpec((1, HEAD_DIM), lambda j, i: (0, 0)))
        args.append(gain.reshape(1, HEAD_DIM).astype(jnp.float32))
        if rope is not None:
            in_specs += [pl.BlockSpec((tm, HEAD_DIM), lambda j, i: (i, 0))] * 2
            args += list(rope)
    return pl.pallas_call(
        partial(_proj_kernel, mode=mode, post_scale=post_scale, rope=rope is not None),
        grid=(n_cols // tn, M // tm),
        in_specs=in_specs,
        out_specs=pl.BlockSpec((tm, tn), lambda j, i: (i, j)),
        out_shape=jax.ShapeDtypeStruct((M, n_cols), out_dtype),
        scratch_shapes=[pltpu.VMEM((K, tn), jnp.bfloat16)],
        compiler_params=pltpu.CompilerParams(
            dimension_semantics=("parallel", "arbitrary"), vmem_limit_bytes=VMEM_LIMIT_BYTES),
        name="project_" + mode,
    )(*args)


def rope_lane_tables(L):
    t = jnp.arange(L, dtype=jnp.int32)
    row = (t // GRID_W).astype(jnp.float32)
    col = (t % GRID_W).astype(jnp.float32)
    nf = HEAD_DIM // 4
    inv = ROPE_BASE ** (-jnp.arange(nf, dtype=jnp.float32) / nf)
    ar, ac = row[:, None] * inv, col[:, None] * inv
    cos = jnp.concatenate([jnp.cos(ar), jnp.cos(ar), jnp.cos(ac), jnp.cos(ac)], axis=-1)
    sin = jnp.concatenate([-jnp.sin(ar), jnp.sin(ar), -jnp.sin(ac), jnp.sin(ac)], axis=-1)
    return cos, sin


_NT = (((1,), (1,)), ((), ()))


def _wa_kernel(sink_ref, q_ref, k_ref, v_ref, kc_ref, vc_ref, o_ref, *, tq, seq):
    g = pl.program_id(0)
    i = pl.program_id(1)
    nwin = tq + 2 * WA_WINDOW
    ws = jnp.clip(i * tq - WA_WINDOW, 0, seq - nwin)
    start = pl.multiple_of(ws, WA_WINDOW)
    kwin = k_ref[pl.ds(start, nwin), :]
    vwin = v_ref[pl.ds(start, nwin), :]
    qpos = i * tq + lax.broadcasted_iota(jnp.int32, (tq, nwin), 0)
    kpos = ws + lax.broadcasted_iota(jnp.int32, (tq, nwin), 1)
    valid = jnp.abs(qpos - kpos) <= WA_WINDOW
    group = WA_HEADS // WA_KV_HEADS
    for hh in range(group):
        q = q_ref[:, hh * HEAD_DIM:(hh + 1) * HEAD_DIM]
        s = jnp.where(valid, lax.dot_general(q, kwin, _NT, preferred_element_type=jnp.float32), NEG_INF)
        sc = lax.dot_general(q, kc_ref[...], _NT, preferred_element_type=jnp.float32)
        sk = sink_ref[g * group + hh]
        m = jnp.maximum(jnp.maximum(jnp.max(s, axis=-1, keepdims=True), jnp.max(sc, axis=-1, keepdims=True)), sk)
        p = jnp.exp(s - m)
        pc = jnp.exp(sc - m)
        denom = jnp.sum(p, axis=-1, keepdims=True) + jnp.sum(pc, axis=-1, keepdims=True) + jnp.exp(sk - m)
        o = (jnp.dot(p.astype(vwin.dtype), vwin, preferred_element_type=jnp.float32)
             + jnp.dot(pc.astype(vwin.dtype), vc_ref[...], preferred_element_type=jnp.float32))
        o_ref[:, hh * HEAD_DIM:(hh + 1) * HEAD_DIM] = (o / denom).astype(o_ref.dtype)


def windowed_attention(q, k, v, kc, vc, sink, tq=256):
    L = q.shape[0]
    Lc = kc.shape[0]
    gw = (WA_HEADS // WA_KV_HEADS) * HEAD_DIM
    slab = pl.BlockSpec((L, HEAD_DIM), lambda g, i, s: (0, g))
    cslab = pl.BlockSpec((Lc, HEAD_DIM), lambda g, i, s: (0, g))
    return pl.pallas_call(
        partial(_wa_kernel, tq=tq, seq=L),
        grid_spec=pltpu.PrefetchScalarGridSpec(
            num_scalar_prefetch=1,
            grid=(WA_KV_HEADS, L // tq),
            in_specs=[pl.BlockSpec((tq, gw), lambda g, i, s: (i, g)), slab, slab, cslab, cslab],
            out_specs=pl.BlockSpec((tq, gw), lambda g, i, s: (i, g)),
        ),
        out_shape=jax.ShapeDtypeStruct((L, WA_HEADS * HEAD_DIM), jnp.bfloat16),
        compiler_params=pltpu.CompilerParams(
            dimension_semantics=("parallel", "parallel"), vmem_limit_bytes=VMEM_LIMIT_BYTES),
        name="windowed_attention",
    )(sink.astype(jnp.float32), q, k, v, kc, vc)


NA_ROW_BLOCK = 4
NA_KEY_ROWS = NA_ROW_BLOCK + NA_WIN_ROWS - 1


NA_HEAD_GROUP = 4


def _na_kernel(q_ref, k_ref, v_ref, kc_ref, vc_ref, bias_ref, o_ref, *, rows):
    i = pl.program_id(1)
    ws = jnp.clip(i * NA_ROW_BLOCK - NA_WIN_ROWS // 2, 0, rows - NA_KEY_ROWS)
    start = pl.multiple_of(ws * GRID_W, GRID_W)
    nk = NA_KEY_ROWS * GRID_W
    for hh in range(NA_HEAD_GROUP):
        cols = slice(hh * HEAD_DIM, (hh + 1) * HEAD_DIM)
        kwin = k_ref[pl.ds(start, nk), cols]
        vwin = v_ref[pl.ds(start, nk), cols]
        q = q_ref[:, cols]
        s = lax.dot_general(q, kwin, _NT, preferred_element_type=jnp.float32) + bias_ref[0, hh]
        sc = lax.dot_general(q, kc_ref[:, cols], _NT, preferred_element_type=jnp.float32)
        m = jnp.maximum(jnp.max(s, axis=-1, keepdims=True), jnp.max(sc, axis=-1, keepdims=True))
        p = jnp.exp(s - m)
        pc = jnp.exp(sc - m)
        denom = jnp.sum(p, axis=-1, keepdims=True) + jnp.sum(pc, axis=-1, keepdims=True)
        o = (jnp.dot(p.astype(vwin.dtype), vwin, preferred_element_type=jnp.float32)
             + jnp.dot(pc.astype(vwin.dtype), vc_ref[:, cols], preferred_element_type=jnp.float32))
        o_ref[:, cols] = (o / denom).astype(o_ref.dtype)


def na_bias_tiles(rpb, rows):
    col = jnp.arange(GRID_W)
    cstart = jnp.clip(col - NA_WIN_COLS // 2, 0, GRID_W - NA_WIN_COLS)
    col_in = (col[None, :] >= cstart[:, None]) & (col[None, :] < cstart[:, None] + NA_WIN_COLS)
    dc_idx = jnp.clip(col[None, :] - col[:, None] + NA_WIN_COLS - 1, 0, 2 * NA_WIN_COLS - 2)
    exact = lax.Precision.HIGHEST
    by_col = jnp.einsum('hrd,qkd->hrqk', rpb.astype(jnp.float32),
                        jax.nn.one_hot(dc_idx, 2 * NA_WIN_COLS - 1, dtype=jnp.float32), precision=exact)
    tiles = []
    for blk in (0, 1, rows // NA_ROW_BLOCK - 1):
        r = blk * NA_ROW_BLOCK
        ws = min(max(r - NA_WIN_ROWS // 2, 0), rows - NA_KEY_ROWS)
        qr = r + jnp.arange(NA_ROW_BLOCK)
        kr = ws + jnp.arange(NA_KEY_ROWS)
        r0 = jnp.clip(qr - NA_WIN_ROWS // 2, 0, rows - NA_WIN_ROWS)
        row_in = (kr[None, :] >= r0[:, None]) & (kr[None, :] < r0[:, None] + NA_WIN_ROWS)
        dr_idx = jnp.clip(kr[None, :] - qr[:, None] + NA_WIN_ROWS - 1, 0, 2 * NA_WIN_ROWS - 2)
        b = jnp.einsum('hrqk,abr->haqbk', by_col,
                       jax.nn.one_hot(dr_idx, 2 * NA_WIN_ROWS - 1, dtype=jnp.float32), precision=exact)
        ok = row_in[:, None, :, None] & col_in[None, :, None, :]
        b = jnp.where(ok[None], b, NEG_INF)
        tiles.append(b.reshape(rpb.shape[0], NA_ROW_BLOCK * GRID_W, NA_KEY_ROWS * GRID_W))
    return jnp.stack(tiles)


def neighbourhood_attention(q, k, v, kc, vc, rpb):
    L = q.shape[0]
    Lc = kc.shape[0]
    rows = L // GRID_W
    nblk = rows // NA_ROW_BLOCK
    tq = NA_ROW_BLOCK * GRID_W
    nk = NA_KEY_ROWS * GRID_W
    bias = na_bias_tiles(rpb, rows)
    gw = NA_HEAD_GROUP * HEAD_DIM
    slab = pl.BlockSpec((L, gw), lambda h, i: (0, h), pipeline_mode=pl.Buffered(1))
    cslab = pl.BlockSpec((Lc, gw), lambda h, i: (0, h))
    variant = lambda h, i: (jnp.where(i == 0, 0, jnp.where(i == nblk - 1, 2, 1)), h, 0, 0)
    return pl.pallas_call(
        partial(_na_kernel, rows=rows),
        grid=(NA_HEADS // NA_HEAD_GROUP, nblk),
        in_specs=[pl.BlockSpec((tq, gw), lambda h, i: (i, h)), slab, slab, cslab, cslab,
                  pl.BlockSpec((1, NA_HEAD_GROUP, tq, nk), variant)],
        out_specs=pl.BlockSpec((tq, gw), lambda h, i: (i, h)),
        out_shape=jax.ShapeDtypeStruct((L, NA_HEADS * HEAD_DIM), jnp.bfloat16),
        compiler_params=pltpu.CompilerParams(
            dimension_semantics=("parallel", "parallel"), vmem_limit_bytes=VMEM_LIMIT_BYTES),
        name="neighbourhood_attention",
    )(q, k, v, kc, vc, bias)


DFT_N1 = 128
DFT_K1 = DFT_N1 // 2 + 1
DFT_K1_PAD = 72


def dft_constants(L):
    N = 2 * L
    N2 = N // DFT_N1
    f32, bf = jnp.float32, jnp.bfloat16
    k1 = jnp.arange(DFT_K1, dtype=jnp.int32)
    n1 = jnp.arange(DFT_N1, dtype=jnp.int32)
    th = ((k1[:, None] * n1[None, :]) % DFT_N1).astype(f32) * (2.0 * math.pi / DFT_N1)
    pad = ((0, DFT_K1_PAD - DFT_K1), (0, 0))
    rows_fwd = jnp.concatenate([jnp.pad(jnp.cos(th), pad), jnp.pad(-jnp.sin(th), pad)], axis=0)
    ck = jnp.where((k1 == 0) | (k1 == DFT_N1 // 2), 1.0, 2.0)[:, None]
    half = DFT_N1 // 2
    rows_inv = jnp.concatenate([jnp.pad(ck * jnp.cos(th[:, :half]), pad),
                                jnp.pad(-ck * jnp.sin(th[:, :half]), pad)], axis=0).T
    n2 = jnp.arange(N2, dtype=jnp.int32)
    kk = k1[:, None] + DFT_N1 * n2[None, :]
    ang = ((kk[:, :, None] * n2[None, None, :]) % N).astype(f32) * (2.0 * math.pi / N)
    gr, gi = jnp.cos(ang), -jnp.sin(ang)
    mid_fwd = jnp.concatenate([jnp.concatenate([gr, -gi], axis=2),
                               jnp.concatenate([gi, gr], axis=2)], axis=1)
    return {
        "rows_fwd": rows_fwd.astype(bf), "rows_fwd_half": rows_fwd[:, :half].astype(bf),
        "rows_inv": rows_inv.astype(bf),
        "mid_fwd": mid_fwd.astype(bf), "mid_inv": jnp.swapaxes(mid_fwd, 1, 2).astype(bf),
    }


def _spectral_fwd_kernel(a_ref, g_ref, o_ref):
    n2 = a_ref.shape[2]
    x = (jnp.dot(g_ref[0, :, :n2], a_ref[0, 0], preferred_element_type=jnp.float32)
         + jnp.dot(g_ref[0, :, n2:], a_ref[1, 0], preferred_element_type=jnp.float32))
    o_ref[0, 0] = x[:n2].astype(o_ref.dtype)
    o_ref[1, 0] = x[n2:].astype(o_ref.dtype)


def spectral_fwd(a, mid_fwd, cb=2048):
    _, _, n2, C = a.shape
    cb = min(cb, C)
    return pl.pallas_call(
        _spectral_fwd_kernel,
        grid=(DFT_K1, C // cb),
        in_specs=[pl.BlockSpec((2, 1, n2, cb), lambda k, c: (0, k, 0, c)),
                  pl.BlockSpec((1, 2 * n2, 2 * n2), lambda k, c: (k, 0, 0))],
        out_specs=pl.BlockSpec((2, 1, n2, cb), lambda k, c: (0, k, 0, c)),
        out_shape=jax.ShapeDtypeStruct((2, DFT_K1, n2, C), jnp.bfloat16),
        compiler_params=pltpu.CompilerParams(
            dimension_semantics=("parallel", "parallel"), vmem_limit_bytes=VMEM_LIMIT_BYTES),
        name="spectral_fwd",
    )(a, mid_fwd)


def _spectral_mid_kernel(a_ref, gf_ref, gi_ref, ks_ref, o_ref):
    n2 = a_ref.shape[2]
    k1 = pl.program_id(0)

    @pl.when(k1 < DFT_K1)
    def _():
        x = (jnp.dot(gf_ref[0, :, :n2], a_ref[0, 0], preferred_element_type=jnp.float32)
             + jnp.dot(gf_ref[0, :, n2:], a_ref[1, 0], preferred_element_type=jnp.float32))
        xr, xi = x[:n2], x[n2:]
        kr, ki = ks_ref[0, 0].astype(jnp.float32), ks_ref[1, 0].astype(jnp.float32)
        yr = (xr * kr - xi * ki).astype(jnp.bfloat16)
        yi = (xr * ki + xi * kr).astype(jnp.bfloat16)
        b = (jnp.dot(gi_ref[0, :, :n2], yr, preferred_element_type=jnp.float32)
             + jnp.dot(gi_ref[0, :, n2:], yi, preferred_element_type=jnp.float32))
        o_ref[0, 0] = b[:n2].astype(o_ref.dtype)
        o_ref[1, 0] = b[n2:].astype(o_ref.dtype)

    @pl.when(k1 >= DFT_K1)
    def _():
        o_ref[...] = jnp.zeros_like(o_ref)


def spectral_mid(a, consts, kspec, col_off, cb=1024):
    _, _, n2, C = a.shape
    cb = min(cb, C)
    off = col_off // cb
    kc = lambda k: jnp.minimum(k, DFT_K1 - 1)
    return pl.pallas_call(
        _spectral_mid_kernel,
        grid=(DFT_K1_PAD, C // cb),
        in_specs=[pl.BlockSpec((2, 1, n2, cb), lambda k, c: (0, k, 0, c)),
                  pl.BlockSpec((1, 2 * n2, 2 * n2), lambda k, c: (kc(k), 0, 0)),
                  pl.BlockSpec((1, 2 * n2, 2 * n2), lambda k, c: (kc(k), 0, 0)),
                  pl.BlockSpec((2, 1, n2, cb), lambda k, c: (0, kc(k), 0, off + c))],
        out_specs=pl.BlockSpec((2, 1, n2, cb), lambda k, c: (0, k, 0, c)),
        out_shape=jax.ShapeDtypeStruct(a.shape, jnp.bfloat16),
        compiler_params=pltpu.CompilerParams(
            dimension_semantics=("parallel", "parallel"), vmem_limit_bytes=VMEM_LIMIT_BYTES),
        name="spectral_mid",
    )(a, consts["mid_fwd"], consts["mid_inv"], kspec)


DFT_ROW_GROUP = 16
DFT_ROW_LANES = 256


def _rows_fwd_kernel(f_ref, u_ref, o_ref):
    x = pltpu.einshape("abc->bac", u_ref[...].astype(jnp.float32))
    rs = [jnp.dot(f_ref[...], x[s].astype(jnp.bfloat16), preferred_element_type=jnp.float32)
          for s in range(x.shape[0])]
    o_ref[...] = pltpu.einshape("abc->bac", jnp.stack(rs)).astype(o_ref.dtype)


def dft_rows_fwd(rows_mat, u3):
    kn, n2, C = u3.shape
    grp = min(DFT_ROW_GROUP, n2)
    tc = min(DFT_ROW_LANES, C)
    m = rows_mat.shape[0]
    return pl.pallas_call(
        _rows_fwd_kernel,
        grid=(n2 // grp, C // tc),
        in_specs=[pl.BlockSpec((m, kn), lambda j, c: (0, 0)),
                  pl.BlockSpec((kn, grp, tc), lambda j, c: (0, j, c))],
        out_specs=pl.BlockSpec((m, grp, tc), lambda j, c: (0, j, c)),
        out_shape=jax.ShapeDtypeStruct((m, n2, C), jnp.bfloat16),
        compiler_params=pltpu.CompilerParams(
            dimension_semantics=("parallel", "parallel"), vmem_limit_bytes=VMEM_LIMIT_BYTES),
        name="dft_rows_fwd",
    )(rows_mat, u3)


def _rows_inv_kernel(f_ref, b_ref, u_ref, gate_ref, scale_ref, d_ref, o_ref):
    x = pltpu.einshape("abc->bac", b_ref[...].astype(jnp.float32))
    ys = [jnp.dot(f_ref[...], x[s].astype(jnp.bfloat16), preferred_element_type=jnp.float32)
          for s in range(x.shape[0])]
    y = pltpu.einshape("abc->bac", jnp.stack(ys))
    o_ref[...] = (gate_ref[...] * (y * scale_ref[...] + u_ref[...] * d_ref[...])).astype(o_ref.dtype)


def dft_rows_inv(rows_inv, b3, u3, gate3, scale, d_skip, out_dtype):
    nr, n2, C = u3.shape
    grp = min(DFT_ROW_GROUP, n2)
    tc = min(DFT_ROW_LANES, C)
    blk = pl.BlockSpec((nr, grp, tc), lambda j, c: (0, j, c))
    vec = pl.BlockSpec((1, 1, tc), lambda j, c: (0, 0, c))
    return pl.pallas_call(
        _rows_inv_kernel,
        grid=(n2 // grp, C // tc),
        in_specs=[pl.BlockSpec(rows_inv.shape, lambda j, c: (0, 0)),
                  pl.BlockSpec((b3.shape[0], grp, tc), lambda j, c: (0, j, c)), blk, blk, vec, vec],
        out_specs=blk,
        out_shape=jax.ShapeDtypeStruct((nr, n2, C), out_dtype),
        compiler_params=pltpu.CompilerParams(
            dimension_semantics=("parallel", "parallel"), vmem_limit_bytes=VMEM_LIMIT_BYTES),
        name="dft_rows_inv",
    )(rows_inv, b3, u3, gate3, scale.reshape(1, 1, C), d_skip.reshape(1, 1, C))


def _conv3_kernel(z_ref, prev_ref, next_ref, w_ref, b_ref, o_ref):
    i = pl.program_id(0)
    z = z_ref[...]
    tm = z.shape[0]
    row = lax.broadcasted_iota(jnp.int32, z.shape, 0)
    prev_row = jnp.where(i > 0, prev_ref[7:8, :], 0.0)
    next_row = jnp.where(i < pl.num_programs(0) - 1, next_ref[0:1, :], 0.0)
    zp = jnp.where(row == 0, prev_row, pltpu.roll(z, 1, 0))
    zn = jnp.where(row == tm - 1, next_row, pltpu.roll(z, tm - 1, 0))
    o_ref[...] = zp * w_ref[0:1, :] + z * w_ref[1:2, :] + zn * w_ref[2:3, :] + b_ref[...]


def short_conv3_part(z, w, b, part, width):
    L = z.shape[0]
    tm = _pick(L, (512, 256, 128, 64, 32, 16, 8))
    tc = _pick(width, (1024, 512, 256, 128))
    off = part * width // tc
    halo = 8
    return pl.pallas_call(
        _conv3_kernel,
        grid=(L // tm, width // tc),
        in_specs=[pl.BlockSpec((tm, tc), lambda i, j: (i, off + j)),
                  pl.BlockSpec((halo, tc), lambda i, j: (jnp.maximum(i * (tm // halo) - 1, 0), off + j)),
                  pl.BlockSpec((halo, tc), lambda i, j: (jnp.minimum((i + 1) * (tm // halo), L // halo - 1), off + j)),
                  pl.BlockSpec((3, tc), lambda i, j: (0, off + j)),
                  pl.BlockSpec((1, tc), lambda i, j: (0, off + j))],
        out_specs=pl.BlockSpec((tm, tc), lambda i, j: (i, j)),
        out_shape=jax.ShapeDtypeStruct((L, width), jnp.float32),
        compiler_params=pltpu.CompilerParams(
            dimension_semantics=("parallel", "parallel"), vmem_limit_bytes=VMEM_LIMIT_BYTES),
        name="short_conv3",
    )(z, z, z, w, b.reshape(1, -1))


FILTER_PAD = 128
FILTER_LANES = 512


def _filter_rows_kernel(feats_ref, feats0_ref, fw1_ref, fb1_ref, fw2_ref, fb2_ref, fr_ref, fw3_ref, rate_ref,
                        rows_ref, a_ref, asum_ref, hid_ref, hid0_ref):
    g = pl.program_id(0)
    exact = lax.Precision.HIGHEST
    n_dir, grp, half, nf = feats_ref.shape

    def hidden(feats):
        h = jnp.sin(fr_ref[0:1, :] * (jnp.dot(feats, fw1_ref[...], precision=exact,
                                              preferred_element_type=jnp.float32) + fb1_ref[...]))
        return jnp.sin(fr_ref[1:2, :] * (jnp.dot(h, fw2_ref[...], precision=exact,
                                                 preferred_element_type=jnp.float32) + fb2_ref[...]))

    @pl.when(pl.program_id(1) == 0)
    def _():
        for d in range(n_dir):
            hid_ref[d] = hidden(feats_ref[d].reshape(grp * half, nf))
        hid0_ref[...] = hidden(feats0_ref[...])

    def taps(h, t, d):
        k = jnp.dot(h.astype(jnp.bfloat16), fw3_ref[d].astype(jnp.bfloat16), preferred_element_type=jnp.float32)
        return k * (jnp.exp(-t * rate_ref[d]) + HY_DECAY_SHIFT)

    k_fwd = taps(hid_ref[0], feats_ref[0].reshape(grp * half, nf)[:, 0:1], 0)
    k_bwd = taps(hid_ref[1], feats_ref[1].reshape(grp * half, nf)[:, 0:1], 1)
    lag0_back = taps(hid0_ref[...], feats0_ref[:, 0:1], 1)[0:1, :]
    first = (lax.broadcasted_iota(jnp.int32, k_fwd.shape, 0) == 0) & (g == 0)
    k_fwd = k_fwd + jnp.where(first, lag0_back, 0.0)
    k_bwd = jnp.where(first, 0.0, k_bwd)
    asum_ref[0] = jnp.sum(jnp.abs(k_fwd), axis=0, keepdims=True) + jnp.sum(jnp.abs(k_bwd), axis=0, keepdims=True)
    kf = k_fwd.astype(jnp.bfloat16)
    kb = k_bwd.astype(jnp.bfloat16)
    rs = [jnp.dot(rows_ref[:, :half], kf[s * half:(s + 1) * half], preferred_element_type=jnp.float32)
          + jnp.dot(rows_ref[:, half:], kb[s * half:(s + 1) * half], preferred_element_type=jnp.float32)
          for s in range(grp)]
    a_ref[...] = pltpu.einshape("abc->bac", jnp.stack(rs)).astype(a_ref.dtype)


def hyena_filter_rows(L, p, rows_fwd):
    f32 = jnp.float32
    N = 2 * L
    n2 = N // DFT_N1
    half = DFT_N1 // 2
    C2 = HY_ORDER * HY_WIDTH
    grp = min(DFT_ROW_GROUP, n2)
    tc = min(FILTER_LANES, C2)
    n = (n2 * jnp.arange(DFT_N1, dtype=jnp.int32)[None, :] + jnp.arange(n2, dtype=jnp.int32)[:, None])
    pos = jnp.where(n < L, n, N - n).astype(f32)[..., None]
    bands = jnp.linspace(1e-4, HY_BANDS - 1, HY_BANDS, dtype=f32)
    ang = (2.0 * math.pi / L) * bands * pos
    feats = jnp.concatenate([pos / (L - 1), jnp.cos(ang), -jnp.sin(ang)], axis=-1)
    feats = jnp.pad(feats, ((0, 0), (0, 0), (0, FILTER_PAD - feats.shape[-1])))
    feats = feats.reshape(n2, 2, half, FILTER_PAD).transpose(1, 0, 2, 3)
    feats0 = jnp.broadcast_to(feats[0, 0, 0:1], (8, FILTER_PAD))
    hid = p['hy_fw2'].shape[0]
    ph = FILTER_PAD - hid
    fw1 = jnp.pad(p['hy_fw1'].astype(f32), ((0, FILTER_PAD - p['hy_fw1'].shape[0]), (0, ph)))
    fb1 = jnp.pad(p['hy_fb1'].astype(f32), (0, ph)).reshape(1, -1)
    fw2 = jnp.pad(p['hy_fw2'].astype(f32), ((0, ph), (0, ph)))
    fb2 = jnp.pad(p['hy_fb2'].astype(f32), (0, ph)).reshape(1, -1)
    fr = jnp.pad(p['hy_freq'].astype(f32), ((0, 0), (0, ph)))
    fw3 = jnp.pad(p['hy_fw3'].astype(f32), ((0, ph), (0, 0))).reshape(FILTER_PAD, HY_DIRS, C2).transpose(1, 0, 2)
    rate = jnp.abs(p['hy_decay'].astype(f32)).reshape(HY_DIRS, 1, C2)
    full = lambda shape: pl.BlockSpec(shape, lambda g, c: (0,) * len(shape))
    m = rows_fwd.shape[0]
    a, asum = pl.pallas_call(
        _filter_rows_kernel,
        grid=(n2 // grp, C2 // tc),
        in_specs=[pl.BlockSpec((2, grp, half, FILTER_PAD), lambda g, c: (0, g, 0, 0)), full((8, FILTER_PAD)),
                  full((FILTER_PAD, FILTER_PAD)), full((1, FILTER_PAD)),
                  full((FILTER_PAD, FILTER_PAD)), full((1, FILTER_PAD)), full((2, FILTER_PAD)),
                  pl.BlockSpec((HY_DIRS, FILTER_PAD, tc), lambda g, c: (0, 0, c)),
                  pl.BlockSpec((HY_DIRS, 1, tc), lambda g, c: (0, 0, c)),
                  full(rows_fwd.shape)],
        out_specs=[pl.BlockSpec((m, grp, tc), lambda g, c: (0, g, c)),
                   pl.BlockSpec((1, 1, tc), lambda g, c: (g, 0, c))],
        out_shape=[jax.ShapeDtypeStruct((m, n2, C2), jnp.bfloat16),
                   jax.ShapeDtypeStruct((n2 // grp, 1, C2), f32)],
        scratch_shapes=[pltpu.VMEM((2, grp * half, FILTER_PAD), f32), pltpu.VMEM((8, FILTER_PAD), f32)],
        compiler_params=pltpu.CompilerParams(
            dimension_semantics=("parallel", "arbitrary"), vmem_limit_bytes=VMEM_LIMIT_BYTES),
        name="hyena_filter_rows",
    )(feats, feats0, fw1, fb1, fw2, fb2, fr, fw3, rate, rows_fwd)
    return a, jnp.sum(asum, axis=(0, 1))


def hyena_latent(z, p, consts):
    L = z.shape[0]
    C = HY_WIDTH
    N = 2 * L
    n2 = N // DFT_N1
    half = DFT_N1 // 2
    v, x1, x2 = (short_conv3_part(z, p['hy_conv_w'], p['hy_conv_b'], i, C) for i in range(3))
    ka, asum = hyena_filter_rows(L, p, consts["rows_fwd"])
    kspec = spectral_fwd(ka.reshape(2, DFT_K1_PAD, n2, HY_ORDER * C), consts["mid_fwd"])
    scale = 1.0 / (N * asum)

    def long_conv(u, gate, order, out_dtype):
        u3 = u.reshape(half, n2, C)
        ua = dft_rows_fwd(consts["rows_fwd_half"], u3)
        bm = spectral_mid(ua.reshape(2, DFT_K1_PAD, n2, C), consts, kspec, order * C)
        y = dft_rows_inv(consts["rows_inv"], bm.reshape(2 * DFT_K1_PAD, n2, C), u3, gate.reshape(half, n2, C),
                         scale[order * C:(order + 1) * C], p['hy_d'][order].astype(jnp.float32), out_dtype)
        return y.reshape(L, C)

    y1 = long_conv(v, x1, 0, jnp.float32)
    return long_conv(y1, x2, 1, jnp.bfloat16)


def rms_norm(x, g):
    xf = x.astype(jnp.float32)
    y = xf * lax.rsqrt(jnp.mean(xf * xf, axis=-1, keepdims=True) + RMS_EPS)
    return (y * g.astype(jnp.float32)).astype(x.dtype)


def modulate(h, shift, scale):
    return h * (1 + scale) + shift


def short_conv3(z, w, b):
    L = z.shape[1]
    zp = jnp.pad(z, ((0, 0), (1, 1), (0, 0)))
    return zp[:, :L] * w[0] + zp[:, 1:L + 1] * w[1] + zp[:, 2:] * w[2] + b


def hyena_filter_spectra(L, fw1, fb1, fw2, fb2, fw3, freq, decay):
    f32 = jnp.float32
    t = jnp.linspace(0.0, 1.0, L, dtype=f32)[:, None]
    pos = jnp.arange(L, dtype=f32)[:, None]
    bands = jnp.linspace(1e-4, HY_BANDS - 1, HY_BANDS, dtype=f32)[None, :]
    ang = (2.0 * math.pi / L) * bands * pos
    feats = jnp.concatenate([t, jnp.cos(ang), -jnp.sin(ang)], axis=-1)
    fr = freq.astype(f32)
    h = jnp.sin(fr[0] * (feats @ fw1.astype(f32) + fb1.astype(f32)))
    h = jnp.sin(fr[1] * (h @ fw2.astype(f32) + fb2.astype(f32)))
    h = (h @ fw3.astype(f32)).reshape(L, HY_DIRS, HY_ORDER, HY_WIDTH)
    rate = jnp.abs(decay.astype(f32)).reshape(HY_DIRS, HY_ORDER, HY_WIDTH)
    h = h * (jnp.exp(-t[:, :, None, None] * rate) + HY_DECAY_SHIFT)
    hf, hb = h[:, 0], h[:, 1]
    k = jnp.concatenate([hf[:1] + hb[:1], hf[1:], jnp.zeros_like(hf[:1]), hb[1:][::-1]], axis=0)
    k = k / jnp.sum(jnp.abs(k), axis=0, keepdims=True)
    return jnp.fft.rfft(k, axis=0)


def bidir_long_conv(u, k_spec, d_skip):
    L = u.shape[1]
    uf = jnp.fft.rfft(u.astype(jnp.float32), n=2 * L, axis=1)
    y = jnp.fft.irfft(uf * k_spec[None], n=2 * L, axis=1)[:, :L]
    return (y + u.astype(jnp.float32) * d_skip.astype(jnp.float32)).astype(u.dtype)


def hyena_branch(z, p):
    L = z.shape[1]
    z = short_conv3(z, p['hy_conv_w'], p['hy_conv_b'])
    v, x1, x2 = jnp.split(z, 3, axis=-1)
    k_spec = hyena_filter_spectra(L, p['hy_fw1'], p['hy_fb1'], p['hy_fw2'], p['hy_fb2'],
                                  p['hy_fw3'], p['hy_freq'], p['hy_decay'])
    y = x1 * bidir_long_conv(v, k_spec[:, 0], p['hy_d'][0])
    return x2 * bidir_long_conv(y, k_spec[:, 1], p['hy_d'][1])


def _resident(shape, index_map):
    return pl.BlockSpec(shape, index_map, pipeline_mode=pl.Buffered(1))


def _merge_kernel(y0_ref, y1_ref, y2_ref, g_ref, wb_ref, o_ref):
    d = o_ref.shape[1]
    m = None
    for b, y_ref in enumerate((y0_ref, y1_ref, y2_ref)):
        t = g_ref[:, b * d:(b + 1) * d] * jnp.dot(y_ref[...], wb_ref[b], preferred_element_type=jnp.float32)
        m = t if m is None else m + t
    o_ref[...] = m.astype(o_ref.dtype)


def merge_gated(y_hy, y_wa, y_na, gates, w_branch):
    M, wbr = y_hy.shape
    D = w_branch.shape[2]
    tm = _pick(M, (256, 128, 64, 32, 16, 8))
    yspec = pl.BlockSpec((tm, wbr), lambda i: (i, 0))
    return pl.pallas_call(
        _merge_kernel,
        grid=(M // tm,),
        in_specs=[yspec, yspec, yspec, pl.BlockSpec((tm, N_BRANCH * D), lambda i: (i, 0)),
                  _resident((N_BRANCH, wbr, D), lambda i: (0, 0, 0))],
        out_specs=pl.BlockSpec((tm, D), lambda i: (i, 0)),
        out_shape=jax.ShapeDtypeStruct((M, D), jnp.bfloat16),
        compiler_params=pltpu.CompilerParams(
            dimension_semantics=("parallel",), vmem_limit_bytes=VMEM_LIMIT_BYTES),
        name="merge_gated",
    )(y_hy, y_wa, y_na, gates, w_branch)


ROUTER_PAD = 128


def _out_proj_kernel(m_ref, w_ref, x_ref, gate_ref, g2_ref, shift_ref, scale_ref, wr_hi_ref, wr_lo_ref,
                     x_out, h_out, lg_out):
    x = x_ref[...] + gate_ref[...] * jnp.dot(m_ref[...], w_ref[...], preferred_element_type=jnp.float32)
    x_out[...] = x
    h = x * lax.rsqrt(jnp.mean(x * x, axis=-1, keepdims=True) + RMS_EPS) * g2_ref[...]
    h = h * (1.0 + scale_ref[...]) + shift_ref[...]
    h_out[...] = h.astype(h_out.dtype)
    h_hi = h.astype(jnp.bfloat16)
    h_lo = (h - h_hi.astype(jnp.float32)).astype(jnp.bfloat16)
    lg_out[...] = (jnp.dot(h_hi, wr_hi_ref[...], preferred_element_type=jnp.float32)
                   + jnp.dot(h_lo, wr_hi_ref[...], preferred_element_type=jnp.float32)
                   + jnp.dot(h_hi, wr_lo_ref[...], preferred_element_type=jnp.float32))


def out_project(m, w_out, x, gate, norm_g, shift, scale, w_router):
    M, D = x.shape
    tm = _pick(M, (256, 128, 64, 32, 16, 8))
    row = pl.BlockSpec((tm, D), lambda i: (i, 0))
    vec = pl.BlockSpec((1, D), lambda i: (0, 0))
    wr = jnp.pad(w_router.astype(jnp.float32), ((0, 0), (0, ROUTER_PAD - w_router.shape[1])))
    wr_hi = wr.astype(jnp.bfloat16)
    wr_lo = (wr - wr_hi.astype(jnp.float32)).astype(jnp.bfloat16)
    v2 = lambda a: a.reshape(1, D).astype(jnp.float32)
    return pl.pallas_call(
        _out_proj_kernel,
        grid=(M // tm,),
        in_specs=[row, _resident((D, D), lambda i: (0, 0)), row, vec, vec, vec, vec,
                  _resident((D, ROUTER_PAD), lambda i: (0, 0)), _resident((D, ROUTER_PAD), lambda i: (0, 0))],
        out_specs=[row, row, pl.BlockSpec((tm, ROUTER_PAD), lambda i: (i, 0))],
        out_shape=[jax.ShapeDtypeStruct((M, D), jnp.float32), jax.ShapeDtypeStruct((M, D), jnp.bfloat16),
                   jax.ShapeDtypeStruct((M, ROUTER_PAD), jnp.float32)],
        compiler_params=pltpu.CompilerParams(
            dimension_semantics=("parallel",), vmem_limit_bytes=VMEM_LIMIT_BYTES),
        name="out_project",
    )(m, w_out, x, v2(gate), v2(norm_g), v2(shift), v2(scale), wr_hi, wr_lo)


def _gate_up_kernel(x_ref, wg_ref, wu_ref, o_ref, wg_bf, wu_bf):
    @pl.when(pl.program_id(2) == 0)
    def _():
        wg_bf[...] = wg_ref[...].astype(jnp.bfloat16)
        wu_bf[...] = wu_ref[...].astype(jnp.bfloat16)

    x = x_ref[0]
    a = jnp.dot(x, wg_bf[...], preferred_element_type=jnp.float32)
    u = jnp.dot(x, wu_bf[...], preferred_element_type=jnp.float32)
    o_ref[0] = (a * jax.nn.sigmoid(a) * u).astype(o_ref.dtype)


def _down_kernel(h_ref, wd_ref, gsel_ref, gate_ref, o_ref, wd_bf):
    @pl.when(pl.program_id(1) == 0)
    def _():
        wd_bf[...] = wd_ref[...].astype(jnp.bfloat16)

    y = jnp.dot(h_ref[0], wd_bf[...], preferred_element_type=jnp.float32)
    o_ref[0] = y * gsel_ref[0] * gate_ref[...]


def expert_ffn(xe, w_gate, w_up, w_down, layer, gsel, out_gate):
    E, cap, D = xe.shape
    F = w_gate.shape[3]
    tm = _pick(cap, (512, 256, 128, 64, 32, 16, 8))
    tn = _pick(F, (512, 256, 128))
    seq = pltpu.CompilerParams(dimension_semantics=("parallel", "parallel", "arbitrary"),
                               vmem_limit_bytes=VMEM_LIMIT_BYTES)
    h = pl.pallas_call(
        _gate_up_kernel,
        grid=(E, F // tn, cap // tm),
        in_specs=[pl.BlockSpec((1, tm, D), lambda e, j, i: (e, i, 0)),
                  pl.BlockSpec((None, None, D, tn), lambda e, j, i: (layer, e, 0, j)),
                  pl.BlockSpec((None, None, D, tn), lambda e, j, i: (layer, e, 0, j))],
        out_specs=pl.BlockSpec((1, tm, tn), lambda e, j, i: (e, i, j)),
        out_shape=jax.ShapeDtypeStruct((E, cap, F), jnp.bfloat16),
        scratch_shapes=[pltpu.VMEM((D, tn), jnp.bfloat16), pltpu.VMEM((D, tn), jnp.bfloat16)],
        compiler_params=seq,
        name="expert_gate_up",
    )(xe, w_gate, w_up)
    return pl.pallas_call(
        _down_kernel,
        grid=(E, cap // tm),
        in_specs=[pl.BlockSpec((1, tm, F), lambda e, i: (e, i, 0)),
                  pl.BlockSpec((None, None, F, D), lambda e, i: (layer, e, 0, 0)),
                  pl.BlockSpec((1, tm, 1), lambda e, i: (e, i, 0)),
                  pl.BlockSpec((1, D), lambda e, i: (0, 0))],
        out_specs=pl.BlockSpec((1, tm, D), lambda e, i: (e, i, 0)),
        out_shape=jax.ShapeDtypeStruct((E, cap, D), jnp.float32),
        scratch_shapes=[pltpu.VMEM((F, D), jnp.bfloat16)],
        compiler_params=pltpu.CompilerParams(dimension_semantics=("parallel", "arbitrary"),
                                             vmem_limit_bytes=VMEM_LIMIT_BYTES),
        name="expert_down",
    )(h, w_down, gsel.reshape(E, cap, 1).astype(jnp.float32), out_gate.reshape(1, D).astype(jnp.float32))


def expert_choice_latent(x1, h2, logits, out_gate, w_gate, w_up, w_down, layer):
    N, D = x1.shape
    cap = EC_CAPACITY_FACTOR * N // N_EXPERTS
    aff = jax.nn.softmax(logits[:, :N_EXPERTS], axis=-1)
    gsel, idx = lax.top_k(aff.T, cap)
    ye = expert_ffn(h2[idx], w_gate, w_up, w_down, layer, gsel, out_gate)
    return x1.at[idx.reshape(-1)].add(ye.reshape(-1, D))


def _ctx_attn_kernel(sink_ref, q_ref, k_ref, v_ref, o_ref, *, use_sink):
    s = lax.dot_general(q_ref[...], k_ref[...], _NT, preferred_element_type=jnp.float32)
    m = jnp.max(s, axis=-1, keepdims=True)
    if use_sink:
        sk = sink_ref[pl.program_id(0)]
        m = jnp.maximum(m, sk)
    p = jnp.exp(s - m)
    denom = jnp.sum(p, axis=-1, keepdims=True)
    if use_sink:
        denom = denom + jnp.exp(sk - m)
    o = jnp.dot(p.astype(v_ref.dtype), v_ref[...], preferred_element_type=jnp.float32)
    o_ref[...] = (o / denom).astype(o_ref.dtype)


def context_attention(q, k, v, sink, n_heads, n_kv_heads):
    Lc = q.shape[0]
    group = n_heads // n_kv_heads
    use_sink = sink is not None
    sink = jnp.zeros((n_heads,), jnp.float32) if sink is None else sink.astype(jnp.float32)
    kv = pl.BlockSpec((Lc, HEAD_DIM), lambda h, s: (0, h // group))
    return pl.pallas_call(
        partial(_ctx_attn_kernel, use_sink=use_sink),
        grid_spec=pltpu.PrefetchScalarGridSpec(
            num_scalar_prefetch=1,
            grid=(n_heads,),
            in_specs=[pl.BlockSpec((Lc, HEAD_DIM), lambda h, s: (0, h)), kv, kv],
            out_specs=pl.BlockSpec((Lc, HEAD_DIM), lambda h, s: (0, h)),
        ),
        out_shape=jax.ShapeDtypeStruct((Lc, n_heads * HEAD_DIM), jnp.bfloat16),
        compiler_params=pltpu.CompilerParams(
            dimension_semantics=("parallel",), vmem_limit_bytes=VMEM_LIMIT_BYTES),
        name="context_attention",
    )(sink, q, k, v)


def trunk_layer(x, ctx, c, c_ctx, p, stacked, layer, update_ctx, consts):
    L = x.shape[0]
    bf = jnp.bfloat16
    w_in = stacked['w_in']
    cond = jnp.pad(jnp.concatenate([jax.nn.silu(c), jax.nn.silu(c_ctx)[None]], axis=0), ((0, 6), (0, 0)))
    mod = matmul(cond, stacked['w_mod'], layer=layer) + p['b_mod']
    mx = jnp.split(mod[0], N_MOD)
    mc = jnp.split(mod[1], N_MOD)
    qk_scale = HEAD_DIM ** -0.5
    nwq = WA_HEADS * HEAD_DIM
    nkv = WA_KV_HEADS * HEAD_DIM
    nna = NA_HEADS * HEAD_DIM
    proj = lambda a, *args: project(a, w_in, layer, *args)

    def mix_and_ffn(res, y_hy, y_wa, y_na, gates, m_vec):
        m = merge_gated(y_hy, y_wa, y_na, gates, p['w_branch'].astype(bf))
        r1, h2, logits = out_project(m, p['w_out'].astype(bf), res, m_vec[2], p['norm2_g'], m_vec[3], m_vec[4],
                                     p['w_router'])
        return expert_choice_latent(r1, h2, logits, m_vec[5], stacked['w_gate'], stacked['w_up'],
                                    stacked['w_down'], layer)

    hcb = norm_modulate(ctx, p['norm1_g'], mc[0], mc[1])
    kc_wa = proj(hcb, OFF_WA_KV, nkv, "headnorm", bf, p['wa_k_norm'])
    vc_wa = proj(hcb, OFF_WA_KV + nkv, nkv, "plain", bf)
    kc_na = proj(hcb, OFF_NA_KV, nna, "headnorm", bf, p['na_k_norm'])
    vc_na = proj(hcb, OFF_NA_KV + nna, nna, "plain", bf)

    hxb = norm_modulate(x, p['norm1_g'], mx[0], mx[1])
    rope = rope_lane_tables(L)
    y_hy = hyena_latent(proj(hxb, OFF_HY, OFF_WA_Q - OFF_HY), p, consts)
    q_wa = proj(hxb, OFF_WA_Q, nwq, "headnorm", bf, p['wa_q_norm'], qk_scale, rope)
    k_wa = proj(hxb, OFF_WA_KV, nkv, "headnorm", bf, p['wa_k_norm'], 1.0, rope)
    v_wa = proj(hxb, OFF_WA_KV + nkv, nkv, "plain", bf)
    y_wa = windowed_attention(q_wa, k_wa, v_wa, kc_wa, vc_wa, p['wa_sink'])
    q_na = proj(hxb, OFF_NA_Q, nna, "headnorm", bf, p['na_q_norm'], qk_scale)
    k_na = proj(hxb, OFF_NA_KV, nna, "headnorm", bf, p['na_k_norm'])
    v_na = proj(hxb, OFF_NA_KV + nna, nna, "plain", bf)
    y_na = neighbourhood_attention(q_na, k_na, v_na, kc_na, vc_na, p['na_rpb'])
    gates = proj(hxb, OFF_GATE, N_BRANCH * D_MODEL, "sigmoid")
    x = mix_and_ffn(x, y_hy, y_wa, y_na, gates, mx)

    if update_ctx:
        yc_hy = hyena_branch(proj(hcb, OFF_HY, OFF_WA_Q - OFF_HY)[None], p)[0].astype(bf)
        qc_wa = proj(hcb, OFF_WA_Q, nwq, "headnorm", bf, p['wa_q_norm'], qk_scale)
        yc_wa = context_attention(qc_wa, kc_wa, vc_wa, p['wa_sink'], WA_HEADS, WA_KV_HEADS)
        qc_na = proj(hcb, OFF_NA_Q, nna, "headnorm", bf, p['na_q_norm'], qk_scale)
        yc_na = context_attention(qc_na, kc_na, vc_na, None, NA_HEADS, NA_HEADS)
        gates_c = proj(hcb, OFF_GATE, N_BRANCH * D_MODEL, "sigmoid")
        ctx = mix_and_ffn(ctx, yc_hy, yc_wa, yc_na, gates_c, mc)
    return x, ctx


def kernel(x, c, ctx, c_ctx, w_mod, b_mod, norm1_g, w_in, hy_conv_w, hy_conv_b, hy_fw1, hy_fb1, hy_fw2, hy_fb2, hy_fw3, hy_freq, hy_decay, hy_d, wa_q_norm, wa_k_norm, wa_sink, na_q_norm, na_k_norm, na_rpb, w_branch, w_out, norm2_g, w_router, w_gate, w_up, w_down):
    consts = dft_constants(x.shape[1])
    stacked = {'w_mod': w_mod, 'w_in': w_in, 'w_gate': w_gate, 'w_up': w_up, 'w_down': w_down}
    xs, cs = x[0], ctx[0]
    for l in range(DEPTH):
        p = {
            'b_mod': b_mod[l], 'norm1_g': norm1_g[l],
            'hy_conv_w': hy_conv_w[l], 'hy_conv_b': hy_conv_b[l], 'hy_fw1': hy_fw1[l], 'hy_fb1': hy_fb1[l],
            'hy_fw2': hy_fw2[l], 'hy_fb2': hy_fb2[l], 'hy_fw3': hy_fw3[l], 'hy_freq': hy_freq[l],
            'hy_decay': hy_decay[l], 'hy_d': hy_d[l], 'wa_q_norm': wa_q_norm[l], 'wa_k_norm': wa_k_norm[l],
            'wa_sink': wa_sink[l], 'na_q_norm': na_q_norm[l], 'na_k_norm': na_k_norm[l], 'na_rpb': na_rpb[l],
            'w_branch': w_branch[l], 'w_out': w_out[l], 'norm2_g': norm2_g[l], 'w_router': w_router[l],
        }
        xs, cs = trunk_layer(xs, cs, c, c_ctx, p, stacked, l, l < DEPTH - 1, consts)
    return xs[None]
```

```python
import math
from functools import partial

import jax
import jax.numpy as jnp
from jax import lax
from jax.experimental import pallas as pl
from jax.experimental.pallas import tpu as pltpu

D_MODEL = 2048
SEQ = 16384
DEPTH = 2
CTX_LEN = 256
GRID_W = 64
HEAD_DIM = 128
BRANCH_WIDTH = 1024
N_BRANCH = 3
N_MOD = 6
RMS_EPS = 1e-6
NEG_INF = -1e30

HY_WIDTH = BRANCH_WIDTH
HY_ORDER = 2
HY_DIRS = 2
HY_BANDS = 16
HY_DECAY_SHIFT = 0.05

WA_HEADS = BRANCH_WIDTH // HEAD_DIM
WA_KV_HEADS = 2
WA_WINDOW = 128
WA_BLOCK = 128

NA_HEADS = BRANCH_WIDTH // HEAD_DIM
NA_WIN_ROWS = 8
NA_WIN_COLS = 16

ROPE_BASE = 10000.0

N_EXPERTS = 16
EC_CAPACITY_FACTOR = 2
D_EXPERT = 1024

OFF_HY = 0
OFF_WA_Q = OFF_HY + 3 * HY_WIDTH
OFF_WA_KV = OFF_WA_Q + WA_HEADS * HEAD_DIM
OFF_NA_Q = OFF_WA_KV + 2 * WA_KV_HEADS * HEAD_DIM
OFF_NA_KV = OFF_NA_Q + NA_HEADS * HEAD_DIM
OFF_GATE = OFF_NA_KV + 2 * NA_HEADS * HEAD_DIM
N_IN = OFF_GATE + N_BRANCH * D_MODEL

VMEM_LIMIT_BYTES = 56 * 1024 * 1024


def _mm_kernel(a_ref, b_ref, o_ref):
    o_ref[...] = jnp.dot(a_ref[...].astype(jnp.bfloat16), b_ref[...].astype(jnp.bfloat16),
                         preferred_element_type=jnp.float32).astype(o_ref.dtype)


def _pick(n, pref):
    for t in pref:
        if n % t == 0:
            return t
    return n


def matmul(a, b, out_dtype=jnp.float32, layer=None):
    M, K = a.shape
    N = b.shape[-1]
    tm = _pick(M, (512, 256, 128, 64, 32, 16, 8))
    tn = _pick(N, (1024, 512, 256, 128))
    if layer is None:
        b_spec = pl.BlockSpec((K, tn), lambda j, i: (0, j))
    else:
        b_spec = pl.BlockSpec((None, K, tn), lambda j, i: (layer, 0, j))
    return pl.pallas_call(
        _mm_kernel,
        grid=(N // tn, M // tm),
        in_specs=[pl.BlockSpec((tm, K), lambda j, i: (i, 0)), b_spec],
        out_specs=pl.BlockSpec((tm, tn), lambda j, i: (i, j)),
        out_shape=jax.ShapeDtypeStruct((M, N), out_dtype),
        compiler_params=pltpu.CompilerParams(
            dimension_semantics=("parallel", "parallel"), vmem_limit_bytes=VMEM_LIMIT_BYTES),
        name="matmul",
    )(a, b)


def _norm_mod_kernel(x_ref, g_ref, shift_ref, scale_ref, o_ref):
    x = x_ref[...]
    y = x * lax.rsqrt(jnp.mean(x * x, axis=-1, keepdims=True) + RMS_EPS) * g_ref[...]
    o_ref[...] = (y * (1.0 + scale_ref[...]) + shift_ref[...]).astype(o_ref.dtype)


def norm_modulate(x, g, shift, scale, out_dtype=jnp.bfloat16):
    M, D = x.shape
    tm = _pick(M, (512, 256, 128, 64, 32, 16, 8))
    vec = pl.BlockSpec((1, D), lambda i: (0, 0))
    return pl.pallas_call(
        _norm_mod_kernel,
        grid=(M // tm,),
        in_specs=[pl.BlockSpec((tm, D), lambda i: (i, 0)), vec, vec, vec],
        out_specs=pl.BlockSpec((tm, D), lambda i: (i, 0)),
        out_shape=jax.ShapeDtypeStruct((M, D), out_dtype),
        compiler_params=pltpu.CompilerParams(
            dimension_semantics=("parallel",), vmem_limit_bytes=VMEM_LIMIT_BYTES),
        name="norm_modulate",
    )(x, g.reshape(1, D), shift.reshape(1, D), scale.reshape(1, D))


def _swap_halves(x):
    lane = lax.broadcasted_iota(jnp.int32, x.shape, 1)
    return jnp.where((lane % 64) < 32, pltpu.roll(x, 96, 1), pltpu.roll(x, 32, 1))


HEADNORM_ROWS = 256
HEADNORM_COLS = 256


def _proj_kernel(*refs, mode, post_scale, rope):
    if mode == "headnorm":
        if rope:
            a_ref, w_ref, gain_ref, cos_ref, sin_ref, o_ref, w_bf = refs
        else:
            a_ref, w_ref, gain_ref, o_ref, w_bf = refs
    else:
        a_ref, w_ref, o_ref, w_bf = refs

    @pl.when(pl.program_id(1) == 0)
    def _():
        w_bf[...] = w_ref[...].astype(jnp.bfloat16)

    if mode == "plain":
        o_ref[...] = jnp.dot(a_ref[...], w_bf[...], preferred_element_type=jnp.float32).astype(o_ref.dtype)
    elif mode == "sigmoid":
        acc = jnp.dot(a_ref[...], w_bf[...], preferred_element_type=jnp.float32)
        o_ref[...] = jax.nn.sigmoid(acc).astype(o_ref.dtype)
    else:
        gain = gain_ref[...] * post_scale
        tm, tn = o_ref.shape
        rows = min(tm, HEADNORM_ROWS) if rope else tm
        width = min(tn, HEADNORM_COLS) if rope else tn

        def chunk(r, carry):
            r0 = pl.multiple_of(r * rows, rows)
            a = a_ref[pl.ds(r0, rows), :]
            for c0 in range(0, tn, width):
                acc = jnp.dot(a, w_bf[:, c0:c0 + width], preferred_element_type=jnp.float32)
                for h in range(width // HEAD_DIM):
                    xh = acc[:, h * HEAD_DIM:(h + 1) * HEAD_DIM]
                    y = xh * lax.rsqrt(jnp.mean(xh * xh, axis=-1, keepdims=True) + RMS_EPS) * gain
                    if rope:
                        y = (y * cos_ref[pl.ds(r0, rows), :]
                             + _swap_halves(y) * sin_ref[pl.ds(r0, rows), :])
                    o_ref[pl.ds(r0, rows), c0 + h * HEAD_DIM:c0 + (h + 1) * HEAD_DIM] = y.astype(o_ref.dtype)
            return carry

        lax.fori_loop(0, tm // rows, chunk, 0)


def project(a, w, layer, col_off, n_cols, mode="plain", out_dtype=jnp.float32, gain=None, post_scale=1.0,
            rope=None):
    M, K = a.shape
    tm = _pick(M, (1024, 512, 256, 128, 64, 32, 16, 8))
    tn = next(t for t in (1024, 768, 512, 256, 128) if n_cols % t == 0 and col_off % t == 0)
    off = col_off // tn
    in_specs = [pl.BlockSpec((tm, K), lambda j, i: (i, 0)),
                pl.BlockSpec((None, K, tn), lambda j, i: (layer, 0, off + j))]
    args = [a, w]
    if mode == "headnorm":
        in_specs.append(pl.BlockSpec((1, HEAD_DIM), lambda j, i: (0, 0)))
        args.append(gain.reshape(1, HEAD_DIM).astype(jnp.float32))
        if rope is not None:
            in_specs += [pl.BlockSpec((tm, HEAD_DIM), lambda j, i: (i, 0))] * 2
            args += list(rope)
    return pl.pallas_call(
        partial(_proj_kernel, mode=mode, post_scale=post_scale, rope=rope is not None),
        grid=(n_cols // tn, M // tm),
        in_specs=in_specs,
        out_specs=pl.BlockSpec((tm, tn), lambda j, i: (i, j)),
        out_shape=jax.ShapeDtypeStruct((M, n_cols), out_dtype),
        scratch_shapes=[pltpu.VMEM((K, tn), jnp.bfloat16)],
        compiler_params=pltpu.CompilerParams(
            dimension_semantics=("parallel", "arbitrary"), vmem_limit_bytes=VMEM_LIMIT_BYTES),
        name="project_" + mode,
    )(*args)


def rope_lane_tables(L):
    t = jnp.arange(L, dtype=jnp.int32)
    row = (t // GRID_W).astype(jnp.float32)
    col = (t % GRID_W).astype(jnp.float32)
    nf = HEAD_DIM // 4
    inv = ROPE_BASE ** (-jnp.arange(nf, dtype=jnp.float32) / nf)
    ar, ac = row[:, None] * inv, col[:, None] * inv
    cos = jnp.concatenate([jnp.cos(ar), jnp.cos(ar), jnp.cos(ac), jnp.cos(ac)], axis=-1)
    sin = jnp.concatenate([-jnp.sin(ar), jnp.sin(ar), -jnp.sin(ac), jnp.sin(ac)], axis=-1)
    return cos, sin


_NT = (((1,), (1,)), ((), ()))


def _wa_kernel(sink_ref, q_ref, k_ref, v_ref, kc_ref, vc_ref, o_ref, *, tq, seq):
    g = pl.program_id(0)
    i = pl.program_id(1)
    nwin = tq + 2 * WA_WINDOW
    ws = jnp.clip(i * tq - WA_WINDOW, 0, seq - nwin)
    start = pl.multiple_of(ws, WA_WINDOW)
    kwin = k_ref[pl.ds(start, nwin), :]
    vwin = v_ref[pl.ds(start, nwin), :]
    qpos = i * tq + lax.broadcasted_iota(jnp.int32, (tq, nwin), 0)
    kpos = ws + lax.broadcasted_iota(jnp.int32, (tq, nwin), 1)
    valid = jnp.abs(qpos - kpos) <= WA_WINDOW
    group = WA_HEADS // WA_KV_HEADS
    for hh in range(group):
        q = q_ref[:, hh * HEAD_DIM:(hh + 1) * HEAD_DIM]
        s = jnp.where(valid, lax.dot_general(q, kwin, _NT, preferred_element_type=jnp.float32), NEG_INF)
        sc = lax.dot_general(q, kc_ref[...], _NT, preferred_element_type=jnp.float32)
        sk = sink_ref[g * group + hh]
        m = jnp.maximum(jnp.maximum(jnp.max(s, axis=-1, keepdims=True), jnp.max(sc, axis=-1, keepdims=True)), sk)
        p = jnp.exp(s - m)
        pc = jnp.exp(sc - m)
        denom = jnp.sum(p, axis=-1, keepdims=True) + jnp.sum(pc, axis=-1, keepdims=True) + jnp.exp(sk - m)
        o = (jnp.dot(p.astype(vwin.dtype), vwin, preferred_element_type=jnp.float32)
             + jnp.dot(pc.astype(vwin.dtype), vc_ref[...], preferred_element_type=jnp.float32))
        o_ref[:, hh * HEAD_DIM:(hh + 1) * HEAD_DIM] = (o / denom).astype(o_ref.dtype)


def windowed_attention(q, k, v, kc, vc, sink, tq=256):
    L = q.shape[0]
    Lc = kc.shape[0]
    gw = (WA_HEADS // WA_KV_HEADS) * HEAD_DIM
    slab = pl.BlockSpec((L, HEAD_DIM), lambda g, i, s: (0, g))
    cslab = pl.BlockSpec((Lc, HEAD_DIM), lambda g, i, s: (0, g))
    return pl.pallas_call(
        partial(_wa_kernel, tq=tq, seq=L),
        grid_spec=pltpu.PrefetchScalarGridSpec(
            num_scalar_prefetch=1,
            grid=(WA_KV_HEADS, L // tq),
            in_specs=[pl.BlockSpec((tq, gw), lambda g, i, s: (i, g)), slab, slab, cslab, cslab],
            out_specs=pl.BlockSpec((tq, gw), lambda g, i, s: (i, g)),
        ),
        out_shape=jax.ShapeDtypeStruct((L, WA_HEADS * HEAD_DIM), jnp.bfloat16),
        compiler_params=pltpu.CompilerParams(
            dimension_semantics=("parallel", "parallel"), vmem_limit_bytes=VMEM_LIMIT_BYTES),
        name="windowed_attention",
    )(sink.astype(jnp.float32), q, k, v, kc, vc)


NA_ROW_BLOCK = 4
NA_KEY_ROWS = NA_ROW_BLOCK + NA_WIN_ROWS - 1


NA_HEAD_GROUP = 4


def _na_kernel(q_ref, k_ref, v_ref, kc_ref, vc_ref, bias_ref, o_ref, *, rows):
    i = pl.program_id(1)
    ws = jnp.clip(i * NA_ROW_BLOCK - NA_WIN_ROWS // 2, 0, rows - NA_KEY_ROWS)
    start = pl.multiple_of(ws * GRID_W, GRID_W)
    nk = NA_KEY_ROWS * GRID_W
    for hh in range(NA_HEAD_GROUP):
        cols = slice(hh * HEAD_DIM, (hh + 1) * HEAD_DIM)
        kwin = k_ref[pl.ds(start, nk), cols]
        vwin = v_ref[pl.ds(start, nk), cols]
        q = q_ref[:, cols]
        s = lax.dot_general(q, kwin, _NT, preferred_element_type=jnp.float32) + bias_ref[0, hh]
        sc = lax.dot_general(q, kc_ref[:, cols], _NT, preferred_element_type=jnp.float32)
        m = jnp.maximum(jnp.max(s, axis=-1, keepdims=True), jnp.max(sc, axis=-1, keepdims=True))
        p = jnp.exp(s - m)
        pc = jnp.exp(sc - m)
        denom = jnp.sum(p, axis=-1, keepdims=True) + jnp.sum(pc, axis=-1, keepdims=True)
        o = (jnp.dot(p.astype(vwin.dtype), vwin, preferred_element_type=jnp.float32)
             + jnp.dot(pc.astype(vwin.dtype), vc_ref[:, cols], preferred_element_type=jnp.float32))
        o_ref[:, cols] = (o / denom).astype(o_ref.dtype)


def na_bias_tiles(rpb, rows):
    col = jnp.arange(GRID_W)
    cstart = jnp.clip(col - NA_WIN_COLS // 2, 0, GRID_W - NA_WIN_COLS)
    col_in = (col[None, :] >= cstart[:, None]) & (col[None, :] < cstart[:, None] + NA_WIN_COLS)
    dc_idx = jnp.clip(col[None, :] - col[:, None] + NA_WIN_COLS - 1, 0, 2 * NA_WIN_COLS - 2)
    exact = lax.Precision.HIGHEST
    by_col = jnp.einsum('hrd,qkd->hrqk', rpb.astype(jnp.float32),
                        jax.nn.one_hot(dc_idx, 2 * NA_WIN_COLS - 1, dtype=jnp.float32), precision=exact)
    tiles = []
    for blk in (0, 1, rows // NA_ROW_BLOCK - 1):
        r = blk * NA_ROW_BLOCK
        ws = min(max(r - NA_WIN_ROWS // 2, 0), rows - NA_KEY_ROWS)
        qr = r + jnp.arange(NA_ROW_BLOCK)
        kr = ws + jnp.arange(NA_KEY_ROWS)
        r0 = jnp.clip(qr - NA_WIN_ROWS // 2, 0, rows - NA_WIN_ROWS)
        row_in = (kr[None, :] >= r0[:, None]) & (kr[None, :] < r0[:, None] + NA_WIN_ROWS)
        dr_idx = jnp.clip(kr[None, :] - qr[:, None] + NA_WIN_ROWS - 1, 0, 2 * NA_WIN_ROWS - 2)
        b = jnp.einsum('hrqk,abr->haqbk', by_col,
                       jax.nn.one_hot(dr_idx, 2 * NA_WIN_ROWS - 1, dtype=jnp.float32), precision=exact)
        ok = row_in[:, None, :, None] & col_in[None, :, None, :]
        b = jnp.where(ok[None], b, NEG_INF)
        tiles.append(b.reshape(rpb.shape[0], NA_ROW_BLOCK * GRID_W, NA_KEY_ROWS * GRID_W))
    return jnp.stack(tiles)


def neighbourhood_attention(q, k, v, kc, vc, rpb):
    L = q.shape[0]
    Lc = kc.shape[0]
    rows = L // GRID_W
    nblk = rows // NA_ROW_BLOCK
    tq = NA_ROW_BLOCK * GRID_W
    nk = NA_KEY_ROWS * GRID_W
    bias = na_bias_tiles(rpb, rows)
    gw = NA_HEAD_GROUP * HEAD_DIM
    slab = pl.BlockSpec((L, gw), lambda h, i: (0, h), pipeline_mode=pl.Buffered(1))
    cslab = pl.BlockSpec((Lc, gw), lambda h, i: (0, h))
    variant = lambda h, i: (jnp.where(i == 0, 0, jnp.where(i == nblk - 1, 2, 1)), h, 0, 0)
    return pl.pallas_call(
        partial(_na_kernel, rows=rows),
        grid=(NA_HEADS // NA_HEAD_GROUP, nblk),
        in_specs=[pl.BlockSpec((tq, gw), lambda h, i: (i, h)), slab, slab, cslab, cslab,
                  pl.BlockSpec((1, NA_HEAD_GROUP, tq, nk), variant)],
        out_specs=pl.BlockSpec((tq, gw), lambda h, i: (i, h)),
        out_shape=jax.ShapeDtypeStruct((L, NA_HEADS * HEAD_DIM), jnp.bfloat16),
        compiler_params=pltpu.CompilerParams(
            dimension_semantics=("parallel", "parallel"), vmem_limit_bytes=VMEM_LIMIT_BYTES),
        name="neighbourhood_attention",
    )(q, k, v, kc, vc, bias)


DFT_N1 = 128
DFT_K1 = DFT_N1 // 2 + 1
DFT_K1_PAD = 72


def dft_constants(L):
    N = 2 * L
    N2 = N // DFT_N1
    f32, bf = jnp.float32, jnp.bfloat16
    k1 = jnp.arange(DFT_K1, dtype=jnp.int32)
    n1 = jnp.arange(DFT_N1, dtype=jnp.int32)
    th = ((k1[:, None] * n1[None, :]) % DFT_N1).astype(f32) * (2.0 * math.pi / DFT_N1)
    pad = ((0, DFT_K1_PAD - DFT_K1), (0, 0))
    rows_fwd = jnp.concatenate([jnp.pad(jnp.cos(th), pad), jnp.pad(-jnp.sin(th), pad)], axis=0)
    ck = jnp.where((k1 == 0) | (k1 == DFT_N1 // 2), 1.0, 2.0)[:, None]
    half = DFT_N1 // 2
    rows_inv = jnp.concatenate([jnp.pad(ck * jnp.cos(th[:, :half]), pad),
                                jnp.pad(-ck * jnp.sin(th[:, :half]), pad)], axis=0).T
    n2 = jnp.arange(N2, dtype=jnp.int32)
    kk = k1[:, None] + DFT_N1 * n2[None, :]
    ang = ((kk[:, :, None] * n2[None, None, :]) % N).astype(f32) * (2.0 * math.pi / N)
    gr, gi = jnp.cos(ang), -jnp.sin(ang)
    mid_fwd = jnp.concatenate([jnp.concatenate([gr, -gi], axis=2),
                               jnp.concatenate([gi, gr], axis=2)], axis=1)
    return {
        "rows_fwd": rows_fwd.astype(bf), "rows_fwd_half": rows_fwd[:, :half].astype(bf),
        "rows_inv": rows_inv.astype(bf),
        "mid_fwd": mid_fwd.astype(bf), "mid_inv": jnp.swapaxes(mid_fwd, 1, 2).astype(bf),
    }


def _spectral_fwd_kernel(a_ref, g_ref, o_ref):
    n2 = a_ref.shape[2]
    x = (jnp.dot(g_ref[0, :, :n2], a_ref[0, 0], preferred_element_type=jnp.float32)
         + jnp.dot(g_ref[0, :, n2:], a_ref[1, 0], preferred_element_type=jnp.float32))
    o_ref[0, 0] = x[:n2].astype(o_ref.dtype)
    o_ref[1, 0] = x[n2:].astype(o_ref.dtype)


def spectral_fwd(a, mid_fwd, cb=2048):
    _, _, n2, C = a.shape
    cb = min(cb, C)
    return pl.pallas_call(
        _spectral_fwd_kernel,
        grid=(DFT_K1, C // cb),
        in_specs=[pl.BlockSpec((2, 1, n2, cb), lambda k, c: (0, k, 0, c)),
                  pl.BlockSpec((1, 2 * n2, 2 * n2), lambda k, c: (k, 0, 0))],
        out_specs=pl.BlockSpec((2, 1, n2, cb), lambda k, c: (0, k, 0, c)),
        out_shape=jax.ShapeDtypeStruct((2, DFT_K1, n2, C), jnp.bfloat16),
        compiler_params=pltpu.CompilerParams(
            dimension_semantics=("parallel", "parallel"), vmem_limit_bytes=VMEM_LIMIT_BYTES),
        name="spectral_fwd",
    )(a, mid_fwd)


def _spectral_mid_kernel(a_ref, gf_ref, gi_ref, ks_ref, o_ref):
    n2 = a_ref.shape[2]
    k1 = pl.program_id(0)

    @pl.when(k1 < DFT_K1)
    def _():
        x = (jnp.dot(gf_ref[0, :, :n2], a_ref[0, 0], preferred_element_type=jnp.float32)
             + jnp.dot(gf_ref[0, :, n2:], a_ref[1, 0], preferred_element_type=jnp.float32))
        xr, xi = x[:n2], x[n2:]
        kr, ki = ks_ref[0, 0].astype(jnp.float32), ks_ref[1, 0].astype(jnp.float32)
        yr = (xr * kr - xi * ki).astype(jnp.bfloat16)
        yi = (xr * ki + xi * kr).astype(jnp.bfloat16)
        b = (jnp.dot(gi_ref[0, :, :n2], yr, preferred_element_type=jnp.float32)
             + jnp.dot(gi_ref[0, :, n2:], yi, preferred_element_type=jnp.float32))
        o_ref[0, 0] = b[:n2].astype(o_ref.dtype)
        o_ref[1, 0] = b[n2:].astype(o_ref.dtype)

    @pl.when(k1 >= DFT_K1)
    def _():
        o_ref[...] = jnp.zeros_like(o_ref)


def spectral_mid(a, consts, kspec, col_off, cb=1024):
    _, _, n2, C = a.shape
    cb = min(cb, C)
    off = col_off // cb
    kc = lambda k: jnp.minimum(k, DFT_K1 - 1)
    return pl.pallas_call(
        _spectral_mid_kernel,
        grid=(DFT_K1_PAD, C // cb),
        in_specs=[pl.BlockSpec((2, 1, n2, cb), lambda k, c: (0, k, 0, c)),
                  pl.BlockSpec((1, 2 * n2, 2 * n2), lambda k, c: (kc(k), 0, 0)),
                  pl.BlockSpec((1, 2 * n2, 2 * n2), lambda k, c: (kc(k), 0, 0)),
                  pl.BlockSpec((2, 1, n2, cb), lambda k, c: (0, kc(k), 0, off + c))],
        out_specs=pl.BlockSpec((2, 1, n2, cb), lambda k, c: (0, k, 0, c)),
        out_shape=jax.ShapeDtypeStruct(a.shape, jnp.bfloat16),
        compiler_params=pltpu.CompilerParams(
            dimension_semantics=("parallel", "parallel"), vmem_limit_bytes=VMEM_LIMIT_BYTES),
        name="spectral_mid",
    )(a, consts["mid_fwd"], consts["mid_inv"], kspec)


DFT_ROW_GROUP = 16
DFT_ROW_LANES = 256


def _rows_fwd_kernel(f_ref, u_ref, o_ref):
    x = pltpu.einshape("abc->bac", u_ref[...].astype(jnp.float32))
    rs = [jnp.dot(f_ref[...], x[s].astype(jnp.bfloat16), preferred_element_type=jnp.float32)
          for s in range(x.shape[0])]
    o_ref[...] = pltpu.einshape("abc->bac", jnp.stack(rs)).astype(o_ref.dtype)


def dft_rows_fwd(rows_mat, u3):
    kn, n2, C = u3.shape
    grp = min(DFT_ROW_GROUP, n2)
    tc = min(DFT_ROW_LANES, C)
    m = rows_mat.shape[0]
    return pl.pallas_call(
        _rows_fwd_kernel,
        grid=(n2 // grp, C // tc),
        in_specs=[pl.BlockSpec((m, kn), lambda j, c: (0, 0)),
                  pl.BlockSpec((kn, grp, tc), lambda j, c: (0, j, c))],
        out_specs=pl.BlockSpec((m, grp, tc), lambda j, c: (0, j, c)),
        out_shape=jax.ShapeDtypeStruct((m, n2, C), jnp.bfloat16),
        compiler_params=pltpu.CompilerParams(
            dimension_semantics=("parallel", "parallel"), vmem_limit_bytes=VMEM_LIMIT_BYTES),
        name="dft_rows_fwd",
    )(rows_mat, u3)


def _rows_inv_kernel(f_ref, b_ref, u_ref, gate_ref, scale_ref, d_ref, o_ref):
    x = pltpu.einshape("abc->bac", b_ref[...].astype(jnp.float32))
    ys = [jnp.dot(f_ref[...], x[s].astype(jnp.bfloat16), preferred_element_type=jnp.float32)
          for s in range(x.shape[0])]
    y = pltpu.einshape("abc->bac", jnp.stack(ys))
    o_ref[...] = (gate_ref[...] * (y * scale_ref[...] + u_ref[...] * d_ref[...])).astype(o_ref.dtype)


def dft_rows_inv(rows_inv, b3, u3, gate3, scale, d_skip, out_dtype):
    nr, n2, C = u3.shape
    grp = min(DFT_ROW_GROUP, n2)
    tc = min(DFT_ROW_LANES, C)
    blk = pl.BlockSpec((nr, grp, tc), lambda j, c: (0, j, c))
    vec = pl.BlockSpec((1, 1, tc), lambda j, c: (0, 0, c))
    return pl.pallas_call(
        _rows_inv_kernel,
        grid=(n2 // grp, C // tc),
        in_specs=[pl.BlockSpec(rows_inv.shape, lambda j, c: (0, 0)),
                  pl.BlockSpec((b3.shape[0], grp, tc), lambda j, c: (0, j, c)), blk, blk, vec, vec],
        out_specs=blk,
        out_shape=jax.ShapeDtypeStruct((nr, n2, C), out_dtype),
        compiler_params=pltpu.CompilerParams(
            dimension_semantics=("parallel", "parallel"), vmem_limit_bytes=VMEM_LIMIT_BYTES),
        name="dft_rows_inv",
    )(rows_inv, b3, u3, gate3, scale.reshape(1, 1, C), d_skip.reshape(1, 1, C))


def _conv3_kernel(z_ref, prev_ref, next_ref, w_ref, b_ref, o_ref):
    i = pl.program_id(0)
    z = z_ref[...]
    tm = z.shape[0]
    row = lax.broadcasted_iota(jnp.int32, z.shape, 0)
    prev_row = jnp.where(i > 0, prev_ref[7:8, :], 0.0)
    next_row = jnp.where(i < pl.num_programs(0) - 1, next_ref[0:1, :], 0.0)
    zp = jnp.where(row == 0, prev_row, pltpu.roll(z, 1, 0))
    zn = jnp.where(row == tm - 1, next_row, pltpu.roll(z, tm - 1, 0))
    o_ref[...] = zp * w_ref[0:1, :] + z * w_ref[1:2, :] + zn * w_ref[2:3, :] + b_ref[...]


def short_conv3_part(z, w, b, part, width):
    L = z.shape[0]
    tm = _pick(L, (512, 256, 128, 64, 32, 16, 8))
    tc = _pick(width, (1024, 512, 256, 128))
    off = part * width // tc
    halo = 8
    return pl.pallas_call(
        _conv3_kernel,
        grid=(L // tm, width // tc),
        in_specs=[pl.BlockSpec((tm, tc), lambda i, j: (i, off + j)),
                  pl.BlockSpec((halo, tc), lambda i, j: (jnp.maximum(i * (tm // halo) - 1, 0), off + j)),
                  pl.BlockSpec((halo, tc), lambda i, j: (jnp.minimum((i + 1) * (tm // halo), L // halo - 1), off + j)),
                  pl.BlockSpec((3, tc), lambda i, j: (0, off + j)),
                  pl.BlockSpec((1, tc), lambda i, j: (0, off + j))],
        out_specs=pl.BlockSpec((tm, tc), lambda i, j: (i, j)),
        out_shape=jax.ShapeDtypeStruct((L, width), jnp.float32),
        compiler_params=pltpu.CompilerParams(
            dimension_semantics=("parallel", "parallel"), vmem_limit_bytes=VMEM_LIMIT_BYTES),
        name="short_conv3",
    )(z, z, z, w, b.reshape(1, -1))


FILTER_PAD = 128
FILTER_LANES = 512


def _filter_rows_kernel(feats_ref, feats0_ref, fw1_ref, fb1_ref, fw2_ref, fb2_ref, fr_ref, fw3_ref, rate_ref,
                        rows_ref, a_ref, asum_ref, hid_ref, hid0_ref):
    g = pl.program_id(0)
    exact = lax.Precision.HIGHEST
    n_dir, grp, half, nf = feats_ref.shape

    def hidden(feats):
        h = jnp.sin(fr_ref[0:1, :] * (jnp.dot(feats, fw1_ref[...], precision=exact,
                                              preferred_element_type=jnp.float32) + fb1_ref[...]))
        return jnp.sin(fr_ref[1:2, :] * (jnp.dot(h, fw2_ref[...], precision=exact,
                                                 preferred_element_type=jnp.float32) + fb2_ref[...]))

    @pl.when(pl.program_id(1) == 0)
    def _():
        for d in range(n_dir):
            hid_ref[d] = hidden(feats_ref[d].reshape(grp * half, nf))
        hid0_ref[...] = hidden(feats0_ref[...])

    def taps(h, t, d):
        k = jnp.dot(h.astype(jnp.bfloat16), fw3_ref[d].astype(jnp.bfloat16), preferred_element_type=jnp.float32)
        return k * (jnp.exp(-t * rate_ref[d]) + HY_DECAY_SHIFT)

    k_fwd = taps(hid_ref[0], feats_ref[0].reshape(grp * half, nf)[:, 0:1], 0)
    k_bwd = taps(hid_ref[1], feats_ref[1].reshape(grp * half, nf)[:, 0:1], 1)
    lag0_back = taps(hid0_ref[...], feats0_ref[:, 0:1], 1)[0:1, :]
    first = (lax.broadcasted_iota(jnp.int32, k_fwd.shape, 0) == 0) & (g == 0)
    k_fwd = k_fwd + jnp.where(first, lag0_back, 0.0)
    k_bwd = jnp.where(first, 0.0, k_bwd)
    asum_ref[0] = jnp.sum(jnp.abs(k_fwd), axis=0, keepdims=True) + jnp.sum(jnp.abs(k_bwd), axis=0, keepdims=True)
    kf = k_fwd.astype(jnp.bfloat16)
    kb = k_bwd.astype(jnp.bfloat16)
    rs = [jnp.dot(rows_ref[:, :half], kf[s * half:(s + 1) * half], preferred_element_type=jnp.float32)
          + jnp.dot(rows_ref[:, half:], kb[s * half:(s + 1) * half], preferred_element_type=jnp.float32)
          for s in range(grp)]
    a_ref[...] = pltpu.einshape("abc->bac", jnp.stack(rs)).astype(a_ref.dtype)


def hyena_filter_rows(L, p, rows_fwd):
    f32 = jnp.float32
    N = 2 * L
    n2 = N // DFT_N1
    half = DFT_N1 // 2
    C2 = HY_ORDER * HY_WIDTH
    grp = min(DFT_ROW_GROUP, n2)
    tc = min(FILTER_LANES, C2)
    n = (n2 * jnp.arange(DFT_N1, dtype=jnp.int32)[None, :] + jnp.arange(n2, dtype=jnp.int32)[:, None])
    pos = jnp.where(n < L, n, N - n).astype(f32)[..., None]
    bands = jnp.linspace(1e-4, HY_BANDS - 1, HY_BANDS, dtype=f32)
    ang = (2.0 * math.pi / L) * bands * pos
    feats = jnp.concatenate([pos / (L - 1), jnp.cos(ang), -jnp.sin(ang)], axis=-1)
    feats = jnp.pad(feats, ((0, 0), (0, 0), (0, FILTER_PAD - feats.shape[-1])))
    feats = feats.reshape(n2, 2, half, FILTER_PAD).transpose(1, 0, 2, 3)
    feats0 = jnp.broadcast_to(feats[0, 0, 0:1], (8, FILTER_PAD))
    hid = p['hy_fw2'].shape[0]
    ph = FILTER_PAD - hid
    fw1 = jnp.pad(p['hy_fw1'].astype(f32), ((0, FILTER_PAD - p['hy_fw1'].shape[0]), (0, ph)))
    fb1 = jnp.pad(p['hy_fb1'].astype(f32), (0, ph)).reshape(1, -1)
    fw2 = jnp.pad(p['hy_fw2'].astype(f32), ((0, ph), (0, ph)))
    fb2 = jnp.pad(p['hy_fb2'].astype(f32), (0, ph)).reshape(1, -1)
    fr = jnp.pad(p['hy_freq'].astype(f32), ((0, 0), (0, ph)))
    fw3 = jnp.pad(p['hy_fw3'].astype(f32), ((0, ph), (0, 0))).reshape(FILTER_PAD, HY_DIRS, C2).transpose(1, 0, 2)
    rate = jnp.abs(p['hy_decay'].astype(f32)).reshape(HY_DIRS, 1, C2)
    full = lambda shape: pl.BlockSpec(shape, lambda g, c: (0,) * len(shape))
    m = rows_fwd.shape[0]
    a, asum = pl.pallas_call(
        _filter_rows_kernel,
        grid=(n2 // grp, C2 // tc),
        in_specs=[pl.BlockSpec((2, grp, half, FILTER_PAD), lambda g, c: (0, g, 0, 0)), full((8, FILTER_PAD)),
                  full((FILTER_PAD, FILTER_PAD)), full((1, FILTER_PAD)),
                  full((FILTER_PAD, FILTER_PAD)), full((1, FILTER_PAD)), full((2, FILTER_PAD)),
                  pl.BlockSpec((HY_DIRS, FILTER_PAD, tc), lambda g, c: (0, 0, c)),
                  pl.BlockSpec((HY_DIRS, 1, tc), lambda g, c: (0, 0, c)),
                  full(rows_fwd.shape)],
        out_specs=[pl.BlockSpec((m, grp, tc), lambda g, c: (0, g, c)),
                   pl.BlockSpec((1, 1, tc), lambda g, c: (g, 0, c))],
        out_shape=[jax.ShapeDtypeStruct((m, n2, C2), jnp.bfloat16),
                   jax.ShapeDtypeStruct((n2 // grp, 1, C2), f32)],
        scratch_shapes=[pltpu.VMEM((2, grp * half, FILTER_PAD), f32), pltpu.VMEM((8, FILTER_PAD), f32)],
        compiler_params=pltpu.CompilerParams(
            dimension_semantics=("parallel", "arbitrary"), vmem_limit_bytes=VMEM_LIMIT_BYTES),
        name="hyena_filter_rows",
    )(feats, feats0, fw1, fb1, fw2, fb2, fr, fw3, rate, rows_fwd)
    return a, jnp.sum(asum, axis=(0, 1))


def hyena_latent(z, p, consts):
    L = z.shape[0]
    C = HY_WIDTH
    N = 2 * L
    n2 = N // DFT_N1
    half = DFT_N1 // 2
    v, x1, x2 = (short_conv3_part(z, p['hy_conv_w'], p['hy_conv_b'], i, C) for i in range(3))
    ka, asum = hyena_filter_rows(L, p, consts["rows_fwd"])
    kspec = spectral_fwd(ka.reshape(2, DFT_K1_PAD, n2, HY_ORDER * C), consts["mid_fwd"])
    scale = 1.0 / (N * asum)

    def long_conv(u, gate, order, out_dtype):
        u3 = u.reshape(half, n2, C)
        ua = dft_rows_fwd(consts["rows_fwd_half"], u3)
        bm = spectral_mid(ua.reshape(2, DFT_K1_PAD, n2, C), consts, kspec, order * C)
        y = dft_rows_inv(consts["rows_inv"], bm.reshape(2 * DFT_K1_PAD, n2, C), u3, gate.reshape(half, n2, C),
                         scale[order * C:(order + 1) * C], p['hy_d'][order].astype(jnp.float32), out_dtype)
        return y.reshape(L, C)

    y1 = long_conv(v, x1, 0, jnp.float32)
    return long_conv(y1, x2, 1, jnp.bfloat16)


def rms_norm(x, g):
    xf = x.astype(jnp.float32)
    y = xf * lax.rsqrt(jnp.mean(xf * xf, axis=-1, keepdims=True) + RMS_EPS)
    return (y * g.astype(jnp.float32)).astype(x.dtype)


def modulate(h, shift, scale):
    return h * (1 + scale) + shift


def short_conv3(z, w, b):
    L = z.shape[1]
    zp = jnp.pad(z, ((0, 0), (1, 1), (0, 0)))
    return zp[:, :L] * w[0] + zp[:, 1:L + 1] * w[1] + zp[:, 2:] * w[2] + b


def hyena_filter_spectra(L, fw1, fb1, fw2, fb2, fw3, freq, decay):
    f32 = jnp.float32
    t = jnp.linspace(0.0, 1.0, L, dtype=f32)[:, None]
    pos = jnp.arange(L, dtype=f32)[:, None]
    bands = jnp.linspace(1e-4, HY_BANDS - 1, HY_BANDS, dtype=f32)[None, :]
    ang = (2.0 * math.pi / L) * bands * pos
    feats = jnp.concatenate([t, jnp.cos(ang), -jnp.sin(ang)], axis=-1)
    fr = freq.astype(f32)
    h = jnp.sin(fr[0] * (feats @ fw1.astype(f32) + fb1.astype(f32)))
    h = jnp.sin(fr[1] * (h @ fw2.astype(f32) + fb2.astype(f32)))
    h = (h @ fw3.astype(f32)).reshape(L, HY_DIRS, HY_ORDER, HY_WIDTH)
    rate = jnp.abs(decay.astype(f32)).reshape(HY_DIRS, HY_ORDER, HY_WIDTH)
    h = h * (jnp.exp(-t[:, :, None, None] * rate) + HY_DECAY_SHIFT)
    hf, hb = h[:, 0], h[:, 1]
    k = jnp.concatenate([hf[:1] + hb[:1], hf[1:], jnp.zeros_like(hf[:1]), hb[1:][::-1]], axis=0)
    k = k / jnp.sum(jnp.abs(k), axis=0, keepdims=True)
    return jnp.fft.rfft(k, axis=0)


def bidir_long_conv(u, k_spec, d_skip):
    L = u.shape[1]
    uf = jnp.fft.rfft(u.astype(jnp.float32), n=2 * L, axis=1)
    y = jnp.fft.irfft(uf * k_spec[None], n=2 * L, axis=1)[:, :L]
    return (y + u.astype(jnp.float32) * d_skip.astype(jnp.float32)).astype(u.dtype)


def hyena_branch(z, p):
    L = z.shape[1]
    z = short_conv3(z, p['hy_conv_w'], p['hy_conv_b'])
    v, x1, x2 = jnp.split(z, 3, axis=-1)
    k_spec = hyena_filter_spectra(L, p['hy_fw1'], p['hy_fb1'], p['hy_fw2'], p['hy_fb2'],
                                  p['hy_fw3'], p['hy_freq'], p['hy_decay'])
    y = x1 * bidir_long_conv(v, k_spec[:, 0], p['hy_d'][0])
    return x2 * bidir_long_conv(y, k_spec[:, 1], p['hy_d'][1])


def _resident(shape, index_map):
    return pl.BlockSpec(shape, index_map, pipeline_mode=pl.Buffered(1))


def _merge_kernel(y0_ref, y1_ref, y2_ref, g_ref, wb_ref, o_ref):
    d = o_ref.shape[1]
    m = None
    for b, y_ref in enumerate((y0_ref, y1_ref, y2_ref)):
        t = g_ref[:, b * d:(b + 1) * d] * jnp.dot(y_ref[...], wb_ref[b], preferred_element_type=jnp.float32)
        m = t if m is None else m + t
    o_ref[...] = m.astype(o_ref.dtype)


def merge_gated(y_hy, y_wa, y_na, gates, w_branch):
    M, wbr = y_hy.shape
    D = w_branch.shape[2]
    tm = _pick(M, (256, 128, 64, 32, 16, 8))
    yspec = pl.BlockSpec((tm, wbr), lambda i: (i, 0))
    return pl.pallas_call(
        _merge_kernel,
        grid=(M // tm,),
        in_specs=[yspec, yspec, yspec, pl.BlockSpec((tm, N_BRANCH * D), lambda i: (i, 0)),
                  _resident((N_BRANCH, wbr, D), lambda i: (0, 0, 0))],
        out_specs=pl.BlockSpec((tm, D), lambda i: (i, 0)),
        out_shape=jax.ShapeDtypeStruct((M, D), jnp.bfloat16),
        compiler_params=pltpu.CompilerParams(
            dimension_semantics=("parallel",), vmem_limit_bytes=VMEM_LIMIT_BYTES),
        name="merge_gated",
    )(y_hy, y_wa, y_na, gates, w_branch)


ROUTER_PAD = 128


def _out_proj_kernel(m_ref, w_ref, x_ref, gate_ref, g2_ref, shift_ref, scale_ref, wr_hi_ref, wr_lo_ref,
                     x_out, h_out, lg_out):
    x = x_ref[...] + gate_ref[...] * jnp.dot(m_ref[...], w_ref[...], preferred_element_type=jnp.float32)
    x_out[...] = x
    h = x * lax.rsqrt(jnp.mean(x * x, axis=-1, keepdims=True) + RMS_EPS) * g2_ref[...]
    h = h * (1.0 + scale_ref[...]) + shift_ref[...]
    h_out[...] = h.astype(h_out.dtype)
    h_hi = h.astype(jnp.bfloat16)
    h_lo = (h - h_hi.astype(jnp.float32)).astype(jnp.bfloat16)
    lg_out[...] = (jnp.dot(h_hi, wr_hi_ref[...], preferred_element_type=jnp.float32)
                   + jnp.dot(h_lo, wr_hi_ref[...], preferred_element_type=jnp.float32)
                   + jnp.dot(h_hi, wr_lo_ref[...], preferred_element_type=jnp.float32))


def out_project(m, w_out, x, gate, norm_g, shift, scale, w_router):
    M, D = x.shape
    tm = _pick(M, (256, 128, 64, 32, 16, 8))
    row = pl.BlockSpec((tm, D), lambda i: (i, 0))
    vec = pl.BlockSpec((1, D), lambda i: (0, 0))
    wr = jnp.pad(w_router.astype(jnp.float32), ((0, 0), (0, ROUTER_PAD - w_router.shape[1])))
    wr_hi = wr.astype(jnp.bfloat16)
    wr_lo = (wr - wr_hi.astype(jnp.float32)).astype(jnp.bfloat16)
    v2 = lambda a: a.reshape(1, D).astype(jnp.float32)
    return pl.pallas_call(
        _out_proj_kernel,
        grid=(M // tm,),
        in_specs=[row, _resident((D, D), lambda i: (0, 0)), row, vec, vec, vec, vec,
                  _resident((D, ROUTER_PAD), lambda i: (0, 0)), _resident((D, ROUTER_PAD), lambda i: (0, 0))],
        out_specs=[row, row, pl.BlockSpec((tm, ROUTER_PAD), lambda i: (i, 0))],
        out_shape=[jax.ShapeDtypeStruct((M, D), jnp.float32), jax.ShapeDtypeStruct((M, D), jnp.bfloat16),
                   jax.ShapeDtypeStruct((M, ROUTER_PAD), jnp.float32)],
        compiler_params=pltpu.CompilerParams(
            dimension_semantics=("parallel",), vmem_limit_bytes=VMEM_LIMIT_BYTES),
        name="out_project",
    )(m, w_out, x, v2(gate), v2(norm_g), v2(shift), v2(scale), wr_hi, wr_lo)


def _gate_up_kernel(x_ref, wg_ref, wu_ref, o_ref, wg_bf, wu_bf):
    @pl.when(pl.program_id(2) == 0)
    def _():
        wg_bf[...] = wg_ref[...].astype(jnp.bfloat16)
        wu_bf[...] = wu_ref[...].astype(jnp.bfloat16)

    x = x_ref[0]
    a = jnp.dot(x, wg_bf[...], preferred_element_type=jnp.float32)
    u = jnp.dot(x, wu_bf[...], preferred_element_type=jnp.float32)
    o_ref[0] = (a * jax.nn.sigmoid(a) * u).astype(o_ref.dtype)


def _down_kernel(h_ref, wd_ref, gsel_ref, gate_ref, o_ref, wd_bf):
    @pl.when(pl.program_id(1) == 0)
    def _():
        wd_bf[...] = wd_ref[...].astype(jnp.bfloat16)

    y = jnp.dot(h_ref[0], wd_bf[...], preferred_element_type=jnp.float32)
    o_ref[0] = (y * gsel_ref[0] * gate_ref[...]).astype(o_ref.dtype)


def expert_ffn(xe, w_gate, w_up, w_down, layer, gsel, out_gate):
    E, cap, D = xe.shape
    F = w_gate.shape[3]
    tm = _pick(cap, (512, 256, 128, 64, 32, 16, 8))
    tn = _pick(F, (1024, 512, 256, 128))
    seq = pltpu.CompilerParams(dimension_semantics=("parallel", "parallel", "arbitrary"),
                               vmem_limit_bytes=VMEM_LIMIT_BYTES)
    h = pl.pallas_call(
        _gate_up_kernel,
        grid=(E, F // tn, cap // tm),
        in_specs=[pl.BlockSpec((1, tm, D), lambda e, j, i: (e, i, 0)),
                  pl.BlockSpec((None, None, D, tn), lambda e, j, i: (layer, e, 0, j)),
                  pl.BlockSpec((None, None, D, tn), lambda e, j, i: (layer, e, 0, j))],
        out_specs=pl.BlockSpec((1, tm, tn), lambda e, j, i: (e, i, j)),
        out_shape=jax.ShapeDtypeStruct((E, cap, F), jnp.bfloat16),
        scratch_shapes=[pltpu.VMEM((D, tn), jnp.bfloat16), pltpu.VMEM((D, tn), jnp.bfloat16)],
        compiler_params=seq,
        name="expert_gate_up",
    )(xe, w_gate, w_up)
    return pl.pallas_call(
        _down_kernel,
        grid=(E, cap // tm),
        in_specs=[pl.BlockSpec((1, tm, F), lambda e, i: (e, i, 0)),
                  pl.BlockSpec((None, None, F, D), lambda e, i: (layer, e, 0, 0)),
                  pl.BlockSpec((1, tm, 1), lambda e, i: (e, i, 0)),
                  pl.BlockSpec((1, D), lambda e, i: (0, 0))],
        out_specs=pl.BlockSpec((1, tm, D), lambda e, i: (e, i, 0)),
        out_shape=jax.ShapeDtypeStruct((E, cap, D), jnp.bfloat16),
        scratch_shapes=[pltpu.VMEM((F, D), jnp.bfloat16)],
        compiler_params=pltpu.CompilerParams(dimension_semantics=("parallel", "arbitrary"),
                                             vmem_limit_bytes=VMEM_LIMIT_BYTES),
        name="expert_down",
    )(h, w_down, gsel.reshape(E, cap, 1).astype(jnp.float32), out_gate.reshape(1, D).astype(jnp.float32))


SEG_ROWS = 512
SEG_WIN = 512
SEG_LANES = 256


def _segment_add_kernel(first_ref, last_ref, tok_ref, ye_ref, x_ref, o_ref, *, n_tokens, win):
    c = pl.program_id(1)

    @pl.when(c == 0)
    def _():
        o_ref[...] = x_ref[...]

    base0 = (first_ref[c] // 8) * 8
    n_win = (last_ref[c] - base0) // win + 1
    tok = tok_ref[0]
    rows = ye_ref[...]

    def window(w, carry):
        lo = base0 + w * win
        base = pl.multiple_of(jnp.minimum(lo, n_tokens - win), 8)
        rid = base + lax.broadcasted_iota(jnp.int32, (win, tok.shape[1]), 0)
        onehot = ((rid == tok) & (tok >= lo)).astype(jnp.bfloat16)
        o_ref[pl.ds(base, win), :] += jnp.dot(onehot, rows, preferred_element_type=jnp.float32)
        return carry

    lax.fori_loop(0, n_win, window, 0)


def segment_add(x, ye_sorted, tok_sorted):
    N, D = x.shape
    P = ye_sorted.shape[0]
    rows = min(SEG_ROWS, P)
    win = min(SEG_WIN, N)
    tc = min(SEG_LANES, D)
    n_chunks = P // rows
    tok3 = tok_sorted.reshape(n_chunks, 1, rows)
    return pl.pallas_call(
        partial(_segment_add_kernel, n_tokens=N, win=win),
        grid_spec=pltpu.PrefetchScalarGridSpec(
            num_scalar_prefetch=2,
            grid=(D // tc, n_chunks),
            in_specs=[pl.BlockSpec((1, 1, rows), lambda d, c, f, l: (c, 0, 0)),
                      pl.BlockSpec((rows, tc), lambda d, c, f, l: (c, d)),
                      pl.BlockSpec((N, tc), lambda d, c, f, l: (0, d), pipeline_mode=pl.Buffered(1))],
            out_specs=pl.BlockSpec((N, tc), lambda d, c, f, l: (0, d)),
        ),
        out_shape=jax.ShapeDtypeStruct((N, D), jnp.float32),
        compiler_params=pltpu.CompilerParams(
            dimension_semantics=("parallel", "arbitrary"), vmem_limit_bytes=VMEM_LIMIT_BYTES),
        name="segment_add",
    )(tok3[:, 0, 0], tok3[:, 0, rows - 1], tok3, ye_sorted, x)


def expert_choice_latent(x1, h2, logits, out_gate, w_gate, w_up, w_down, layer):
    N, D = x1.shape
    cap = EC_CAPACITY_FACTOR * N // N_EXPERTS
    aff = jax.nn.softmax(logits[:, :N_EXPERTS], axis=-1)
    gsel, idx = lax.top_k(aff.T, cap)
    ye = expert_ffn(h2[idx], w_gate, w_up, w_down, layer, gsel, out_gate)
    tok = idx.reshape(-1)
    order = jnp.argsort(tok)
    return segment_add(x1, ye.reshape(-1, D)[order], tok[order])


def _ctx_attn_kernel(sink_ref, q_ref, k_ref, v_ref, o_ref, *, use_sink):
    s = lax.dot_general(q_ref[...], k_ref[...], _NT, preferred_element_type=jnp.float32)
    m = jnp.max(s, axis=-1, keepdims=True)
    if use_sink:
        sk = sink_ref[pl.program_id(0)]
        m = jnp.maximum(m, sk)
    p = jnp.exp(s - m)
    denom = jnp.sum(p, axis=-1, keepdims=True)
    if use_sink:
        denom = denom + jnp.exp(sk - m)
    o = jnp.dot(p.astype(v_ref.dtype), v_ref[...], preferred_element_type=jnp.float32)
    o_ref[...] = (o / denom).astype(o_ref.dtype)


def context_attention(q, k, v, sink, n_heads, n_kv_heads):
    Lc = q.shape[0]
    group = n_heads // n_kv_heads
    use_sink = sink is not None
    sink = jnp.zeros((n_heads,), jnp.float32) if sink is None else sink.astype(jnp.float32)
    kv = pl.BlockSpec((Lc, HEAD_DIM), lambda h, s: (0, h // group))
    return pl.pallas_call(
        partial(_ctx_attn_kernel, use_sink=use_sink),
        grid_spec=pltpu.PrefetchScalarGridSpec(
            num_scalar_prefetch=1,
            grid=(n_heads,),
            in_specs=[pl.BlockSpec((Lc, HEAD_DIM), lambda h, s: (0, h)), kv, kv],
            out_specs=pl.BlockSpec((Lc, HEAD_DIM), lambda h, s: (0, h)),
        ),
        out_shape=jax.ShapeDtypeStruct((Lc, n_heads * HEAD_DIM), jnp.bfloat16),
        compiler_params=pltpu.CompilerParams(
            dimension_semantics=("parallel",), vmem_limit_bytes=VMEM_LIMIT_BYTES),
        name="context_attention",
    )(sink, q, k, v)


def trunk_layer(x, ctx, c, c_ctx, p, stacked, layer, update_ctx, consts):
    L = x.shape[0]
    bf = jnp.bfloat16
    w_in = stacked['w_in']
    cond = jnp.pad(jnp.concatenate([jax.nn.silu(c), jax.nn.silu(c_ctx)[None]], axis=0), ((0, 6), (0, 0)))
    mod = matmul(cond, stacked['w_mod'], layer=layer) + p['b_mod']
    mx = jnp.split(mod[0], N_MOD)
    mc = jnp.split(mod[1], N_MOD)
    qk_scale = HEAD_DIM ** -0.5
    nwq = WA_HEADS * HEAD_DIM
    nkv = WA_KV_HEADS * HEAD_DIM
    nna = NA_HEADS * HEAD_DIM
    proj = lambda a, *args: project(a, w_in, layer, *args)

    def mix_and_ffn(res, y_hy, y_wa, y_na, gates, m_vec):
        m = merge_gated(y_hy, y_wa, y_na, gates, p['w_branch'].astype(bf))
        r1, h2, logits = out_project(m, p['w_out'].astype(bf), res, m_vec[2], p['norm2_g'], m_vec[3], m_vec[4],
                                     p['w_router'])
        return expert_choice_latent(r1, h2, logits, m_vec[5], stacked['w_gate'], stacked['w_up'],
                                    stacked['w_down'], layer)

    hcb = norm_modulate(ctx, p['norm1_g'], mc[0], mc[1])
    kc_wa = proj(hcb, OFF_WA_KV, nkv, "headnorm", bf, p['wa_k_norm'])
    vc_wa = proj(hcb, OFF_WA_KV + nkv, nkv, "plain", bf)
    kc_na = proj(hcb, OFF_NA_KV, nna, "headnorm", bf, p['na_k_norm'])
    vc_na = proj(hcb, OFF_NA_KV + nna, nna, "plain", bf)

    hxb = norm_modulate(x, p['norm1_g'], mx[0], mx[1])
    rope = rope_lane_tables(L)
    y_hy = hyena_latent(proj(hxb, OFF_HY, OFF_WA_Q - OFF_HY), p, consts)
    q_wa = proj(hxb, OFF_WA_Q, nwq, "headnorm", bf, p['wa_q_norm'], qk_scale, rope)
    k_wa = proj(hxb, OFF_WA_KV, nkv, "headnorm", bf, p['wa_k_norm'], 1.0, rope)
    v_wa = proj(hxb, OFF_WA_KV + nkv, nkv, "plain", bf)
    y_wa = windowed_attention(q_wa, k_wa, v_wa, kc_wa, vc_wa, p['wa_sink'])
    q_na = proj(hxb, OFF_NA_Q, nna, "headnorm", bf, p['na_q_norm'], qk_scale)
    k_na = proj(hxb, OFF_NA_KV, nna, "headnorm", bf, p['na_k_norm'])
    v_na = proj(hxb, OFF_NA_KV + nna, nna, "plain", bf)
    y_na = neighbourhood_attention(q_na, k_na, v_na, kc_na, vc_na, p['na_rpb'])
    gates = proj(hxb, OFF_GATE, N_BRANCH * D_MODEL, "sigmoid")
    x = mix_and_ffn(x, y_hy, y_wa, y_na, gates, mx)

    if update_ctx:
        yc_hy = hyena_branch(proj(hcb, OFF_HY, OFF_WA_Q - OFF_HY)[None], p)[0].astype(bf)
        qc_wa = proj(hcb, OFF_WA_Q, nwq, "headnorm", bf, p['wa_q_norm'], qk_scale)
        yc_wa = context_attention(qc_wa, kc_wa, vc_wa, p['wa_sink'], WA_HEADS, WA_KV_HEADS)
        qc_na = proj(hcb, OFF_NA_Q, nna, "headnorm", bf, p['na_q_norm'], qk_scale)
        yc_na = context_attention(qc_na, kc_na, vc_na, None, NA_HEADS, NA_HEADS)
        gates_c = proj(hcb, OFF_GATE, N_BRANCH * D_MODEL, "sigmoid")
        ctx = mix_and_ffn(ctx, yc_hy, yc_wa, yc_na, gates_c, mc)
    return x, ctx


def kernel(x, c, ctx, c_ctx, w_mod, b_mod, norm1_g, w_in, hy_conv_w, hy_conv_b, hy_fw1, hy_fb1, hy_fw2, hy_fb2, hy_fw3, hy_freq, hy_decay, hy_d, wa_q_norm, wa_k_norm, wa_sink, na_q_norm, na_k_norm, na_rpb, w_branch, w_out, norm2_g, w_router, w_gate, w_up, w_down):
    consts = dft_constants(x.shape[1])
    stacked = {'w_mod': w_mod, 'w_in': w_in, 'w_gate': w_gate, 'w_up': w_up, 'w_down': w_down}
    xs, cs = x[0], ctx[0]
    for l in range(DEPTH):
        p = {
            'b_mod': b_mod[l], 'norm1_g': norm1_g[l],
            'hy_conv_w': hy_conv_w[l], 'hy_conv_b': hy_conv_b[l], 'hy_fw1': hy_fw1[l], 'hy_fb1': hy_fb1[l],
            'hy_fw2': hy_fw2[l], 'hy_fb2': hy_fb2[l], 'hy_fw3': hy_fw3[l], 'hy_freq': hy_freq[l],
            'hy_decay': hy_decay[l], 'hy_d': hy_d[l], 'wa_q_norm': wa_q_norm[l], 'wa_k_norm': wa_k_norm[l],
            'wa_sink': wa_sink[l], 'na_q_norm': na_q_norm[l], 'na_k_norm': na_k_norm[l], 'na_rpb': na_rpb[l],
            'w_branch': w_branch[l], 'w_out': w_out[l], 'norm2_g': norm2_g[l], 'w_router': w_router[l],
        }
        xs, cs = trunk_layer(xs, cs, c, c_ctx, p, stacked, l, l < DEPTH - 1, consts)
    return xs[None]
```

```python
import math
from functools import partial

import jax
import jax.numpy as jnp
from jax import lax
from jax.experimental import pallas as pl
from jax.experimental.pallas import tpu as pltpu

D_MODEL = 2048
SEQ = 16384
DEPTH = 2
CTX_LEN = 256
GRID_W = 64
HEAD_DIM = 128
BRANCH_WIDTH = 1024
N_BRANCH = 3
N_MOD = 6
RMS_EPS = 1e-6
NEG_INF = -1e30

HY_WIDTH = BRANCH_WIDTH
HY_ORDER = 2
HY_DIRS = 2
HY_BANDS = 16
HY_DECAY_SHIFT = 0.05

WA_HEADS = BRANCH_WIDTH // HEAD_DIM
WA_KV_HEADS = 2
WA_WINDOW = 128
WA_BLOCK = 128

NA_HEADS = BRANCH_WIDTH // HEAD_DIM
NA_WIN_ROWS = 8
NA_WIN_COLS = 16

ROPE_BASE = 10000.0

N_EXPERTS = 16
EC_CAPACITY_FACTOR = 2
D_EXPERT = 1024

OFF_HY = 0
OFF_WA_Q = OFF_HY + 3 * HY_WIDTH
OFF_WA_KV = OFF_WA_Q + WA_HEADS * HEAD_DIM
OFF_NA_Q = OFF_WA_KV + 2 * WA_KV_HEADS * HEAD_DIM
OFF_NA_KV = OFF_NA_Q + NA_HEADS * HEAD_DIM
OFF_GATE = OFF_NA_KV + 2 * NA_HEADS * HEAD_DIM
N_IN = OFF_GATE + N_BRANCH * D_MODEL

VMEM_LIMIT_BYTES = 56 * 1024 * 1024


def _mm_kernel(a_ref, b_ref, o_ref):
    o_ref[...] = jnp.dot(a_ref[...].astype(jnp.bfloat16), b_ref[...].astype(jnp.bfloat16),
                         preferred_element_type=jnp.float32).astype(o_ref.dtype)


def _pick(n, pref):
    for t in pref:
        if n % t == 0:
            return t
    return n


def matmul(a, b, out_dtype=jnp.float32, layer=None):
    M, K = a.shape
    N = b.shape[-1]
    tm = _pick(M, (512, 256, 128, 64, 32, 16, 8))
    tn = _pick(N, (1024, 512, 256, 128))
    if layer is None:
        b_spec = pl.BlockSpec((K, tn), lambda j, i: (0, j))
    else:
        b_spec = pl.BlockSpec((None, K, tn), lambda j, i: (layer, 0, j))
    return pl.pallas_call(
        _mm_kernel,
        grid=(N // tn, M // tm),
        in_specs=[pl.BlockSpec((tm, K), lambda j, i: (i, 0)), b_spec],
        out_specs=pl.BlockSpec((tm, tn), lambda j, i: (i, j)),
        out_shape=jax.ShapeDtypeStruct((M, N), out_dtype),
        compiler_params=pltpu.CompilerParams(
            dimension_semantics=("parallel", "parallel"), vmem_limit_bytes=VMEM_LIMIT_BYTES),
        name="matmul",
    )(a, b)


def _norm_mod_kernel(x_ref, g_ref, shift_ref, scale_ref, o_ref):
    x = x_ref[...]
    y = x * lax.rsqrt(jnp.mean(x * x, axis=-1, keepdims=True) + RMS_EPS) * g_ref[...]
    o_ref[...] = (y * (1.0 + scale_ref[...]) + shift_ref[...]).astype(o_ref.dtype)


def norm_modulate(x, g, shift, scale, out_dtype=jnp.bfloat16):
    M, D = x.shape
    tm = _pick(M, (512, 256, 128, 64, 32, 16, 8))
    vec = pl.BlockSpec((1, D), lambda i: (0, 0))
    return pl.pallas_call(
        _norm_mod_kernel,
        grid=(M // tm,),
        in_specs=[pl.BlockSpec((tm, D), lambda i: (i, 0)), vec, vec, vec],
        out_specs=pl.BlockSpec((tm, D), lambda i: (i, 0)),
        out_shape=jax.ShapeDtypeStruct((M, D), out_dtype),
        compiler_params=pltpu.CompilerParams(
            dimension_semantics=("parallel",), vmem_limit_bytes=VMEM_LIMIT_BYTES),
        name="norm_modulate",
    )(x, g.reshape(1, D), shift.reshape(1, D), scale.reshape(1, D))


def _swap_halves(x):
    lane = lax.broadcasted_iota(jnp.int32, x.shape, 1)
    return jnp.where((lane % 64) < 32, pltpu.roll(x, 96, 1), pltpu.roll(x, 32, 1))


HEADNORM_ROWS = 256
HEADNORM_COLS = 256


def _proj_kernel(*refs, mode, post_scale, rope):
    if mode == "headnorm":
        if rope:
            a_ref, w_ref, gain_ref, cos_ref, sin_ref, o_ref, w_bf = refs
        else:
            a_ref, w_ref, gain_ref, o_ref, w_bf = refs
    else:
        a_ref, w_ref, o_ref, w_bf = refs

    @pl.when(pl.program_id(1) == 0)
    def _():
        w_bf[...] = w_ref[...].astype(jnp.bfloat16)

    if mode == "plain":
        o_ref[...] = jnp.dot(a_ref[...], w_bf[...], preferred_element_type=jnp.float32).astype(o_ref.dtype)
    elif mode == "sigmoid":
        acc = jnp.dot(a_ref[...], w_bf[...], preferred_element_type=jnp.float32)
        o_ref[...] = jax.nn.sigmoid(acc).astype(o_ref.dtype)
    else:
        gain = gain_ref[...] * post_scale
        tm, tn = o_ref.shape
        rows = min(tm, HEADNORM_ROWS) if rope else tm
        width = min(tn, HEADNORM_COLS) if rope else tn

        def chunk(r, carry):
            r0 = pl.multiple_of(r * rows, rows)
            a = a_ref[pl.ds(r0, rows), :]
            for c0 in range(0, tn, width):
                acc = jnp.dot(a, w_bf[:, c0:c0 + width], preferred_element_type=jnp.float32)
                for h in range(width // HEAD_DIM):
                    xh = acc[:, h * HEAD_DIM:(h + 1) * HEAD_DIM]
                    y = xh * lax.rsqrt(jnp.mean(xh * xh, axis=-1, keepdims=True) + RMS_EPS) * gain
                    if rope:
                        y = (y * cos_ref[pl.ds(r0, rows), :]
                             + _swap_halves(y) * sin_ref[pl.ds(r0, rows), :])
                    o_ref[pl.ds(r0, rows), c0 + h * HEAD_DIM:c0 + (h + 1) * HEAD_DIM] = y.astype(o_ref.dtype)
            return carry

        lax.fori_loop(0, tm // rows, chunk, 0)


def project(a, w, layer, col_off, n_cols, mode="plain", out_dtype=jnp.float32, gain=None, post_scale=1.0,
            rope=None):
    M, K = a.shape
    tm = _pick(M, (1024, 512, 256, 128, 64, 32, 16, 8))
    tn = next(t for t in (1024, 768, 512, 256, 128) if n_cols % t == 0 and col_off % t == 0)
    off = col_off // tn
    in_specs = [pl.BlockSpec((tm, K), lambda j, i: (i, 0)),
                pl.BlockSpec((None, K, tn), lambda j, i: (layer, 0, off + j))]
    args = [a, w]
    if mode == "headnorm":
        in_specs.append(pl.BlockSpec((1, HEAD_DIM), lambda j, i: (0, 0)))
        args.append(gain.reshape(1, HEAD_DIM).astype(jnp.float32))
        if rope is not None:
            in_specs += [pl.BlockSpec((tm, HEAD_DIM), lambda j, i: (i, 0))] * 2
            args += list(rope)
    return pl.pallas_call(
        partial(_proj_kernel, mode=mode, post_scale=post_scale, rope=rope is not None),
        grid=(n_cols // tn, M // tm),
        in_specs=in_specs,
        out_specs=pl.BlockSpec((tm, tn), lambda j, i: (i, j)),
        out_shape=jax.ShapeDtypeStruct((M, n_cols), out_dtype),
        scratch_shapes=[pltpu.VMEM((K, tn), jnp.bfloat16)],
        compiler_params=pltpu.CompilerParams(
            dimension_semantics=("parallel", "arbitrary"), vmem_limit_bytes=VMEM_LIMIT_BYTES),
        name="project_" + mode,
    )(*args)


def rope_lane_tables(L):
    t = jnp.arange(L, dtype=jnp.int32)
    row = (t // GRID_W).astype(jnp.float32)
    col = (t % GRID_W).astype(jnp.float32)
    nf = HEAD_DIM // 4
    inv = ROPE_BASE ** (-jnp.arange(nf, dtype=jnp.float32) / nf)
    ar, ac = row[:, None] * inv, col[:, None] * inv
    cos = jnp.concatenate([jnp.cos(ar), jnp.cos(ar), jnp.cos(ac), jnp.cos(ac)], axis=-1)
    sin = jnp.concatenate([-jnp.sin(ar), jnp.sin(ar), -jnp.sin(ac), jnp.sin(ac)], axis=-1)
    return cos, sin


_NT = (((1,), (1,)), ((), ()))


def _wa_kernel(sink_ref, q_ref, k_ref, v_ref, kc_ref, vc_ref, o_ref, *, tq, seq):
    g = pl.program_id(0)
    i = pl.program_id(1)
    nwin = tq + 2 * WA_WINDOW
    ws = jnp.clip(i * tq - WA_WINDOW, 0, seq - nwin)
    start = pl.multiple_of(ws, WA_WINDOW)
    kwin = k_ref[pl.ds(start, nwin), :]
    vwin = v_ref[pl.ds(start, nwin), :]
    qpos = i * tq + lax.broadcasted_iota(jnp.int32, (tq, nwin), 0)
    kpos = ws + lax.broadcasted_iota(jnp.int32, (tq, nwin), 1)
    valid = jnp.abs(qpos - kpos) <= WA_WINDOW
    group = WA_HEADS // WA_KV_HEADS
    for hh in range(group):
        q = q_ref[:, hh * HEAD_DIM:(hh + 1) * HEAD_DIM]
        s = jnp.where(valid, lax.dot_general(q, kwin, _NT, preferred_element_type=jnp.float32), NEG_INF)
        sc = lax.dot_general(q, kc_ref[...], _NT, preferred_element_type=jnp.float32)
        sk = sink_ref[g * group + hh]
        m = jnp.maximum(jnp.maximum(jnp.max(s, axis=-1, keepdims=True), jnp.max(sc, axis=-1, keepdims=True)), sk)
        p = jnp.exp(s - m)
        pc = jnp.exp(sc - m)
        denom = jnp.sum(p, axis=-1, keepdims=True) + jnp.sum(pc, axis=-1, keepdims=True) + jnp.exp(sk - m)
        o = (jnp.dot(p.astype(vwin.dtype), vwin, preferred_element_type=jnp.float32)
             + jnp.dot(pc.astype(vwin.dtype), vc_ref[...], preferred_element_type=jnp.float32))
        o_ref[:, hh * HEAD_DIM:(hh + 1) * HEAD_DIM] = (o / denom).astype(o_ref.dtype)


def windowed_attention(q, k, v, kc, vc, sink, tq=256):
    L = q.shape[0]
    Lc = kc.shape[0]
    gw = (WA_HEADS // WA_KV_HEADS) * HEAD_DIM
    slab = pl.BlockSpec((L, HEAD_DIM), lambda g, i, s: (0, g))
    cslab = pl.BlockSpec((Lc, HEAD_DIM), lambda g, i, s: (0, g))
    return pl.pallas_call(
        partial(_wa_kernel, tq=tq, seq=L),
        grid_spec=pltpu.PrefetchScalarGridSpec(
            num_scalar_prefetch=1,
            grid=(WA_KV_HEADS, L // tq),
            in_specs=[pl.BlockSpec((tq, gw), lambda g, i, s: (i, g)), slab, slab, cslab, cslab],
            out_specs=pl.BlockSpec((tq, gw), lambda g, i, s: (i, g)),
        ),
        out_shape=jax.ShapeDtypeStruct((L, WA_HEADS * HEAD_DIM), jnp.bfloat16),
        compiler_params=pltpu.CompilerParams(
            dimension_semantics=("parallel", "parallel"), vmem_limit_bytes=VMEM_LIMIT_BYTES),
        name="windowed_attention",
    )(sink.astype(jnp.float32), q, k, v, kc, vc)


NA_ROW_BLOCK = 4
NA_KEY_ROWS = NA_ROW_BLOCK + NA_WIN_ROWS - 1


NA_HEAD_GROUP = 4


def _na_kernel(q_ref, k_ref, v_ref, kc_ref, vc_ref, bias_ref, o_ref, *, rows):
    i = pl.program_id(1)
    ws = jnp.clip(i * NA_ROW_BLOCK - NA_WIN_ROWS // 2, 0, rows - NA_KEY_ROWS)
    start = pl.multiple_of(ws * GRID_W, GRID_W)
    nk = NA_KEY_ROWS * GRID_W
    for hh in range(NA_HEAD_GROUP):
        cols = slice(hh * HEAD_DIM, (hh + 1) * HEAD_DIM)
        kwin = k_ref[pl.ds(start, nk), cols]
        vwin = v_ref[pl.ds(start, nk), cols]
        q = q_ref[:, cols]
        s = lax.dot_general(q, kwin, _NT, preferred_element_type=jnp.float32) + bias_ref[0, hh]
        sc = lax.dot_general(q, kc_ref[:, cols], _NT, preferred_element_type=jnp.float32)
        m = jnp.maximum(jnp.max(s, axis=-1, keepdims=True), jnp.max(sc, axis=-1, keepdims=True))
        p = jnp.exp(s - m)
        pc = jnp.exp(sc - m)
        denom = jnp.sum(p, axis=-1, keepdims=True) + jnp.sum(pc, axis=-1, keepdims=True)
        o = (jnp.dot(p.astype(vwin.dtype), vwin, preferred_element_type=jnp.float32)
             + jnp.dot(pc.astype(vwin.dtype), vc_ref[:, cols], preferred_element_type=jnp.float32))
        o_ref[:, cols] = (o / denom).astype(o_ref.dtype)


def na_bias_tiles(rpb, rows):
    col = jnp.arange(GRID_W)
    cstart = jnp.clip(col - NA_WIN_COLS // 2, 0, GRID_W - NA_WIN_COLS)
    col_in = (col[None, :] >= cstart[:, None]) & (col[None, :] < cstart[:, None] + NA_WIN_COLS)
    dc_idx = jnp.clip(col[None, :] - col[:, None] + NA_WIN_COLS - 1, 0, 2 * NA_WIN_COLS - 2)
    exact = lax.Precision.HIGHEST
    by_col = jnp.einsum('hrd,qkd->hrqk', rpb.astype(jnp.float32),
                        jax.nn.one_hot(dc_idx, 2 * NA_WIN_COLS - 1, dtype=jnp.float32), precision=exact)
    tiles = []
    for blk in (0, 1, rows // NA_ROW_BLOCK - 1):
        r = blk * NA_ROW_BLOCK
        ws = min(max(r - NA_WIN_ROWS // 2, 0), rows - NA_KEY_ROWS)
        qr = r + jnp.arange(NA_ROW_BLOCK)
        kr = ws + jnp.arange(NA_KEY_ROWS)
        r0 = jnp.clip(qr - NA_WIN_ROWS // 2, 0, rows - NA_WIN_ROWS)
        row_in = (kr[None, :] >= r0[:, None]) & (kr[None, :] < r0[:, None] + NA_WIN_ROWS)
        dr_idx = jnp.clip(kr[None, :] - qr[:, None] + NA_WIN_ROWS - 1, 0, 2 * NA_WIN_ROWS - 2)
        b = jnp.einsum('hrqk,abr->haqbk', by_col,
                       jax.nn.one_hot(dr_idx, 2 * NA_WIN_ROWS - 1, dtype=jnp.float32), precision=exact)
        ok = row_in[:, None, :, None] & col_in[None, :, None, :]
        b = jnp.where(ok[None], b, NEG_INF)
        tiles.append(b.reshape(rpb.shape[0], NA_ROW_BLOCK * GRID_W, NA_KEY_ROWS * GRID_W))
    return jnp.stack(tiles)


def neighbourhood_attention(q, k, v, kc, vc, rpb):
    L = q.shape[0]
    Lc = kc.shape[0]
    rows = L // GRID_W
    nblk = rows // NA_ROW_BLOCK
    tq = NA_ROW_BLOCK * GRID_W
    nk = NA_KEY_ROWS * GRID_W
    bias = na_bias_tiles(rpb, rows)
    gw = NA_HEAD_GROUP * HEAD_DIM
    slab = pl.BlockSpec((L, gw), lambda h, i: (0, h), pipeline_mode=pl.Buffered(1))
    cslab = pl.BlockSpec((Lc, gw), lambda h, i: (0, h))
    variant = lambda h, i: (jnp.where(i == 0, 0, jnp.where(i == nblk - 1, 2, 1)), h, 0, 0)
    return pl.pallas_call(
        partial(_na_kernel, rows=rows),
        grid=(NA_HEADS // NA_HEAD_GROUP, nblk),
        in_specs=[pl.BlockSpec((tq, gw), lambda h, i: (i, h)), slab, slab, cslab, cslab,
                  pl.BlockSpec((1, NA_HEAD_GROUP, tq, nk), variant)],
        out_specs=pl.BlockSpec((tq, gw), lambda h, i: (i, h)),
        out_shape=jax.ShapeDtypeStruct((L, NA_HEADS * HEAD_DIM), jnp.bfloat16),
        compiler_params=pltpu.CompilerParams(
            dimension_semantics=("parallel", "parallel"), vmem_limit_bytes=VMEM_LIMIT_BYTES),
        name="neighbourhood_attention",
    )(q, k, v, kc, vc, bias)


DFT_N1 = 128
DFT_K1 = DFT_N1 // 2 + 1
DFT_K1_PAD = 72


def dft_constants(L):
    N = 2 * L
    N2 = N // DFT_N1
    f32, bf = jnp.float32, jnp.bfloat16
    k1 = jnp.arange(DFT_K1, dtype=jnp.int32)
    n1 = jnp.arange(DFT_N1, dtype=jnp.int32)
    th = ((k1[:, None] * n1[None, :]) % DFT_N1).astype(f32) * (2.0 * math.pi / DFT_N1)
    pad = ((0, DFT_K1_PAD - DFT_K1), (0, 0))
    rows_fwd = jnp.concatenate([jnp.pad(jnp.cos(th), pad), jnp.pad(-jnp.sin(th), pad)], axis=0)
    ck = jnp.where((k1 == 0) | (k1 == DFT_N1 // 2), 1.0, 2.0)[:, None]
    half = DFT_N1 // 2
    rows_inv = jnp.concatenate([jnp.pad(ck * jnp.cos(th[:, :half]), pad),
                                jnp.pad(-ck * jnp.sin(th[:, :half]), pad)], axis=0).T
    n2 = jnp.arange(N2, dtype=jnp.int32)
    kk = k1[:, None] + DFT_N1 * n2[None, :]
    ang = ((kk[:, :, None] * n2[None, None, :]) % N).astype(f32) * (2.0 * math.pi / N)
    gr, gi = jnp.cos(ang), -jnp.sin(ang)
    mid_fwd = jnp.concatenate([jnp.concatenate([gr, -gi], axis=2),
                               jnp.concatenate([gi, gr], axis=2)], axis=1)
    return {
        "rows_fwd": rows_fwd.astype(bf), "rows_fwd_half": rows_fwd[:, :half].astype(bf),
        "rows_inv": rows_inv.astype(bf),
        "mid_fwd": mid_fwd.astype(bf),
    }


def _spectral_fwd_kernel(a_ref, g_ref, o_ref):
    n2 = a_ref.shape[2]
    x = (jnp.dot(g_ref[0, :, :n2], a_ref[0, 0], preferred_element_type=jnp.float32)
         + jnp.dot(g_ref[0, :, n2:], a_ref[1, 0], preferred_element_type=jnp.float32))
    o_ref[0, 0] = x[:n2].astype(o_ref.dtype)
    o_ref[1, 0] = x[n2:].astype(o_ref.dtype)


def spectral_fwd(a, mid_fwd, cb=2048):
    _, _, n2, C = a.shape
    cb = min(cb, C)
    return pl.pallas_call(
        _spectral_fwd_kernel,
        grid=(DFT_K1, C // cb),
        in_specs=[pl.BlockSpec((2, 1, n2, cb), lambda k, c: (0, k, 0, c)),
                  pl.BlockSpec((1, 2 * n2, 2 * n2), lambda k, c: (k, 0, 0))],
        out_specs=pl.BlockSpec((2, 1, n2, cb), lambda k, c: (0, k, 0, c)),
        out_shape=jax.ShapeDtypeStruct((2, DFT_K1, n2, C), jnp.bfloat16),
        compiler_params=pltpu.CompilerParams(
            dimension_semantics=("parallel", "parallel"), vmem_limit_bytes=VMEM_LIMIT_BYTES),
        name="spectral_fwd",
    )(a, mid_fwd)


_TN = (((0,), (0,)), ((), ()))


def _spectral_mid_kernel(a_ref, gf_ref, ks_ref, o_ref):
    n2 = a_ref.shape[2]
    k1 = pl.program_id(0)

    @pl.when(k1 < DFT_K1)
    def _():
        x = (jnp.dot(gf_ref[0, :, :n2], a_ref[0, 0], preferred_element_type=jnp.float32)
             + jnp.dot(gf_ref[0, :, n2:], a_ref[1, 0], preferred_element_type=jnp.float32))
        xr, xi = x[:n2], x[n2:]
        kr, ki = ks_ref[0, 0].astype(jnp.float32), ks_ref[1, 0].astype(jnp.float32)
        yr = (xr * kr - xi * ki).astype(jnp.bfloat16)
        yi = (xr * ki + xi * kr).astype(jnp.bfloat16)
        b = (lax.dot_general(gf_ref[0, :n2, :], yr, _TN, preferred_element_type=jnp.float32)
             + lax.dot_general(gf_ref[0, n2:, :], yi, _TN, preferred_element_type=jnp.float32))
        o_ref[0, 0] = b[:n2].astype(o_ref.dtype)
        o_ref[1, 0] = b[n2:].astype(o_ref.dtype)

    @pl.when(k1 >= DFT_K1)
    def _():
        o_ref[...] = jnp.zeros_like(o_ref)


def spectral_mid(a, consts, kspec, col_off, cb=1024):
    _, _, n2, C = a.shape
    cb = min(cb, C)
    off = col_off // cb
    kc = lambda k: jnp.minimum(k, DFT_K1 - 1)
    return pl.pallas_call(
        _spectral_mid_kernel,
        grid=(DFT_K1_PAD, C // cb),
        in_specs=[pl.BlockSpec((2, 1, n2, cb), lambda k, c: (0, k, 0, c)),
                  pl.BlockSpec((1, 2 * n2, 2 * n2), lambda k, c: (kc(k), 0, 0)),
                  pl.BlockSpec((2, 1, n2, cb), lambda k, c: (0, kc(k), 0, off + c))],
        out_specs=pl.BlockSpec((2, 1, n2, cb), lambda k, c: (0, k, 0, c)),
        out_shape=jax.ShapeDtypeStruct(a.shape, jnp.bfloat16),
        compiler_params=pltpu.CompilerParams(
            dimension_semantics=("parallel", "parallel"), vmem_limit_bytes=VMEM_LIMIT_BYTES),
        name="spectral_mid",
    )(a, consts["mid_fwd"], kspec)


DFT_ROW_GROUP = 16
DFT_ROW_LANES = 256


def _rows_fwd_kernel(f_ref, u_ref, o_ref):
    x = pltpu.einshape("abc->bac", u_ref[...].astype(jnp.float32))
    rs = [jnp.dot(f_ref[...], x[s].astype(jnp.bfloat16), preferred_element_type=jnp.float32)
          for s in range(x.shape[0])]
    o_ref[...] = pltpu.einshape("abc->bac", jnp.stack(rs)).astype(o_ref.dtype)


def dft_rows_fwd(rows_mat, u3):
    kn, n2, C = u3.shape
    grp = min(DFT_ROW_GROUP, n2)
    tc = min(DFT_ROW_LANES, C)
    m = rows_mat.shape[0]
    return pl.pallas_call(
        _rows_fwd_kernel,
        grid=(n2 // grp, C // tc),
        in_specs=[pl.BlockSpec((m, kn), lambda j, c: (0, 0)),
                  pl.BlockSpec((kn, grp, tc), lambda j, c: (0, j, c))],
        out_specs=pl.BlockSpec((m, grp, tc), lambda j, c: (0, j, c)),
        out_shape=jax.ShapeDtypeStruct((m, n2, C), jnp.bfloat16),
        compiler_params=pltpu.CompilerParams(
            dimension_semantics=("parallel", "parallel"), vmem_limit_bytes=VMEM_LIMIT_BYTES),
        name="dft_rows_fwd",
    )(rows_mat, u3)


def _rows_inv_kernel(f_ref, b_ref, u_ref, gate_ref, scale_ref, d_ref, o_ref):
    x = pltpu.einshape("abc->bac", b_ref[...].astype(jnp.float32))
    ys = [jnp.dot(f_ref[...], x[s].astype(jnp.bfloat16), preferred_element_type=jnp.float32)
          for s in range(x.shape[0])]
    y = pltpu.einshape("abc->bac", jnp.stack(ys))
    o_ref[...] = (gate_ref[...] * (y * scale_ref[...] + u_ref[...] * d_ref[...])).astype(o_ref.dtype)


def dft_rows_inv(rows_inv, b3, u3, gate3, scale, d_skip, out_dtype):
    nr, n2, C = u3.shape
    grp = min(DFT_ROW_GROUP, n2)
    tc = min(DFT_ROW_LANES, C)
    blk = pl.BlockSpec((nr, grp, tc), lambda j, c: (0, j, c))
    vec = pl.BlockSpec((1, 1, tc), lambda j, c: (0, 0, c))
    return pl.pallas_call(
        _rows_inv_kernel,
        grid=(n2 // grp, C // tc),
        in_specs=[pl.BlockSpec(rows_inv.shape, lambda j, c: (0, 0)),
                  pl.BlockSpec((b3.shape[0], grp, tc), lambda j, c: (0, j, c)), blk, blk, vec, vec],
        out_specs=blk,
        out_shape=jax.ShapeDtypeStruct((nr, n2, C), out_dtype),
        compiler_params=pltpu.CompilerParams(
            dimension_semantics=("parallel", "parallel"), vmem_limit_bytes=VMEM_LIMIT_BYTES),
        name="dft_rows_inv",
    )(rows_inv, b3, u3, gate3, scale.reshape(1, 1, C), d_skip.reshape(1, 1, C))


def _conv3_kernel(z_ref, prev_ref, next_ref, w_ref, b_ref, o_ref):
    i = pl.program_id(0)
    z = z_ref[...]
    tm = z.shape[0]
    row = lax.broadcasted_iota(jnp.int32, z.shape, 0)
    prev_row = jnp.where(i > 0, prev_ref[7:8, :], 0.0)
    next_row = jnp.where(i < pl.num_programs(0) - 1, next_ref[0:1, :], 0.0)
    zp = jnp.where(row == 0, prev_row, pltpu.roll(z, 1, 0))
    zn = jnp.where(row == tm - 1, next_row, pltpu.roll(z, tm - 1, 0))
    o_ref[...] = zp * w_ref[0:1, :] + z * w_ref[1:2, :] + zn * w_ref[2:3, :] + b_ref[...]


def short_conv3_part(z, w, b, part, width):
    L = z.shape[0]
    tm = _pick(L, (512, 256, 128, 64, 32, 16, 8))
    tc = _pick(width, (1024, 512, 256, 128))
    off = part * width // tc
    halo = 8
    return pl.pallas_call(
        _conv3_kernel,
        grid=(L // tm, width // tc),
        in_specs=[pl.BlockSpec((tm, tc), lambda i, j: (i, off + j)),
                  pl.BlockSpec((halo, tc), lambda i, j: (jnp.maximum(i * (tm // halo) - 1, 0), off + j)),
                  pl.BlockSpec((halo, tc), lambda i, j: (jnp.minimum((i + 1) * (tm // halo), L // halo - 1), off + j)),
                  pl.BlockSpec((3, tc), lambda i, j: (0, off + j)),
                  pl.BlockSpec((1, tc), lambda i, j: (0, off + j))],
        out_specs=pl.BlockSpec((tm, tc), lambda i, j: (i, j)),
        out_shape=jax.ShapeDtypeStruct((L, width), jnp.float32),
        compiler_params=pltpu.CompilerParams(
            dimension_semantics=("parallel", "parallel"), vmem_limit_bytes=VMEM_LIMIT_BYTES),
        name="short_conv3",
    )(z, z, z, w, b.reshape(1, -1))


FILTER_PAD = 128
FILTER_LANES = 512


def _filter_rows_kernel(feats_ref, feats0_ref, fw1_ref, fb1_ref, fw2_ref, fb2_ref, fr_ref, fw3_ref, rate_ref,
                        rows_ref, a_ref, asum_ref, hid_ref, hid0_ref):
    g = pl.program_id(0)
    exact = lax.Precision.HIGHEST
    n_dir, prow, nf2 = feats_ref.shape
    hid = fw3_ref.shape[1]
    half = rows_ref.shape[1] // 2
    grp = 2 * prow // half

    def hidden(feats):
        h = jnp.sin(fr_ref[0:1, :] * (jnp.dot(feats, fw1_ref[...], precision=exact,
                                              preferred_element_type=jnp.float32) + fb1_ref[...]))
        return jnp.sin(fr_ref[1:2, :] * (jnp.dot(h, fw2_ref[...], precision=exact,
                                                 preferred_element_type=jnp.float32) + fb2_ref[...]))

    @pl.when(pl.program_id(1) == 0)
    def _():
        for d in range(n_dir):
            hid_ref[d] = hidden(feats_ref[d])
        hid0_ref[...] = hidden(feats0_ref[...])

    def taps(h, t, d):
        k = jnp.dot(h.astype(jnp.bfloat16), fw3_ref[d].astype(jnp.bfloat16), preferred_element_type=jnp.float32)
        return k * (jnp.exp(-t * rate_ref[d]) + HY_DECAY_SHIFT)

    def direction(d):
        h2, f = hid_ref[d], feats_ref[d]
        return jnp.concatenate([taps(h2[:, :hid], f[:, 0:1], d),
                                taps(h2[:, hid:], f[:, nf2 // 2:nf2 // 2 + 1], d)], axis=0)

    k_fwd = direction(0)
    k_bwd = direction(1)
    lag0_back = taps(hid0_ref[:, :hid], feats0_ref[:, 0:1], 1)[0:1, :]
    first = (lax.broadcasted_iota(jnp.int32, k_fwd.shape, 0) == 0) & (g == 0)
    k_fwd = k_fwd + jnp.where(first, lag0_back, 0.0)
    k_bwd = jnp.where(first, 0.0, k_bwd)
    asum_ref[0] = jnp.sum(jnp.abs(k_fwd), axis=0, keepdims=True) + jnp.sum(jnp.abs(k_bwd), axis=0, keepdims=True)
    kf = k_fwd.astype(jnp.bfloat16)
    kb = k_bwd.astype(jnp.bfloat16)
    rs = [jnp.dot(rows_ref[:, :half], kf[s * half:(s + 1) * half], preferred_element_type=jnp.float32)
          + jnp.dot(rows_ref[:, half:], kb[s * half:(s + 1) * half], preferred_element_type=jnp.float32)
          for s in range(grp)]
    a_ref[...] = pltpu.einshape("abc->bac", jnp.stack(rs)).astype(a_ref.dtype)


def _block_diag2(w):
    z = jnp.zeros_like(w)
    return jnp.concatenate([jnp.concatenate([w, z], axis=1), jnp.concatenate([z, w], axis=1)], axis=0)


def hyena_filter_rows(L, p, rows_fwd):
    f32 = jnp.float32
    N = 2 * L
    n2 = N // DFT_N1
    half = DFT_N1 // 2
    C2 = HY_ORDER * HY_WIDTH
    grp = min(DFT_ROW_GROUP, n2)
    tc = min(FILTER_LANES, C2)
    prow = grp * half // 2
    n = (n2 * jnp.arange(DFT_N1, dtype=jnp.int32)[None, :] + jnp.arange(n2, dtype=jnp.int32)[:, None])
    pos = jnp.where(n < L, n, N - n).astype(f32)[..., None]
    bands = jnp.linspace(1e-4, HY_BANDS - 1, HY_BANDS, dtype=f32)
    ang = (2.0 * math.pi / L) * bands * pos
    feats = jnp.concatenate([pos / (L - 1), jnp.cos(ang), -jnp.sin(ang)], axis=-1)
    feats = jnp.pad(feats, ((0, 0), (0, 0), (0, FILTER_PAD - feats.shape[-1])))
    feats = feats.reshape(n2 // grp, grp, 2, half, FILTER_PAD).transpose(2, 0, 1, 3, 4)
    feats = feats.reshape(2, n2 // grp, 2, prow, FILTER_PAD).transpose(0, 1, 3, 2, 4)
    feats = feats.reshape(2, (n2 // grp) * prow, 2 * FILTER_PAD)
    feats0 = jnp.broadcast_to(feats[0, 0:1], (8, 2 * FILTER_PAD))
    hid = p['hy_fw2'].shape[0]
    fw1 = _block_diag2(jnp.pad(p['hy_fw1'].astype(f32), ((0, FILTER_PAD - p['hy_fw1'].shape[0]), (0, 0))))
    fb1 = jnp.tile(p['hy_fb1'].astype(f32), 2).reshape(1, -1)
    fw2 = _block_diag2(p['hy_fw2'].astype(f32))
    fb2 = jnp.tile(p['hy_fb2'].astype(f32), 2).reshape(1, -1)
    fr = jnp.tile(p['hy_freq'].astype(f32), (1, 2))
    fw3 = p['hy_fw3'].astype(f32).reshape(hid, HY_DIRS, C2).transpose(1, 0, 2)
    rate = jnp.abs(p['hy_decay'].astype(f32)).reshape(HY_DIRS, 1, C2)
    full = lambda shape: pl.BlockSpec(shape, lambda g, c: (0,) * len(shape))
    m = rows_fwd.shape[0]
    a, asum = pl.pallas_call(
        _filter_rows_kernel,
        grid=(n2 // grp, C2 // tc),
        in_specs=[pl.BlockSpec((2, prow, 2 * FILTER_PAD), lambda g, c: (0, g, 0)), full((8, 2 * FILTER_PAD)),
                  full((2 * FILTER_PAD, 2 * hid)), full((1, 2 * hid)),
                  full((2 * hid, 2 * hid)), full((1, 2 * hid)), full((2, 2 * hid)),
                  pl.BlockSpec((HY_DIRS, hid, tc), lambda g, c: (0, 0, c)),
                  pl.BlockSpec((HY_DIRS, 1, tc), lambda g, c: (0, 0, c)),
                  full(rows_fwd.shape)],
        out_specs=[pl.BlockSpec((m, grp, tc), lambda g, c: (0, g, c)),
                   pl.BlockSpec((1, 1, tc), lambda g, c: (g, 0, c))],
        out_shape=[jax.ShapeDtypeStruct((m, n2, C2), jnp.bfloat16),
                   jax.ShapeDtypeStruct((n2 // grp, 1, C2), f32)],
        scratch_shapes=[pltpu.VMEM((2, prow, 2 * hid), f32), pltpu.VMEM((8, 2 * hid), f32)],
        compiler_params=pltpu.CompilerParams(
            dimension_semantics=("parallel", "arbitrary"), vmem_limit_bytes=VMEM_LIMIT_BYTES),
        name="hyena_filter_rows",
    )(feats, feats0, fw1, fb1, fw2, fb2, fr, fw3, rate, rows_fwd)
    return a, jnp.sum(asum, axis=(0, 1))


def hyena_latent(z, p, consts):
    L = z.shape[0]
    C = HY_WIDTH
    N = 2 * L
    n2 = N // DFT_N1
    half = DFT_N1 // 2
    v, x1, x2 = (short_conv3_part(z, p['hy_conv_w'], p['hy_conv_b'], i, C) for i in range(3))
    ka, asum = hyena_filter_rows(L, p, consts["rows_fwd"])
    kspec = spectral_fwd(ka.reshape(2, DFT_K1_PAD, n2, HY_ORDER * C), consts["mid_fwd"])
    scale = 1.0 / (N * asum)

    def long_conv(u, gate, order, out_dtype):
        u3 = u.reshape(half, n2, C)
        ua = dft_rows_fwd(consts["rows_fwd_half"], u3)
        bm = spectral_mid(ua.reshape(2, DFT_K1_PAD, n2, C), consts, kspec, order * C)
        y = dft_rows_inv(consts["rows_inv"], bm.reshape(2 * DFT_K1_PAD, n2, C), u3, gate.reshape(half, n2, C),
                         scale[order * C:(order + 1) * C], p['hy_d'][order].astype(jnp.float32), out_dtype)
        return y.reshape(L, C)

    y1 = long_conv(v, x1, 0, jnp.float32)
    return long_conv(y1, x2, 1, jnp.bfloat16)


def rms_norm(x, g):
    xf = x.astype(jnp.float32)
    y = xf * lax.rsqrt(jnp.mean(xf * xf, axis=-1, keepdims=True) + RMS_EPS)
    return (y * g.astype(jnp.float32)).astype(x.dtype)


def modulate(h, shift, scale):
    return h * (1 + scale) + shift


def short_conv3(z, w, b):
    L = z.shape[1]
    zp = jnp.pad(z, ((0, 0), (1, 1), (0, 0)))
    return zp[:, :L] * w[0] + zp[:, 1:L + 1] * w[1] + zp[:, 2:] * w[2] + b


def hyena_filter_spectra(L, fw1, fb1, fw2, fb2, fw3, freq, decay):
    f32 = jnp.float32
    t = jnp.linspace(0.0, 1.0, L, dtype=f32)[:, None]
    pos = jnp.arange(L, dtype=f32)[:, None]
    bands = jnp.linspace(1e-4, HY_BANDS - 1, HY_BANDS, dtype=f32)[None, :]
    ang = (2.0 * math.pi / L) * bands * pos
    feats = jnp.concatenate([t, jnp.cos(ang), -jnp.sin(ang)], axis=-1)
    fr = freq.astype(f32)
    h = jnp.sin(fr[0] * (feats @ fw1.astype(f32) + fb1.astype(f32)))
    h = jnp.sin(fr[1] * (h @ fw2.astype(f32) + fb2.astype(f32)))
    h = (h @ fw3.astype(f32)).reshape(L, HY_DIRS, HY_ORDER, HY_WIDTH)
    rate = jnp.abs(decay.astype(f32)).reshape(HY_DIRS, HY_ORDER, HY_WIDTH)
    h = h * (jnp.exp(-t[:, :, None, None] * rate) + HY_DECAY_SHIFT)
    hf, hb = h[:, 0], h[:, 1]
    k = jnp.concatenate([hf[:1] + hb[:1], hf[1:], jnp.zeros_like(hf[:1]), hb[1:][::-1]], axis=0)
    k = k / jnp.sum(jnp.abs(k), axis=0, keepdims=True)
    return jnp.fft.rfft(k, axis=0)


def bidir_long_conv(u, k_spec, d_skip):
    L = u.shape[1]
    uf = jnp.fft.rfft(u.astype(jnp.float32), n=2 * L, axis=1)
    y = jnp.fft.irfft(uf * k_spec[None], n=2 * L, axis=1)[:, :L]
    return (y + u.astype(jnp.float32) * d_skip.astype(jnp.float32)).astype(u.dtype)


def hyena_branch(z, p):
    L = z.shape[1]
    z = short_conv3(z, p['hy_conv_w'], p['hy_conv_b'])
    v, x1, x2 = jnp.split(z, 3, axis=-1)
    k_spec = hyena_filter_spectra(L, p['hy_fw1'], p['hy_fb1'], p['hy_fw2'], p['hy_fb2'],
                                  p['hy_fw3'], p['hy_freq'], p['hy_decay'])
    y = x1 * bidir_long_conv(v, k_spec[:, 0], p['hy_d'][0])
    return x2 * bidir_long_conv(y, k_spec[:, 1], p['hy_d'][1])


def _resident(shape, index_map):
    return pl.BlockSpec(shape, index_map, pipeline_mode=pl.Buffered(1))


def _merge_kernel(y0_ref, y1_ref, y2_ref, g_ref, wb_ref, o_ref):
    d = o_ref.shape[1]
    m = None
    for b, y_ref in enumerate((y0_ref, y1_ref, y2_ref)):
        t = g_ref[:, b * d:(b + 1) * d] * jnp.dot(y_ref[...], wb_ref[b], preferred_element_type=jnp.float32)
        m = t if m is None else m + t
    o_ref[...] = m.astype(o_ref.dtype)


def merge_gated(y_hy, y_wa, y_na, gates, w_branch):
    M, wbr = y_hy.shape
    D = w_branch.shape[2]
    tm = _pick(M, (256, 128, 64, 32, 16, 8))
    yspec = pl.BlockSpec((tm, wbr), lambda i: (i, 0))
    return pl.pallas_call(
        _merge_kernel,
        grid=(M // tm,),
        in_specs=[yspec, yspec, yspec, pl.BlockSpec((tm, N_BRANCH * D), lambda i: (i, 0)),
                  _resident((N_BRANCH, wbr, D), lambda i: (0, 0, 0))],
        out_specs=pl.BlockSpec((tm, D), lambda i: (i, 0)),
        out_shape=jax.ShapeDtypeStruct((M, D), jnp.bfloat16),
        compiler_params=pltpu.CompilerParams(
            dimension_semantics=("parallel",), vmem_limit_bytes=VMEM_LIMIT_BYTES),
        name="merge_gated",
    )(y_hy, y_wa, y_na, gates, w_branch)


ROUTER_PAD = 128


def _out_proj_kernel(m_ref, w_ref, x_ref, gate_ref, g2_ref, shift_ref, scale_ref, wr_hi_ref, wr_lo_ref,
                     x_out, h_out, lg_out):
    x = x_ref[...] + gate_ref[...] * jnp.dot(m_ref[...], w_ref[...], preferred_element_type=jnp.float32)
    x_out[...] = x
    h = x * lax.rsqrt(jnp.mean(x * x, axis=-1, keepdims=True) + RMS_EPS) * g2_ref[...]
    h = h * (1.0 + scale_ref[...]) + shift_ref[...]
    h_out[...] = h.astype(h_out.dtype)
    h_hi = h.astype(jnp.bfloat16)
    h_lo = (h - h_hi.astype(jnp.float32)).astype(jnp.bfloat16)
    lg_out[...] = (jnp.dot(h_hi, wr_hi_ref[...], preferred_element_type=jnp.float32)
                   + jnp.dot(h_lo, wr_hi_ref[...], preferred_element_type=jnp.float32)
                   + jnp.dot(h_hi, wr_lo_ref[...], preferred_element_type=jnp.float32))


def out_project(m, w_out, x, gate, norm_g, shift, scale, w_router):
    M, D = x.shape
    tm = _pick(M, (256, 128, 64, 32, 16, 8))
    row = pl.BlockSpec((tm, D), lambda i: (i, 0))
    vec = pl.BlockSpec((1, D), lambda i: (0, 0))
    wr = jnp.pad(w_router.astype(jnp.float32), ((0, 0), (0, ROUTER_PAD - w_router.shape[1])))
    wr_hi = wr.astype(jnp.bfloat16)
    wr_lo = (wr - wr_hi.astype(jnp.float32)).astype(jnp.bfloat16)
    v2 = lambda a: a.reshape(1, D).astype(jnp.float32)
    return pl.pallas_call(
        _out_proj_kernel,
        grid=(M // tm,),
        in_specs=[row, _resident((D, D), lambda i: (0, 0)), row, vec, vec, vec, vec,
                  _resident((D, ROUTER_PAD), lambda i: (0, 0)), _resident((D, ROUTER_PAD), lambda i: (0, 0))],
        out_specs=[row, row, pl.BlockSpec((tm, ROUTER_PAD), lambda i: (i, 0))],
        out_shape=[jax.ShapeDtypeStruct((M, D), jnp.float32), jax.ShapeDtypeStruct((M, D), jnp.bfloat16),
                   jax.ShapeDtypeStruct((M, ROUTER_PAD), jnp.float32)],
        compiler_params=pltpu.CompilerParams(
            dimension_semantics=("parallel",), vmem_limit_bytes=VMEM_LIMIT_BYTES),
        name="out_project",
    )(m, w_out, x, v2(gate), v2(norm_g), v2(shift), v2(scale), wr_hi, wr_lo)


def _gate_up_kernel(x_ref, wg_ref, wu_ref, o_ref, wg_bf, wu_bf):
    @pl.when(pl.program_id(2) == 0)
    def _():
        wg_bf[...] = wg_ref[...].astype(jnp.bfloat16)
        wu_bf[...] = wu_ref[...].astype(jnp.bfloat16)

    x = x_ref[0]
    a = jnp.dot(x, wg_bf[...], preferred_element_type=jnp.float32)
    u = jnp.dot(x, wu_bf[...], preferred_element_type=jnp.float32)
    o_ref[0] = (a * jax.nn.sigmoid(a) * u).astype(o_ref.dtype)


def _down_kernel(h_ref, wd_ref, gsel_ref, gate_ref, o_ref, wd_bf):
    @pl.when(pl.program_id(1) == 0)
    def _():
        wd_bf[...] = wd_ref[...].astype(jnp.bfloat16)

    y = jnp.dot(h_ref[0], wd_bf[...], preferred_element_type=jnp.float32)
    o_ref[0] = (y * gsel_ref[0] * gate_ref[...]).astype(o_ref.dtype)


def expert_ffn(xe, w_gate, w_up, w_down, layer, gsel, out_gate):
    E, cap, D = xe.shape
    F = w_gate.shape[3]
    tm = _pick(cap, (512, 256, 128, 64, 32, 16, 8))
    tn = _pick(F, (1024, 512, 256, 128))
    seq = pltpu.CompilerParams(dimension_semantics=("parallel", "parallel", "arbitrary"),
                               vmem_limit_bytes=VMEM_LIMIT_BYTES)
    h = pl.pallas_call(
        _gate_up_kernel,
        grid=(E, F // tn, cap // tm),
        in_specs=[pl.BlockSpec((1, tm, D), lambda e, j, i: (e, i, 0)),
                  pl.BlockSpec((None, None, D, tn), lambda e, j, i: (layer, e, 0, j)),
                  pl.BlockSpec((None, None, D, tn), lambda e, j, i: (layer, e, 0, j))],
        out_specs=pl.BlockSpec((1, tm, tn), lambda e, j, i: (e, i, j)),
        out_shape=jax.ShapeDtypeStruct((E, cap, F), jnp.bfloat16),
        scratch_shapes=[pltpu.VMEM((D, tn), jnp.bfloat16), pltpu.VMEM((D, tn), jnp.bfloat16)],
        compiler_params=seq,
        name="expert_gate_up",
    )(xe, w_gate, w_up)
    return pl.pallas_call(
        _down_kernel,
        grid=(E, cap // tm),
        in_specs=[pl.BlockSpec((1, tm, F), lambda e, i: (e, i, 0)),
                  pl.BlockSpec((None, None, F, D), lambda e, i: (layer, e, 0, 0)),
                  pl.BlockSpec((1, tm, 1), lambda e, i: (e, i, 0)),
                  pl.BlockSpec((1, D), lambda e, i: (0, 0))],
        out_specs=pl.BlockSpec((1, tm, D), lambda e, i: (e, i, 0)),
        out_shape=jax.ShapeDtypeStruct((E, cap, D), jnp.bfloat16),
        scratch_shapes=[pltpu.VMEM((F, D), jnp.bfloat16)],
        compiler_params=pltpu.CompilerParams(dimension_semantics=("parallel", "arbitrary"),
                                             vmem_limit_bytes=VMEM_LIMIT_BYTES),
        name="expert_down",
    )(h, w_down, gsel.reshape(E, cap, 1).astype(jnp.float32), out_gate.reshape(1, D).astype(jnp.float32))


SEG_ROWS = 512
SEG_WIN = 512
SEG_LANES = 256


def _segment_add_kernel(first_ref, last_ref, tok_ref, ye_ref, x_ref, o_ref, *, n_tokens, win):
    c = pl.program_id(1)

    @pl.when(c == 0)
    def _():
        o_ref[...] = x_ref[...]

    base0 = (first_ref[c] // 8) * 8
    n_win = (last_ref[c] - base0) // win + 1
    tok = tok_ref[0]
    rows = ye_ref[...]

    def window(w, carry):
        lo = base0 + w * win
        base = pl.multiple_of(jnp.minimum(lo, n_tokens - win), 8)
        rid = base + lax.broadcasted_iota(jnp.int32, (win, tok.shape[1]), 0)
        onehot = ((rid == tok) & (tok >= lo)).astype(jnp.bfloat16)
        o_ref[pl.ds(base, win), :] += jnp.dot(onehot, rows, preferred_element_type=jnp.float32)
        return carry

    lax.fori_loop(0, n_win, window, 0)


def segment_add(x, ye_sorted, tok_sorted):
    N, D = x.shape
    P = ye_sorted.shape[0]
    rows = min(SEG_ROWS, P)
    win = min(SEG_WIN, N)
    tc = min(SEG_LANES, D)
    n_chunks = P // rows
    tok3 = tok_sorted.reshape(n_chunks, 1, rows)
    return pl.pallas_call(
        partial(_segment_add_kernel, n_tokens=N, win=win),
        grid_spec=pltpu.PrefetchScalarGridSpec(
            num_scalar_prefetch=2,
            grid=(D // tc, n_chunks),
            in_specs=[pl.BlockSpec((1, 1, rows), lambda d, c, f, l: (c, 0, 0)),
                      pl.BlockSpec((rows, tc), lambda d, c, f, l: (c, d)),
                      pl.BlockSpec((N, tc), lambda d, c, f, l: (0, d), pipeline_mode=pl.Buffered(1))],
            out_specs=pl.BlockSpec((N, tc), lambda d, c, f, l: (0, d)),
        ),
        out_shape=jax.ShapeDtypeStruct((N, D), jnp.float32),
        compiler_params=pltpu.CompilerParams(
            dimension_semantics=("parallel", "arbitrary"), vmem_limit_bytes=VMEM_LIMIT_BYTES),
        name="segment_add",
    )(tok3[:, 0, 0], tok3[:, 0, rows - 1], tok3, ye_sorted, x)


def expert_choice_latent(x1, h2, logits, out_gate, w_gate, w_up, w_down, layer):
    N, D = x1.shape
    cap = EC_CAPACITY_FACTOR * N // N_EXPERTS
    aff = jax.nn.softmax(logits[:, :N_EXPERTS], axis=-1)
    gsel, idx = lax.top_k(aff.T, cap)
    ye = expert_ffn(h2[idx], w_gate, w_up, w_down, layer, gsel, out_gate)
    tok = idx.reshape(-1)
    order = jnp.argsort(tok)
    return segment_add(x1, ye.reshape(-1, D)[order], tok[order])


def _ctx_attn_kernel(sink_ref, q_ref, k_ref, v_ref, o_ref, *, use_sink):
    s = lax.dot_general(q_ref[...], k_ref[...], _NT, preferred_element_type=jnp.float32)
    m = jnp.max(s, axis=-1, keepdims=True)
    if use_sink:
        sk = sink_ref[pl.program_id(0)]
        m = jnp.maximum(m, sk)
    p = jnp.exp(s - m)
    denom = jnp.sum(p, axis=-1, keepdims=True)
    if use_sink:
        denom = denom + jnp.exp(sk - m)
    o = jnp.dot(p.astype(v_ref.dtype), v_ref[...], preferred_element_type=jnp.float32)
    o_ref[...] = (o / denom).astype(o_ref.dtype)


def context_attention(q, k, v, sink, n_heads, n_kv_heads):
    Lc = q.shape[0]
    group = n_heads // n_kv_heads
    use_sink = sink is not None
    sink = jnp.zeros((n_heads,), jnp.float32) if sink is None else sink.astype(jnp.float32)
    kv = pl.BlockSpec((Lc, HEAD_DIM), lambda h, s: (0, h // group))
    return pl.pallas_call(
        partial(_ctx_attn_kernel, use_sink=use_sink),
        grid_spec=pltpu.PrefetchScalarGridSpec(
            num_scalar_prefetch=1,
            grid=(n_heads,),
            in_specs=[pl.BlockSpec((Lc, HEAD_DIM), lambda h, s: (0, h)), kv, kv],
            out_specs=pl.BlockSpec((Lc, HEAD_DIM), lambda h, s: (0, h)),
        ),
        out_shape=jax.ShapeDtypeStruct((Lc, n_heads * HEAD_DIM), jnp.bfloat16),
        compiler_params=pltpu.CompilerParams(
            dimension_semantics=("parallel",), vmem_limit_bytes=VMEM_LIMIT_BYTES),
        name="context_attention",
    )(sink, q, k, v)


def trunk_layer(x, ctx, c, c_ctx, p, stacked, layer, update_ctx, consts):
    L = x.shape[0]
    bf = jnp.bfloat16
    w_in = stacked['w_in']
    cond = jnp.pad(jnp.concatenate([jax.nn.silu(c), jax.nn.silu(c_ctx)[None]], axis=0), ((0, 6), (0, 0)))
    mod = matmul(cond, stacked['w_mod'], layer=layer) + p['b_mod']
    mx = jnp.split(mod[0], N_MOD)
    mc = jnp.split(mod[1], N_MOD)
    qk_scale = HEAD_DIM ** -0.5
    nwq = WA_HEADS * HEAD_DIM
    nkv = WA_KV_HEADS * HEAD_DIM
    nna = NA_HEADS * HEAD_DIM
    proj = lambda a, *args: project(a, w_in, layer, *args)

    def mix_and_ffn(res, y_hy, y_wa, y_na, gates, m_vec):
        m = merge_gated(y_hy, y_wa, y_na, gates, p['w_branch'].astype(bf))
        r1, h2, logits = out_project(m, p['w_out'].astype(bf), res, m_vec[2], p['norm2_g'], m_vec[3], m_vec[4],
                                     p['w_router'])
        return expert_choice_latent(r1, h2, logits, m_vec[5], stacked['w_gate'], stacked['w_up'],
                                    stacked['w_down'], layer)

    hcb = norm_modulate(ctx, p['norm1_g'], mc[0], mc[1])
    kc_wa = proj(hcb, OFF_WA_KV, nkv, "headnorm", bf, p['wa_k_norm'])
    vc_wa = proj(hcb, OFF_WA_KV + nkv, nkv, "plain", bf)
    kc_na = proj(hcb, OFF_NA_KV, nna, "headnorm", bf, p['na_k_norm'])
    vc_na = proj(hcb, OFF_NA_KV + nna, nna, "plain", bf)

    hxb = norm_modulate(x, p['norm1_g'], mx[0], mx[1])
    rope = rope_lane_tables(L)
    y_hy = hyena_latent(proj(hxb, OFF_HY, OFF_WA_Q - OFF_HY), p, consts)
    q_wa = proj(hxb, OFF_WA_Q, nwq, "headnorm", bf, p['wa_q_norm'], qk_scale, rope)
    k_wa = proj(hxb, OFF_WA_KV, nkv, "headnorm", bf, p['wa_k_norm'], 1.0, rope)
    v_wa = proj(hxb, OFF_WA_KV + nkv, nkv, "plain", bf)
    y_wa = windowed_attention(q_wa, k_wa, v_wa, kc_wa, vc_wa, p['wa_sink'])
    q_na = proj(hxb, OFF_NA_Q, nna, "headnorm", bf, p['na_q_norm'], qk_scale)
    k_na = proj(hxb, OFF_NA_KV, nna, "headnorm", bf, p['na_k_norm'])
    v_na = proj(hxb, OFF_NA_KV + nna, nna, "plain", bf)
    y_na = neighbourhood_attention(q_na, k_na, v_na, kc_na, vc_na, p['na_rpb'])
    gates = proj(hxb, OFF_GATE, N_BRANCH * D_MODEL, "sigmoid")
    x = mix_and_ffn(x, y_hy, y_wa, y_na, gates, mx)

    if update_ctx:
        yc_hy = hyena_branch(proj(hcb, OFF_HY, OFF_WA_Q - OFF_HY)[None], p)[0].astype(bf)
        qc_wa = proj(hcb, OFF_WA_Q, nwq, "headnorm", bf, p['wa_q_norm'], qk_scale)
        yc_wa = context_attention(qc_wa, kc_wa, vc_wa, p['wa_sink'], WA_HEADS, WA_KV_HEADS)
        qc_na = proj(hcb, OFF_NA_Q, nna, "headnorm", bf, p['na_q_norm'], qk_scale)
        yc_na = context_attention(qc_na, kc_na, vc_na, None, NA_HEADS, NA_HEADS)
        gates_c = proj(hcb, OFF_GATE, N_BRANCH * D_MODEL, "sigmoid")
        ctx = mix_and_ffn(ctx, yc_hy, yc_wa, yc_na, gates_c, mc)
    return x, ctx


def kernel(x, c, ctx, c_ctx, w_mod, b_mod, norm1_g, w_in, hy_conv_w, hy_conv_b, hy_fw1, hy_fb1, hy_fw2, hy_fb2, hy_fw3, hy_freq, hy_decay, hy_d, wa_q_norm, wa_k_norm, wa_sink, na_q_norm, na_k_norm, na_rpb, w_branch, w_out, norm2_g, w_router, w_gate, w_up, w_down):
    consts = dft_constants(x.shape[1])
    stacked = {'w_mod': w_mod, 'w_in': w_in, 'w_gate': w_gate, 'w_up': w_up, 'w_down': w_down}
    xs, cs = x[0], ctx[0]
    for l in range(DEPTH):
        p = {
            'b_mod': b_mod[l], 'norm1_g': norm1_g[l],
            'hy_conv_w': hy_conv_w[l], 'hy_conv_b': hy_conv_b[l], 'hy_fw1': hy_fw1[l], 'hy_fb1': hy_fb1[l],
            'hy_fw2': hy_fw2[l], 'hy_fb2': hy_fb2[l], 'hy_fw3': hy_fw3[l], 'hy_freq': hy_freq[l],
            'hy_decay': hy_decay[l], 'hy_d': hy_d[l], 'wa_q_norm': wa_q_norm[l], 'wa_k_norm': wa_k_norm[l],
            'wa_sink': wa_sink[l], 'na_q_norm': na_q_norm[l], 'na_k_norm': na_k_norm[l], 'na_rpb': na_rpb[l],
            'w_branch': w_branch[l], 'w_out': w_out[l], 'norm2_g': norm2_g[l], 'w_router': w_router[l],
        }
        xs, cs = trunk_layer(xs, cs, c, c_ctx, p, stacked, l, l < DEPTH - 1, consts)
    return xs[None]
```

```python
import math
from functools import partial

import jax
import jax.numpy as jnp
from jax import lax
from jax.experimental import pallas as pl
from jax.experimental.pallas import tpu as pltpu

D_MODEL = 2048
SEQ = 16384
DEPTH = 2
CTX_LEN = 256
GRID_W = 64
HEAD_DIM = 128
BRANCH_WIDTH = 1024
N_BRANCH = 3
N_MOD = 6
RMS_EPS = 1e-6
NEG_INF = -1e30

HY_WIDTH = BRANCH_WIDTH
HY_ORDER = 2
HY_DIRS = 2
HY_BANDS = 16
HY_DECAY_SHIFT = 0.05

WA_HEADS = BRANCH_WIDTH // HEAD_DIM
WA_KV_HEADS = 2
WA_WINDOW = 128
WA_BLOCK = 128

NA_HEADS = BRANCH_WIDTH // HEAD_DIM
NA_WIN_ROWS = 8
NA_WIN_COLS = 16

ROPE_BASE = 10000.0

N_EXPERTS = 16
EC_CAPACITY_FACTOR = 2
D_EXPERT = 1024

OFF_HY = 0
OFF_WA_Q = OFF_HY + 3 * HY_WIDTH
OFF_WA_KV = OFF_WA_Q + WA_HEADS * HEAD_DIM
OFF_NA_Q = OFF_WA_KV + 2 * WA_KV_HEADS * HEAD_DIM
OFF_NA_KV = OFF_NA_Q + NA_HEADS * HEAD_DIM
OFF_GATE = OFF_NA_KV + 2 * NA_HEADS * HEAD_DIM
N_IN = OFF_GATE + N_BRANCH * D_MODEL

VMEM_LIMIT_BYTES = 56 * 1024 * 1024


def _mm_kernel(a_ref, b_ref, o_ref):
    o_ref[...] = jnp.dot(a_ref[...].astype(jnp.bfloat16), b_ref[...].astype(jnp.bfloat16),
                         preferred_element_type=jnp.float32).astype(o_ref.dtype)


def _pick(n, pref):
    for t in pref:
        if n % t == 0:
            return t
    return n


def matmul(a, b, out_dtype=jnp.float32, layer=None):
    M, K = a.shape
    N = b.shape[-1]
    tm = _pick(M, (512, 256, 128, 64, 32, 16, 8))
    tn = _pick(N, (1024, 512, 256, 128))
    if layer is None:
        b_spec = pl.BlockSpec((K, tn), lambda j, i: (0, j))
    else:
        b_spec = pl.BlockSpec((None, K, tn), lambda j, i: (layer, 0, j))
    return pl.pallas_call(
        _mm_kernel,
        grid=(N // tn, M // tm),
        in_specs=[pl.BlockSpec((tm, K), lambda j, i: (i, 0)), b_spec],
        out_specs=pl.BlockSpec((tm, tn), lambda j, i: (i, j)),
        out_shape=jax.ShapeDtypeStruct((M, N), out_dtype),
        compiler_params=pltpu.CompilerParams(
            dimension_semantics=("parallel", "parallel"), vmem_limit_bytes=VMEM_LIMIT_BYTES),
        name="matmul",
    )(a, b)


def _norm_mod_kernel(x_ref, g_ref, shift_ref, scale_ref, o_ref):
    x = x_ref[...]
    y = x * lax.rsqrt(jnp.mean(x * x, axis=-1, keepdims=True) + RMS_EPS) * g_ref[...]
    o_ref[...] = (y * (1.0 + scale_ref[...]) + shift_ref[...]).astype(o_ref.dtype)


def norm_modulate(x, g, shift, scale, out_dtype=jnp.bfloat16):
    M, D = x.shape
    tm = _pick(M, (512, 256, 128, 64, 32, 16, 8))
    vec = pl.BlockSpec((1, D), lambda i: (0, 0))
    return pl.pallas_call(
        _norm_mod_kernel,
        grid=(M // tm,),
        in_specs=[pl.BlockSpec((tm, D), lambda i: (i, 0)), vec, vec, vec],
        out_specs=pl.BlockSpec((tm, D), lambda i: (i, 0)),
        out_shape=jax.ShapeDtypeStruct((M, D), out_dtype),
        compiler_params=pltpu.CompilerParams(
            dimension_semantics=("parallel",), vmem_limit_bytes=VMEM_LIMIT_BYTES),
        name="norm_modulate",
    )(x, g.reshape(1, D), shift.reshape(1, D), scale.reshape(1, D))


def _swap_halves(x):
    lane = lax.broadcasted_iota(jnp.int32, x.shape, 1)
    return jnp.where((lane % 64) < 32, pltpu.roll(x, 96, 1), pltpu.roll(x, 32, 1))


HEADNORM_ROWS = 256
HEADNORM_COLS = 256


def _proj_kernel(*refs, mode, rope, norm_cols, n_tiles):
    if mode == "headnorm":
        if rope:
            a_ref, w_ref, gain_ref, cos_ref, sin_ref, o_ref, w_bf = refs
        else:
            a_ref, w_ref, gain_ref, o_ref, w_bf = refs
    else:
        a_ref, w_ref, o_ref, w_bf = refs

    @pl.when(pl.program_id(1) == 0)
    def _():
        w_bf[...] = w_ref[...].astype(jnp.bfloat16)

    if mode == "plain":
        o_ref[...] = jnp.dot(a_ref[...], w_bf[...], preferred_element_type=jnp.float32).astype(o_ref.dtype)
        return
    if mode == "sigmoid":
        acc = jnp.dot(a_ref[...], w_bf[...], preferred_element_type=jnp.float32)
        o_ref[...] = jax.nn.sigmoid(acc).astype(o_ref.dtype)
        return

    tm, tn = o_ref.shape
    rows = min(tm, HEADNORM_ROWS) if rope else tm
    width = min(tn, HEADNORM_COLS) if rope else tn

    def epilogue(norm_heads):
        def chunk(r, carry):
            r0 = pl.multiple_of(r * rows, rows)
            a = a_ref[pl.ds(r0, rows), :]
            for c0 in range(0, tn, width):
                acc = jnp.dot(a, w_bf[:, c0:c0 + width], preferred_element_type=jnp.float32)
                for h in range(width // HEAD_DIM):
                    head = c0 // HEAD_DIM + h
                    y = acc[:, h * HEAD_DIM:(h + 1) * HEAD_DIM]
                    if head < norm_heads:
                        y = y * lax.rsqrt(jnp.mean(y * y, axis=-1, keepdims=True) + RMS_EPS) * gain_ref[head:head + 1, :]
                        if rope:
                            y = (y * cos_ref[pl.ds(r0, rows), :]
                                 + _swap_halves(y) * sin_ref[pl.ds(r0, rows), :])
                    o_ref[pl.ds(r0, rows), c0 + h * HEAD_DIM:c0 + (h + 1) * HEAD_DIM] = y.astype(o_ref.dtype)
            return carry

        lax.fori_loop(0, tm // rows, chunk, 0)

    heads = tn // HEAD_DIM
    if n_tiles == 1 or norm_cols >= n_tiles * tn:
        epilogue(min(heads, norm_cols // HEAD_DIM))
    else:
        norm_tiles = norm_cols // tn
        j = pl.program_id(0)
        pl.when(j < norm_tiles)(lambda: epilogue(heads))
        pl.when(j >= norm_tiles)(lambda: epilogue(0))


def project(a, w, layer, col_off, n_cols, mode="plain", out_dtype=jnp.float32, gain=None, rope=None,
            norm_cols=None):
    M, K = a.shape
    tm = _pick(M, (1024, 512, 256, 128, 64, 32, 16, 8))
    norm_cols = n_cols if norm_cols is None else norm_cols
    tn = next(t for t in (1024, 768, 512, 256, 128)
              if n_cols % t == 0 and col_off % t == 0 and (norm_cols % t == 0 or t == n_cols))
    off = col_off // tn
    n_tiles = n_cols // tn
    in_specs = [pl.BlockSpec((tm, K), lambda j, i: (i, 0)),
                pl.BlockSpec((None, K, tn), lambda j, i: (layer, 0, off + j))]
    args = [a, w]
    if mode == "headnorm":
        assert n_tiles == 1 or norm_cols % tn == 0
        heads = tn // HEAD_DIM
        gain = jnp.broadcast_to(gain.astype(jnp.float32).reshape(-1, HEAD_DIM), (n_cols // HEAD_DIM, HEAD_DIM))
        in_specs.append(pl.BlockSpec((None, heads, HEAD_DIM), lambda j, i: (j, 0, 0)))
        args.append(gain.reshape(n_tiles, heads, HEAD_DIM))
        if rope is not None:
            in_specs += [pl.BlockSpec((tm, HEAD_DIM), lambda j, i: (i, 0))] * 2
            args += list(rope)
    return pl.pallas_call(
        partial(_proj_kernel, mode=mode, rope=rope is not None, norm_cols=norm_cols, n_tiles=n_tiles),
        grid=(n_tiles, M // tm),
        in_specs=in_specs,
        out_specs=pl.BlockSpec((tm, tn), lambda j, i: (i, j)),
        out_shape=jax.ShapeDtypeStruct((M, n_cols), out_dtype),
        scratch_shapes=[pltpu.VMEM((K, tn), jnp.bfloat16)],
        compiler_params=pltpu.CompilerParams(
            dimension_semantics=("parallel", "arbitrary"), vmem_limit_bytes=VMEM_LIMIT_BYTES),
        name="project_" + mode,
    )(*args)


def rope_lane_tables(L):
    t = jnp.arange(L, dtype=jnp.int32)
    row = (t // GRID_W).astype(jnp.float32)
    col = (t % GRID_W).astype(jnp.float32)
    nf = HEAD_DIM // 4
    inv = ROPE_BASE ** (-jnp.arange(nf, dtype=jnp.float32) / nf)
    ar, ac = row[:, None] * inv, col[:, None] * inv
    cos = jnp.concatenate([jnp.cos(ar), jnp.cos(ar), jnp.cos(ac), jnp.cos(ac)], axis=-1)
    sin = jnp.concatenate([-jnp.sin(ar), jnp.sin(ar), -jnp.sin(ac), jnp.sin(ac)], axis=-1)
    return cos, sin


_NT = (((1,), (1,)), ((), ()))


def _wa_kernel(sink_ref, q_ref, k_ref, v_ref, kc_ref, vc_ref, o_ref, *, tq, seq):
    g = pl.program_id(0)
    i = pl.program_id(1)
    nwin = tq + 2 * WA_WINDOW
    ws = jnp.clip(i * tq - WA_WINDOW, 0, seq - nwin)
    start = pl.multiple_of(ws, WA_WINDOW)
    kwin = k_ref[pl.ds(start, nwin), :]
    vwin = v_ref[pl.ds(start, nwin), :]
    qpos = i * tq + lax.broadcasted_iota(jnp.int32, (tq, nwin), 0)
    kpos = ws + lax.broadcasted_iota(jnp.int32, (tq, nwin), 1)
    valid = jnp.abs(qpos - kpos) <= WA_WINDOW
    group = WA_HEADS // WA_KV_HEADS
    for hh in range(group):
        q = q_ref[:, hh * HEAD_DIM:(hh + 1) * HEAD_DIM]
        s = jnp.where(valid, lax.dot_general(q, kwin, _NT, preferred_element_type=jnp.float32), NEG_INF)
        sc = lax.dot_general(q, kc_ref[...], _NT, preferred_element_type=jnp.float32)
        sk = sink_ref[g * group + hh]
        m = jnp.maximum(jnp.maximum(jnp.max(s, axis=-1, keepdims=True), jnp.max(sc, axis=-1, keepdims=True)), sk)
        p = jnp.exp(s - m)
        pc = jnp.exp(sc - m)
        denom = jnp.sum(p, axis=-1, keepdims=True) + jnp.sum(pc, axis=-1, keepdims=True) + jnp.exp(sk - m)
        o = (jnp.dot(p.astype(vwin.dtype), vwin, preferred_element_type=jnp.float32)
             + jnp.dot(pc.astype(vwin.dtype), vc_ref[...], preferred_element_type=jnp.float32))
        o_ref[:, hh * HEAD_DIM:(hh + 1) * HEAD_DIM] = (o / denom).astype(o_ref.dtype)


def windowed_attention(q, kv, kvc, sink, tq=256):
    L = q.shape[0]
    Lc = kvc.shape[0]
    gw = (WA_HEADS // WA_KV_HEADS) * HEAD_DIM
    k_slab = pl.BlockSpec((L, HEAD_DIM), lambda g, i, s: (0, g))
    v_slab = pl.BlockSpec((L, HEAD_DIM), lambda g, i, s: (0, WA_KV_HEADS + g))
    kc_slab = pl.BlockSpec((Lc, HEAD_DIM), lambda g, i, s: (0, g))
    vc_slab = pl.BlockSpec((Lc, HEAD_DIM), lambda g, i, s: (0, WA_KV_HEADS + g))
    return pl.pallas_call(
        partial(_wa_kernel, tq=tq, seq=L),
        grid_spec=pltpu.PrefetchScalarGridSpec(
            num_scalar_prefetch=1,
            grid=(WA_KV_HEADS, L // tq),
            in_specs=[pl.BlockSpec((tq, gw), lambda g, i, s: (i, g)), k_slab, v_slab, kc_slab, vc_slab],
            out_specs=pl.BlockSpec((tq, gw), lambda g, i, s: (i, g)),
        ),
        out_shape=jax.ShapeDtypeStruct((L, WA_HEADS * HEAD_DIM), jnp.bfloat16),
        compiler_params=pltpu.CompilerParams(
            dimension_semantics=("parallel", "parallel"), vmem_limit_bytes=VMEM_LIMIT_BYTES),
        name="windowed_attention",
    )(sink.astype(jnp.float32), q, kv, kv, kvc, kvc)


NA_ROW_BLOCK = 4
NA_KEY_ROWS = NA_ROW_BLOCK + NA_WIN_ROWS - 1


NA_HEAD_GROUP = 4


def _na_kernel(q_ref, k_ref, v_ref, kc_ref, vc_ref, bias_ref, o_ref, *, rows):
    i = pl.program_id(1)
    ws = jnp.clip(i * NA_ROW_BLOCK - NA_WIN_ROWS // 2, 0, rows - NA_KEY_ROWS)
    start = pl.multiple_of(ws * GRID_W, GRID_W)
    nk = NA_KEY_ROWS * GRID_W
    for hh in range(NA_HEAD_GROUP):
        cols = slice(hh * HEAD_DIM, (hh + 1) * HEAD_DIM)
        kwin = k_ref[pl.ds(start, nk), cols]
        vwin = v_ref[pl.ds(start, nk), cols]
        q = q_ref[:, cols]
        s = lax.dot_general(q, kwin, _NT, preferred_element_type=jnp.float32) + bias_ref[0, hh]
        sc = lax.dot_general(q, kc_ref[:, cols], _NT, preferred_element_type=jnp.float32)
        m = jnp.maximum(jnp.max(s, axis=-1, keepdims=True), jnp.max(sc, axis=-1, keepdims=True))
        p = jnp.exp(s - m)
        pc = jnp.exp(sc - m)
        denom = jnp.sum(p, axis=-1, keepdims=True) + jnp.sum(pc, axis=-1, keepdims=True)
        o = (jnp.dot(p.astype(vwin.dtype), vwin, preferred_element_type=jnp.float32)
             + jnp.dot(pc.astype(vwin.dtype), vc_ref[:, cols], preferred_element_type=jnp.float32))
        o_ref[:, cols] = (o / denom).astype(o_ref.dtype)


def na_bias_tiles(rpb, rows):
    col = jnp.arange(GRID_W)
    cstart = jnp.clip(col - NA_WIN_COLS // 2, 0, GRID_W - NA_WIN_COLS)
    col_in = (col[None, :] >= cstart[:, None]) & (col[None, :] < cstart[:, None] + NA_WIN_COLS)
    dc_idx = jnp.clip(col[None, :] - col[:, None] + NA_WIN_COLS - 1, 0, 2 * NA_WIN_COLS - 2)
    exact = lax.Precision.HIGHEST
    by_col = jnp.einsum('hrd,qkd->hrqk', rpb.astype(jnp.float32),
                        jax.nn.one_hot(dc_idx, 2 * NA_WIN_COLS - 1, dtype=jnp.float32), precision=exact)
    tiles = []
    for blk in (0, 1, rows // NA_ROW_BLOCK - 1):
        r = blk * NA_ROW_BLOCK
        ws = min(max(r - NA_WIN_ROWS // 2, 0), rows - NA_KEY_ROWS)
        qr = r + jnp.arange(NA_ROW_BLOCK)
        kr = ws + jnp.arange(NA_KEY_ROWS)
        r0 = jnp.clip(qr - NA_WIN_ROWS // 2, 0, rows - NA_WIN_ROWS)
        row_in = (kr[None, :] >= r0[:, None]) & (kr[None, :] < r0[:, None] + NA_WIN_ROWS)
        dr_idx = jnp.clip(kr[None, :] - qr[:, None] + NA_WIN_ROWS - 1, 0, 2 * NA_WIN_ROWS - 2)
        b = jnp.einsum('hrqk,abr->haqbk', by_col,
                       jax.nn.one_hot(dr_idx, 2 * NA_WIN_ROWS - 1, dtype=jnp.float32), precision=exact)
        ok = row_in[:, None, :, None] & col_in[None, :, None, :]
        b = jnp.where(ok[None], b, NEG_INF)
        tiles.append(b.reshape(rpb.shape[0], NA_ROW_BLOCK * GRID_W, NA_KEY_ROWS * GRID_W))
    return jnp.stack(tiles)


def neighbourhood_attention(qkv, kvc, rpb):
    L = qkv.shape[0]
    Lc = kvc.shape[0]
    rows = L // GRID_W
    nblk = rows // NA_ROW_BLOCK
    tq = NA_ROW_BLOCK * GRID_W
    nk = NA_KEY_ROWS * GRID_W
    bias = na_bias_tiles(rpb, rows)
    gw = NA_HEAD_GROUP * HEAD_DIM
    ng = NA_HEADS // NA_HEAD_GROUP
    slab = lambda part: pl.BlockSpec((L, gw), lambda h, i: (0, part * ng + h), pipeline_mode=pl.Buffered(1))
    cslab = lambda part: pl.BlockSpec((Lc, gw), lambda h, i: (0, part * ng + h))
    variant = lambda h, i: (jnp.where(i == 0, 0, jnp.where(i == nblk - 1, 2, 1)), h, 0, 0)
    return pl.pallas_call(
        partial(_na_kernel, rows=rows),
        grid=(ng, nblk),
        in_specs=[pl.BlockSpec((tq, gw), lambda h, i: (i, h)), slab(1), slab(2), cslab(0), cslab(1),
                  pl.BlockSpec((1, NA_HEAD_GROUP, tq, nk), variant)],
        out_specs=pl.BlockSpec((tq, gw), lambda h, i: (i, h)),
        out_shape=jax.ShapeDtypeStruct((L, NA_HEADS * HEAD_DIM), jnp.bfloat16),
        compiler_params=pltpu.CompilerParams(
            dimension_semantics=("parallel", "parallel"), vmem_limit_bytes=VMEM_LIMIT_BYTES),
        name="neighbourhood_attention",
    )(qkv, qkv, qkv, kvc, kvc, bias)


DFT_N1 = 128
DFT_K1 = DFT_N1 // 2 + 1
DFT_K1_PAD = 72


def dft_constants(L):
    N = 2 * L
    N2 = N // DFT_N1
    f32, bf = jnp.float32, jnp.bfloat16
    k1 = jnp.arange(DFT_K1, dtype=jnp.int32)
    n1 = jnp.arange(DFT_N1, dtype=jnp.int32)
    th = ((k1[:, None] * n1[None, :]) % DFT_N1).astype(f32) * (2.0 * math.pi / DFT_N1)
    pad = ((0, DFT_K1_PAD - DFT_K1), (0, 0))
    rows_fwd = jnp.concatenate([jnp.pad(jnp.cos(th), pad), jnp.pad(-jnp.sin(th), pad)], axis=0)
    ck = jnp.where((k1 == 0) | (k1 == DFT_N1 // 2), 1.0, 2.0)[:, None]
    half = DFT_N1 // 2
    rows_inv = jnp.concatenate([jnp.pad(ck * jnp.cos(th[:, :half]), pad),
                                jnp.pad(-ck * jnp.sin(th[:, :half]), pad)], axis=0).T
    n2 = jnp.arange(N2, dtype=jnp.int32)
    kk = k1[:, None] + DFT_N1 * n2[None, :]
    ang = ((kk[:, :, None] * n2[None, None, :]) % N).astype(f32) * (2.0 * math.pi / N)
    gr, gi = jnp.cos(ang), -jnp.sin(ang)
    mid_fwd = jnp.concatenate([jnp.concatenate([gr, -gi], axis=2),
                               jnp.concatenate([gi, gr], axis=2)], axis=1)
    return {
        "rows_fwd": rows_fwd.astype(bf), "rows_fwd_half": rows_fwd[:, :half].astype(bf),
        "rows_inv": rows_inv.astype(bf),
        "mid_fwd": mid_fwd.astype(bf),
    }


def _spectral_fwd_kernel(a_ref, g_ref, o_ref):
    n2 = a_ref.shape[2]
    x = (jnp.dot(g_ref[0, :, :n2], a_ref[0, 0], preferred_element_type=jnp.float32)
         + jnp.dot(g_ref[0, :, n2:], a_ref[1, 0], preferred_element_type=jnp.float32))
    o_ref[0, 0] = x[:n2].astype(o_ref.dtype)
    o_ref[1, 0] = x[n2:].astype(o_ref.dtype)


def spectral_fwd(a, mid_fwd, cb=2048):
    _, _, n2, C = a.shape
    cb = min(cb, C)
    return pl.pallas_call(
        _spectral_fwd_kernel,
        grid=(DFT_K1, C // cb),
        in_specs=[pl.BlockSpec((2, 1, n2, cb), lambda k, c: (0, k, 0, c)),
                  pl.BlockSpec((1, 2 * n2, 2 * n2), lambda k, c: (k, 0, 0))],
        out_specs=pl.BlockSpec((2, 1, n2, cb), lambda k, c: (0, k, 0, c)),
        out_shape=jax.ShapeDtypeStruct((2, DFT_K1, n2, C), jnp.bfloat16),
        compiler_params=pltpu.CompilerParams(
            dimension_semantics=("parallel", "parallel"), vmem_limit_bytes=VMEM_LIMIT_BYTES),
        name="spectral_fwd",
    )(a, mid_fwd)


_TN = (((0,), (0,)), ((), ()))


def _spectral_mid_kernel(a_ref, gf_ref, ks_ref, o_ref):
    n2 = a_ref.shape[2]
    k1 = pl.program_id(0)

    @pl.when(k1 < DFT_K1)
    def _():
        x = (jnp.dot(gf_ref[0, :, :n2], a_ref[0, 0], preferred_element_type=jnp.float32)
             + jnp.dot(gf_ref[0, :, n2:], a_ref[1, 0], preferred_element_type=jnp.float32))
        xr, xi = x[:n2], x[n2:]
        kr, ki = ks_ref[0, 0].astype(jnp.float32), ks_ref[1, 0].astype(jnp.float32)
        yr = (xr * kr - xi * ki).astype(jnp.bfloat16)
        yi = (xr * ki + xi * kr).astype(jnp.bfloat16)
        b = (lax.dot_general(gf_ref[0, :n2, :], yr, _TN, preferred_element_type=jnp.float32)
             + lax.dot_general(gf_ref[0, n2:, :], yi, _TN, preferred_element_type=jnp.float32))
        o_ref[0, 0] = b[:n2].astype(o_ref.dtype)
        o_ref[1, 0] = b[n2:].astype(o_ref.dtype)

    @pl.when(k1 >= DFT_K1)
    def _():
        o_ref[...] = jnp.zeros_like(o_ref)


def spectral_mid(a, consts, kspec, col_off, cb=1024):
    _, _, n2, C = a.shape
    cb = min(cb, C)
    off = col_off // cb
    kc = lambda k: jnp.minimum(k, DFT_K1 - 1)
    return pl.pallas_call(
        _spectral_mid_kernel,
        grid=(DFT_K1_PAD, C // cb),
        in_specs=[pl.BlockSpec((2, 1, n2, cb), lambda k, c: (0, k, 0, c)),
                  pl.BlockSpec((1, 2 * n2, 2 * n2), lambda k, c: (kc(k), 0, 0)),
                  pl.BlockSpec((2, 1, n2, cb), lambda k, c: (0, kc(k), 0, off + c))],
        out_specs=pl.BlockSpec((2, 1, n2, cb), lambda k, c: (0, k, 0, c)),
        out_shape=jax.ShapeDtypeStruct(a.shape, jnp.bfloat16),
        compiler_params=pltpu.CompilerParams(
            dimension_semantics=("parallel", "parallel"), vmem_limit_bytes=VMEM_LIMIT_BYTES),
        name="spectral_mid",
    )(a, consts["mid_fwd"], kspec)


DFT_ROW_GROUP = 16
DFT_ROW_LANES = 256


def _rows_fwd_kernel(f_ref, u_ref, o_ref):
    x = pltpu.einshape("abc->bac", u_ref[...].astype(jnp.float32))
    rs = [jnp.dot(f_ref[...], x[s].astype(jnp.bfloat16), preferred_element_type=jnp.float32)
          for s in range(x.shape[0])]
    o_ref[...] = pltpu.einshape("abc->bac", jnp.stack(rs)).astype(o_ref.dtype)


def dft_rows_fwd(rows_mat, u3):
    kn, n2, C = u3.shape
    grp = min(DFT_ROW_GROUP, n2)
    tc = min(DFT_ROW_LANES, C)
    m = rows_mat.shape[0]
    return pl.pallas_call(
        _rows_fwd_kernel,
        grid=(n2 // grp, C // tc),
        in_specs=[pl.BlockSpec((m, kn), lambda j, c: (0, 0)),
                  pl.BlockSpec((kn, grp, tc), lambda j, c: (0, j, c))],
        out_specs=pl.BlockSpec((m, grp, tc), lambda j, c: (0, j, c)),
        out_shape=jax.ShapeDtypeStruct((m, n2, C), jnp.bfloat16),
        compiler_params=pltpu.CompilerParams(
            dimension_semantics=("parallel", "parallel"), vmem_limit_bytes=VMEM_LIMIT_BYTES),
        name="dft_rows_fwd",
    )(rows_mat, u3)


def _rows_inv_kernel(f_ref, b_ref, u_ref, gate_ref, scale_ref, d_ref, o_ref):
    x = pltpu.einshape("abc->bac", b_ref[...].astype(jnp.float32))
    ys = [jnp.dot(f_ref[...], x[s].astype(jnp.bfloat16), preferred_element_type=jnp.float32)
          for s in range(x.shape[0])]
    y = pltpu.einshape("abc->bac", jnp.stack(ys))
    o_ref[...] = (gate_ref[...] * (y * scale_ref[...] + u_ref[...] * d_ref[...])).astype(o_ref.dtype)


def dft_rows_inv(rows_inv, b3, u3, gate3, scale, d_skip, out_dtype):
    nr, n2, C = u3.shape
    grp = min(DFT_ROW_GROUP, n2)
    tc = min(DFT_ROW_LANES, C)
    blk = pl.BlockSpec((nr, grp, tc), lambda j, c: (0, j, c))
    vec = pl.BlockSpec((1, 1, tc), lambda j, c: (0, 0, c))
    return pl.pallas_call(
        _rows_inv_kernel,
        grid=(n2 // grp, C // tc),
        in_specs=[pl.BlockSpec(rows_inv.shape, lambda j, c: (0, 0)),
                  pl.BlockSpec((b3.shape[0], grp, tc), lambda j, c: (0, j, c)), blk, blk, vec, vec],
        out_specs=blk,
        out_shape=jax.ShapeDtypeStruct((nr, n2, C), out_dtype),
        compiler_params=pltpu.CompilerParams(
            dimension_semantics=("parallel", "parallel"), vmem_limit_bytes=VMEM_LIMIT_BYTES),
        name="dft_rows_inv",
    )(rows_inv, b3, u3, gate3, scale.reshape(1, 1, C), d_skip.reshape(1, 1, C))


def _conv3_kernel(z_ref, prev_ref, next_ref, w_ref, b_ref, o_ref):
    i = pl.program_id(0)
    z = z_ref[...]
    tm = z.shape[0]
    row = lax.broadcasted_iota(jnp.int32, z.shape, 0)
    prev_row = jnp.where(i > 0, prev_ref[7:8, :], 0.0)
    next_row = jnp.where(i < pl.num_programs(0) - 1, next_ref[0:1, :], 0.0)
    zp = jnp.where(row == 0, prev_row, pltpu.roll(z, 1, 0))
    zn = jnp.where(row == tm - 1, next_row, pltpu.roll(z, tm - 1, 0))
    o_ref[...] = zp * w_ref[0:1, :] + z * w_ref[1:2, :] + zn * w_ref[2:3, :] + b_ref[...]


def short_conv3_part(z, w, b, part, width):
    L = z.shape[0]
    tm = _pick(L, (512, 256, 128, 64, 32, 16, 8))
    tc = _pick(width, (1024, 512, 256, 128))
    off = part * width // tc
    halo = 8
    return pl.pallas_call(
        _conv3_kernel,
        grid=(L // tm, width // tc),
        in_specs=[pl.BlockSpec((tm, tc), lambda i, j: (i, off + j)),
                  pl.BlockSpec((halo, tc), lambda i, j: (jnp.maximum(i * (tm // halo) - 1, 0), off + j)),
                  pl.BlockSpec((halo, tc), lambda i, j: (jnp.minimum((i + 1) * (tm // halo), L // halo - 1), off + j)),
                  pl.BlockSpec((3, tc), lambda i, j: (0, off + j)),
                  pl.BlockSpec((1, tc), lambda i, j: (0, off + j))],
        out_specs=pl.BlockSpec((tm, tc), lambda i, j: (i, j)),
        out_shape=jax.ShapeDtypeStruct((L, width), jnp.float32),
        compiler_params=pltpu.CompilerParams(
            dimension_semantics=("parallel", "parallel"), vmem_limit_bytes=VMEM_LIMIT_BYTES),
        name="short_conv3",
    )(z, z, z, w, b.reshape(1, -1))


FILTER_PAD = 128
FILTER_LANES = 512


def _filter_rows_kernel(feats_ref, feats0_ref, fw1_ref, fb1_ref, fw2_ref, fb2_ref, fr_ref, fw3_ref, rate_ref,
                        rows_ref, a_ref, asum_ref, hid_ref, hid0_ref):
    g = pl.program_id(0)
    exact = lax.Precision.HIGHEST
    n_dir, prow, nf2 = feats_ref.shape
    hid = fw3_ref.shape[1]
    half = rows_ref.shape[1] // 2
    grp = 2 * prow // half

    def hidden(feats):
        h = jnp.sin(fr_ref[0:1, :] * (jnp.dot(feats, fw1_ref[...], precision=exact,
                                              preferred_element_type=jnp.float32) + fb1_ref[...]))
        return jnp.sin(fr_ref[1:2, :] * (jnp.dot(h, fw2_ref[...], precision=exact,
                                                 preferred_element_type=jnp.float32) + fb2_ref[...]))

    @pl.when(pl.program_id(1) == 0)
    def _():
        for d in range(n_dir):
            hid_ref[d] = hidden(feats_ref[d])
        hid0_ref[...] = hidden(feats0_ref[...])

    def taps(h, t, d):
        k = jnp.dot(h.astype(jnp.bfloat16), fw3_ref[d].astype(jnp.bfloat16), preferred_element_type=jnp.float32)
        return k * (jnp.exp(-t * rate_ref[d]) + HY_DECAY_SHIFT)

    def direction(d):
        h2, f = hid_ref[d], feats_ref[d]
        return jnp.concatenate([taps(h2[:, :hid], f[:, 0:1], d),
                                taps(h2[:, hid:], f[:, nf2 // 2:nf2 // 2 + 1], d)], axis=0)

    k_fwd = direction(0)
    k_bwd = direction(1)
    lag0_back = taps(hid0_ref[:, :hid], feats0_ref[:, 0:1], 1)[0:1, :]
    first = (lax.broadcasted_iota(jnp.int32, k_fwd.shape, 0) == 0) & (g == 0)
    k_fwd = k_fwd + jnp.where(first, lag0_back, 0.0)
    k_bwd = jnp.where(first, 0.0, k_bwd)
    asum_ref[0] = jnp.sum(jnp.abs(k_fwd), axis=0, keepdims=True) + jnp.sum(jnp.abs(k_bwd), axis=0, keepdims=True)
    kf = k_fwd.astype(jnp.bfloat16)
    kb = k_bwd.astype(jnp.bfloat16)
    rs = [jnp.dot(rows_ref[:, :half], kf[s * half:(s + 1) * half], preferred_element_type=jnp.float32)
          + jnp.dot(rows_ref[:, half:], kb[s * half:(s + 1) * half], preferred_element_type=jnp.float32)
          for s in range(grp)]
    a_ref[...] = pltpu.einshape("abc->bac", jnp.stack(rs)).astype(a_ref.dtype)


def _block_diag2(w):
    z = jnp.zeros_like(w)
    return jnp.concatenate([jnp.concatenate([w, z], axis=1), jnp.concatenate([z, w], axis=1)], axis=0)


def hyena_filter_rows(L, p, rows_fwd):
    f32 = jnp.float32
    N = 2 * L
    n2 = N // DFT_N1
    half = DFT_N1 // 2
    C2 = HY_ORDER * HY_WIDTH
    grp = min(DFT_ROW_GROUP, n2)
    tc = min(FILTER_LANES, C2)
    prow = grp * half // 2
    n = (n2 * jnp.arange(DFT_N1, dtype=jnp.int32)[None, :] + jnp.arange(n2, dtype=jnp.int32)[:, None])
    pos = jnp.where(n < L, n, N - n).astype(f32)[..., None]
    bands = jnp.linspace(1e-4, HY_BANDS - 1, HY_BANDS, dtype=f32)
    ang = (2.0 * math.pi / L) * bands * pos
    feats = jnp.concatenate([pos / (L - 1), jnp.cos(ang), -jnp.sin(ang)], axis=-1)
    feats = jnp.pad(feats, ((0, 0), (0, 0), (0, FILTER_PAD - feats.shape[-1])))
    feats = feats.reshape(n2 // grp, grp, 2, half, FILTER_PAD).transpose(2, 0, 1, 3, 4)
    feats = feats.reshape(2, n2 // grp, 2, prow, FILTER_PAD).transpose(0, 1, 3, 2, 4)
    feats = feats.reshape(2, (n2 // grp) * prow, 2 * FILTER_PAD)
    feats0 = jnp.broadcast_to(feats[0, 0:1], (8, 2 * FILTER_PAD))
    hid = p['hy_fw2'].shape[0]
    fw1 = _block_diag2(jnp.pad(p['hy_fw1'].astype(f32), ((0, FILTER_PAD - p['hy_fw1'].shape[0]), (0, 0))))
    fb1 = jnp.tile(p['hy_fb1'].astype(f32), 2).reshape(1, -1)
    fw2 = _block_diag2(p['hy_fw2'].astype(f32))
    fb2 = jnp.tile(p['hy_fb2'].astype(f32), 2).reshape(1, -1)
    fr = jnp.tile(p['hy_freq'].astype(f32), (1, 2))
    fw3 = p['hy_fw3'].astype(f32).reshape(hid, HY_DIRS, C2).transpose(1, 0, 2)
    rate = jnp.abs(p['hy_decay'].astype(f32)).reshape(HY_DIRS, 1, C2)
    full = lambda shape: pl.BlockSpec(shape, lambda g, c: (0,) * len(shape))
    m = rows_fwd.shape[0]
    a, asum = pl.pallas_call(
        _filter_rows_kernel,
        grid=(n2 // grp, C2 // tc),
        in_specs=[pl.BlockSpec((2, prow, 2 * FILTER_PAD), lambda g, c: (0, g, 0)), full((8, 2 * FILTER_PAD)),
                  full((2 * FILTER_PAD, 2 * hid)), full((1, 2 * hid)),
                  full((2 * hid, 2 * hid)), full((1, 2 * hid)), full((2, 2 * hid)),
                  pl.BlockSpec((HY_DIRS, hid, tc), lambda g, c: (0, 0, c)),
                  pl.BlockSpec((HY_DIRS, 1, tc), lambda g, c: (0, 0, c)),
                  full(rows_fwd.shape)],
        out_specs=[pl.BlockSpec((m, grp, tc), lambda g, c: (0, g, c)),
                   pl.BlockSpec((1, 1, tc), lambda g, c: (g, 0, c))],
        out_shape=[jax.ShapeDtypeStruct((m, n2, C2), jnp.bfloat16),
                   jax.ShapeDtypeStruct((n2 // grp, 1, C2), f32)],
        scratch_shapes=[pltpu.VMEM((2, prow, 2 * hid), f32), pltpu.VMEM((8, 2 * hid), f32)],
        compiler_params=pltpu.CompilerParams(
            dimension_semantics=("parallel", "arbitrary"), vmem_limit_bytes=VMEM_LIMIT_BYTES),
        name="hyena_filter_rows",
    )(feats, feats0, fw1, fb1, fw2, fb2, fr, fw3, rate, rows_fwd)
    return a, jnp.sum(asum, axis=(0, 1))


def hyena_latent(z, p, consts):
    L = z.shape[0]
    C = HY_WIDTH
    N = 2 * L
    n2 = N // DFT_N1
    half = DFT_N1 // 2
    v, x1, x2 = (short_conv3_part(z, p['hy_conv_w'], p['hy_conv_b'], i, C) for i in range(3))
    ka, asum = hyena_filter_rows(L, p, consts["rows_fwd"])
    kspec = spectral_fwd(ka.reshape(2, DFT_K1_PAD, n2, HY_ORDER * C), consts["mid_fwd"])
    scale = 1.0 / (N * asum)

    def long_conv(u, gate, order, out_dtype):
        u3 = u.reshape(half, n2, C)
        ua = dft_rows_fwd(consts["rows_fwd_half"], u3)
        bm = spectral_mid(ua.reshape(2, DFT_K1_PAD, n2, C), consts, kspec, order * C)
        y = dft_rows_inv(consts["rows_inv"], bm.reshape(2 * DFT_K1_PAD, n2, C), u3, gate.reshape(half, n2, C),
                         scale[order * C:(order + 1) * C], p['hy_d'][order].astype(jnp.float32), out_dtype)
        return y.reshape(L, C)

    y1 = long_conv(v, x1, 0, jnp.float32)
    return long_conv(y1, x2, 1, jnp.bfloat16)


def rms_norm(x, g):
    xf = x.astype(jnp.float32)
    y = xf * lax.rsqrt(jnp.mean(xf * xf, axis=-1, keepdims=True) + RMS_EPS)
    return (y * g.astype(jnp.float32)).astype(x.dtype)


def modulate(h, shift, scale):
    return h * (1 + scale) + shift


def short_conv3(z, w, b):
    L = z.shape[1]
    zp = jnp.pad(z, ((0, 0), (1, 1), (0, 0)))
    return zp[:, :L] * w[0] + zp[:, 1:L + 1] * w[1] + zp[:, 2:] * w[2] + b


def hyena_filter_spectra(L, fw1, fb1, fw2, fb2, fw3, freq, decay):
    f32 = jnp.float32
    t = jnp.linspace(0.0, 1.0, L, dtype=f32)[:, None]
    pos = jnp.arange(L, dtype=f32)[:, None]
    bands = jnp.linspace(1e-4, HY_BANDS - 1, HY_BANDS, dtype=f32)[None, :]
    ang = (2.0 * math.pi / L) * bands * pos
    feats = jnp.concatenate([t, jnp.cos(ang), -jnp.sin(ang)], axis=-1)
    fr = freq.astype(f32)
    h = jnp.sin(fr[0] * (feats @ fw1.astype(f32) + fb1.astype(f32)))
    h = jnp.sin(fr[1] * (h @ fw2.astype(f32) + fb2.astype(f32)))
    h = (h @ fw3.astype(f32)).reshape(L, HY_DIRS, HY_ORDER, HY_WIDTH)
    rate = jnp.abs(decay.astype(f32)).reshape(HY_DIRS, HY_ORDER, HY_WIDTH)
    h = h * (jnp.exp(-t[:, :, None, None] * rate) + HY_DECAY_SHIFT)
    hf, hb = h[:, 0], h[:, 1]
    k = jnp.concatenate([hf[:1] + hb[:1], hf[1:], jnp.zeros_like(hf[:1]), hb[1:][::-1]], axis=0)
    k = k / jnp.sum(jnp.abs(k), axis=0, keepdims=True)
    return jnp.fft.rfft(k, axis=0)


def bidir_long_conv(u, k_spec, d_skip):
    L = u.shape[1]
    uf = jnp.fft.rfft(u.astype(jnp.float32), n=2 * L, axis=1)
    y = jnp.fft.irfft(uf * k_spec[None], n=2 * L, axis=1)[:, :L]
    return (y + u.astype(jnp.float32) * d_skip.astype(jnp.float32)).astype(u.dtype)


def hyena_branch(z, p):
    L = z.shape[1]
    z = short_conv3(z, p['hy_conv_w'], p['hy_conv_b'])
    v, x1, x2 = jnp.split(z, 3, axis=-1)
    k_spec = hyena_filter_spectra(L, p['hy_fw1'], p['hy_fb1'], p['hy_fw2'], p['hy_fb2'],
                                  p['hy_fw3'], p['hy_freq'], p['hy_decay'])
    y = x1 * bidir_long_conv(v, k_spec[:, 0], p['hy_d'][0])
    return x2 * bidir_long_conv(y, k_spec[:, 1], p['hy_d'][1])


def _resident(shape, index_map):
    return pl.BlockSpec(shape, index_map, pipeline_mode=pl.Buffered(1))


def _merge_kernel(y0_ref, y1_ref, y2_ref, g_ref, wb_ref, o_ref):
    d = o_ref.shape[1]
    m = None
    for b, y_ref in enumerate((y0_ref, y1_ref, y2_ref)):
        t = g_ref[:, b * d:(b + 1) * d] * jnp.dot(y_ref[...], wb_ref[b], preferred_element_type=jnp.float32)
        m = t if m is None else m + t
    o_ref[...] = m.astype(o_ref.dtype)


def merge_gated(y_hy, y_wa, y_na, gates, w_branch):
    M, wbr = y_hy.shape
    D = w_branch.shape[2]
    tm = _pick(M, (256, 128, 64, 32, 16, 8))
    yspec = pl.BlockSpec((tm, wbr), lambda i: (i, 0))
    return pl.pallas_call(
        _merge_kernel,
        grid=(M // tm,),
        in_specs=[yspec, yspec, yspec, pl.BlockSpec((tm, N_BRANCH * D), lambda i: (i, 0)),
                  _resident((N_BRANCH, wbr, D), lambda i: (0, 0, 0))],
        out_specs=pl.BlockSpec((tm, D), lambda i: (i, 0)),
        out_shape=jax.ShapeDtypeStruct((M, D), jnp.bfloat16),
        compiler_params=pltpu.CompilerParams(
            dimension_semantics=("parallel",), vmem_limit_bytes=VMEM_LIMIT_BYTES),
        name="merge_gated",
    )(y_hy, y_wa, y_na, gates, w_branch)


ROUTER_PAD = 128


def _out_proj_kernel(m_ref, w_ref, x_ref, gate_ref, g2_ref, shift_ref, scale_ref, wr_hi_ref, wr_lo_ref,
                     x_out, h_out, lg_out):
    x = x_ref[...] + gate_ref[...] * jnp.dot(m_ref[...], w_ref[...], preferred_element_type=jnp.float32)
    x_out[...] = x
    h = x * lax.rsqrt(jnp.mean(x * x, axis=-1, keepdims=True) + RMS_EPS) * g2_ref[...]
    h = h * (1.0 + scale_ref[...]) + shift_ref[...]
    h_out[...] = h.astype(h_out.dtype)
    h_hi = h.astype(jnp.bfloat16)
    h_lo = (h - h_hi.astype(jnp.float32)).astype(jnp.bfloat16)
    lg_out[...] = (jnp.dot(h_hi, wr_hi_ref[...], preferred_element_type=jnp.float32)
                   + jnp.dot(h_lo, wr_hi_ref[...], preferred_element_type=jnp.float32)
                   + jnp.dot(h_hi, wr_lo_ref[...], preferred_element_type=jnp.float32))


def out_project(m, w_out, x, gate, norm_g, shift, scale, w_router):
    M, D = x.shape
    tm = _pick(M, (256, 128, 64, 32, 16, 8))
    row = pl.BlockSpec((tm, D), lambda i: (i, 0))
    vec = pl.BlockSpec((1, D), lambda i: (0, 0))
    wr = jnp.pad(w_router.astype(jnp.float32), ((0, 0), (0, ROUTER_PAD - w_router.shape[1])))
    wr_hi = wr.astype(jnp.bfloat16)
    wr_lo = (wr - wr_hi.astype(jnp.float32)).astype(jnp.bfloat16)
    v2 = lambda a: a.reshape(1, D).astype(jnp.float32)
    return pl.pallas_call(
        _out_proj_kernel,
        grid=(M // tm,),
        in_specs=[row, _resident((D, D), lambda i: (0, 0)), row, vec, vec, vec, vec,
                  _resident((D, ROUTER_PAD), lambda i: (0, 0)), _resident((D, ROUTER_PAD), lambda i: (0, 0))],
        out_specs=[row, row, pl.BlockSpec((tm, ROUTER_PAD), lambda i: (i, 0))],
        out_shape=[jax.ShapeDtypeStruct((M, D), jnp.float32), jax.ShapeDtypeStruct((M, D), jnp.bfloat16),
                   jax.ShapeDtypeStruct((M, ROUTER_PAD), jnp.float32)],
        compiler_params=pltpu.CompilerParams(
            dimension_semantics=("parallel",), vmem_limit_bytes=VMEM_LIMIT_BYTES),
        name="out_project",
    )(m, w_out, x, v2(gate), v2(norm_g), v2(shift), v2(scale), wr_hi, wr_lo)


def _gate_up_kernel(x_ref, wg_ref, wu_ref, o_ref, wg_bf, wu_bf):
    @pl.when(pl.program_id(2) == 0)
    def _():
        wg_bf[...] = wg_ref[...].astype(jnp.bfloat16)
        wu_bf[...] = wu_ref[...].astype(jnp.bfloat16)

    x = x_ref[0]
    a = jnp.dot(x, wg_bf[...], preferred_element_type=jnp.float32)
    u = jnp.dot(x, wu_bf[...], preferred_element_type=jnp.float32)
    o_ref[0] = (a * jax.nn.sigmoid(a) * u).astype(o_ref.dtype)


def _down_kernel(h_ref, wd_ref, gsel_ref, gate_ref, o_ref, wd_bf):
    @pl.when(pl.program_id(1) == 0)
    def _():
        wd_bf[...] = wd_ref[...].astype(jnp.bfloat16)

    y = jnp.dot(h_ref[0], wd_bf[...], preferred_element_type=jnp.float32)
    o_ref[0] = (y * gsel_ref[0] * gate_ref[...]).astype(o_ref.dtype)


def expert_ffn(xe, w_gate, w_up, w_down, layer, gsel, out_gate):
    E, cap, D = xe.shape
    F = w_gate.shape[3]
    tm = _pick(cap, (512, 256, 128, 64, 32, 16, 8))
    tn = _pick(F, (1024, 512, 256, 128))
    seq = pltpu.CompilerParams(dimension_semantics=("parallel", "parallel", "arbitrary"),
                               vmem_limit_bytes=VMEM_LIMIT_BYTES)
    h = pl.pallas_call(
        _gate_up_kernel,
        grid=(E, F // tn, cap // tm),
        in_specs=[pl.BlockSpec((1, tm, D), lambda e, j, i: (e, i, 0)),
                  pl.BlockSpec((None, None, D, tn), lambda e, j, i: (layer, e, 0, j)),
                  pl.BlockSpec((None, None, D, tn), lambda e, j, i: (layer, e, 0, j))],
        out_specs=pl.BlockSpec((1, tm, tn), lambda e, j, i: (e, i, j)),
        out_shape=jax.ShapeDtypeStruct((E, cap, F), jnp.bfloat16),
        scratch_shapes=[pltpu.VMEM((D, tn), jnp.bfloat16), pltpu.VMEM((D, tn), jnp.bfloat16)],
        compiler_params=seq,
        name="expert_gate_up",
    )(xe, w_gate, w_up)
    return pl.pallas_call(
        _down_kernel,
        grid=(E, cap // tm),
        in_specs=[pl.BlockSpec((1, tm, F), lambda e, i: (e, i, 0)),
                  pl.BlockSpec((None, None, F, D), lambda e, i: (layer, e, 0, 0)),
                  pl.BlockSpec((1, tm, 1), lambda e, i: (e, i, 0)),
                  pl.BlockSpec((1, D), lambda e, i: (0, 0))],
        out_specs=pl.BlockSpec((1, tm, D), lambda e, i: (e, i, 0)),
        out_shape=jax.ShapeDtypeStruct((E, cap, D), jnp.bfloat16),
        scratch_shapes=[pltpu.VMEM((F, D), jnp.bfloat16)],
        compiler_params=pltpu.CompilerParams(dimension_semantics=("parallel", "arbitrary"),
                                             vmem_limit_bytes=VMEM_LIMIT_BYTES),
        name="expert_down",
    )(h, w_down, gsel.reshape(E, cap, 1).astype(jnp.float32), out_gate.reshape(1, D).astype(jnp.float32))


SEG_ROWS = 512
SEG_WIN = 512
SEG_LANES = 256


def _segment_add_kernel(first_ref, last_ref, tok_ref, ye_ref, x_ref, o_ref, *, n_tokens, win):
    c = pl.program_id(1)

    @pl.when(c == 0)
    def _():
        o_ref[...] = x_ref[...]

    base0 = (first_ref[c] // 8) * 8
    n_win = (last_ref[c] - base0) // win + 1
    tok = tok_ref[0]
    rows = ye_ref[...]

    def window(w, carry):
        lo = base0 + w * win
        base = pl.multiple_of(jnp.minimum(lo, n_tokens - win), 8)
        rid = base + lax.broadcasted_iota(jnp.int32, (win, tok.shape[1]), 0)
        onehot = ((rid == tok) & (tok >= lo)).astype(jnp.bfloat16)
        o_ref[pl.ds(base, win), :] += jnp.dot(onehot, rows, preferred_element_type=jnp.float32)
        return carry

    lax.fori_loop(0, n_win, window, 0)


def segment_add(x, ye_sorted, tok_sorted):
    N, D = x.shape
    P = ye_sorted.shape[0]
    rows = min(SEG_ROWS, P)
    win = min(SEG_WIN, N)
    tc = min(SEG_LANES, D)
    n_chunks = P // rows
    tok3 = tok_sorted.reshape(n_chunks, 1, rows)
    return pl.pallas_call(
        partial(_segment_add_kernel, n_tokens=N, win=win),
        grid_spec=pltpu.PrefetchScalarGridSpec(
            num_scalar_prefetch=2,
            grid=(D // tc, n_chunks),
            in_specs=[pl.BlockSpec((1, 1, rows), lambda d, c, f, l: (c, 0, 0)),
                      pl.BlockSpec((rows, tc), lambda d, c, f, l: (c, d)),
                      pl.BlockSpec((N, tc), lambda d, c, f, l: (0, d), pipeline_mode=pl.Buffered(1))],
            out_specs=pl.BlockSpec((N, tc), lambda d, c, f, l: (0, d)),
        ),
        out_shape=jax.ShapeDtypeStruct((N, D), jnp.float32),
        compiler_params=pltpu.CompilerParams(
            dimension_semantics=("parallel", "arbitrary"), vmem_limit_bytes=VMEM_LIMIT_BYTES),
        name="segment_add",
    )(tok3[:, 0, 0], tok3[:, 0, rows - 1], tok3, ye_sorted, x)


def expert_choice_latent(x1, h2, logits, out_gate, w_gate, w_up, w_down, layer):
    N, D = x1.shape
    cap = EC_CAPACITY_FACTOR * N // N_EXPERTS
    aff = jax.nn.softmax(logits[:, :N_EXPERTS], axis=-1)
    gsel, idx = lax.top_k(aff.T, cap)
    ye = expert_ffn(h2[idx], w_gate, w_up, w_down, layer, gsel, out_gate)
    tok = idx.reshape(-1)
    order = jnp.argsort(tok)
    return segment_add(x1, ye.reshape(-1, D)[order], tok[order])


def _ctx_attn_kernel(sink_ref, q_ref, k_ref, v_ref, o_ref, *, use_sink):
    s = lax.dot_general(q_ref[...], k_ref[...], _NT, preferred_element_type=jnp.float32)
    m = jnp.max(s, axis=-1, keepdims=True)
    if use_sink:
        sk = sink_ref[pl.program_id(0)]
        m = jnp.maximum(m, sk)
    p = jnp.exp(s - m)
    denom = jnp.sum(p, axis=-1, keepdims=True)
    if use_sink:
        denom = denom + jnp.exp(sk - m)
    o = jnp.dot(p.astype(v_ref.dtype), v_ref[...], preferred_element_type=jnp.float32)
    o_ref[...] = (o / denom).astype(o_ref.dtype)


def context_attention(q, q_blk, kv, k_blk, v_blk, sink, n_heads, n_kv_heads):
    Lc = q.shape[0]
    group = n_heads // n_kv_heads
    use_sink = sink is not None
    sink = jnp.zeros((n_heads,), jnp.float32) if sink is None else sink.astype(jnp.float32)
    return pl.pallas_call(
        partial(_ctx_attn_kernel, use_sink=use_sink),
        grid_spec=pltpu.PrefetchScalarGridSpec(
            num_scalar_prefetch=1,
            grid=(n_heads,),
            in_specs=[pl.BlockSpec((Lc, HEAD_DIM), lambda h, s: (0, q_blk + h)),
                      pl.BlockSpec((Lc, HEAD_DIM), lambda h, s: (0, k_blk + h // group)),
                      pl.BlockSpec((Lc, HEAD_DIM), lambda h, s: (0, v_blk + h // group))],
            out_specs=pl.BlockSpec((Lc, HEAD_DIM), lambda h, s: (0, h)),
        ),
        out_shape=jax.ShapeDtypeStruct((Lc, n_heads * HEAD_DIM), jnp.bfloat16),
        compiler_params=pltpu.CompilerParams(
            dimension_semantics=("parallel",), vmem_limit_bytes=VMEM_LIMIT_BYTES),
        name="context_attention",
    )(sink, q, kv, kv)


def trunk_layer(x, ctx, c, c_ctx, p, stacked, layer, update_ctx, consts):
    L = x.shape[0]
    bf = jnp.bfloat16
    w_in = stacked['w_in']
    cond = jnp.pad(jnp.concatenate([jax.nn.silu(c), jax.nn.silu(c_ctx)[None]], axis=0), ((0, 6), (0, 0)))
    mod = matmul(cond, stacked['w_mod'], layer=layer) + p['b_mod']
    mx = jnp.split(mod[0], N_MOD)
    mc = jnp.split(mod[1], N_MOD)
    qk_scale = HEAD_DIM ** -0.5
    nwq = WA_HEADS * HEAD_DIM
    nkv = WA_KV_HEADS * HEAD_DIM
    nna = NA_HEADS * HEAD_DIM
    proj = lambda a, *args: project(a, w_in, layer, *args)

    def mix_and_ffn(res, y_hy, y_wa, y_na, gates, m_vec):
        m = merge_gated(y_hy, y_wa, y_na, gates, p['w_branch'].astype(bf))
        r1, h2, logits = out_project(m, p['w_out'].astype(bf), res, m_vec[2], p['norm2_g'], m_vec[3], m_vec[4],
                                     p['w_router'])
        return expert_choice_latent(r1, h2, logits, m_vec[5], stacked['w_gate'], stacked['w_up'],
                                    stacked['w_down'], layer)

    ones = jnp.ones((HEAD_DIM,), jnp.float32)
    wa_k_gain = jnp.stack([p['wa_k_norm']] * WA_KV_HEADS + [ones] * WA_KV_HEADS)
    na_gain = jnp.stack([p['na_q_norm'] * qk_scale] * NA_HEADS + [p['na_k_norm']] * NA_HEADS + [ones] * NA_HEADS)

    hcb = norm_modulate(ctx, p['norm1_g'], mc[0], mc[1])
    kvc_wa = proj(hcb, OFF_WA_KV, 2 * nkv, "headnorm", bf, wa_k_gain, None, nkv)
    qkvc_na = proj(hcb, OFF_NA_Q, 3 * nna, "headnorm", bf, na_gain, None, 2 * nna)
    kvc_na = qkvc_na[:, nna:]

    hxb = norm_modulate(x, p['norm1_g'], mx[0], mx[1])
    rope = rope_lane_tables(L)
    y_hy = hyena_latent(proj(hxb, OFF_HY, OFF_WA_Q - OFF_HY), p, consts)
    q_wa = proj(hxb, OFF_WA_Q, nwq, "headnorm", bf, p['wa_q_norm'] * qk_scale, rope)
    kv_wa = proj(hxb, OFF_WA_KV, 2 * nkv, "headnorm", bf, wa_k_gain, rope, nkv)
    y_wa = windowed_attention(q_wa, kv_wa, kvc_wa, p['wa_sink'])
    qkv_na = proj(hxb, OFF_NA_Q, 3 * nna, "headnorm", bf, na_gain, None, 2 * nna)
    y_na = neighbourhood_attention(qkv_na, kvc_na, p['na_rpb'])
    gates = proj(hxb, OFF_GATE, N_BRANCH * D_MODEL, "sigmoid")
    x = mix_and_ffn(x, y_hy, y_wa, y_na, gates, mx)

    if update_ctx:
        yc_hy = hyena_branch(proj(hcb, OFF_HY, OFF_WA_Q - OFF_HY)[None], p)[0].astype(bf)
        qc_wa = proj(hcb, OFF_WA_Q, nwq, "headnorm", bf, p['wa_q_norm'] * qk_scale)
        yc_wa = context_attention(qc_wa, 0, kvc_wa, 0, WA_KV_HEADS, p['wa_sink'], WA_HEADS, WA_KV_HEADS)
        yc_na = context_attention(qkvc_na, 0, qkvc_na, NA_HEADS, 2 * NA_HEADS, None, NA_HEADS, NA_HEADS)
        gates_c = proj(hcb, OFF_GATE, N_BRANCH * D_MODEL, "sigmoid")
        ctx = mix_and_ffn(ctx, yc_hy, yc_wa, yc_na, gates_c, mc)
    return x, ctx


def kernel(x, c, ctx, c_ctx, w_mod, b_mod, norm1_g, w_in, hy_conv_w, hy_conv_b, hy_fw1, hy_fb1, hy_fw2, hy_fb2, hy_fw3, hy_freq, hy_decay, hy_d, wa_q_norm, wa_k_norm, wa_sink, na_q_norm, na_k_norm, na_rpb, w_branch, w_out, norm2_g, w_router, w_gate, w_up, w_down):
    consts = dft_constants(x.shape[1])
    stacked = {'w_mod': w_mod, 'w_in': w_in, 'w_gate': w_gate, 'w_up': w_up, 'w_down': w_down}
    xs, cs = x[0], ctx[0]
    for l in range(DEPTH):
        p = {
            'b_mod': b_mod[l], 'norm1_g': norm1_g[l],
            'hy_conv_w': hy_conv_w[l], 'hy_conv_b': hy_conv_b[l], 'hy_fw1': hy_fw1[l], 'hy_fb1': hy_fb1[l],
            'hy_fw2': hy_fw2[l], 'hy_fb2': hy_fb2[l], 'hy_fw3': hy_fw3[l], 'hy_freq': hy_freq[l],
            'hy_decay': hy_decay[l], 'hy_d': hy_d[l], 'wa_q_norm': wa_q_norm[l], 'wa_k_norm': wa_k_norm[l],
            'wa_sink': wa_sink[l], 'na_q_norm': na_q_norm[l], 'na_k_norm': na_k_norm[l], 'na_rpb': na_rpb[l],
            'w_branch': w_branch[l], 'w_out': w_out[l], 'norm2_g': norm2_g[l], 'w_router': w_router[l],
        }
        xs, cs = trunk_layer(xs, cs, c, c_ctx, p, stacked, l, l < DEPTH - 1, consts)
    return xs[None]
```

```python
import math
from functools import partial

import jax
import jax.numpy as jnp
from jax import lax
from jax.experimental import pallas as pl
from jax.experimental.pallas import tpu as pltpu

D_MODEL = 2048
SEQ = 16384
DEPTH = 2
CTX_LEN = 256
GRID_W = 64
HEAD_DIM = 128
BRANCH_WIDTH = 1024
N_BRANCH = 3
N_MOD = 6
RMS_EPS = 1e-6
NEG_INF = -1e30

HY_WIDTH = BRANCH_WIDTH
HY_ORDER = 2
HY_DIRS = 2
HY_BANDS = 16
HY_DECAY_SHIFT = 0.05

WA_HEADS = BRANCH_WIDTH // HEAD_DIM
WA_KV_HEADS = 2
WA_WINDOW = 128
WA_BLOCK = 128

NA_HEADS = BRANCH_WIDTH // HEAD_DIM
NA_WIN_ROWS = 8
NA_WIN_COLS = 16

ROPE_BASE = 10000.0

N_EXPERTS = 16
EC_CAPACITY_FACTOR = 2
D_EXPERT = 1024

OFF_HY = 0
OFF_WA_Q = OFF_HY + 3 * HY_WIDTH
OFF_WA_KV = OFF_WA_Q + WA_HEADS * HEAD_DIM
OFF_NA_Q = OFF_WA_KV + 2 * WA_KV_HEADS * HEAD_DIM
OFF_NA_KV = OFF_NA_Q + NA_HEADS * HEAD_DIM
OFF_GATE = OFF_NA_KV + 2 * NA_HEADS * HEAD_DIM
N_IN = OFF_GATE + N_BRANCH * D_MODEL

VMEM_LIMIT_BYTES = 56 * 1024 * 1024


def _mm_kernel(a_ref, b_ref, o_ref):
    o_ref[...] = jnp.dot(a_ref[...].astype(jnp.bfloat16), b_ref[...].astype(jnp.bfloat16),
                         preferred_element_type=jnp.float32).astype(o_ref.dtype)


def _pick(n, pref):
    for t in pref:
        if n % t == 0:
            return t
    return n


def matmul(a, b, out_dtype=jnp.float32, layer=None):
    M, K = a.shape
    N = b.shape[-1]
    tm = _pick(M, (512, 256, 128, 64, 32, 16, 8))
    tn = _pick(N, (1024, 512, 256, 128))
    if layer is None:
        b_spec = pl.BlockSpec((K, tn), lambda j, i: (0, j))
    else:
        b_spec = pl.BlockSpec((None, K, tn), lambda j, i: (layer, 0, j))
    return pl.pallas_call(
        _mm_kernel,
        grid=(N // tn, M // tm),
        in_specs=[pl.BlockSpec((tm, K), lambda j, i: (i, 0)), b_spec],
        out_specs=pl.BlockSpec((tm, tn), lambda j, i: (i, j)),
        out_shape=jax.ShapeDtypeStruct((M, N), out_dtype),
        compiler_params=pltpu.CompilerParams(
            dimension_semantics=("parallel", "parallel"), vmem_limit_bytes=VMEM_LIMIT_BYTES),
        name="matmul",
    )(a, b)


def _norm_mod_kernel(x_ref, g_ref, shift_ref, scale_ref, o_ref):
    x = x_ref[...]
    y = x * lax.rsqrt(jnp.mean(x * x, axis=-1, keepdims=True) + RMS_EPS) * g_ref[...]
    o_ref[...] = (y * (1.0 + scale_ref[...]) + shift_ref[...]).astype(o_ref.dtype)


def norm_modulate(x, g, shift, scale, out_dtype=jnp.bfloat16):
    M, D = x.shape
    tm = _pick(M, (512, 256, 128, 64, 32, 16, 8))
    vec = pl.BlockSpec((1, D), lambda i: (0, 0))
    return pl.pallas_call(
        _norm_mod_kernel,
        grid=(M // tm,),
        in_specs=[pl.BlockSpec((tm, D), lambda i: (i, 0)), vec, vec, vec],
        out_specs=pl.BlockSpec((tm, D), lambda i: (i, 0)),
        out_shape=jax.ShapeDtypeStruct((M, D), out_dtype),
        compiler_params=pltpu.CompilerParams(
            dimension_semantics=("parallel",), vmem_limit_bytes=VMEM_LIMIT_BYTES),
        name="norm_modulate",
    )(x, g.reshape(1, D), shift.reshape(1, D), scale.reshape(1, D))


def _swap_halves(x):
    lane = lax.broadcasted_iota(jnp.int32, x.shape, 1)
    return jnp.where((lane % 64) < 32, pltpu.roll(x, 96, 1), pltpu.roll(x, 32, 1))


HEADNORM_ROWS = 256
HEADNORM_COLS = 256


def _proj_kernel(*refs, mode, rope, norm_cols, n_tiles):
    if mode == "headnorm":
        if rope:
            a_ref, w_ref, gain_ref, cos_ref, sin_ref, o_ref, w_bf = refs
        else:
            a_ref, w_ref, gain_ref, o_ref, w_bf = refs
    else:
        a_ref, w_ref, o_ref, w_bf = refs

    @pl.when(pl.program_id(1) == 0)
    def _():
        w_bf[...] = w_ref[...].astype(jnp.bfloat16)

    if mode == "plain":
        o_ref[...] = jnp.dot(a_ref[...], w_bf[...], preferred_element_type=jnp.float32).astype(o_ref.dtype)
        return
    if mode == "sigmoid":
        acc = jnp.dot(a_ref[...], w_bf[...], preferred_element_type=jnp.float32)
        o_ref[...] = jax.nn.sigmoid(acc).astype(o_ref.dtype)
        return

    tm, tn = o_ref.shape
    rows = min(tm, HEADNORM_ROWS) if rope else tm
    width = min(tn, HEADNORM_COLS) if rope else tn

    def epilogue(norm_heads):
        def chunk(r, carry):
            r0 = pl.multiple_of(r * rows, rows)
            a = a_ref[pl.ds(r0, rows), :]
            for c0 in range(0, tn, width):
                acc = jnp.dot(a, w_bf[:, c0:c0 + width], preferred_element_type=jnp.float32)
                for h in range(width // HEAD_DIM):
                    head = c0 // HEAD_DIM + h
                    y = acc[:, h * HEAD_DIM:(h + 1) * HEAD_DIM]
                    if head < norm_heads:
                        y = y * lax.rsqrt(jnp.mean(y * y, axis=-1, keepdims=True) + RMS_EPS) * gain_ref[head:head + 1, :]
                        if rope:
                            y = (y * cos_ref[pl.ds(r0, rows), :]
                                 + _swap_halves(y) * sin_ref[pl.ds(r0, rows), :])
                    o_ref[pl.ds(r0, rows), c0 + h * HEAD_DIM:c0 + (h + 1) * HEAD_DIM] = y.astype(o_ref.dtype)
            return carry

        lax.fori_loop(0, tm // rows, chunk, 0)

    heads = tn // HEAD_DIM
    if n_tiles == 1 or norm_cols >= n_tiles * tn:
        epilogue(min(heads, norm_cols // HEAD_DIM))
    else:
        norm_tiles = norm_cols // tn
        j = pl.program_id(0)
        pl.when(j < norm_tiles)(lambda: epilogue(heads))
        pl.when(j >= norm_tiles)(lambda: epilogue(0))


def project(a, w, layer, col_off, n_cols, mode="plain", out_dtype=jnp.float32, gain=None, rope=None,
            norm_cols=None):
    M, K = a.shape
    tm = _pick(M, (1024, 512, 256, 128, 64, 32, 16, 8))
    norm_cols = n_cols if norm_cols is None else norm_cols
    tn = next(t for t in (1024, 768, 512, 256, 128)
              if n_cols % t == 0 and col_off % t == 0 and (norm_cols % t == 0 or t == n_cols))
    off = col_off // tn
    n_tiles = n_cols // tn
    in_specs = [pl.BlockSpec((tm, K), lambda j, i: (i, 0)),
                pl.BlockSpec((None, K, tn), lambda j, i: (layer, 0, off + j))]
    args = [a, w]
    if mode == "headnorm":
        assert n_tiles == 1 or norm_cols % tn == 0
        heads = tn // HEAD_DIM
        gain = jnp.broadcast_to(gain.astype(jnp.float32).reshape(-1, HEAD_DIM), (n_cols // HEAD_DIM, HEAD_DIM))
        in_specs.append(pl.BlockSpec((None, heads, HEAD_DIM), lambda j, i: (j, 0, 0)))
        args.append(gain.reshape(n_tiles, heads, HEAD_DIM))
        if rope is not None:
            in_specs += [pl.BlockSpec((tm, HEAD_DIM), lambda j, i: (i, 0))] * 2
            args += list(rope)
    return pl.pallas_call(
        partial(_proj_kernel, mode=mode, rope=rope is not None, norm_cols=norm_cols, n_tiles=n_tiles),
        grid=(n_tiles, M // tm),
        in_specs=in_specs,
        out_specs=pl.BlockSpec((tm, tn), lambda j, i: (i, j)),
        out_shape=jax.ShapeDtypeStruct((M, n_cols), out_dtype),
        scratch_shapes=[pltpu.VMEM((K, tn), jnp.bfloat16)],
        compiler_params=pltpu.CompilerParams(
            dimension_semantics=("parallel", "arbitrary"), vmem_limit_bytes=VMEM_LIMIT_BYTES),
        name="project_" + mode,
    )(*args)


def rope_lane_tables(L):
    t = jnp.arange(L, dtype=jnp.int32)
    row = (t // GRID_W).astype(jnp.float32)
    col = (t % GRID_W).astype(jnp.float32)
    nf = HEAD_DIM // 4
    inv = ROPE_BASE ** (-jnp.arange(nf, dtype=jnp.float32) / nf)
    ar, ac = row[:, None] * inv, col[:, None] * inv
    cos = jnp.concatenate([jnp.cos(ar), jnp.cos(ar), jnp.cos(ac), jnp.cos(ac)], axis=-1)
    sin = jnp.concatenate([-jnp.sin(ar), jnp.sin(ar), -jnp.sin(ac), jnp.sin(ac)], axis=-1)
    return cos, sin


_NT = (((1,), (1,)), ((), ()))


def _wa_kernel(sink_ref, q_ref, k_ref, v_ref, kc_ref, vc_ref, o_ref, *, tq, seq):
    g = pl.program_id(0)
    i = pl.program_id(1)
    nwin = tq + 2 * WA_WINDOW
    ws = jnp.clip(i * tq - WA_WINDOW, 0, seq - nwin)
    start = pl.multiple_of(ws, WA_WINDOW)
    kwin = k_ref[pl.ds(start, nwin), :]
    vwin = v_ref[pl.ds(start, nwin), :]
    qpos = i * tq + lax.broadcasted_iota(jnp.int32, (tq, nwin), 0)
    kpos = ws + lax.broadcasted_iota(jnp.int32, (tq, nwin), 1)
    valid = jnp.abs(qpos - kpos) <= WA_WINDOW
    group = WA_HEADS // WA_KV_HEADS
    for hh in range(group):
        q = q_ref[:, hh * HEAD_DIM:(hh + 1) * HEAD_DIM]
        s = jnp.where(valid, lax.dot_general(q, kwin, _NT, preferred_element_type=jnp.float32), NEG_INF)
        sc = lax.dot_general(q, kc_ref[...], _NT, preferred_element_type=jnp.float32)
        sk = sink_ref[g * group + hh]
        m = jnp.maximum(jnp.maximum(jnp.max(s, axis=-1, keepdims=True), jnp.max(sc, axis=-1, keepdims=True)), sk)
        p = jnp.exp(s - m)
        pc = jnp.exp(sc - m)
        denom = jnp.sum(p, axis=-1, keepdims=True) + jnp.sum(pc, axis=-1, keepdims=True) + jnp.exp(sk - m)
        o = (jnp.dot(p.astype(vwin.dtype), vwin, preferred_element_type=jnp.float32)
             + jnp.dot(pc.astype(vwin.dtype), vc_ref[...], preferred_element_type=jnp.float32))
        o_ref[:, hh * HEAD_DIM:(hh + 1) * HEAD_DIM] = (o / denom).astype(o_ref.dtype)


def windowed_attention(q, kv, kvc, sink, tq=256):
    L = q.shape[0]
    Lc = kvc.shape[0]
    gw = (WA_HEADS // WA_KV_HEADS) * HEAD_DIM
    k_slab = pl.BlockSpec((L, HEAD_DIM), lambda g, i, s: (0, g))
    v_slab = pl.BlockSpec((L, HEAD_DIM), lambda g, i, s: (0, WA_KV_HEADS + g))
    kc_slab = pl.BlockSpec((Lc, HEAD_DIM), lambda g, i, s: (0, g))
    vc_slab = pl.BlockSpec((Lc, HEAD_DIM), lambda g, i, s: (0, WA_KV_HEADS + g))
    return pl.pallas_call(
        partial(_wa_kernel, tq=tq, seq=L),
        grid_spec=pltpu.PrefetchScalarGridSpec(
            num_scalar_prefetch=1,
            grid=(WA_KV_HEADS, L // tq),
            in_specs=[pl.BlockSpec((tq, gw), lambda g, i, s: (i, g)), k_slab, v_slab, kc_slab, vc_slab],
            out_specs=pl.BlockSpec((tq, gw), lambda g, i, s: (i, g)),
        ),
        out_shape=jax.ShapeDtypeStruct((L, WA_HEADS * HEAD_DIM), jnp.bfloat16),
        compiler_params=pltpu.CompilerParams(
            dimension_semantics=("parallel", "parallel"), vmem_limit_bytes=VMEM_LIMIT_BYTES),
        name="windowed_attention",
    )(sink.astype(jnp.float32), q, kv, kv, kvc, kvc)


NA_ROW_BLOCK = 4
NA_KEY_ROWS = NA_ROW_BLOCK + NA_WIN_ROWS - 1


NA_HEAD_GROUP = 4


def _na_kernel(q_ref, k_ref, v_ref, kc_ref, vc_ref, bias_ref, o_ref, *, rows):
    i = pl.program_id(1)
    ws = jnp.clip(i * NA_ROW_BLOCK - NA_WIN_ROWS // 2, 0, rows - NA_KEY_ROWS)
    start = pl.multiple_of(ws * GRID_W, GRID_W)
    nk = NA_KEY_ROWS * GRID_W
    for hh in range(NA_HEAD_GROUP):
        cols = slice(hh * HEAD_DIM, (hh + 1) * HEAD_DIM)
        kwin = k_ref[pl.ds(start, nk), cols]
        vwin = v_ref[pl.ds(start, nk), cols]
        q = q_ref[:, cols]
        s = lax.dot_general(q, kwin, _NT, preferred_element_type=jnp.float32) + bias_ref[0, hh]
        sc = lax.dot_general(q, kc_ref[:, cols], _NT, preferred_element_type=jnp.float32)
        m = jnp.maximum(jnp.max(s, axis=-1, keepdims=True), jnp.max(sc, axis=-1, keepdims=True))
        p = jnp.exp(s - m)
        pc = jnp.exp(sc - m)
        denom = jnp.sum(p, axis=-1, keepdims=True) + jnp.sum(pc, axis=-1, keepdims=True)
        o = (jnp.dot(p.astype(vwin.dtype), vwin, preferred_element_type=jnp.float32)
             + jnp.dot(pc.astype(vwin.dtype), vc_ref[:, cols], preferred_element_type=jnp.float32))
        o_ref[:, cols] = (o / denom).astype(o_ref.dtype)


def na_bias_tiles(rpb, rows):
    col = jnp.arange(GRID_W)
    cstart = jnp.clip(col - NA_WIN_COLS // 2, 0, GRID_W - NA_WIN_COLS)
    col_in = (col[None, :] >= cstart[:, None]) & (col[None, :] < cstart[:, None] + NA_WIN_COLS)
    dc_idx = jnp.clip(col[None, :] - col[:, None] + NA_WIN_COLS - 1, 0, 2 * NA_WIN_COLS - 2)
    exact = lax.Precision.HIGHEST
    by_col = jnp.einsum('hrd,qkd->hrqk', rpb.astype(jnp.float32),
                        jax.nn.one_hot(dc_idx, 2 * NA_WIN_COLS - 1, dtype=jnp.float32), precision=exact)
    tiles = []
    for blk in (0, 1, rows // NA_ROW_BLOCK - 1):
        r = blk * NA_ROW_BLOCK
        ws = min(max(r - NA_WIN_ROWS // 2, 0), rows - NA_KEY_ROWS)
        qr = r + jnp.arange(NA_ROW_BLOCK)
        kr = ws + jnp.arange(NA_KEY_ROWS)
        r0 = jnp.clip(qr - NA_WIN_ROWS // 2, 0, rows - NA_WIN_ROWS)
        row_in = (kr[None, :] >= r0[:, None]) & (kr[None, :] < r0[:, None] + NA_WIN_ROWS)
        dr_idx = jnp.clip(kr[None, :] - qr[:, None] + NA_WIN_ROWS - 1, 0, 2 * NA_WIN_ROWS - 2)
        b = jnp.einsum('hrqk,abr->haqbk', by_col,
                       jax.nn.one_hot(dr_idx, 2 * NA_WIN_ROWS - 1, dtype=jnp.float32), precision=exact)
        ok = row_in[:, None, :, None] & col_in[None, :, None, :]
        b = jnp.where(ok[None], b, NEG_INF)
        tiles.append(b.reshape(rpb.shape[0], NA_ROW_BLOCK * GRID_W, NA_KEY_ROWS * GRID_W))
    return jnp.stack(tiles)


def neighbourhood_attention(qkv, kvc, rpb):
    L = qkv.shape[0]
    Lc = kvc.shape[0]
    rows = L // GRID_W
    nblk = rows // NA_ROW_BLOCK
    tq = NA_ROW_BLOCK * GRID_W
    nk = NA_KEY_ROWS * GRID_W
    bias = na_bias_tiles(rpb, rows)
    gw = NA_HEAD_GROUP * HEAD_DIM
    ng = NA_HEADS // NA_HEAD_GROUP
    slab = lambda part: pl.BlockSpec((L, gw), lambda h, i: (0, part * ng + h), pipeline_mode=pl.Buffered(1))
    cslab = lambda part: pl.BlockSpec((Lc, gw), lambda h, i: (0, part * ng + h))
    variant = lambda h, i: (jnp.where(i == 0, 0, jnp.where(i == nblk - 1, 2, 1)), h, 0, 0)
    return pl.pallas_call(
        partial(_na_kernel, rows=rows),
        grid=(ng, nblk),
        in_specs=[pl.BlockSpec((tq, gw), lambda h, i: (i, h)), slab(1), slab(2), cslab(0), cslab(1),
                  pl.BlockSpec((1, NA_HEAD_GROUP, tq, nk), variant)],
        out_specs=pl.BlockSpec((tq, gw), lambda h, i: (i, h)),
        out_shape=jax.ShapeDtypeStruct((L, NA_HEADS * HEAD_DIM), jnp.bfloat16),
        compiler_params=pltpu.CompilerParams(
            dimension_semantics=("parallel", "parallel"), vmem_limit_bytes=VMEM_LIMIT_BYTES),
        name="neighbourhood_attention",
    )(qkv, qkv, qkv, kvc, kvc, bias)


DFT_N1 = 128
DFT_K1 = DFT_N1 // 2 + 1
DFT_K1_PAD = 72


def dft_constants(L):
    N = 2 * L
    N2 = N // DFT_N1
    f32, bf = jnp.float32, jnp.bfloat16
    k1 = jnp.arange(DFT_K1, dtype=jnp.int32)
    n1 = jnp.arange(DFT_N1, dtype=jnp.int32)
    th = ((k1[:, None] * n1[None, :]) % DFT_N1).astype(f32) * (2.0 * math.pi / DFT_N1)
    pad = ((0, DFT_K1_PAD - DFT_K1), (0, 0))
    rows_fwd = jnp.concatenate([jnp.pad(jnp.cos(th), pad), jnp.pad(-jnp.sin(th), pad)], axis=0)
    ck = jnp.where((k1 == 0) | (k1 == DFT_N1 // 2), 1.0, 2.0)[:, None]
    half = DFT_N1 // 2
    rows_inv = jnp.concatenate([jnp.pad(ck * jnp.cos(th[:, :half]), pad),
                                jnp.pad(-ck * jnp.sin(th[:, :half]), pad)], axis=0).T
    n2 = jnp.arange(N2, dtype=jnp.int32)
    alpha = ((k1[:, None] * n2[None, :]) % N).astype(f32) * (2.0 * math.pi / N)
    beta = ((n2[:, None] * n2[None, :]) % N2).astype(f32) * (2.0 * math.pi / N2)
    twr, twi = jnp.cos(alpha)[:, None, :], -jnp.sin(alpha)[:, None, :]
    fr, fi = jnp.cos(beta)[None], -jnp.sin(beta)[None]
    gr, gi = twr * fr - twi * fi, twr * fi + twi * fr
    mid_fwd = jnp.concatenate([jnp.concatenate([gr, -gi], axis=2),
                               jnp.concatenate([gi, gr], axis=2)], axis=1)
    return {
        "rows_fwd": rows_fwd.astype(bf), "rows_fwd_half": rows_fwd[:, :half].astype(bf),
        "rows_inv": rows_inv.astype(bf),
        "mid_fwd": mid_fwd.astype(bf),
    }


def _spectral_fwd_kernel(a_ref, g_ref, o_ref):
    n2 = a_ref.shape[2]
    x = (jnp.dot(g_ref[0, :, :n2], a_ref[0, 0], preferred_element_type=jnp.float32)
         + jnp.dot(g_ref[0, :, n2:], a_ref[1, 0], preferred_element_type=jnp.float32))
    o_ref[0, 0] = x[:n2].astype(o_ref.dtype)
    o_ref[1, 0] = x[n2:].astype(o_ref.dtype)


def spectral_fwd(a, mid_fwd, cb=2048):
    _, _, n2, C = a.shape
    cb = min(cb, C)
    return pl.pallas_call(
        _spectral_fwd_kernel,
        grid=(DFT_K1, C // cb),
        in_specs=[pl.BlockSpec((2, 1, n2, cb), lambda k, c: (0, k, 0, c)),
                  pl.BlockSpec((1, 2 * n2, 2 * n2), lambda k, c: (k, 0, 0))],
        out_specs=pl.BlockSpec((2, 1, n2, cb), lambda k, c: (0, k, 0, c)),
        out_shape=jax.ShapeDtypeStruct((2, DFT_K1, n2, C), jnp.bfloat16),
        compiler_params=pltpu.CompilerParams(
            dimension_semantics=("parallel", "parallel"), vmem_limit_bytes=VMEM_LIMIT_BYTES),
        name="spectral_fwd",
    )(a, mid_fwd)


_TN = (((0,), (0,)), ((), ()))


def _spectral_mid_kernel(a_ref, gf_ref, ks_ref, o_ref):
    n2 = a_ref.shape[2]
    k1 = pl.program_id(0)

    @pl.when(k1 < DFT_K1)
    def _():
        x = (jnp.dot(gf_ref[0, :, :n2], a_ref[0, 0], preferred_element_type=jnp.float32)
             + jnp.dot(gf_ref[0, :, n2:], a_ref[1, 0], preferred_element_type=jnp.float32))
        xr, xi = x[:n2], x[n2:]
        kr, ki = ks_ref[0, 0].astype(jnp.float32), ks_ref[1, 0].astype(jnp.float32)
        yr = (xr * kr - xi * ki).astype(jnp.bfloat16)
        yi = (xr * ki + xi * kr).astype(jnp.bfloat16)
        b = (lax.dot_general(gf_ref[0, :n2, :], yr, _TN, preferred_element_type=jnp.float32)
             + lax.dot_general(gf_ref[0, n2:, :], yi, _TN, preferred_element_type=jnp.float32))
        o_ref[0, 0] = b[:n2].astype(o_ref.dtype)
        o_ref[1, 0] = b[n2:].astype(o_ref.dtype)

    @pl.when(k1 >= DFT_K1)
    def _():
        o_ref[...] = jnp.zeros_like(o_ref)


def spectral_mid(a, consts, kspec, col_off, cb=1024):
    _, _, n2, C = a.shape
    cb = min(cb, C)
    off = col_off // cb
    kc = lambda k: jnp.minimum(k, DFT_K1 - 1)
    return pl.pallas_call(
        _spectral_mid_kernel,
        grid=(DFT_K1_PAD, C // cb),
        in_specs=[pl.BlockSpec((2, 1, n2, cb), lambda k, c: (0, k, 0, c)),
                  pl.BlockSpec((1, 2 * n2, 2 * n2), lambda k, c: (kc(k), 0, 0)),
                  pl.BlockSpec((2, 1, n2, cb), lambda k, c: (0, kc(k), 0, off + c))],
        out_specs=pl.BlockSpec((2, 1, n2, cb), lambda k, c: (0, k, 0, c)),
        out_shape=jax.ShapeDtypeStruct(a.shape, jnp.bfloat16),
        compiler_params=pltpu.CompilerParams(
            dimension_semantics=("parallel", "parallel"), vmem_limit_bytes=VMEM_LIMIT_BYTES),
        name="spectral_mid",
    )(a, consts["mid_fwd"], kspec)


DFT_ROW_GROUP = 16
DFT_ROW_LANES = 256


def _rows_fwd_kernel(f_ref, u_ref, o_ref):
    x = pltpu.einshape("abc->bac", u_ref[...].astype(jnp.float32))
    rs = [jnp.dot(f_ref[...], x[s].astype(jnp.bfloat16), preferred_element_type=jnp.float32)
          for s in range(x.shape[0])]
    o_ref[...] = pltpu.einshape("abc->bac", jnp.stack(rs)).astype(o_ref.dtype)


def dft_rows_fwd(rows_mat, u3):
    kn, n2, C = u3.shape
    grp = min(DFT_ROW_GROUP, n2)
    tc = min(DFT_ROW_LANES, C)
    m = rows_mat.shape[0]
    return pl.pallas_call(
        _rows_fwd_kernel,
        grid=(n2 // grp, C // tc),
        in_specs=[pl.BlockSpec((m, kn), lambda j, c: (0, 0)),
                  pl.BlockSpec((kn, grp, tc), lambda j, c: (0, j, c))],
        out_specs=pl.BlockSpec((m, grp, tc), lambda j, c: (0, j, c)),
        out_shape=jax.ShapeDtypeStruct((m, n2, C), jnp.bfloat16),
        compiler_params=pltpu.CompilerParams(
            dimension_semantics=("parallel", "parallel"), vmem_limit_bytes=VMEM_LIMIT_BYTES),
        name="dft_rows_fwd",
    )(rows_mat, u3)


def _rows_inv_kernel(f_ref, b_ref, u_ref, gate_ref, scale_ref, d_ref, o_ref):
    x = pltpu.einshape("abc->bac", b_ref[...].astype(jnp.float32))
    ys = [jnp.dot(f_ref[...], x[s].astype(jnp.bfloat16), preferred_element_type=jnp.float32)
          for s in range(x.shape[0])]
    y = pltpu.einshape("abc->bac", jnp.stack(ys))
    o_ref[...] = (gate_ref[...] * (y * scale_ref[...] + u_ref[...] * d_ref[...])).astype(o_ref.dtype)


def dft_rows_inv(rows_inv, b3, u3, gate3, scale, d_skip, out_dtype):
    nr, n2, C = u3.shape
    grp = min(DFT_ROW_GROUP, n2)
    tc = min(DFT_ROW_LANES, C)
    blk = pl.BlockSpec((nr, grp, tc), lambda j, c: (0, j, c))
    vec = pl.BlockSpec((1, 1, tc), lambda j, c: (0, 0, c))
    return pl.pallas_call(
        _rows_inv_kernel,
        grid=(n2 // grp, C // tc),
        in_specs=[pl.BlockSpec(rows_inv.shape, lambda j, c: (0, 0)),
                  pl.BlockSpec((b3.shape[0], grp, tc), lambda j, c: (0, j, c)), blk, blk, vec, vec],
        out_specs=blk,
        out_shape=jax.ShapeDtypeStruct((nr, n2, C), out_dtype),
        compiler_params=pltpu.CompilerParams(
            dimension_semantics=("parallel", "parallel"), vmem_limit_bytes=VMEM_LIMIT_BYTES),
        name="dft_rows_inv",
    )(rows_inv, b3, u3, gate3, scale.reshape(1, 1, C), d_skip.reshape(1, 1, C))


def _conv3_kernel(z_ref, prev_ref, next_ref, w_ref, b_ref, o_ref):
    i = pl.program_id(0)
    z = z_ref[...]
    tm = z.shape[0]
    row = lax.broadcasted_iota(jnp.int32, z.shape, 0)
    prev_row = jnp.where(i > 0, prev_ref[7:8, :], 0.0)
    next_row = jnp.where(i < pl.num_programs(0) - 1, next_ref[0:1, :], 0.0)
    zp = jnp.where(row == 0, prev_row, pltpu.roll(z, 1, 0))
    zn = jnp.where(row == tm - 1, next_row, pltpu.roll(z, tm - 1, 0))
    o_ref[...] = zp * w_ref[0:1, :] + z * w_ref[1:2, :] + zn * w_ref[2:3, :] + b_ref[...]


def short_conv3_part(z, w, b, part, width):
    L = z.shape[0]
    tm = _pick(L, (512, 256, 128, 64, 32, 16, 8))
    tc = _pick(width, (1024, 512, 256, 128))
    off = part * width // tc
    halo = 8
    return pl.pallas_call(
        _conv3_kernel,
        grid=(L // tm, width // tc),
        in_specs=[pl.BlockSpec((tm, tc), lambda i, j: (i, off + j)),
                  pl.BlockSpec((halo, tc), lambda i, j: (jnp.maximum(i * (tm // halo) - 1, 0), off + j)),
                  pl.BlockSpec((halo, tc), lambda i, j: (jnp.minimum((i + 1) * (tm // halo), L // halo - 1), off + j)),
                  pl.BlockSpec((3, tc), lambda i, j: (0, off + j)),
                  pl.BlockSpec((1, tc), lambda i, j: (0, off + j))],
        out_specs=pl.BlockSpec((tm, tc), lambda i, j: (i, j)),
        out_shape=jax.ShapeDtypeStruct((L, width), jnp.float32),
        compiler_params=pltpu.CompilerParams(
            dimension_semantics=("parallel", "parallel"), vmem_limit_bytes=VMEM_LIMIT_BYTES),
        name="short_conv3",
    )(z, z, z, w, b.reshape(1, -1))


FILTER_PAD = 128
FILTER_LANES = 512


def _filter_rows_kernel(feats_ref, feats0_ref, fw1_ref, fb1_ref, fw2_ref, fb2_ref, fr_ref, fw3_ref, rate_ref,
                        rows_ref, a_ref, asum_ref, hid_ref, hid0_ref):
    g = pl.program_id(0)
    exact = lax.Precision.HIGHEST
    n_dir, prow, nf2 = feats_ref.shape
    hid = fw3_ref.shape[1]
    half = rows_ref.shape[1] // 2
    grp = 2 * prow // half

    def hidden(feats):
        h = jnp.sin(fr_ref[0:1, :] * (jnp.dot(feats, fw1_ref[...], precision=exact,
                                              preferred_element_type=jnp.float32) + fb1_ref[...]))
        return jnp.sin(fr_ref[1:2, :] * (jnp.dot(h, fw2_ref[...], precision=exact,
                                                 preferred_element_type=jnp.float32) + fb2_ref[...]))

    @pl.when(pl.program_id(1) == 0)
    def _():
        for d in range(n_dir):
            hid_ref[d] = hidden(feats_ref[d])
        hid0_ref[...] = hidden(feats0_ref[...])

    def taps(h, t, d):
        k = jnp.dot(h.astype(jnp.bfloat16), fw3_ref[d].astype(jnp.bfloat16), preferred_element_type=jnp.float32)
        return k * (jnp.exp(-t * rate_ref[d]) + HY_DECAY_SHIFT)

    def direction(d):
        h2, f = hid_ref[d], feats_ref[d]
        return jnp.concatenate([taps(h2[:, :hid], f[:, 0:1], d),
                                taps(h2[:, hid:], f[:, nf2 // 2:nf2 // 2 + 1], d)], axis=0)

    k_fwd = direction(0)
    k_bwd = direction(1)
    lag0_back = taps(hid0_ref[:, :hid], feats0_ref[:, 0:1], 1)[0:1, :]
    first = (lax.broadcasted_iota(jnp.int32, k_fwd.shape, 0) == 0) & (g == 0)
    k_fwd = k_fwd + jnp.where(first, lag0_back, 0.0)
    k_bwd = jnp.where(first, 0.0, k_bwd)
    asum_ref[0] = jnp.sum(jnp.abs(k_fwd), axis=0, keepdims=True) + jnp.sum(jnp.abs(k_bwd), axis=0, keepdims=True)
    kf = k_fwd.astype(jnp.bfloat16)
    kb = k_bwd.astype(jnp.bfloat16)
    rs = [jnp.dot(rows_ref[:, :half], kf[s * half:(s + 1) * half], preferred_element_type=jnp.float32)
          + jnp.dot(rows_ref[:, half:], kb[s * half:(s + 1) * half], preferred_element_type=jnp.float32)
          for s in range(grp)]
    a_ref[...] = pltpu.einshape("abc->bac", jnp.stack(rs)).astype(a_ref.dtype)


def _block_diag2(w):
    z = jnp.zeros_like(w)
    return jnp.concatenate([jnp.concatenate([w, z], axis=1), jnp.concatenate([z, w], axis=1)], axis=0)


def hyena_filter_rows(L, p, rows_fwd):
    f32 = jnp.float32
    N = 2 * L
    n2 = N // DFT_N1
    half = DFT_N1 // 2
    C2 = HY_ORDER * HY_WIDTH
    grp = min(DFT_ROW_GROUP, n2)
    tc = min(FILTER_LANES, C2)
    prow = grp * half // 2
    n = (n2 * jnp.arange(DFT_N1, dtype=jnp.int32)[None, :] + jnp.arange(n2, dtype=jnp.int32)[:, None])
    pos = jnp.where(n < L, n, N - n).astype(f32)[..., None]
    bands = jnp.linspace(1e-4, HY_BANDS - 1, HY_BANDS, dtype=f32)
    ang = (2.0 * math.pi / L) * bands * pos
    feats = jnp.concatenate([pos / (L - 1), jnp.cos(ang), -jnp.sin(ang)], axis=-1)
    feats = jnp.pad(feats, ((0, 0), (0, 0), (0, FILTER_PAD - feats.shape[-1])))
    feats = feats.reshape(n2 // grp, grp, 2, half, FILTER_PAD).transpose(2, 0, 1, 3, 4)
    feats = feats.reshape(2, n2 // grp, 2, prow, FILTER_PAD).transpose(0, 1, 3, 2, 4)
    feats = feats.reshape(2, (n2 // grp) * prow, 2 * FILTER_PAD)
    feats0 = jnp.broadcast_to(feats[0, 0:1], (8, 2 * FILTER_PAD))
    hid = p['hy_fw2'].shape[0]
    fw1 = _block_diag2(jnp.pad(p['hy_fw1'].astype(f32), ((0, FILTER_PAD - p['hy_fw1'].shape[0]), (0, 0))))
    fb1 = jnp.tile(p['hy_fb1'].astype(f32), 2).reshape(1, -1)
    fw2 = _block_diag2(p['hy_fw2'].astype(f32))
    fb2 = jnp.tile(p['hy_fb2'].astype(f32), 2).reshape(1, -1)
    fr = jnp.tile(p['hy_freq'].astype(f32), (1, 2))
    fw3 = p['hy_fw3'].astype(f32).reshape(hid, HY_DIRS, C2).transpose(1, 0, 2)
    rate = jnp.abs(p['hy_decay'].astype(f32)).reshape(HY_DIRS, 1, C2)
    full = lambda shape: pl.BlockSpec(shape, lambda g, c: (0,) * len(shape))
    m = rows_fwd.shape[0]
    a, asum = pl.pallas_call(
        _filter_rows_kernel,
        grid=(n2 // grp, C2 // tc),
        in_specs=[pl.BlockSpec((2, prow, 2 * FILTER_PAD), lambda g, c: (0, g, 0)), full((8, 2 * FILTER_PAD)),
                  full((2 * FILTER_PAD, 2 * hid)), full((1, 2 * hid)),
                  full((2 * hid, 2 * hid)), full((1, 2 * hid)), full((2, 2 * hid)),
                  pl.BlockSpec((HY_DIRS, hid, tc), lambda g, c: (0, 0, c)),
                  pl.BlockSpec((HY_DIRS, 1, tc), lambda g, c: (0, 0, c)),
                  full(rows_fwd.shape)],
        out_specs=[pl.BlockSpec((m, grp, tc), lambda g, c: (0, g, c)),
                   pl.BlockSpec((1, 1, tc), lambda g, c: (g, 0, c))],
        out_shape=[jax.ShapeDtypeStruct((m, n2, C2), jnp.bfloat16),
                   jax.ShapeDtypeStruct((n2 // grp, 1, C2), f32)],
        scratch_shapes=[pltpu.VMEM((2, prow, 2 * hid), f32), pltpu.VMEM((8, 2 * hid), f32)],
        compiler_params=pltpu.CompilerParams(
            dimension_semantics=("parallel", "arbitrary"), vmem_limit_bytes=VMEM_LIMIT_BYTES),
        name="hyena_filter_rows",
    )(feats, feats0, fw1, fb1, fw2, fb2, fr, fw3, rate, rows_fwd)
    return a, jnp.sum(asum, axis=(0, 1))


def hyena_latent(z, p, consts):
    L = z.shape[0]
    C = HY_WIDTH
    N = 2 * L
    n2 = N // DFT_N1
    half = DFT_N1 // 2
    v, x1, x2 = (short_conv3_part(z, p['hy_conv_w'], p['hy_conv_b'], i, C) for i in range(3))
    ka, asum = hyena_filter_rows(L, p, consts["rows_fwd"])
    kspec = spectral_fwd(ka.reshape(2, DFT_K1_PAD, n2, HY_ORDER * C), consts["mid_fwd"])
    scale = 1.0 / (N * asum)

    def long_conv(u, gate, order, out_dtype):
        u3 = u.reshape(half, n2, C)
        ua = dft_rows_fwd(consts["rows_fwd_half"], u3)
        bm = spectral_mid(ua.reshape(2, DFT_K1_PAD, n2, C), consts, kspec, order * C)
        y = dft_rows_inv(consts["rows_inv"], bm.reshape(2 * DFT_K1_PAD, n2, C), u3, gate.reshape(half, n2, C),
                         scale[order * C:(order + 1) * C], p['hy_d'][order].astype(jnp.float32), out_dtype)
        return y.reshape(L, C)

    y1 = long_conv(v, x1, 0, jnp.float32)
    return long_conv(y1, x2, 1, jnp.bfloat16)


def _resident(shape, index_map):
    return pl.BlockSpec(shape, index_map, pipeline_mode=pl.Buffered(1))


def _merge_kernel(y0_ref, y1_ref, y2_ref, g_ref, wb_ref, o_ref):
    d = o_ref.shape[1]
    m = None
    for b, y_ref in enumerate((y0_ref, y1_ref, y2_ref)):
        t = g_ref[:, b * d:(b + 1) * d] * jnp.dot(y_ref[...], wb_ref[b], preferred_element_type=jnp.float32)
        m = t if m is None else m + t
    o_ref[...] = m.astype(o_ref.dtype)


def merge_gated(y_hy, y_wa, y_na, gates, w_branch):
    M, wbr = y_hy.shape
    D = w_branch.shape[2]
    tm = _pick(M, (256, 128, 64, 32, 16, 8))
    yspec = pl.BlockSpec((tm, wbr), lambda i: (i, 0))
    return pl.pallas_call(
        _merge_kernel,
        grid=(M // tm,),
        in_specs=[yspec, yspec, yspec, pl.BlockSpec((tm, N_BRANCH * D), lambda i: (i, 0)),
                  _resident((N_BRANCH, wbr, D), lambda i: (0, 0, 0))],
        out_specs=pl.BlockSpec((tm, D), lambda i: (i, 0)),
        out_shape=jax.ShapeDtypeStruct((M, D), jnp.bfloat16),
        compiler_params=pltpu.CompilerParams(
            dimension_semantics=("parallel",), vmem_limit_bytes=VMEM_LIMIT_BYTES),
        name="merge_gated",
    )(y_hy, y_wa, y_na, gates, w_branch)


ROUTER_PAD = 128


def _out_proj_kernel(m_ref, w_ref, x_ref, gate_ref, g2_ref, shift_ref, scale_ref, wr_hi_ref, wr_lo_ref,
                     x_out, h_out, lg_out):
    x = x_ref[...] + gate_ref[...] * jnp.dot(m_ref[...], w_ref[...], preferred_element_type=jnp.float32)
    x_out[...] = x
    h = x * lax.rsqrt(jnp.mean(x * x, axis=-1, keepdims=True) + RMS_EPS) * g2_ref[...]
    h = h * (1.0 + scale_ref[...]) + shift_ref[...]
    h_out[...] = h.astype(h_out.dtype)
    h_hi = h.astype(jnp.bfloat16)
    h_lo = (h - h_hi.astype(jnp.float32)).astype(jnp.bfloat16)
    lg_out[...] = (jnp.dot(h_hi, wr_hi_ref[...], preferred_element_type=jnp.float32)
                   + jnp.dot(h_lo, wr_hi_ref[...], preferred_element_type=jnp.float32)
                   + jnp.dot(h_hi, wr_lo_ref[...], preferred_element_type=jnp.float32))


def out_project(m, w_out, x, gate, norm_g, shift, scale, w_router):
    M, D = x.shape
    tm = _pick(M, (256, 128, 64, 32, 16, 8))
    row = pl.BlockSpec((tm, D), lambda i: (i, 0))
    vec = pl.BlockSpec((1, D), lambda i: (0, 0))
    wr = jnp.pad(w_router.astype(jnp.float32), ((0, 0), (0, ROUTER_PAD - w_router.shape[1])))
    wr_hi = wr.astype(jnp.bfloat16)
    wr_lo = (wr - wr_hi.astype(jnp.float32)).astype(jnp.bfloat16)
    v2 = lambda a: a.reshape(1, D).astype(jnp.float32)
    return pl.pallas_call(
        _out_proj_kernel,
        grid=(M // tm,),
        in_specs=[row, _resident((D, D), lambda i: (0, 0)), row, vec, vec, vec, vec,
                  _resident((D, ROUTER_PAD), lambda i: (0, 0)), _resident((D, ROUTER_PAD), lambda i: (0, 0))],
        out_specs=[row, row, pl.BlockSpec((tm, ROUTER_PAD), lambda i: (i, 0))],
        out_shape=[jax.ShapeDtypeStruct((M, D), jnp.float32), jax.ShapeDtypeStruct((M, D), jnp.bfloat16),
                   jax.ShapeDtypeStruct((M, ROUTER_PAD), jnp.float32)],
        compiler_params=pltpu.CompilerParams(
            dimension_semantics=("parallel",), vmem_limit_bytes=VMEM_LIMIT_BYTES),
        name="out_project",
    )(m, w_out, x, v2(gate), v2(norm_g), v2(shift), v2(scale), wr_hi, wr_lo)


def _gate_up_kernel(x_ref, wg_ref, wu_ref, o_ref, wg_bf, wu_bf):
    @pl.when(pl.program_id(2) == 0)
    def _():
        wg_bf[...] = wg_ref[...].astype(jnp.bfloat16)
        wu_bf[...] = wu_ref[...].astype(jnp.bfloat16)

    x = x_ref[0]
    a = jnp.dot(x, wg_bf[...], preferred_element_type=jnp.float32)
    u = jnp.dot(x, wu_bf[...], preferred_element_type=jnp.float32)
    o_ref[0] = (a * jax.nn.sigmoid(a) * u).astype(o_ref.dtype)


def _down_kernel(h_ref, wd_ref, gsel_ref, gate_ref, o_ref, wd_bf):
    @pl.when(pl.program_id(1) == 0)
    def _():
        wd_bf[...] = wd_ref[...].astype(jnp.bfloat16)

    y = jnp.dot(h_ref[0], wd_bf[...], preferred_element_type=jnp.float32)
    o_ref[0] = (y * gsel_ref[0] * gate_ref[...]).astype(o_ref.dtype)


def expert_ffn(xe, w_gate, w_up, w_down, layer, gsel, out_gate):
    E, cap, D = xe.shape
    F = w_gate.shape[3]
    tm = _pick(cap, (512, 256, 128, 64, 32, 16, 8))
    tn = _pick(F, (1024, 512, 256, 128))
    seq = pltpu.CompilerParams(dimension_semantics=("parallel", "parallel", "arbitrary"),
                               vmem_limit_bytes=VMEM_LIMIT_BYTES)
    h = pl.pallas_call(
        _gate_up_kernel,
        grid=(E, F // tn, cap // tm),
        in_specs=[pl.BlockSpec((1, tm, D), lambda e, j, i: (e, i, 0)),
                  pl.BlockSpec((None, None, D, tn), lambda e, j, i: (layer, e, 0, j)),
                  pl.BlockSpec((None, None, D, tn), lambda e, j, i: (layer, e, 0, j))],
        out_specs=pl.BlockSpec((1, tm, tn), lambda e, j, i: (e, i, j)),
        out_shape=jax.ShapeDtypeStruct((E, cap, F), jnp.bfloat16),
        scratch_shapes=[pltpu.VMEM((D, tn), jnp.bfloat16), pltpu.VMEM((D, tn), jnp.bfloat16)],
        compiler_params=seq,
        name="expert_gate_up",
    )(xe, w_gate, w_up)
    return pl.pallas_call(
        _down_kernel,
        grid=(E, cap // tm),
        in_specs=[pl.BlockSpec((1, tm, F), lambda e, i: (e, i, 0)),
                  pl.BlockSpec((None, None, F, D), lambda e, i: (layer, e, 0, 0)),
                  pl.BlockSpec((1, tm, 1), lambda e, i: (e, i, 0)),
                  pl.BlockSpec((1, D), lambda e, i: (0, 0))],
        out_specs=pl.BlockSpec((1, tm, D), lambda e, i: (e, i, 0)),
        out_shape=jax.ShapeDtypeStruct((E, cap, D), jnp.bfloat16),
        scratch_shapes=[pltpu.VMEM((F, D), jnp.bfloat16)],
        compiler_params=pltpu.CompilerParams(dimension_semantics=("parallel", "arbitrary"),
                                             vmem_limit_bytes=VMEM_LIMIT_BYTES),
        name="expert_down",
    )(h, w_down, gsel.reshape(E, cap, 1).astype(jnp.float32), out_gate.reshape(1, D).astype(jnp.float32))


SEG_ROWS = 512
SEG_WIN = 512
SEG_LANES = 256


def _segment_add_kernel(first_ref, last_ref, tok_ref, ye_ref, x_ref, o_ref, *, n_tokens, win):
    c = pl.program_id(1)

    @pl.when(c == 0)
    def _():
        o_ref[...] = x_ref[...]

    base0 = (first_ref[c] // 8) * 8
    n_win = (last_ref[c] - base0) // win + 1
    tok = tok_ref[0]
    rows = ye_ref[...]

    def window(w, carry):
        lo = base0 + w * win
        base = pl.multiple_of(jnp.minimum(lo, n_tokens - win), 8)
        rid = base + lax.broadcasted_iota(jnp.int32, (win, tok.shape[1]), 0)
        onehot = ((rid == tok) & (tok >= lo)).astype(jnp.bfloat16)
        o_ref[pl.ds(base, win), :] += jnp.dot(onehot, rows, preferred_element_type=jnp.float32)
        return carry

    lax.fori_loop(0, n_win, window, 0)


def segment_add(x, ye_sorted, tok_sorted):
    N, D = x.shape
    P = ye_sorted.shape[0]
    rows = min(SEG_ROWS, P)
    win = min(SEG_WIN, N)
    tc = min(SEG_LANES, D)
    n_chunks = P // rows
    tok3 = tok_sorted.reshape(n_chunks, 1, rows)
    return pl.pallas_call(
        partial(_segment_add_kernel, n_tokens=N, win=win),
        grid_spec=pltpu.PrefetchScalarGridSpec(
            num_scalar_prefetch=2,
            grid=(D // tc, n_chunks),
            in_specs=[pl.BlockSpec((1, 1, rows), lambda d, c, f, l: (c, 0, 0)),
                      pl.BlockSpec((rows, tc), lambda d, c, f, l: (c, d)),
                      pl.BlockSpec((N, tc), lambda d, c, f, l: (0, d), pipeline_mode=pl.Buffered(1))],
            out_specs=pl.BlockSpec((N, tc), lambda d, c, f, l: (0, d)),
        ),
        out_shape=jax.ShapeDtypeStruct((N, D), jnp.float32),
        compiler_params=pltpu.CompilerParams(
            dimension_semantics=("parallel", "arbitrary"), vmem_limit_bytes=VMEM_LIMIT_BYTES),
        name="segment_add",
    )(tok3[:, 0, 0], tok3[:, 0, rows - 1], tok3, ye_sorted, x)


def expert_choice_latent(x1, h2, logits, out_gate, w_gate, w_up, w_down, layer):
    N, D = x1.shape
    cap = EC_CAPACITY_FACTOR * N // N_EXPERTS
    aff = jax.nn.softmax(logits[:, :N_EXPERTS], axis=-1)
    gsel, idx = lax.top_k(aff.T, cap)
    ye = expert_ffn(h2[idx], w_gate, w_up, w_down, layer, gsel, out_gate)
    tok = idx.reshape(-1)
    order = jnp.argsort(tok)
    return segment_add(x1, ye.reshape(-1, D)[order], tok[order])


def _ctx_attn_kernel(sink_ref, q_ref, k_ref, v_ref, o_ref, *, use_sink):
    s = lax.dot_general(q_ref[...], k_ref[...], _NT, preferred_element_type=jnp.float32)
    m = jnp.max(s, axis=-1, keepdims=True)
    if use_sink:
        sk = sink_ref[pl.program_id(0)]
        m = jnp.maximum(m, sk)
    p = jnp.exp(s - m)
    denom = jnp.sum(p, axis=-1, keepdims=True)
    if use_sink:
        denom = denom + jnp.exp(sk - m)
    o = jnp.dot(p.astype(v_ref.dtype), v_ref[...], preferred_element_type=jnp.float32)
    o_ref[...] = (o / denom).astype(o_ref.dtype)


def context_attention(q, q_blk, kv, k_blk, v_blk, sink, n_heads, n_kv_heads):
    Lc = q.shape[0]
    group = n_heads // n_kv_heads
    use_sink = sink is not None
    sink = jnp.zeros((n_heads,), jnp.float32) if sink is None else sink.astype(jnp.float32)
    return pl.pallas_call(
        partial(_ctx_attn_kernel, use_sink=use_sink),
        grid_spec=pltpu.PrefetchScalarGridSpec(
            num_scalar_prefetch=1,
            grid=(n_heads,),
            in_specs=[pl.BlockSpec((Lc, HEAD_DIM), lambda h, s: (0, q_blk + h)),
                      pl.BlockSpec((Lc, HEAD_DIM), lambda h, s: (0, k_blk + h // group)),
                      pl.BlockSpec((Lc, HEAD_DIM), lambda h, s: (0, v_blk + h // group))],
            out_specs=pl.BlockSpec((Lc, HEAD_DIM), lambda h, s: (0, h)),
        ),
        out_shape=jax.ShapeDtypeStruct((Lc, n_heads * HEAD_DIM), jnp.bfloat16),
        compiler_params=pltpu.CompilerParams(
            dimension_semantics=("parallel",), vmem_limit_bytes=VMEM_LIMIT_BYTES),
        name="context_attention",
    )(sink, q, kv, kv)


def trunk_layer(x, ctx, c, c_ctx, p, stacked, layer, update_ctx, consts, consts_ctx):
    L = x.shape[0]
    bf = jnp.bfloat16
    w_in = stacked['w_in']
    cond = jnp.pad(jnp.concatenate([jax.nn.silu(c), jax.nn.silu(c_ctx)[None]], axis=0), ((0, 6), (0, 0)))
    mod = matmul(cond, stacked['w_mod'], layer=layer) + p['b_mod']
    mx = jnp.split(mod[0], N_MOD)
    mc = jnp.split(mod[1], N_MOD)
    qk_scale = HEAD_DIM ** -0.5
    nwq = WA_HEADS * HEAD_DIM
    nkv = WA_KV_HEADS * HEAD_DIM
    nna = NA_HEADS * HEAD_DIM
    proj = lambda a, *args: project(a, w_in, layer, *args)

    def mix_and_ffn(res, y_hy, y_wa, y_na, gates, m_vec):
        m = merge_gated(y_hy, y_wa, y_na, gates, p['w_branch'].astype(bf))
        r1, h2, logits = out_project(m, p['w_out'].astype(bf), res, m_vec[2], p['norm2_g'], m_vec[3], m_vec[4],
                                     p['w_router'])
        return expert_choice_latent(r1, h2, logits, m_vec[5], stacked['w_gate'], stacked['w_up'],
                                    stacked['w_down'], layer)

    ones = jnp.ones((HEAD_DIM,), jnp.float32)
    wa_k_gain = jnp.stack([p['wa_k_norm']] * WA_KV_HEADS + [ones] * WA_KV_HEADS)
    na_gain = jnp.stack([p['na_q_norm'] * qk_scale] * NA_HEADS + [p['na_k_norm']] * NA_HEADS + [ones] * NA_HEADS)

    hcb = norm_modulate(ctx, p['norm1_g'], mc[0], mc[1])
    kvc_wa = proj(hcb, OFF_WA_KV, 2 * nkv, "headnorm", bf, wa_k_gain, None, nkv)
    qkvc_na = proj(hcb, OFF_NA_Q, 3 * nna, "headnorm", bf, na_gain, None, 2 * nna)
    kvc_na = qkvc_na[:, nna:]

    hxb = norm_modulate(x, p['norm1_g'], mx[0], mx[1])
    rope = rope_lane_tables(L)
    y_hy = hyena_latent(proj(hxb, OFF_HY, OFF_WA_Q - OFF_HY), p, consts)
    q_wa = proj(hxb, OFF_WA_Q, nwq, "headnorm", bf, p['wa_q_norm'] * qk_scale, rope)
    kv_wa = proj(hxb, OFF_WA_KV, 2 * nkv, "headnorm", bf, wa_k_gain, rope, nkv)
    y_wa = windowed_attention(q_wa, kv_wa, kvc_wa, p['wa_sink'])
    qkv_na = proj(hxb, OFF_NA_Q, 3 * nna, "headnorm", bf, na_gain, None, 2 * nna)
    y_na = neighbourhood_attention(qkv_na, kvc_na, p['na_rpb'])
    gates = proj(hxb, OFF_GATE, N_BRANCH * D_MODEL, "sigmoid")
    x = mix_and_ffn(x, y_hy, y_wa, y_na, gates, mx)

    if update_ctx:
        yc_hy = hyena_latent(proj(hcb, OFF_HY, OFF_WA_Q - OFF_HY), p, consts_ctx)
        qc_wa = proj(hcb, OFF_WA_Q, nwq, "headnorm", bf, p['wa_q_norm'] * qk_scale)
        yc_wa = context_attention(qc_wa, 0, kvc_wa, 0, WA_KV_HEADS, p['wa_sink'], WA_HEADS, WA_KV_HEADS)
        yc_na = context_attention(qkvc_na, 0, qkvc_na, NA_HEADS, 2 * NA_HEADS, None, NA_HEADS, NA_HEADS)
        gates_c = proj(hcb, OFF_GATE, N_BRANCH * D_MODEL, "sigmoid")
        ctx = mix_and_ffn(ctx, yc_hy, yc_wa, yc_na, gates_c, mc)
    return x, ctx


def kernel(x, c, ctx, c_ctx, w_mod, b_mod, norm1_g, w_in, hy_conv_w, hy_conv_b, hy_fw1, hy_fb1, hy_fw2, hy_fb2, hy_fw3, hy_freq, hy_decay, hy_d, wa_q_norm, wa_k_norm, wa_sink, na_q_norm, na_k_norm, na_rpb, w_branch, w_out, norm2_g, w_router, w_gate, w_up, w_down):
    consts = dft_constants(x.shape[1])
    consts_ctx = dft_constants(ctx.shape[1])
    stacked = {'w_mod': w_mod, 'w_in': w_in, 'w_gate': w_gate, 'w_up': w_up, 'w_down': w_down}
    xs, cs = x[0], ctx[0]
    for l in range(DEPTH):
        p = {
            'b_mod': b_mod[l], 'norm1_g': norm1_g[l],
            'hy_conv_w': hy_conv_w[l], 'hy_conv_b': hy_conv_b[l], 'hy_fw1': hy_fw1[l], 'hy_fb1': hy_fb1[l],
            'hy_fw2': hy_fw2[l], 'hy_fb2': hy_fb2[l], 'hy_fw3': hy_fw3[l], 'hy_freq': hy_freq[l],
            'hy_decay': hy_decay[l], 'hy_d': hy_d[l], 'wa_q_norm': wa_q_norm[l], 'wa_k_norm': wa_k_norm[l],
            'wa_sink': wa_sink[l], 'na_q_norm': na_q_norm[l], 'na_k_norm': na_k_norm[l], 'na_rpb': na_rpb[l],
            'w_branch': w_branch[l], 'w_out': w_out[l], 'norm2_g': norm2_g[l], 'w_router': w_router[l],
        }
        xs, cs = trunk_layer(xs, cs, c, c_ctx, p, stacked, l, l < DEPTH - 1, consts, consts_ctx)
    return xs[None]
```

```python
import math
from functools import partial

import jax
import jax.numpy as jnp
from jax import lax
from jax.experimental import pallas as pl
from jax.experimental.pallas import tpu as pltpu

D_MODEL = 2048
SEQ = 16384
DEPTH = 2
CTX_LEN = 256
GRID_W = 64
HEAD_DIM = 128
BRANCH_WIDTH = 1024
N_BRANCH = 3
N_MOD = 6
RMS_EPS = 1e-6
NEG_INF = -1e30

HY_WIDTH = BRANCH_WIDTH
HY_ORDER = 2
HY_DIRS = 2
HY_BANDS = 16
HY_DECAY_SHIFT = 0.05

WA_HEADS = BRANCH_WIDTH // HEAD_DIM
WA_KV_HEADS = 2
WA_WINDOW = 128
WA_BLOCK = 128

NA_HEADS = BRANCH_WIDTH // HEAD_DIM
NA_WIN_ROWS = 8
NA_WIN_COLS = 16

ROPE_BASE = 10000.0

N_EXPERTS = 16
EC_CAPACITY_FACTOR = 2
D_EXPERT = 1024

OFF_HY = 0
OFF_WA_Q = OFF_HY + 3 * HY_WIDTH
OFF_WA_KV = OFF_WA_Q + WA_HEADS * HEAD_DIM
OFF_NA_Q = OFF_WA_KV + 2 * WA_KV_HEADS * HEAD_DIM
OFF_NA_KV = OFF_NA_Q + NA_HEADS * HEAD_DIM
OFF_GATE = OFF_NA_KV + 2 * NA_HEADS * HEAD_DIM
N_IN = OFF_GATE + N_BRANCH * D_MODEL

VMEM_LIMIT_BYTES = 56 * 1024 * 1024


def _mm_kernel(a_ref, b_ref, o_ref):
    o_ref[...] = jnp.dot(a_ref[...].astype(jnp.bfloat16), b_ref[...].astype(jnp.bfloat16),
                         preferred_element_type=jnp.float32).astype(o_ref.dtype)


def _pick(n, pref):
    for t in pref:
        if n % t == 0:
            return t
    return n


def matmul(a, b, out_dtype=jnp.float32, layer=None):
    M, K = a.shape
    N = b.shape[-1]
    tm = _pick(M, (512, 256, 128, 64, 32, 16, 8))
    tn = _pick(N, (1024, 512, 256, 128))
    if layer is None:
        b_spec = pl.BlockSpec((K, tn), lambda j, i: (0, j))
    else:
        b_spec = pl.BlockSpec((None, K, tn), lambda j, i: (layer, 0, j))
    return pl.pallas_call(
        _mm_kernel,
        grid=(N // tn, M // tm),
        in_specs=[pl.BlockSpec((tm, K), lambda j, i: (i, 0)), b_spec],
        out_specs=pl.BlockSpec((tm, tn), lambda j, i: (i, j)),
        out_shape=jax.ShapeDtypeStruct((M, N), out_dtype),
        compiler_params=pltpu.CompilerParams(
            dimension_semantics=("parallel", "parallel"), vmem_limit_bytes=VMEM_LIMIT_BYTES),
        name="matmul",
    )(a, b)


def _norm_mod_kernel(x_ref, g_ref, shift_ref, scale_ref, o_ref):
    x = x_ref[...]
    y = x * lax.rsqrt(jnp.mean(x * x, axis=-1, keepdims=True) + RMS_EPS) * g_ref[...]
    o_ref[...] = (y * (1.0 + scale_ref[...]) + shift_ref[...]).astype(o_ref.dtype)


def norm_modulate(x, g, shift, scale, out_dtype=jnp.bfloat16):
    M, D = x.shape
    tm = _pick(M, (512, 256, 128, 64, 32, 16, 8))
    vec = pl.BlockSpec((1, D), lambda i: (0, 0))
    return pl.pallas_call(
        _norm_mod_kernel,
        grid=(M // tm,),
        in_specs=[pl.BlockSpec((tm, D), lambda i: (i, 0)), vec, vec, vec],
        out_specs=pl.BlockSpec((tm, D), lambda i: (i, 0)),
        out_shape=jax.ShapeDtypeStruct((M, D), out_dtype),
        compiler_params=pltpu.CompilerParams(
            dimension_semantics=("parallel",), vmem_limit_bytes=VMEM_LIMIT_BYTES),
        name="norm_modulate",
    )(x, g.reshape(1, D), shift.reshape(1, D), scale.reshape(1, D))


def _swap_halves(x):
    lane = lax.broadcasted_iota(jnp.int32, x.shape, 1)
    return jnp.where((lane % 64) < 32, pltpu.roll(x, 96, 1), pltpu.roll(x, 32, 1))


HEADNORM_ROWS = 256
HEADNORM_COLS = 256


def _sigmoid(x):
    return 0.5 * (jnp.tanh(0.5 * x) + 1.0)


def _proj_kernel(*refs, mode, rope, norm_cols, n_tiles):
    if mode == "headnorm":
        if rope:
            a_ref, w_ref, gain_ref, cos_ref, sin_ref, o_ref, w_bf = refs
        else:
            a_ref, w_ref, gain_ref, o_ref, w_bf = refs
    else:
        a_ref, w_ref, o_ref, w_bf = refs

    @pl.when(pl.program_id(1) == 0)
    def _():
        w_bf[...] = w_ref[...].astype(jnp.bfloat16)

    if mode == "plain":
        o_ref[...] = jnp.dot(a_ref[...], w_bf[...], preferred_element_type=jnp.float32).astype(o_ref.dtype)
        return
    if mode == "sigmoid":
        acc = jnp.dot(a_ref[...], w_bf[...], preferred_element_type=jnp.float32)
        o_ref[...] = _sigmoid(acc).astype(o_ref.dtype)
        return

    tm, tn = o_ref.shape
    rows = min(tm, HEADNORM_ROWS) if rope else tm
    width = min(tn, HEADNORM_COLS) if rope else tn

    def epilogue(norm_heads):
        def chunk(r, carry):
            r0 = pl.multiple_of(r * rows, rows)
            a = a_ref[pl.ds(r0, rows), :]
            for c0 in range(0, tn, width):
                acc = jnp.dot(a, w_bf[:, c0:c0 + width], preferred_element_type=jnp.float32)
                for h in range(width // HEAD_DIM):
                    head = c0 // HEAD_DIM + h
                    y = acc[:, h * HEAD_DIM:(h + 1) * HEAD_DIM]
                    if head < norm_heads:
                        y = y * lax.rsqrt(jnp.mean(y * y, axis=-1, keepdims=True) + RMS_EPS) * gain_ref[head:head + 1, :]
                        if rope:
                            y = (y * cos_ref[pl.ds(r0, rows), :]
                                 + _swap_halves(y) * sin_ref[pl.ds(r0, rows), :])
                    o_ref[pl.ds(r0, rows), c0 + h * HEAD_DIM:c0 + (h + 1) * HEAD_DIM] = y.astype(o_ref.dtype)
            return carry

        lax.fori_loop(0, tm // rows, chunk, 0)

    heads = tn // HEAD_DIM
    if n_tiles == 1 or norm_cols >= n_tiles * tn:
        epilogue(min(heads, norm_cols // HEAD_DIM))
    else:
        norm_tiles = norm_cols // tn
        j = pl.program_id(0)
        pl.when(j < norm_tiles)(lambda: epilogue(heads))
        pl.when(j >= norm_tiles)(lambda: epilogue(0))


def project(a, w, layer, col_off, n_cols, mode="plain", out_dtype=jnp.float32, gain=None, rope=None,
            norm_cols=None):
    M, K = a.shape
    tm = _pick(M, (1024, 512, 256, 128, 64, 32, 16, 8))
    norm_cols = n_cols if norm_cols is None else norm_cols
    tn = next(t for t in (1024, 768, 512, 256, 128)
              if n_cols % t == 0 and col_off % t == 0 and (norm_cols % t == 0 or t == n_cols))
    off = col_off // tn
    n_tiles = n_cols // tn
    in_specs = [pl.BlockSpec((tm, K), lambda j, i: (i, 0)),
                pl.BlockSpec((None, K, tn), lambda j, i: (layer, 0, off + j))]
    args = [a, w]
    if mode == "headnorm":
        assert n_tiles == 1 or norm_cols % tn == 0
        heads = tn // HEAD_DIM
        gain = jnp.broadcast_to(gain.astype(jnp.float32).reshape(-1, HEAD_DIM), (n_cols // HEAD_DIM, HEAD_DIM))
        in_specs.append(pl.BlockSpec((None, heads, HEAD_DIM), lambda j, i: (j, 0, 0)))
        args.append(gain.reshape(n_tiles, heads, HEAD_DIM))
        if rope is not None:
            in_specs += [pl.BlockSpec((tm, HEAD_DIM), lambda j, i: (i, 0))] * 2
            args += list(rope)
    return pl.pallas_call(
        partial(_proj_kernel, mode=mode, rope=rope is not None, norm_cols=norm_cols, n_tiles=n_tiles),
        grid=(n_tiles, M // tm),
        in_specs=in_specs,
        out_specs=pl.BlockSpec((tm, tn), lambda j, i: (i, j)),
        out_shape=jax.ShapeDtypeStruct((M, n_cols), out_dtype),
        scratch_shapes=[pltpu.VMEM((K, tn), jnp.bfloat16)],
        compiler_params=pltpu.CompilerParams(
            dimension_semantics=("parallel", "arbitrary"), vmem_limit_bytes=VMEM_LIMIT_BYTES),
        name="project_" + mode,
    )(*args)


def rope_lane_tables(L):
    t = jnp.arange(L, dtype=jnp.int32)
    row = (t // GRID_W).astype(jnp.float32)
    col = (t % GRID_W).astype(jnp.float32)
    nf = HEAD_DIM // 4
    inv = ROPE_BASE ** (-jnp.arange(nf, dtype=jnp.float32) / nf)
    ar, ac = row[:, None] * inv, col[:, None] * inv
    cos = jnp.concatenate([jnp.cos(ar), jnp.cos(ar), jnp.cos(ac), jnp.cos(ac)], axis=-1)
    sin = jnp.concatenate([-jnp.sin(ar), jnp.sin(ar), -jnp.sin(ac), jnp.sin(ac)], axis=-1)
    return cos, sin


_NT = (((1,), (1,)), ((), ()))


def _wa_kernel(sink_ref, q_ref, k_ref, v_ref, kc_ref, vc_ref, o_ref, *, tq, seq):
    g = pl.program_id(0)
    i = pl.program_id(1)
    nwin = tq + 2 * WA_WINDOW
    ws = jnp.clip(i * tq - WA_WINDOW, 0, seq - nwin)
    start = pl.multiple_of(ws, WA_WINDOW)
    kwin = k_ref[pl.ds(start, nwin), :]
    vwin = v_ref[pl.ds(start, nwin), :]
    qpos = i * tq + lax.broadcasted_iota(jnp.int32, (tq, nwin), 0)
    kpos = ws + lax.broadcasted_iota(jnp.int32, (tq, nwin), 1)
    valid = jnp.abs(qpos - kpos) <= WA_WINDOW
    group = WA_HEADS // WA_KV_HEADS
    for hh in range(group):
        q = q_ref[:, hh * HEAD_DIM:(hh + 1) * HEAD_DIM]
        s = jnp.where(valid, lax.dot_general(q, kwin, _NT, preferred_element_type=jnp.float32), NEG_INF)
        sc = lax.dot_general(q, kc_ref[...], _NT, preferred_element_type=jnp.float32)
        sk = sink_ref[g * group + hh]
        m = jnp.maximum(jnp.maximum(jnp.max(s, axis=-1, keepdims=True), jnp.max(sc, axis=-1, keepdims=True)), sk)
        p = jnp.exp(s - m)
        pc = jnp.exp(sc - m)
        denom = jnp.sum(p, axis=-1, keepdims=True) + jnp.sum(pc, axis=-1, keepdims=True) + jnp.exp(sk - m)
        o = (jnp.dot(p.astype(vwin.dtype), vwin, preferred_element_type=jnp.float32)
             + jnp.dot(pc.astype(vwin.dtype), vc_ref[...], preferred_element_type=jnp.float32))
        o_ref[:, hh * HEAD_DIM:(hh + 1) * HEAD_DIM] = (o / denom).astype(o_ref.dtype)


def windowed_attention(q, kv, kvc, sink, tq=256):
    L = q.shape[0]
    Lc = kvc.shape[0]
    gw = (WA_HEADS // WA_KV_HEADS) * HEAD_DIM
    k_slab = pl.BlockSpec((L, HEAD_DIM), lambda g, i, s: (0, g))
    v_slab = pl.BlockSpec((L, HEAD_DIM), lambda g, i, s: (0, WA_KV_HEADS + g))
    kc_slab = pl.BlockSpec((Lc, HEAD_DIM), lambda g, i, s: (0, g))
    vc_slab = pl.BlockSpec((Lc, HEAD_DIM), lambda g, i, s: (0, WA_KV_HEADS + g))
    return pl.pallas_call(
        partial(_wa_kernel, tq=tq, seq=L),
        grid_spec=pltpu.PrefetchScalarGridSpec(
            num_scalar_prefetch=1,
            grid=(WA_KV_HEADS, L // tq),
            in_specs=[pl.BlockSpec((tq, gw), lambda g, i, s: (i, g)), k_slab, v_slab, kc_slab, vc_slab],
            out_specs=pl.BlockSpec((tq, gw), lambda g, i, s: (i, g)),
        ),
        out_shape=jax.ShapeDtypeStruct((L, WA_HEADS * HEAD_DIM), jnp.bfloat16),
        compiler_params=pltpu.CompilerParams(
            dimension_semantics=("parallel", "parallel"), vmem_limit_bytes=VMEM_LIMIT_BYTES),
        name="windowed_attention",
    )(sink.astype(jnp.float32), q, kv, kv, kvc, kvc)


NA_ROW_BLOCK = 4
NA_KEY_ROWS = NA_ROW_BLOCK + NA_WIN_ROWS - 1


NA_HEAD_GROUP = 4


def _na_kernel(q_ref, k_ref, v_ref, kc_ref, vc_ref, bias_ref, o_ref, *, rows):
    i = pl.program_id(1)
    ws = jnp.clip(i * NA_ROW_BLOCK - NA_WIN_ROWS // 2, 0, rows - NA_KEY_ROWS)
    start = pl.multiple_of(ws * GRID_W, GRID_W)
    nk = NA_KEY_ROWS * GRID_W
    for hh in range(NA_HEAD_GROUP):
        cols = slice(hh * HEAD_DIM, (hh + 1) * HEAD_DIM)
        kwin = k_ref[pl.ds(start, nk), cols]
        vwin = v_ref[pl.ds(start, nk), cols]
        q = q_ref[:, cols]
        s = lax.dot_general(q, kwin, _NT, preferred_element_type=jnp.float32) + bias_ref[0, hh]
        sc = lax.dot_general(q, kc_ref[:, cols], _NT, preferred_element_type=jnp.float32)
        m = jnp.maximum(jnp.max(s, axis=-1, keepdims=True), jnp.max(sc, axis=-1, keepdims=True))
        p = jnp.exp(s - m)
        pc = jnp.exp(sc - m)
        denom = jnp.sum(p, axis=-1, keepdims=True) + jnp.sum(pc, axis=-1, keepdims=True)
        o = (jnp.dot(p.astype(vwin.dtype), vwin, preferred_element_type=jnp.float32)
             + jnp.dot(pc.astype(vwin.dtype), vc_ref[:, cols], preferred_element_type=jnp.float32))
        o_ref[:, cols] = (o / denom).astype(o_ref.dtype)


def na_bias_tiles(rpb, rows):
    col = jnp.arange(GRID_W)
    cstart = jnp.clip(col - NA_WIN_COLS // 2, 0, GRID_W - NA_WIN_COLS)
    col_in = (col[None, :] >= cstart[:, None]) & (col[None, :] < cstart[:, None] + NA_WIN_COLS)
    dc_idx = jnp.clip(col[None, :] - col[:, None] + NA_WIN_COLS - 1, 0, 2 * NA_WIN_COLS - 2)
    exact = lax.Precision.HIGHEST
    by_col = jnp.einsum('hrd,qkd->hrqk', rpb.astype(jnp.float32),
                        jax.nn.one_hot(dc_idx, 2 * NA_WIN_COLS - 1, dtype=jnp.float32), precision=exact)
    tiles = []
    for blk in (0, 1, rows // NA_ROW_BLOCK - 1):
        r = blk * NA_ROW_BLOCK
        ws = min(max(r - NA_WIN_ROWS // 2, 0), rows - NA_KEY_ROWS)
        qr = r + jnp.arange(NA_ROW_BLOCK)
        kr = ws + jnp.arange(NA_KEY_ROWS)
        r0 = jnp.clip(qr - NA_WIN_ROWS // 2, 0, rows - NA_WIN_ROWS)
        row_in = (kr[None, :] >= r0[:, None]) & (kr[None, :] < r0[:, None] + NA_WIN_ROWS)
        dr_idx = jnp.clip(kr[None, :] - qr[:, None] + NA_WIN_ROWS - 1, 0, 2 * NA_WIN_ROWS - 2)
        b = jnp.einsum('hrqk,abr->haqbk', by_col,
                       jax.nn.one_hot(dr_idx, 2 * NA_WIN_ROWS - 1, dtype=jnp.float32), precision=exact)
        ok = row_in[:, None, :, None] & col_in[None, :, None, :]
        b = jnp.where(ok[None], b, NEG_INF)
        tiles.append(b.reshape(rpb.shape[0], NA_ROW_BLOCK * GRID_W, NA_KEY_ROWS * GRID_W))
    return jnp.stack(tiles)


def neighbourhood_attention(qkv, kvc, rpb):
    L = qkv.shape[0]
    Lc = kvc.shape[0]
    rows = L // GRID_W
    nblk = rows // NA_ROW_BLOCK
    tq = NA_ROW_BLOCK * GRID_W
    nk = NA_KEY_ROWS * GRID_W
    bias = na_bias_tiles(rpb, rows)
    gw = NA_HEAD_GROUP * HEAD_DIM
    ng = NA_HEADS // NA_HEAD_GROUP
    slab = lambda part: pl.BlockSpec((L, gw), lambda h, i: (0, part * ng + h), pipeline_mode=pl.Buffered(1))
    cslab = lambda part: pl.BlockSpec((Lc, gw), lambda h, i: (0, part * ng + h))
    variant = lambda h, i: (jnp.where(i == 0, 0, jnp.where(i == nblk - 1, 2, 1)), h, 0, 0)
    return pl.pallas_call(
        partial(_na_kernel, rows=rows),
        grid=(ng, nblk),
        in_specs=[pl.BlockSpec((tq, gw), lambda h, i: (i, h)), slab(1), slab(2), cslab(0), cslab(1),
                  pl.BlockSpec((1, NA_HEAD_GROUP, tq, nk), variant)],
        out_specs=pl.BlockSpec((tq, gw), lambda h, i: (i, h)),
        out_shape=jax.ShapeDtypeStruct((L, NA_HEADS * HEAD_DIM), jnp.bfloat16),
        compiler_params=pltpu.CompilerParams(
            dimension_semantics=("parallel", "parallel"), vmem_limit_bytes=VMEM_LIMIT_BYTES),
        name="neighbourhood_attention",
    )(qkv, qkv, qkv, kvc, kvc, bias)


DFT_N1 = 128
DFT_K1 = DFT_N1 // 2 + 1
DFT_K1_PAD = 72


def dft_constants(L):
    N = 2 * L
    N2 = N // DFT_N1
    f32, bf = jnp.float32, jnp.bfloat16
    k1 = jnp.arange(DFT_K1, dtype=jnp.int32)
    n1 = jnp.arange(DFT_N1, dtype=jnp.int32)
    th = ((k1[:, None] * n1[None, :]) % DFT_N1).astype(f32) * (2.0 * math.pi / DFT_N1)
    pad = ((0, DFT_K1_PAD - DFT_K1), (0, 0))
    rows_fwd = jnp.concatenate([jnp.pad(jnp.cos(th), pad), jnp.pad(-jnp.sin(th), pad)], axis=0)
    ck = jnp.where((k1 == 0) | (k1 == DFT_N1 // 2), 1.0, 2.0)[:, None]
    half = DFT_N1 // 2
    rows_inv = jnp.concatenate([jnp.pad(ck * jnp.cos(th[:, :half]), pad),
                                jnp.pad(-ck * jnp.sin(th[:, :half]), pad)], axis=0).T
    n2 = jnp.arange(N2, dtype=jnp.int32)
    alpha = ((k1[:, None] * n2[None, :]) % N).astype(f32) * (2.0 * math.pi / N)
    beta = ((n2[:, None] * n2[None, :]) % N2).astype(f32) * (2.0 * math.pi / N2)
    twr, twi = jnp.cos(alpha)[:, None, :], -jnp.sin(alpha)[:, None, :]
    fr, fi = jnp.cos(beta)[None], -jnp.sin(beta)[None]
    gr, gi = twr * fr - twi * fi, twr * fi + twi * fr
    mid_fwd = jnp.concatenate([jnp.concatenate([gr, -gi], axis=2),
                               jnp.concatenate([gi, gr], axis=2)], axis=1)
    return {
        "rows_fwd": rows_fwd.astype(bf), "rows_fwd_half": rows_fwd[:, :half].astype(bf),
        "rows_inv": rows_inv.astype(bf),
        "mid_fwd": mid_fwd.astype(bf),
    }


def _spectral_fwd_kernel(a_ref, g_ref, o_ref):
    n2 = a_ref.shape[2]
    x = (jnp.dot(g_ref[0, :, :n2], a_ref[0, 0], preferred_element_type=jnp.float32)
         + jnp.dot(g_ref[0, :, n2:], a_ref[1, 0], preferred_element_type=jnp.float32))
    o_ref[0, 0] = x[:n2].astype(o_ref.dtype)
    o_ref[1, 0] = x[n2:].astype(o_ref.dtype)


def spectral_fwd(a, mid_fwd, cb=2048):
    _, _, n2, C = a.shape
    cb = min(cb, C)
    return pl.pallas_call(
        _spectral_fwd_kernel,
        grid=(DFT_K1, C // cb),
        in_specs=[pl.BlockSpec((2, 1, n2, cb), lambda k, c: (0, k, 0, c)),
                  pl.BlockSpec((1, 2 * n2, 2 * n2), lambda k, c: (k, 0, 0))],
        out_specs=pl.BlockSpec((2, 1, n2, cb), lambda k, c: (0, k, 0, c)),
        out_shape=jax.ShapeDtypeStruct((2, DFT_K1, n2, C), jnp.bfloat16),
        compiler_params=pltpu.CompilerParams(
            dimension_semantics=("parallel", "parallel"), vmem_limit_bytes=VMEM_LIMIT_BYTES),
        name="spectral_fwd",
    )(a, mid_fwd)


_TN = (((0,), (0,)), ((), ()))


def _spectral_mid_kernel(a_ref, gf_ref, ks_ref, o_ref):
    n2 = a_ref.shape[2]
    k1 = pl.program_id(0)

    @pl.when(k1 < DFT_K1)
    def _():
        x = (jnp.dot(gf_ref[0, :, :n2], a_ref[0, 0], preferred_element_type=jnp.float32)
             + jnp.dot(gf_ref[0, :, n2:], a_ref[1, 0], preferred_element_type=jnp.float32))
        xr, xi = x[:n2], x[n2:]
        kr, ki = ks_ref[0, 0].astype(jnp.float32), ks_ref[1, 0].astype(jnp.float32)
        yr = (xr * kr - xi * ki).astype(jnp.bfloat16)
        yi = (xr * ki + xi * kr).astype(jnp.bfloat16)
        b = (lax.dot_general(gf_ref[0, :n2, :], yr, _TN, preferred_element_type=jnp.float32)
             + lax.dot_general(gf_ref[0, n2:, :], yi, _TN, preferred_element_type=jnp.float32))
        o_ref[0, 0] = b[:n2].astype(o_ref.dtype)
        o_ref[1, 0] = b[n2:].astype(o_ref.dtype)

    @pl.when(k1 >= DFT_K1)
    def _():
        o_ref[...] = jnp.zeros_like(o_ref)


def spectral_mid(a, consts, kspec, col_off, cb=1024):
    _, _, n2, C = a.shape
    cb = min(cb, C)
    off = col_off // cb
    kc = lambda k: jnp.minimum(k, DFT_K1 - 1)
    return pl.pallas_call(
        _spectral_mid_kernel,
        grid=(DFT_K1_PAD, C // cb),
        in_specs=[pl.BlockSpec((2, 1, n2, cb), lambda k, c: (0, k, 0, c)),
                  pl.BlockSpec((1, 2 * n2, 2 * n2), lambda k, c: (kc(k), 0, 0)),
                  pl.BlockSpec((2, 1, n2, cb), lambda k, c: (0, kc(k), 0, off + c))],
        out_specs=pl.BlockSpec((2, 1, n2, cb), lambda k, c: (0, k, 0, c)),
        out_shape=jax.ShapeDtypeStruct(a.shape, jnp.bfloat16),
        compiler_params=pltpu.CompilerParams(
            dimension_semantics=("parallel", "parallel"), vmem_limit_bytes=VMEM_LIMIT_BYTES),
        name="spectral_mid",
    )(a, consts["mid_fwd"], kspec)


DFT_ROW_GROUP = 16
DFT_ROW_LANES = 256


def _rows_fwd_kernel(f_ref, u_ref, o_ref):
    x = pltpu.einshape("abc->bac", u_ref[...].astype(jnp.float32))
    rs = [jnp.dot(f_ref[...], x[s].astype(jnp.bfloat16), preferred_element_type=jnp.float32)
          for s in range(x.shape[0])]
    o_ref[...] = pltpu.einshape("abc->bac", jnp.stack(rs)).astype(o_ref.dtype)


def dft_rows_fwd(rows_mat, u3):
    kn, n2, C = u3.shape
    grp = min(DFT_ROW_GROUP, n2)
    tc = min(DFT_ROW_LANES, C)
    m = rows_mat.shape[0]
    return pl.pallas_call(
        _rows_fwd_kernel,
        grid=(n2 // grp, C // tc),
        in_specs=[pl.BlockSpec((m, kn), lambda j, c: (0, 0)),
                  pl.BlockSpec((kn, grp, tc), lambda j, c: (0, j, c))],
        out_specs=pl.BlockSpec((m, grp, tc), lambda j, c: (0, j, c)),
        out_shape=jax.ShapeDtypeStruct((m, n2, C), jnp.bfloat16),
        compiler_params=pltpu.CompilerParams(
            dimension_semantics=("parallel", "parallel"), vmem_limit_bytes=VMEM_LIMIT_BYTES),
        name="dft_rows_fwd",
    )(rows_mat, u3)


def _rows_inv_kernel(f_ref, b_ref, u_ref, gate_ref, scale_ref, d_ref, o_ref):
    x = pltpu.einshape("abc->bac", b_ref[...].astype(jnp.float32))
    ys = [jnp.dot(f_ref[...], x[s].astype(jnp.bfloat16), preferred_element_type=jnp.float32)
          for s in range(x.shape[0])]
    y = pltpu.einshape("abc->bac", jnp.stack(ys))
    o_ref[...] = (gate_ref[...] * (y * scale_ref[...] + u_ref[...] * d_ref[...])).astype(o_ref.dtype)


def dft_rows_inv(rows_inv, b3, u3, gate3, scale, d_skip, out_dtype):
    nr, n2, C = u3.shape
    grp = min(DFT_ROW_GROUP, n2)
    tc = min(DFT_ROW_LANES, C)
    blk = pl.BlockSpec((nr, grp, tc), lambda j, c: (0, j, c))
    vec = pl.BlockSpec((1, 1, tc), lambda j, c: (0, 0, c))
    return pl.pallas_call(
        _rows_inv_kernel,
        grid=(n2 // grp, C // tc),
        in_specs=[pl.BlockSpec(rows_inv.shape, lambda j, c: (0, 0)),
                  pl.BlockSpec((b3.shape[0], grp, tc), lambda j, c: (0, j, c)), blk, blk, vec, vec],
        out_specs=blk,
        out_shape=jax.ShapeDtypeStruct((nr, n2, C), out_dtype),
        compiler_params=pltpu.CompilerParams(
            dimension_semantics=("parallel", "parallel"), vmem_limit_bytes=VMEM_LIMIT_BYTES),
        name="dft_rows_inv",
    )(rows_inv, b3, u3, gate3, scale.reshape(1, 1, C), d_skip.reshape(1, 1, C))


def _conv3_kernel(z_ref, prev_ref, next_ref, w_ref, b_ref, o_ref):
    i = pl.program_id(0)
    z = z_ref[...]
    tm = z.shape[0]
    row = lax.broadcasted_iota(jnp.int32, z.shape, 0)
    prev_row = jnp.where(i > 0, prev_ref[7:8, :], 0.0)
    next_row = jnp.where(i < pl.num_programs(0) - 1, next_ref[0:1, :], 0.0)
    zp = jnp.where(row == 0, prev_row, pltpu.roll(z, 1, 0))
    zn = jnp.where(row == tm - 1, next_row, pltpu.roll(z, tm - 1, 0))
    o_ref[...] = zp * w_ref[0:1, :] + z * w_ref[1:2, :] + zn * w_ref[2:3, :] + b_ref[...]


def short_conv3_part(z, w, b, part, width):
    L = z.shape[0]
    tm = _pick(L, (512, 256, 128, 64, 32, 16, 8))
    tc = _pick(width, (1024, 512, 256, 128))
    off = part * width // tc
    halo = 8
    return pl.pallas_call(
        _conv3_kernel,
        grid=(L // tm, width // tc),
        in_specs=[pl.BlockSpec((tm, tc), lambda i, j: (i, off + j)),
                  pl.BlockSpec((halo, tc), lambda i, j: (jnp.maximum(i * (tm // halo) - 1, 0), off + j)),
                  pl.BlockSpec((halo, tc), lambda i, j: (jnp.minimum((i + 1) * (tm // halo), L // halo - 1), off + j)),
                  pl.BlockSpec((3, tc), lambda i, j: (0, off + j)),
                  pl.BlockSpec((1, tc), lambda i, j: (0, off + j))],
        out_specs=pl.BlockSpec((tm, tc), lambda i, j: (i, j)),
        out_shape=jax.ShapeDtypeStruct((L, width), jnp.float32),
        compiler_params=pltpu.CompilerParams(
            dimension_semantics=("parallel", "parallel"), vmem_limit_bytes=VMEM_LIMIT_BYTES),
        name="short_conv3",
    )(z, z, z, w, b.reshape(1, -1))


FILTER_PAD = 128
FILTER_LANES = 512


def _filter_rows_kernel(feats_ref, feats0_ref, fw1_ref, fb1_ref, fw2_ref, fb2_ref, fr_ref, fw3_ref, rate_ref,
                        rows_ref, a_ref, asum_ref, hid_ref, hid0_ref):
    g = pl.program_id(0)
    exact = lax.Precision.HIGHEST
    n_dir, prow, nf2 = feats_ref.shape
    hid = fw3_ref.shape[1]
    half = rows_ref.shape[1] // 2
    grp = 2 * prow // half

    def hidden(feats):
        h = jnp.sin(fr_ref[0:1, :] * (jnp.dot(feats, fw1_ref[...], precision=exact,
                                              preferred_element_type=jnp.float32) + fb1_ref[...]))
        return jnp.sin(fr_ref[1:2, :] * (jnp.dot(h, fw2_ref[...], precision=exact,
                                                 preferred_element_type=jnp.float32) + fb2_ref[...]))

    @pl.when(pl.program_id(1) == 0)
    def _():
        for d in range(n_dir):
            hid_ref[d] = hidden(feats_ref[d])
        hid0_ref[...] = hidden(feats0_ref[...])

    def taps(h, t, d):
        k = jnp.dot(h.astype(jnp.bfloat16), fw3_ref[d].astype(jnp.bfloat16), preferred_element_type=jnp.float32)
        return k * (jnp.exp(-t * rate_ref[d]) + HY_DECAY_SHIFT)

    def direction(d):
        h2, f = hid_ref[d], feats_ref[d]
        return jnp.concatenate([taps(h2[:, :hid], f[:, 0:1], d),
                                taps(h2[:, hid:], f[:, nf2 // 2:nf2 // 2 + 1], d)], axis=0)

    k_fwd = direction(0)
    k_bwd = direction(1)
    lag0_back = taps(hid0_ref[:, :hid], feats0_ref[:, 0:1], 1)[0:1, :]
    first = (lax.broadcasted_iota(jnp.int32, k_fwd.shape, 0) == 0) & (g == 0)
    k_fwd = k_fwd + jnp.where(first, lag0_back, 0.0)
    k_bwd = jnp.where(first, 0.0, k_bwd)
    asum_ref[0] = jnp.sum(jnp.abs(k_fwd), axis=0, keepdims=True) + jnp.sum(jnp.abs(k_bwd), axis=0, keepdims=True)
    kf = k_fwd.astype(jnp.bfloat16)
    kb = k_bwd.astype(jnp.bfloat16)
    rs = [jnp.dot(rows_ref[:, :half], kf[s * half:(s + 1) * half], preferred_element_type=jnp.float32)
          + jnp.dot(rows_ref[:, half:], kb[s * half:(s + 1) * half], preferred_element_type=jnp.float32)
          for s in range(grp)]
    a_ref[...] = pltpu.einshape("abc->bac", jnp.stack(rs)).astype(a_ref.dtype)


def _block_diag2(w):
    z = jnp.zeros_like(w)
    return jnp.concatenate([jnp.concatenate([w, z], axis=1), jnp.concatenate([z, w], axis=1)], axis=0)


def hyena_filter_rows(L, p, rows_fwd):
    f32 = jnp.float32
    N = 2 * L
    n2 = N // DFT_N1
    half = DFT_N1 // 2
    C2 = HY_ORDER * HY_WIDTH
    grp = min(DFT_ROW_GROUP, n2)
    tc = min(FILTER_LANES, C2)
    prow = grp * half // 2
    n = (n2 * jnp.arange(DFT_N1, dtype=jnp.int32)[None, :] + jnp.arange(n2, dtype=jnp.int32)[:, None])
    pos = jnp.where(n < L, n, N - n).astype(f32)[..., None]
    bands = jnp.linspace(1e-4, HY_BANDS - 1, HY_BANDS, dtype=f32)
    ang = (2.0 * math.pi / L) * bands * pos
    feats = jnp.concatenate([pos / (L - 1), jnp.cos(ang), -jnp.sin(ang)], axis=-1)
    feats = jnp.pad(feats, ((0, 0), (0, 0), (0, FILTER_PAD - feats.shape[-1])))
    feats = feats.reshape(n2 // grp, grp, 2, half, FILTER_PAD).transpose(2, 0, 1, 3, 4)
    feats = feats.reshape(2, n2 // grp, 2, prow, FILTER_PAD).transpose(0, 1, 3, 2, 4)
    feats = feats.reshape(2, (n2 // grp) * prow, 2 * FILTER_PAD)
    feats0 = jnp.broadcast_to(feats[0, 0:1], (8, 2 * FILTER_PAD))
    hid = p['hy_fw2'].shape[0]
    fw1 = _block_diag2(jnp.pad(p['hy_fw1'].astype(f32), ((0, FILTER_PAD - p['hy_fw1'].shape[0]), (0, 0))))
    fb1 = jnp.tile(p['hy_fb1'].astype(f32), 2).reshape(1, -1)
    fw2 = _block_diag2(p['hy_fw2'].astype(f32))
    fb2 = jnp.tile(p['hy_fb2'].astype(f32), 2).reshape(1, -1)
    fr = jnp.tile(p['hy_freq'].astype(f32), (1, 2))
    fw3 = p['hy_fw3'].astype(f32).reshape(hid, HY_DIRS, C2).transpose(1, 0, 2)
    rate = jnp.abs(p['hy_decay'].astype(f32)).reshape(HY_DIRS, 1, C2)
    full = lambda shape: pl.BlockSpec(shape, lambda g, c: (0,) * len(shape))
    m = rows_fwd.shape[0]
    a, asum = pl.pallas_call(
        _filter_rows_kernel,
        grid=(n2 // grp, C2 // tc),
        in_specs=[pl.BlockSpec((2, prow, 2 * FILTER_PAD), lambda g, c: (0, g, 0)), full((8, 2 * FILTER_PAD)),
                  full((2 * FILTER_PAD, 2 * hid)), full((1, 2 * hid)),
                  full((2 * hid, 2 * hid)), full((1, 2 * hid)), full((2, 2 * hid)),
                  pl.BlockSpec((HY_DIRS, hid, tc), lambda g, c: (0, 0, c)),
                  pl.BlockSpec((HY_DIRS, 1, tc), lambda g, c: (0, 0, c)),
                  full(rows_fwd.shape)],
        out_specs=[pl.BlockSpec((m, grp, tc), lambda g, c: (0, g, c)),
                   pl.BlockSpec((1, 1, tc), lambda g, c: (g, 0, c))],
        out_shape=[jax.ShapeDtypeStruct((m, n2, C2), jnp.bfloat16),
                   jax.ShapeDtypeStruct((n2 // grp, 1, C2), f32)],
        scratch_shapes=[pltpu.VMEM((2, prow, 2 * hid), f32), pltpu.VMEM((8, 2 * hid), f32)],
        compiler_params=pltpu.CompilerParams(
            dimension_semantics=("parallel", "arbitrary"), vmem_limit_bytes=VMEM_LIMIT_BYTES),
        name="hyena_filter_rows",
    )(feats, feats0, fw1, fb1, fw2, fb2, fr, fw3, rate, rows_fwd)
    return a, jnp.sum(asum, axis=(0, 1))


def hyena_latent(z, p, consts):
    L = z.shape[0]
    C = HY_WIDTH
    N = 2 * L
    n2 = N // DFT_N1
    half = DFT_N1 // 2
    v, x1, x2 = (short_conv3_part(z, p['hy_conv_w'], p['hy_conv_b'], i, C) for i in range(3))
    ka, asum = hyena_filter_rows(L, p, consts["rows_fwd"])
    kspec = spectral_fwd(ka.reshape(2, DFT_K1_PAD, n2, HY_ORDER * C), consts["mid_fwd"])
    scale = 1.0 / (N * asum)

    def long_conv(u, gate, order, out_dtype):
        u3 = u.reshape(half, n2, C)
        ua = dft_rows_fwd(consts["rows_fwd_half"], u3)
        bm = spectral_mid(ua.reshape(2, DFT_K1_PAD, n2, C), consts, kspec, order * C)
        y = dft_rows_inv(consts["rows_inv"], bm.reshape(2 * DFT_K1_PAD, n2, C), u3, gate.reshape(half, n2, C),
                         scale[order * C:(order + 1) * C], p['hy_d'][order].astype(jnp.float32), out_dtype)
        return y.reshape(L, C)

    y1 = long_conv(v, x1, 0, jnp.float32)
    return long_conv(y1, x2, 1, jnp.bfloat16)


def _resident(shape, index_map):
    return pl.BlockSpec(shape, index_map, pipeline_mode=pl.Buffered(1))


def _merge_kernel(y0_ref, y1_ref, y2_ref, g_ref, wb_ref, o_ref):
    d = o_ref.shape[1]
    m = None
    for b, y_ref in enumerate((y0_ref, y1_ref, y2_ref)):
        t = g_ref[:, b * d:(b + 1) * d] * jnp.dot(y_ref[...], wb_ref[b], preferred_element_type=jnp.float32)
        m = t if m is None else m + t
    o_ref[...] = m.astype(o_ref.dtype)


def merge_gated(y_hy, y_wa, y_na, gates, w_branch):
    M, wbr = y_hy.shape
    D = w_branch.shape[2]
    tm = _pick(M, (256, 128, 64, 32, 16, 8))
    yspec = pl.BlockSpec((tm, wbr), lambda i: (i, 0))
    return pl.pallas_call(
        _merge_kernel,
        grid=(M // tm,),
        in_specs=[yspec, yspec, yspec, pl.BlockSpec((tm, N_BRANCH * D), lambda i: (i, 0)),
                  _resident((N_BRANCH, wbr, D), lambda i: (0, 0, 0))],
        out_specs=pl.BlockSpec((tm, D), lambda i: (i, 0)),
        out_shape=jax.ShapeDtypeStruct((M, D), jnp.bfloat16),
        compiler_params=pltpu.CompilerParams(
            dimension_semantics=("parallel",), vmem_limit_bytes=VMEM_LIMIT_BYTES),
        name="merge_gated",
    )(y_hy, y_wa, y_na, gates, w_branch)


ROUTER_PAD = 128


def _out_proj_kernel(m_ref, w_ref, x_ref, gate_ref, g2_ref, shift_ref, scale_ref, wr_hi_ref, wr_lo_ref,
                     x_out, h_out, lg_out):
    x = x_ref[...] + gate_ref[...] * jnp.dot(m_ref[...], w_ref[...], preferred_element_type=jnp.float32)
    x_out[...] = x
    h = x * lax.rsqrt(jnp.mean(x * x, axis=-1, keepdims=True) + RMS_EPS) * g2_ref[...]
    h = h * (1.0 + scale_ref[...]) + shift_ref[...]
    h_out[...] = h.astype(h_out.dtype)
    h_hi = h.astype(jnp.bfloat16)
    h_lo = (h - h_hi.astype(jnp.float32)).astype(jnp.bfloat16)
    lg_out[...] = (jnp.dot(h_hi, wr_hi_ref[...], preferred_element_type=jnp.float32)
                   + jnp.dot(h_lo, wr_hi_ref[...], preferred_element_type=jnp.float32)
                   + jnp.dot(h_hi, wr_lo_ref[...], preferred_element_type=jnp.float32))


def out_project(m, w_out, x, gate, norm_g, shift, scale, w_router):
    M, D = x.shape
    tm = _pick(M, (256, 128, 64, 32, 16, 8))
    row = pl.BlockSpec((tm, D), lambda i: (i, 0))
    vec = pl.BlockSpec((1, D), lambda i: (0, 0))
    wr = jnp.pad(w_router.astype(jnp.float32), ((0, 0), (0, ROUTER_PAD - w_router.shape[1])))
    wr_hi = wr.astype(jnp.bfloat16)
    wr_lo = (wr - wr_hi.astype(jnp.float32)).astype(jnp.bfloat16)
    v2 = lambda a: a.reshape(1, D).astype(jnp.float32)
    return pl.pallas_call(
        _out_proj_kernel,
        grid=(M // tm,),
        in_specs=[row, _resident((D, D), lambda i: (0, 0)), row, vec, vec, vec, vec,
                  _resident((D, ROUTER_PAD), lambda i: (0, 0)), _resident((D, ROUTER_PAD), lambda i: (0, 0))],
        out_specs=[row, row, pl.BlockSpec((tm, ROUTER_PAD), lambda i: (i, 0))],
        out_shape=[jax.ShapeDtypeStruct((M, D), jnp.float32), jax.ShapeDtypeStruct((M, D), jnp.bfloat16),
                   jax.ShapeDtypeStruct((M, ROUTER_PAD), jnp.float32)],
        compiler_params=pltpu.CompilerParams(
            dimension_semantics=("parallel",), vmem_limit_bytes=VMEM_LIMIT_BYTES),
        name="out_project",
    )(m, w_out, x, v2(gate), v2(norm_g), v2(shift), v2(scale), wr_hi, wr_lo)


def _gate_up_kernel(x_ref, wg_ref, wu_ref, o_ref, wg_bf, wu_bf):
    @pl.when(pl.program_id(2) == 0)
    def _():
        wg_bf[...] = wg_ref[...].astype(jnp.bfloat16)
        wu_bf[...] = wu_ref[...].astype(jnp.bfloat16)

    x = x_ref[0]
    a = jnp.dot(x, wg_bf[...], preferred_element_type=jnp.float32)
    u = jnp.dot(x, wu_bf[...], preferred_element_type=jnp.float32)
    o_ref[0] = (a * _sigmoid(a) * u).astype(o_ref.dtype)


def _down_kernel(h_ref, wd_ref, gsel_ref, gate_ref, o_ref, wd_bf):
    @pl.when(pl.program_id(1) == 0)
    def _():
        wd_bf[...] = wd_ref[...].astype(jnp.bfloat16)

    y = jnp.dot(h_ref[0], wd_bf[...], preferred_element_type=jnp.float32)
    o_ref[0] = (y * gsel_ref[0] * gate_ref[...]).astype(o_ref.dtype)


def expert_ffn(xe, w_gate, w_up, w_down, layer, gsel, out_gate):
    E, cap, D = xe.shape
    F = w_gate.shape[3]
    tm = _pick(cap, (512, 256, 128, 64, 32, 16, 8))
    tn = _pick(F, (1024, 512, 256, 128))
    seq = pltpu.CompilerParams(dimension_semantics=("parallel", "parallel", "arbitrary"),
                               vmem_limit_bytes=VMEM_LIMIT_BYTES)
    h = pl.pallas_call(
        _gate_up_kernel,
        grid=(E, F // tn, cap // tm),
        in_specs=[pl.BlockSpec((1, tm, D), lambda e, j, i: (e, i, 0)),
                  pl.BlockSpec((None, None, D, tn), lambda e, j, i: (layer, e, 0, j)),
                  pl.BlockSpec((None, None, D, tn), lambda e, j, i: (layer, e, 0, j))],
        out_specs=pl.BlockSpec((1, tm, tn), lambda e, j, i: (e, i, j)),
        out_shape=jax.ShapeDtypeStruct((E, cap, F), jnp.bfloat16),
        scratch_shapes=[pltpu.VMEM((D, tn), jnp.bfloat16), pltpu.VMEM((D, tn), jnp.bfloat16)],
        compiler_params=seq,
        name="expert_gate_up",
    )(xe, w_gate, w_up)
    td = _pick(cap, (1024, 512, 256, 128, 64, 32, 16, 8))
    return pl.pallas_call(
        _down_kernel,
        grid=(E, cap // td),
        in_specs=[pl.BlockSpec((1, td, F), lambda e, i: (e, i, 0)),
                  pl.BlockSpec((None, None, F, D), lambda e, i: (layer, e, 0, 0)),
                  pl.BlockSpec((1, td, 1), lambda e, i: (e, i, 0)),
                  pl.BlockSpec((1, D), lambda e, i: (0, 0))],
        out_specs=pl.BlockSpec((1, td, D), lambda e, i: (e, i, 0)),
        out_shape=jax.ShapeDtypeStruct((E, cap, D), jnp.bfloat16),
        scratch_shapes=[pltpu.VMEM((F, D), jnp.bfloat16)],
        compiler_params=pltpu.CompilerParams(dimension_semantics=("parallel", "arbitrary"),
                                             vmem_limit_bytes=VMEM_LIMIT_BYTES),
        name="expert_down",
    )(h, w_down, gsel.reshape(E, cap, 1).astype(jnp.float32), out_gate.reshape(1, D).astype(jnp.float32))


SEG_ROWS = 512
SEG_WIN = 512
SEG_LANES = 256


def _segment_add_kernel(first_ref, last_ref, tok_ref, ye_ref, x_ref, o_ref, *, n_tokens, win):
    c = pl.program_id(1)

    @pl.when(c == 0)
    def _():
        o_ref[...] = x_ref[...]

    base0 = (first_ref[c] // 8) * 8
    n_win = (last_ref[c] - base0) // win + 1
    tok = tok_ref[0]
    rows = ye_ref[...]

    def window(w, carry):
        lo = base0 + w * win
        base = pl.multiple_of(jnp.minimum(lo, n_tokens - win), 8)
        rid = base + lax.broadcasted_iota(jnp.int32, (win, tok.shape[1]), 0)
        onehot = ((rid == tok) & (tok >= lo)).astype(jnp.bfloat16)
        o_ref[pl.ds(base, win), :] += jnp.dot(onehot, rows, preferred_element_type=jnp.float32)
        return carry

    lax.fori_loop(0, n_win, window, 0)


def segment_add(x, ye_sorted, tok_sorted):
    N, D = x.shape
    P = ye_sorted.shape[0]
    rows = min(SEG_ROWS, P)
    win = min(SEG_WIN, N)
    tc = min(SEG_LANES, D)
    n_chunks = P // rows
    tok3 = tok_sorted.reshape(n_chunks, 1, rows)
    return pl.pallas_call(
        partial(_segment_add_kernel, n_tokens=N, win=win),
        grid_spec=pltpu.PrefetchScalarGridSpec(
            num_scalar_prefetch=2,
            grid=(D // tc, n_chunks),
            in_specs=[pl.BlockSpec((1, 1, rows), lambda d, c, f, l: (c, 0, 0)),
                      pl.BlockSpec((rows, tc), lambda d, c, f, l: (c, d)),
                      pl.BlockSpec((N, tc), lambda d, c, f, l: (0, d), pipeline_mode=pl.Buffered(1))],
            out_specs=pl.BlockSpec((N, tc), lambda d, c, f, l: (0, d)),
        ),
        out_shape=jax.ShapeDtypeStruct((N, D), jnp.float32),
        compiler_params=pltpu.CompilerParams(
            dimension_semantics=("parallel", "arbitrary"), vmem_limit_bytes=VMEM_LIMIT_BYTES),
        name="segment_add",
    )(tok3[:, 0, 0], tok3[:, 0, rows - 1], tok3, ye_sorted, x)


def expert_choice_latent(x1, h2, logits, out_gate, w_gate, w_up, w_down, layer):
    N, D = x1.shape
    cap = EC_CAPACITY_FACTOR * N // N_EXPERTS
    aff = jax.nn.softmax(logits[:, :N_EXPERTS], axis=-1)
    gsel, idx = lax.top_k(aff.T, cap)
    ye = expert_ffn(h2[idx], w_gate, w_up, w_down, layer, gsel, out_gate)
    tok = idx.reshape(-1)
    order = jnp.argsort(tok)
    return segment_add(x1, ye.reshape(-1, D)[order], tok[order])


def _ctx_attn_kernel(sink_ref, q_ref, k_ref, v_ref, o_ref, *, use_sink):
    s = lax.dot_general(q_ref[...], k_ref[...], _NT, preferred_element_type=jnp.float32)
    m = jnp.max(s, axis=-1, keepdims=True)
    if use_sink:
        sk = sink_ref[pl.program_id(0)]
        m = jnp.maximum(m, sk)
    p = jnp.exp(s - m)
    denom = jnp.sum(p, axis=-1, keepdims=True)
    if use_sink:
        denom = denom + jnp.exp(sk - m)
    o = jnp.dot(p.astype(v_ref.dtype), v_ref[...], preferred_element_type=jnp.float32)
    o_ref[...] = (o / denom).astype(o_ref.dtype)


def context_attention(q, q_blk, kv, k_blk, v_blk, sink, n_heads, n_kv_heads):
    Lc = q.shape[0]
    group = n_heads // n_kv_heads
    use_sink = sink is not None
    sink = jnp.zeros((n_heads,), jnp.float32) if sink is None else sink.astype(jnp.float32)
    return pl.pallas_call(
        partial(_ctx_attn_kernel, use_sink=use_sink),
        grid_spec=pltpu.PrefetchScalarGridSpec(
            num_scalar_prefetch=1,
            grid=(n_heads,),
            in_specs=[pl.BlockSpec((Lc, HEAD_DIM), lambda h, s: (0, q_blk + h)),
                      pl.BlockSpec((Lc, HEAD_DIM), lambda h, s: (0, k_blk + h // group)),
                      pl.BlockSpec((Lc, HEAD_DIM), lambda h, s: (0, v_blk + h // group))],
            out_specs=pl.BlockSpec((Lc, HEAD_DIM), lambda h, s: (0, h)),
        ),
        out_shape=jax.ShapeDtypeStruct((Lc, n_heads * HEAD_DIM), jnp.bfloat16),
        compiler_params=pltpu.CompilerParams(
            dimension_semantics=("parallel",), vmem_limit_bytes=VMEM_LIMIT_BYTES),
        name="context_attention",
    )(sink, q, kv, kv)


def trunk_layer(x, ctx, c, c_ctx, p, stacked, layer, update_ctx, consts, consts_ctx):
    L = x.shape[0]
    bf = jnp.bfloat16
    w_in = stacked['w_in']
    cond = jnp.pad(jnp.concatenate([jax.nn.silu(c), jax.nn.silu(c_ctx)[None]], axis=0), ((0, 6), (0, 0)))
    mod = matmul(cond, stacked['w_mod'], layer=layer) + p['b_mod']
    mx = jnp.split(mod[0], N_MOD)
    mc = jnp.split(mod[1], N_MOD)
    qk_scale = HEAD_DIM ** -0.5
    nwq = WA_HEADS * HEAD_DIM
    nkv = WA_KV_HEADS * HEAD_DIM
    nna = NA_HEADS * HEAD_DIM
    proj = lambda a, *args: project(a, w_in, layer, *args)

    def mix_and_ffn(res, y_hy, y_wa, y_na, gates, m_vec):
        m = merge_gated(y_hy, y_wa, y_na, gates, p['w_branch'].astype(bf))
        r1, h2, logits = out_project(m, p['w_out'].astype(bf), res, m_vec[2], p['norm2_g'], m_vec[3], m_vec[4],
                                     p['w_router'])
        return expert_choice_latent(r1, h2, logits, m_vec[5], stacked['w_gate'], stacked['w_up'],
                                    stacked['w_down'], layer)

    ones = jnp.ones((HEAD_DIM,), jnp.float32)
    wa_k_gain = jnp.stack([p['wa_k_norm']] * WA_KV_HEADS + [ones] * WA_KV_HEADS)
    na_gain = jnp.stack([p['na_q_norm'] * qk_scale] * NA_HEADS + [p['na_k_norm']] * NA_HEADS + [ones] * NA_HEADS)

    hcb = norm_modulate(ctx, p['norm1_g'], mc[0], mc[1])
    kvc_wa = proj(hcb, OFF_WA_KV, 2 * nkv, "headnorm", bf, wa_k_gain, None, nkv)
    qkvc_na = proj(hcb, OFF_NA_Q, 3 * nna, "headnorm", bf, na_gain, None, 2 * nna)
    kvc_na = qkvc_na[:, nna:]

    hxb = norm_modulate(x, p['norm1_g'], mx[0], mx[1])
    rope = rope_lane_tables(L)
    y_hy = hyena_latent(proj(hxb, OFF_HY, OFF_WA_Q - OFF_HY), p, consts)
    q_wa = proj(hxb, OFF_WA_Q, nwq, "headnorm", bf, p['wa_q_norm'] * qk_scale, rope)
    kv_wa = proj(hxb, OFF_WA_KV, 2 * nkv, "headnorm", bf, wa_k_gain, rope, nkv)
    y_wa = windowed_attention(q_wa, kv_wa, kvc_wa, p['wa_sink'])
    qkv_na = proj(hxb, OFF_NA_Q, 3 * nna, "headnorm", bf, na_gain, None, 2 * nna)
    y_na = neighbourhood_attention(qkv_na, kvc_na, p['na_rpb'])
    gates = proj(hxb, OFF_GATE, N_BRANCH * D_MODEL, "sigmoid")
    x = mix_and_ffn(x, y_hy, y_wa, y_na, gates, mx)

    if update_ctx:
        yc_hy = hyena_latent(proj(hcb, OFF_HY, OFF_WA_Q - OFF_HY), p, consts_ctx)
        qc_wa = proj(hcb, OFF_WA_Q, nwq, "headnorm", bf, p['wa_q_norm'] * qk_scale)
        yc_wa = context_attention(qc_wa, 0, kvc_wa, 0, WA_KV_HEADS, p['wa_sink'], WA_HEADS, WA_KV_HEADS)
        yc_na = context_attention(qkvc_na, 0, qkvc_na, NA_HEADS, 2 * NA_HEADS, None, NA_HEADS, NA_HEADS)
        gates_c = proj(hcb, OFF_GATE, N_BRANCH * D_MODEL, "sigmoid")
        ctx = mix_and_ffn(ctx, yc_hy, yc_wa, yc_na, gates_c, mc)
    return x, ctx


def kernel(x, c, ctx, c_ctx, w_mod, b_mod, norm1_g, w_in, hy_conv_w, hy_conv_b, hy_fw1, hy_fb1, hy_fw2, hy_fb2, hy_fw3, hy_freq, hy_decay, hy_d, wa_q_norm, wa_k_norm, wa_sink, na_q_norm, na_k_norm, na_rpb, w_branch, w_out, norm2_g, w_router, w_gate, w_up, w_down):
    consts = dft_constants(x.shape[1])
    consts_ctx = dft_constants(ctx.shape[1])
    stacked = {'w_mod': w_mod, 'w_in': w_in, 'w_gate': w_gate, 'w_up': w_up, 'w_down': w_down}
    xs, cs = x[0], ctx[0]
    for l in range(DEPTH):
        p = {
            'b_mod': b_mod[l], 'norm1_g': norm1_g[l],
            'hy_conv_w': hy_conv_w[l], 'hy_conv_b': hy_conv_b[l], 'hy_fw1': hy_fw1[l], 'hy_fb1': hy_fb1[l],
            'hy_fw2': hy_fw2[l], 'hy_fb2': hy_fb2[l], 'hy_fw3': hy_fw3[l], 'hy_freq': hy_freq[l],
            'hy_decay': hy_decay[l], 'hy_d': hy_d[l], 'wa_q_norm': wa_q_norm[l], 'wa_k_norm': wa_k_norm[l],
            'wa_sink': wa_sink[l], 'na_q_norm': na_q_norm[l], 'na_k_norm': na_k_norm[l], 'na_rpb': na_rpb[l],
            'w_branch': w_branch[l], 'w_out': w_out[l], 'norm2_g': norm2_g[l], 'w_router': w_router[l],
        }
        xs, cs = trunk_layer(xs, cs, c, c_ctx, p, stacked, l, l < DEPTH - 1, consts, consts_ctx)
    return xs[None]
```

```python
import math
from functools import partial

import jax
import jax.numpy as jnp
from jax import lax
from jax.experimental import pallas as pl
from jax.experimental.pallas import tpu as pltpu

D_MODEL = 2048
SEQ = 16384
DEPTH = 2
CTX_LEN = 256
GRID_W = 64
HEAD_DIM = 128
BRANCH_WIDTH = 1024
N_BRANCH = 3
N_MOD = 6
RMS_EPS = 1e-6
NEG_INF = -1e30

HY_WIDTH = BRANCH_WIDTH
HY_ORDER = 2
HY_DIRS = 2
HY_BANDS = 16
HY_DECAY_SHIFT = 0.05

WA_HEADS = BRANCH_WIDTH // HEAD_DIM
WA_KV_HEADS = 2
WA_WINDOW = 128
WA_BLOCK = 128

NA_HEADS = BRANCH_WIDTH // HEAD_DIM
NA_WIN_ROWS = 8
NA_WIN_COLS = 16

ROPE_BASE = 10000.0

N_EXPERTS = 16
EC_CAPACITY_FACTOR = 2
D_EXPERT = 1024

OFF_HY = 0
OFF_WA_Q = OFF_HY + 3 * HY_WIDTH
OFF_WA_KV = OFF_WA_Q + WA_HEADS * HEAD_DIM
OFF_NA_Q = OFF_WA_KV + 2 * WA_KV_HEADS * HEAD_DIM
OFF_NA_KV = OFF_NA_Q + NA_HEADS * HEAD_DIM
OFF_GATE = OFF_NA_KV + 2 * NA_HEADS * HEAD_DIM
N_IN = OFF_GATE + N_BRANCH * D_MODEL

VMEM_LIMIT_BYTES = 56 * 1024 * 1024


def _mm_kernel(a_ref, b_ref, o_ref):
    o_ref[...] = jnp.dot(a_ref[...].astype(jnp.bfloat16), b_ref[...].astype(jnp.bfloat16),
                         preferred_element_type=jnp.float32).astype(o_ref.dtype)


def _pick(n, pref):
    for t in pref:
        if n % t == 0:
            return t
    return n


def matmul(a, b, out_dtype=jnp.float32, layer=None):
    M, K = a.shape
    N = b.shape[-1]
    tm = _pick(M, (512, 256, 128, 64, 32, 16, 8))
    tn = _pick(N, (1024, 512, 256, 128))
    if layer is None:
        b_spec = pl.BlockSpec((K, tn), lambda j, i: (0, j))
    else:
        b_spec = pl.BlockSpec((None, K, tn), lambda j, i: (layer, 0, j))
    return pl.pallas_call(
        _mm_kernel,
        grid=(N // tn, M // tm),
        in_specs=[pl.BlockSpec((tm, K), lambda j, i: (i, 0)), b_spec],
        out_specs=pl.BlockSpec((tm, tn), lambda j, i: (i, j)),
        out_shape=jax.ShapeDtypeStruct((M, N), out_dtype),
        compiler_params=pltpu.CompilerParams(
            dimension_semantics=("parallel", "parallel"), vmem_limit_bytes=VMEM_LIMIT_BYTES),
        name="matmul",
    )(a, b)


def _norm_mod_kernel(x_ref, g_ref, shift_ref, scale_ref, o_ref):
    x = x_ref[...]
    y = x * lax.rsqrt(jnp.mean(x * x, axis=-1, keepdims=True) + RMS_EPS) * g_ref[...]
    o_ref[...] = (y * (1.0 + scale_ref[...]) + shift_ref[...]).astype(o_ref.dtype)


def norm_modulate(x, g, shift, scale, out_dtype=jnp.bfloat16):
    M, D = x.shape
    tm = _pick(M, (512, 256, 128, 64, 32, 16, 8))
    vec = pl.BlockSpec((1, D), lambda i: (0, 0))
    return pl.pallas_call(
        _norm_mod_kernel,
        grid=(M // tm,),
        in_specs=[pl.BlockSpec((tm, D), lambda i: (i, 0)), vec, vec, vec],
        out_specs=pl.BlockSpec((tm, D), lambda i: (i, 0)),
        out_shape=jax.ShapeDtypeStruct((M, D), out_dtype),
        compiler_params=pltpu.CompilerParams(
            dimension_semantics=("parallel",), vmem_limit_bytes=VMEM_LIMIT_BYTES),
        name="norm_modulate",
    )(x, g.reshape(1, D), shift.reshape(1, D), scale.reshape(1, D))


def _swap_halves(x):
    lane = lax.broadcasted_iota(jnp.int32, x.shape, 1)
    return jnp.where((lane % 64) < 32, pltpu.roll(x, 96, 1), pltpu.roll(x, 32, 1))


HEADNORM_ROWS = 256
HEADNORM_COLS = 256


def _sigmoid(x):
    return 0.5 * (jnp.tanh(0.5 * x) + 1.0)


def _proj_kernel(*refs, mode, rope, norm_cols, n_tiles):
    if mode == "headnorm":
        if rope:
            a_ref, w_ref, gain_ref, cos_ref, sin_ref, o_ref, w_bf = refs
        else:
            a_ref, w_ref, gain_ref, o_ref, w_bf = refs
    else:
        a_ref, w_ref, o_ref, w_bf = refs

    @pl.when(pl.program_id(1) == 0)
    def _():
        w_bf[...] = w_ref[...].astype(jnp.bfloat16)

    if mode == "plain":
        o_ref[...] = jnp.dot(a_ref[...], w_bf[...], preferred_element_type=jnp.float32).astype(o_ref.dtype)
        return
    if mode == "sigmoid":
        acc = jnp.dot(a_ref[...], w_bf[...], preferred_element_type=jnp.float32)
        o_ref[...] = _sigmoid(acc).astype(o_ref.dtype)
        return

    tm, tn = o_ref.shape
    rows = min(tm, HEADNORM_ROWS) if rope else tm
    width = min(tn, HEADNORM_COLS) if rope else tn

    def epilogue(norm_heads):
        def chunk(r, carry):
            r0 = pl.multiple_of(r * rows, rows)
            a = a_ref[pl.ds(r0, rows), :]
            for c0 in range(0, tn, width):
                acc = jnp.dot(a, w_bf[:, c0:c0 + width], preferred_element_type=jnp.float32)
                for h in range(width // HEAD_DIM):
                    head = c0 // HEAD_DIM + h
                    y = acc[:, h * HEAD_DIM:(h + 1) * HEAD_DIM]
                    if head < norm_heads:
                        y = y * lax.rsqrt(jnp.mean(y * y, axis=-1, keepdims=True) + RMS_EPS) * gain_ref[head:head + 1, :]
                        if rope:
                            y = (y * cos_ref[pl.ds(r0, rows), :]
                                 + _swap_halves(y) * sin_ref[pl.ds(r0, rows), :])
                    o_ref[pl.ds(r0, rows), c0 + h * HEAD_DIM:c0 + (h + 1) * HEAD_DIM] = y.astype(o_ref.dtype)
            return carry

        lax.fori_loop(0, tm // rows, chunk, 0)

    heads = tn // HEAD_DIM
    if n_tiles == 1 or norm_cols >= n_tiles * tn:
        epilogue(min(heads, norm_cols // HEAD_DIM))
    else:
        norm_tiles = norm_cols // tn
        j = pl.program_id(0)
        pl.when(j < norm_tiles)(lambda: epilogue(heads))
        pl.when(j >= norm_tiles)(lambda: epilogue(0))


def project(a, w, layer, col_off, n_cols, mode="plain", out_dtype=jnp.float32, gain=None, rope=None,
            norm_cols=None):
    M, K = a.shape
    tm = _pick(M, (1024, 512, 256, 128, 64, 32, 16, 8))
    norm_cols = n_cols if norm_cols is None else norm_cols
    tn = next(t for t in (1024, 768, 512, 256, 128)
              if n_cols % t == 0 and col_off % t == 0 and (norm_cols % t == 0 or t == n_cols))
    off = col_off // tn
    n_tiles = n_cols // tn
    in_specs = [pl.BlockSpec((tm, K), lambda j, i: (i, 0)),
                pl.BlockSpec((None, K, tn), lambda j, i: (layer, 0, off + j))]
    args = [a, w]
    if mode == "headnorm":
        assert n_tiles == 1 or norm_cols % tn == 0
        heads = tn // HEAD_DIM
        gain = jnp.broadcast_to(gain.astype(jnp.float32).reshape(-1, HEAD_DIM), (n_cols // HEAD_DIM, HEAD_DIM))
        in_specs.append(pl.BlockSpec((None, heads, HEAD_DIM), lambda j, i: (j, 0, 0)))
        args.append(gain.reshape(n_tiles, heads, HEAD_DIM))
        if rope is not None:
            in_specs += [pl.BlockSpec((tm, HEAD_DIM), lambda j, i: (i, 0))] * 2
            args += list(rope)
    return pl.pallas_call(
        partial(_proj_kernel, mode=mode, rope=rope is not None, norm_cols=norm_cols, n_tiles=n_tiles),
        grid=(n_tiles, M // tm),
        in_specs=in_specs,
        out_specs=pl.BlockSpec((tm, tn), lambda j, i: (i, j)),
        out_shape=jax.ShapeDtypeStruct((M, n_cols), out_dtype),
        scratch_shapes=[pltpu.VMEM((K, tn), jnp.bfloat16)],
        compiler_params=pltpu.CompilerParams(
            dimension_semantics=("parallel", "arbitrary"), vmem_limit_bytes=VMEM_LIMIT_BYTES),
        name="project_" + mode,
    )(*args)


def rope_lane_tables(L):
    t = jnp.arange(L, dtype=jnp.int32)
    row = (t // GRID_W).astype(jnp.float32)
    col = (t % GRID_W).astype(jnp.float32)
    nf = HEAD_DIM // 4
    inv = ROPE_BASE ** (-jnp.arange(nf, dtype=jnp.float32) / nf)
    ar, ac = row[:, None] * inv, col[:, None] * inv
    cos = jnp.concatenate([jnp.cos(ar), jnp.cos(ar), jnp.cos(ac), jnp.cos(ac)], axis=-1)
    sin = jnp.concatenate([-jnp.sin(ar), jnp.sin(ar), -jnp.sin(ac), jnp.sin(ac)], axis=-1)
    return cos, sin


_NT = (((1,), (1,)), ((), ()))


def _wa_kernel(sink_ref, q_ref, k_ref, v_ref, kc_ref, vc_ref, o_ref, *, tq, seq):
    g = pl.program_id(0)
    i = pl.program_id(1)
    nwin = tq + 2 * WA_WINDOW
    ws = jnp.clip(i * tq - WA_WINDOW, 0, seq - nwin)
    start = pl.multiple_of(ws, WA_WINDOW)
    kwin = k_ref[pl.ds(start, nwin), :]
    vwin = v_ref[pl.ds(start, nwin), :]
    qpos = i * tq + lax.broadcasted_iota(jnp.int32, (tq, nwin), 0)
    kpos = ws + lax.broadcasted_iota(jnp.int32, (tq, nwin), 1)
    valid = jnp.abs(qpos - kpos) <= WA_WINDOW
    group = WA_HEADS // WA_KV_HEADS
    for hh in range(group):
        q = q_ref[:, hh * HEAD_DIM:(hh + 1) * HEAD_DIM]
        s = jnp.where(valid, lax.dot_general(q, kwin, _NT, preferred_element_type=jnp.float32), NEG_INF)
        sc = lax.dot_general(q, kc_ref[...], _NT, preferred_element_type=jnp.float32)
        sk = sink_ref[g * group + hh]
        m = jnp.maximum(jnp.maximum(jnp.max(s, axis=-1, keepdims=True), jnp.max(sc, axis=-1, keepdims=True)), sk)
        p = jnp.exp(s - m)
        pc = jnp.exp(sc - m)
        denom = jnp.sum(p, axis=-1, keepdims=True) + jnp.sum(pc, axis=-1, keepdims=True) + jnp.exp(sk - m)
        o = (jnp.dot(p.astype(vwin.dtype), vwin, preferred_element_type=jnp.float32)
             + jnp.dot(pc.astype(vwin.dtype), vc_ref[...], preferred_element_type=jnp.float32))
        o_ref[:, hh * HEAD_DIM:(hh + 1) * HEAD_DIM] = (o / denom).astype(o_ref.dtype)


def windowed_attention(q, kv, kvc, sink, tq=256):
    L = q.shape[0]
    Lc = kvc.shape[0]
    gw = (WA_HEADS // WA_KV_HEADS) * HEAD_DIM
    k_slab = pl.BlockSpec((L, HEAD_DIM), lambda g, i, s: (0, g))
    v_slab = pl.BlockSpec((L, HEAD_DIM), lambda g, i, s: (0, WA_KV_HEADS + g))
    kc_slab = pl.BlockSpec((Lc, HEAD_DIM), lambda g, i, s: (0, g))
    vc_slab = pl.BlockSpec((Lc, HEAD_DIM), lambda g, i, s: (0, WA_KV_HEADS + g))
    return pl.pallas_call(
        partial(_wa_kernel, tq=tq, seq=L),
        grid_spec=pltpu.PrefetchScalarGridSpec(
            num_scalar_prefetch=1,
            grid=(WA_KV_HEADS, L // tq),
            in_specs=[pl.BlockSpec((tq, gw), lambda g, i, s: (i, g)), k_slab, v_slab, kc_slab, vc_slab],
            out_specs=pl.BlockSpec((tq, gw), lambda g, i, s: (i, g)),
        ),
        out_shape=jax.ShapeDtypeStruct((L, WA_HEADS * HEAD_DIM), jnp.bfloat16),
        compiler_params=pltpu.CompilerParams(
            dimension_semantics=("parallel", "parallel"), vmem_limit_bytes=VMEM_LIMIT_BYTES),
        name="windowed_attention",
    )(sink.astype(jnp.float32), q, kv, kv, kvc, kvc)


NA_ROW_BLOCK = 4
NA_KEY_ROWS = NA_ROW_BLOCK + NA_WIN_ROWS - 1


NA_HEAD_GROUP = 4


def _na_kernel(q_ref, k_ref, v_ref, kc_ref, vc_ref, bias_ref, o_ref, *, rows):
    i = pl.program_id(1)
    ws = jnp.clip(i * NA_ROW_BLOCK - NA_WIN_ROWS // 2, 0, rows - NA_KEY_ROWS)
    start = pl.multiple_of(ws * GRID_W, GRID_W)
    nk = NA_KEY_ROWS * GRID_W
    for hh in range(NA_HEAD_GROUP):
        cols = slice(hh * HEAD_DIM, (hh + 1) * HEAD_DIM)
        kwin = k_ref[pl.ds(start, nk), cols]
        vwin = v_ref[pl.ds(start, nk), cols]
        q = q_ref[:, cols]
        s = lax.dot_general(q, kwin, _NT, preferred_element_type=jnp.float32) + bias_ref[0, hh]
        sc = lax.dot_general(q, kc_ref[:, cols], _NT, preferred_element_type=jnp.float32)
        m = jnp.maximum(jnp.max(s, axis=-1, keepdims=True), jnp.max(sc, axis=-1, keepdims=True))
        p = jnp.exp(s - m)
        pc = jnp.exp(sc - m)
        denom = jnp.sum(p, axis=-1, keepdims=True) + jnp.sum(pc, axis=-1, keepdims=True)
        o = (jnp.dot(p.astype(vwin.dtype), vwin, preferred_element_type=jnp.float32)
             + jnp.dot(pc.astype(vwin.dtype), vc_ref[:, cols], preferred_element_type=jnp.float32))
        o_ref[:, cols] = (o / denom).astype(o_ref.dtype)


def na_bias_tiles(rpb, rows):
    col = jnp.arange(GRID_W)
    cstart = jnp.clip(col - NA_WIN_COLS // 2, 0, GRID_W - NA_WIN_COLS)
    col_in = (col[None, :] >= cstart[:, None]) & (col[None, :] < cstart[:, None] + NA_WIN_COLS)
    dc_idx = jnp.clip(col[None, :] - col[:, None] + NA_WIN_COLS - 1, 0, 2 * NA_WIN_COLS - 2)
    exact = lax.Precision.HIGHEST
    by_col = jnp.einsum('hrd,qkd->hrqk', rpb.astype(jnp.float32),
                        jax.nn.one_hot(dc_idx, 2 * NA_WIN_COLS - 1, dtype=jnp.float32), precision=exact)
    tiles = []
    for blk in (0, 1, rows // NA_ROW_BLOCK - 1):
        r = blk * NA_ROW_BLOCK
        ws = min(max(r - NA_WIN_ROWS // 2, 0), rows - NA_KEY_ROWS)
        qr = r + jnp.arange(NA_ROW_BLOCK)
        kr = ws + jnp.arange(NA_KEY_ROWS)
        r0 = jnp.clip(qr - NA_WIN_ROWS // 2, 0, rows - NA_WIN_ROWS)
        row_in = (kr[None, :] >= r0[:, None]) & (kr[None, :] < r0[:, None] + NA_WIN_ROWS)
        dr_idx = jnp.clip(kr[None, :] - qr[:, None] + NA_WIN_ROWS - 1, 0, 2 * NA_WIN_ROWS - 2)
        b = jnp.einsum('hrqk,abr->haqbk', by_col,
                       jax.nn.one_hot(dr_idx, 2 * NA_WIN_ROWS - 1, dtype=jnp.float32), precision=exact)
        ok = row_in[:, None, :, None] & col_in[None, :, None, :]
        b = jnp.where(ok[None], b, NEG_INF)
        tiles.append(b.reshape(rpb.shape[0], NA_ROW_BLOCK * GRID_W, NA_KEY_ROWS * GRID_W))
    return jnp.stack(tiles)


def neighbourhood_attention(qkv, kvc, rpb):
    L = qkv.shape[0]
    Lc = kvc.shape[0]
    rows = L // GRID_W
    nblk = rows // NA_ROW_BLOCK
    tq = NA_ROW_BLOCK * GRID_W
    nk = NA_KEY_ROWS * GRID_W
    bias = na_bias_tiles(rpb, rows)
    gw = NA_HEAD_GROUP * HEAD_DIM
    ng = NA_HEADS // NA_HEAD_GROUP
    slab = lambda part: pl.BlockSpec((L, gw), lambda h, i: (0, part * ng + h), pipeline_mode=pl.Buffered(1))
    cslab = lambda part: pl.BlockSpec((Lc, gw), lambda h, i: (0, part * ng + h))
    variant = lambda h, i: (jnp.where(i == 0, 0, jnp.where(i == nblk - 1, 2, 1)), h, 0, 0)
    return pl.pallas_call(
        partial(_na_kernel, rows=rows),
        grid=(ng, nblk),
        in_specs=[pl.BlockSpec((tq, gw), lambda h, i: (i, h)), slab(1), slab(2), cslab(0), cslab(1),
                  pl.BlockSpec((1, NA_HEAD_GROUP, tq, nk), variant)],
        out_specs=pl.BlockSpec((tq, gw), lambda h, i: (i, h)),
        out_shape=jax.ShapeDtypeStruct((L, NA_HEADS * HEAD_DIM), jnp.bfloat16),
        compiler_params=pltpu.CompilerParams(
            dimension_semantics=("parallel", "parallel"), vmem_limit_bytes=VMEM_LIMIT_BYTES),
        name="neighbourhood_attention",
    )(qkv, qkv, qkv, kvc, kvc, bias)


DFT_N1 = 128
DFT_K1 = DFT_N1 // 2 + 1
DFT_K1_PAD = 72


def dft_constants(L):
    N = 2 * L
    N2 = N // DFT_N1
    f32, bf = jnp.float32, jnp.bfloat16
    k1 = jnp.arange(DFT_K1, dtype=jnp.int32)
    n1 = jnp.arange(DFT_N1, dtype=jnp.int32)
    th = ((k1[:, None] * n1[None, :]) % DFT_N1).astype(f32) * (2.0 * math.pi / DFT_N1)
    pad = ((0, DFT_K1_PAD - DFT_K1), (0, 0))
    rows_fwd = jnp.concatenate([jnp.pad(jnp.cos(th), pad), jnp.pad(-jnp.sin(th), pad)], axis=0)
    ck = jnp.where((k1 == 0) | (k1 == DFT_N1 // 2), 1.0, 2.0)[:, None]
    half = DFT_N1 // 2
    rows_inv = jnp.concatenate([jnp.pad(ck * jnp.cos(th[:, :half]), pad),
                                jnp.pad(-ck * jnp.sin(th[:, :half]), pad)], axis=0).T
    n2 = jnp.arange(N2, dtype=jnp.int32)
    alpha = ((k1[:, None] * n2[None, :]) % N).astype(f32) * (2.0 * math.pi / N)
    beta = ((n2[:, None] * n2[None, :]) % N2).astype(f32) * (2.0 * math.pi / N2)
    twr, twi = jnp.cos(alpha)[:, None, :], -jnp.sin(alpha)[:, None, :]
    fr, fi = jnp.cos(beta)[None], -jnp.sin(beta)[None]
    gr, gi = twr * fr - twi * fi, twr * fi + twi * fr
    mid_fwd = jnp.concatenate([jnp.concatenate([gr, -gi], axis=2),
                               jnp.concatenate([gi, gr], axis=2)], axis=1)
    return {
        "rows_fwd": rows_fwd.astype(bf), "rows_fwd_half": rows_fwd[:, :half].astype(bf),
        "rows_inv": rows_inv.astype(bf),
        "mid_fwd": mid_fwd.astype(bf),
    }


def _spectral_fwd_kernel(a_ref, g_ref, o_ref):
    n2 = a_ref.shape[2]
    x = (jnp.dot(g_ref[0, :, :n2], a_ref[0, 0], preferred_element_type=jnp.float32)
         + jnp.dot(g_ref[0, :, n2:], a_ref[1, 0], preferred_element_type=jnp.float32))
    o_ref[0, 0] = x[:n2].astype(o_ref.dtype)
    o_ref[1, 0] = x[n2:].astype(o_ref.dtype)


def spectral_fwd(a, mid_fwd, cb=2048):
    _, _, n2, C = a.shape
    cb = min(cb, C)
    return pl.pallas_call(
        _spectral_fwd_kernel,
        grid=(DFT_K1, C // cb),
        in_specs=[pl.BlockSpec((2, 1, n2, cb), lambda k, c: (0, k, 0, c)),
                  pl.BlockSpec((1, 2 * n2, 2 * n2), lambda k, c: (k, 0, 0))],
        out_specs=pl.BlockSpec((2, 1, n2, cb), lambda k, c: (0, k, 0, c)),
        out_shape=jax.ShapeDtypeStruct((2, DFT_K1, n2, C), jnp.bfloat16),
        compiler_params=pltpu.CompilerParams(
            dimension_semantics=("parallel", "parallel"), vmem_limit_bytes=VMEM_LIMIT_BYTES),
        name="spectral_fwd",
    )(a, mid_fwd)


_TN = (((0,), (0,)), ((), ()))


def _spectral_mid_kernel(a_ref, gf_ref, ks_ref, o_ref):
    n2 = a_ref.shape[2]
    k1 = pl.program_id(0)

    @pl.when(k1 < DFT_K1)
    def _():
        x = (jnp.dot(gf_ref[0, :, :n2], a_ref[0, 0], preferred_element_type=jnp.float32)
             + jnp.dot(gf_ref[0, :, n2:], a_ref[1, 0], preferred_element_type=jnp.float32))
        xr, xi = x[:n2], x[n2:]
        kr, ki = ks_ref[0, 0].astype(jnp.float32), ks_ref[1, 0].astype(jnp.float32)
        yr = (xr * kr - xi * ki).astype(jnp.bfloat16)
        yi = (xr * ki + xi * kr).astype(jnp.bfloat16)
        b = (lax.dot_general(gf_ref[0, :n2, :], yr, _TN, preferred_element_type=jnp.float32)
             + lax.dot_general(gf_ref[0, n2:, :], yi, _TN, preferred_element_type=jnp.float32))
        o_ref[0, 0] = b[:n2].astype(o_ref.dtype)
        o_ref[1, 0] = b[n2:].astype(o_ref.dtype)

    @pl.when(k1 >= DFT_K1)
    def _():
        o_ref[...] = jnp.zeros_like(o_ref)


def spectral_mid(a, consts, kspec, col_off, cb=1024):
    _, _, n2, C = a.shape
    cb = min(cb, C)
    off = col_off // cb
    kc = lambda k: jnp.minimum(k, DFT_K1 - 1)
    return pl.pallas_call(
        _spectral_mid_kernel,
        grid=(DFT_K1_PAD, C // cb),
        in_specs=[pl.BlockSpec((2, 1, n2, cb), lambda k, c: (0, k, 0, c)),
                  pl.BlockSpec((1, 2 * n2, 2 * n2), lambda k, c: (kc(k), 0, 0)),
                  pl.BlockSpec((2, 1, n2, cb), lambda k, c: (0, kc(k), 0, off + c))],
        out_specs=pl.BlockSpec((2, 1, n2, cb), lambda k, c: (0, k, 0, c)),
        out_shape=jax.ShapeDtypeStruct(a.shape, jnp.bfloat16),
        compiler_params=pltpu.CompilerParams(
            dimension_semantics=("parallel", "parallel"), vmem_limit_bytes=VMEM_LIMIT_BYTES),
        name="spectral_mid",
    )(a, consts["mid_fwd"], kspec)


DFT_ROW_GROUP = 16
DFT_ROW_LANES = 256


def _rows_fwd_kernel(f_ref, u_ref, o_ref):
    x = pltpu.einshape("abc->bac", u_ref[...].astype(jnp.float32))
    rs = [jnp.dot(f_ref[...], x[s].astype(jnp.bfloat16), preferred_element_type=jnp.float32)
          for s in range(x.shape[0])]
    o_ref[...] = pltpu.einshape("abc->bac", jnp.stack(rs)).astype(o_ref.dtype)


def dft_rows_fwd(rows_mat, u3):
    kn, n2, C = u3.shape
    grp = min(DFT_ROW_GROUP, n2)
    tc = min(DFT_ROW_LANES, C)
    m = rows_mat.shape[0]
    return pl.pallas_call(
        _rows_fwd_kernel,
        grid=(n2 // grp, C // tc),
        in_specs=[pl.BlockSpec((m, kn), lambda j, c: (0, 0)),
                  pl.BlockSpec((kn, grp, tc), lambda j, c: (0, j, c))],
        out_specs=pl.BlockSpec((m, grp, tc), lambda j, c: (0, j, c)),
        out_shape=jax.ShapeDtypeStruct((m, n2, C), jnp.bfloat16),
        compiler_params=pltpu.CompilerParams(
            dimension_semantics=("parallel", "parallel"), vmem_limit_bytes=VMEM_LIMIT_BYTES),
        name="dft_rows_fwd",
    )(rows_mat, u3)


def _rows_inv_kernel(f_ref, b_ref, u_ref, gate_ref, scale_ref, d_ref, o_ref):
    x = pltpu.einshape("abc->bac", b_ref[...].astype(jnp.float32))
    ys = [jnp.dot(f_ref[...], x[s].astype(jnp.bfloat16), preferred_element_type=jnp.float32)
          for s in range(x.shape[0])]
    y = pltpu.einshape("abc->bac", jnp.stack(ys))
    o_ref[...] = (gate_ref[...] * (y * scale_ref[...] + u_ref[...] * d_ref[...])).astype(o_ref.dtype)


def dft_rows_inv(rows_inv, b3, u3, gate3, scale, d_skip, out_dtype):
    nr, n2, C = u3.shape
    grp = min(DFT_ROW_GROUP, n2)
    tc = min(DFT_ROW_LANES, C)
    blk = pl.BlockSpec((nr, grp, tc), lambda j, c: (0, j, c))
    vec = pl.BlockSpec((1, 1, tc), lambda j, c: (0, 0, c))
    return pl.pallas_call(
        _rows_inv_kernel,
        grid=(n2 // grp, C // tc),
        in_specs=[pl.BlockSpec(rows_inv.shape, lambda j, c: (0, 0)),
                  pl.BlockSpec((b3.shape[0], grp, tc), lambda j, c: (0, j, c)), blk, blk, vec, vec],
        out_specs=blk,
        out_shape=jax.ShapeDtypeStruct((nr, n2, C), out_dtype),
        compiler_params=pltpu.CompilerParams(
            dimension_semantics=("parallel", "parallel"), vmem_limit_bytes=VMEM_LIMIT_BYTES),
        name="dft_rows_inv",
    )(rows_inv, b3, u3, gate3, scale.reshape(1, 1, C), d_skip.reshape(1, 1, C))


def _conv3_kernel(z_ref, prev_ref, next_ref, w_ref, b_ref, o_ref):
    i = pl.program_id(0)
    z = z_ref[...]
    tm = z.shape[0]
    row = lax.broadcasted_iota(jnp.int32, z.shape, 0)
    prev_row = jnp.where(i > 0, prev_ref[7:8, :], 0.0)
    next_row = jnp.where(i < pl.num_programs(0) - 1, next_ref[0:1, :], 0.0)
    zp = jnp.where(row == 0, prev_row, pltpu.roll(z, 1, 0))
    zn = jnp.where(row == tm - 1, next_row, pltpu.roll(z, tm - 1, 0))
    o_ref[...] = zp * w_ref[0:1, :] + z * w_ref[1:2, :] + zn * w_ref[2:3, :] + b_ref[...]


def short_conv3_part(z, w, b, part, width):
    L = z.shape[0]
    tm = _pick(L, (512, 256, 128, 64, 32, 16, 8))
    tc = _pick(width, (1024, 512, 256, 128))
    off = part * width // tc
    halo = 8
    return pl.pallas_call(
        _conv3_kernel,
        grid=(L // tm, width // tc),
        in_specs=[pl.BlockSpec((tm, tc), lambda i, j: (i, off + j)),
                  pl.BlockSpec((halo, tc), lambda i, j: (jnp.maximum(i * (tm // halo) - 1, 0), off + j)),
                  pl.BlockSpec((halo, tc), lambda i, j: (jnp.minimum((i + 1) * (tm // halo), L // halo - 1), off + j)),
                  pl.BlockSpec((3, tc), lambda i, j: (0, off + j)),
                  pl.BlockSpec((1, tc), lambda i, j: (0, off + j))],
        out_specs=pl.BlockSpec((tm, tc), lambda i, j: (i, j)),
        out_shape=jax.ShapeDtypeStruct((L, width), jnp.float32),
        compiler_params=pltpu.CompilerParams(
            dimension_semantics=("parallel", "parallel"), vmem_limit_bytes=VMEM_LIMIT_BYTES),
        name="short_conv3",
    )(z, z, z, w, b.reshape(1, -1))


FILTER_PAD = 128
FILTER_LANES = 512


def _filter_rows_kernel(feats_ref, feats0_ref, fw1_ref, fb1_ref, fw2_ref, fb2_ref, fr_ref, fw3_ref, rate_ref,
                        rows_ref, a_ref, asum_ref, hid_ref, hid0_ref):
    g = pl.program_id(0)
    exact = lax.Precision.HIGHEST
    n_dir, prow, nf2 = feats_ref.shape
    hid = fw3_ref.shape[1]
    half = rows_ref.shape[1] // 2
    grp = 2 * prow // half

    def hidden(feats):
        h = jnp.sin(fr_ref[0:1, :] * (jnp.dot(feats, fw1_ref[...], precision=exact,
                                              preferred_element_type=jnp.float32) + fb1_ref[...]))
        return jnp.sin(fr_ref[1:2, :] * (jnp.dot(h, fw2_ref[...], precision=exact,
                                                 preferred_element_type=jnp.float32) + fb2_ref[...]))

    @pl.when(pl.program_id(1) == 0)
    def _():
        for d in range(n_dir):
            hid_ref[d] = hidden(feats_ref[d])
        hid0_ref[...] = hidden(feats0_ref[...])

    def taps(h, t, d):
        k = jnp.dot(h.astype(jnp.bfloat16), fw3_ref[d].astype(jnp.bfloat16), preferred_element_type=jnp.float32)
        return k * (jnp.exp(-t * rate_ref[d]) + HY_DECAY_SHIFT)

    def direction(d):
        h2, f = hid_ref[d], feats_ref[d]
        return jnp.concatenate([taps(h2[:, :hid], f[:, 0:1], d),
                                taps(h2[:, hid:], f[:, nf2 // 2:nf2 // 2 + 1], d)], axis=0)

    k_fwd = direction(0)
    k_bwd = direction(1)
    lag0_back = taps(hid0_ref[:, :hid], feats0_ref[:, 0:1], 1)[0:1, :]
    first = (lax.broadcasted_iota(jnp.int32, k_fwd.shape, 0) == 0) & (g == 0)
    k_fwd = k_fwd + jnp.where(first, lag0_back, 0.0)
    k_bwd = jnp.where(first, 0.0, k_bwd)
    asum_ref[0] = jnp.sum(jnp.abs(k_fwd), axis=0, keepdims=True) + jnp.sum(jnp.abs(k_bwd), axis=0, keepdims=True)
    kf = k_fwd.astype(jnp.bfloat16)
    kb = k_bwd.astype(jnp.bfloat16)
    rs = [jnp.dot(rows_ref[:, :half], kf[s * half:(s + 1) * half], preferred_element_type=jnp.float32)
          + jnp.dot(rows_ref[:, half:], kb[s * half:(s + 1) * half], preferred_element_type=jnp.float32)
          for s in range(grp)]
    a_ref[...] = pltpu.einshape("abc->bac", jnp.stack(rs)).astype(a_ref.dtype)


def _block_diag2(w):
    z = jnp.zeros_like(w)
    return jnp.concatenate([jnp.concatenate([w, z], axis=1), jnp.concatenate([z, w], axis=1)], axis=0)


def hyena_filter_rows(L, p, rows_fwd):
    f32 = jnp.float32
    N = 2 * L
    n2 = N // DFT_N1
    half = DFT_N1 // 2
    C2 = HY_ORDER * HY_WIDTH
    grp = min(DFT_ROW_GROUP, n2)
    tc = min(FILTER_LANES, C2)
    prow = grp * half // 2
    n = (n2 * jnp.arange(DFT_N1, dtype=jnp.int32)[None, :] + jnp.arange(n2, dtype=jnp.int32)[:, None])
    pos = jnp.where(n < L, n, N - n).astype(f32)[..., None]
    bands = jnp.linspace(1e-4, HY_BANDS - 1, HY_BANDS, dtype=f32)
    ang = (2.0 * math.pi / L) * bands * pos
    feats = jnp.concatenate([pos / (L - 1), jnp.cos(ang), -jnp.sin(ang)], axis=-1)
    feats = jnp.pad(feats, ((0, 0), (0, 0), (0, FILTER_PAD - feats.shape[-1])))
    feats = feats.reshape(n2 // grp, grp, 2, half, FILTER_PAD).transpose(2, 0, 1, 3, 4)
    feats = feats.reshape(2, n2 // grp, 2, prow, FILTER_PAD).transpose(0, 1, 3, 2, 4)
    feats = feats.reshape(2, (n2 // grp) * prow, 2 * FILTER_PAD)
    feats0 = jnp.broadcast_to(feats[0, 0:1], (8, 2 * FILTER_PAD))
    hid = p['hy_fw2'].shape[0]
    fw1 = _block_diag2(jnp.pad(p['hy_fw1'].astype(f32), ((0, FILTER_PAD - p['hy_fw1'].shape[0]), (0, 0))))
    fb1 = jnp.tile(p['hy_fb1'].astype(f32), 2).reshape(1, -1)
    fw2 = _block_diag2(p['hy_fw2'].astype(f32))
    fb2 = jnp.tile(p['hy_fb2'].astype(f32), 2).reshape(1, -1)
    fr = jnp.tile(p['hy_freq'].astype(f32), (1, 2))
    fw3 = p['hy_fw3'].astype(f32).reshape(hid, HY_DIRS, C2).transpose(1, 0, 2)
    rate = jnp.abs(p['hy_decay'].astype(f32)).reshape(HY_DIRS, 1, C2)
    full = lambda shape: pl.BlockSpec(shape, lambda g, c: (0,) * len(shape))
    m = rows_fwd.shape[0]
    a, asum = pl.pallas_call(
        _filter_rows_kernel,
        grid=(n2 // grp, C2 // tc),
        in_specs=[pl.BlockSpec((2, prow, 2 * FILTER_PAD), lambda g, c: (0, g, 0)), full((8, 2 * FILTER_PAD)),
                  full((2 * FILTER_PAD, 2 * hid)), full((1, 2 * hid)),
                  full((2 * hid, 2 * hid)), full((1, 2 * hid)), full((2, 2 * hid)),
                  pl.BlockSpec((HY_DIRS, hid, tc), lambda g, c: (0, 0, c)),
                  pl.BlockSpec((HY_DIRS, 1, tc), lambda g, c: (0, 0, c)),
                  full(rows_fwd.shape)],
        out_specs=[pl.BlockSpec((m, grp, tc), lambda g, c: (0, g, c)),
                   pl.BlockSpec((1, 1, tc), lambda g, c: (g, 0, c))],
        out_shape=[jax.ShapeDtypeStruct((m, n2, C2), jnp.bfloat16),
                   jax.ShapeDtypeStruct((n2 // grp, 1, C2), f32)],
        scratch_shapes=[pltpu.VMEM((2, prow, 2 * hid), f32), pltpu.VMEM((8, 2 * hid), f32)],
        compiler_params=pltpu.CompilerParams(
            dimension_semantics=("parallel", "arbitrary"), vmem_limit_bytes=VMEM_LIMIT_BYTES),
        name="hyena_filter_rows",
    )(feats, feats0, fw1, fb1, fw2, fb2, fr, fw3, rate, rows_fwd)
    return a, jnp.sum(asum, axis=(0, 1))


def hyena_latent(z, p, consts):
    L = z.shape[0]
    C = HY_WIDTH
    N = 2 * L
    n2 = N // DFT_N1
    half = DFT_N1 // 2
    v, x1, x2 = (short_conv3_part(z, p['hy_conv_w'], p['hy_conv_b'], i, C) for i in range(3))
    ka, asum = hyena_filter_rows(L, p, consts["rows_fwd"])
    kspec = spectral_fwd(ka.reshape(2, DFT_K1_PAD, n2, HY_ORDER * C), consts["mid_fwd"])
    scale = 1.0 / (N * asum)

    def long_conv(u, gate, order, out_dtype):
        u3 = u.reshape(half, n2, C)
        ua = dft_rows_fwd(consts["rows_fwd_half"], u3)
        bm = spectral_mid(ua.reshape(2, DFT_K1_PAD, n2, C), consts, kspec, order * C)
        y = dft_rows_inv(consts["rows_inv"], bm.reshape(2 * DFT_K1_PAD, n2, C), u3, gate.reshape(half, n2, C),
                         scale[order * C:(order + 1) * C], p['hy_d'][order].astype(jnp.float32), out_dtype)
        return y.reshape(L, C)

    y1 = long_conv(v, x1, 0, jnp.float32)
    return long_conv(y1, x2, 1, jnp.bfloat16)


def _resident(shape, index_map):
    return pl.BlockSpec(shape, index_map, pipeline_mode=pl.Buffered(1))


def _merge_kernel(y0_ref, y1_ref, y2_ref, g_ref, wb_ref, o_ref):
    d = o_ref.shape[1]
    m = None
    for b, y_ref in enumerate((y0_ref, y1_ref, y2_ref)):
        t = g_ref[:, b * d:(b + 1) * d] * jnp.dot(y_ref[...], wb_ref[b], preferred_element_type=jnp.float32)
        m = t if m is None else m + t
    o_ref[...] = m.astype(o_ref.dtype)


def merge_gated(y_hy, y_wa, y_na, gates, w_branch):
    M, wbr = y_hy.shape
    D = w_branch.shape[2]
    tm = _pick(M, (256, 128, 64, 32, 16, 8))
    yspec = pl.BlockSpec((tm, wbr), lambda i: (i, 0))
    return pl.pallas_call(
        _merge_kernel,
        grid=(M // tm,),
        in_specs=[yspec, yspec, yspec, pl.BlockSpec((tm, N_BRANCH * D), lambda i: (i, 0)),
                  _resident((N_BRANCH, wbr, D), lambda i: (0, 0, 0))],
        out_specs=pl.BlockSpec((tm, D), lambda i: (i, 0)),
        out_shape=jax.ShapeDtypeStruct((M, D), jnp.bfloat16),
        compiler_params=pltpu.CompilerParams(
            dimension_semantics=("parallel",), vmem_limit_bytes=VMEM_LIMIT_BYTES),
        name="merge_gated",
    )(y_hy, y_wa, y_na, gates, w_branch)


ROUTER_PAD = 128


def _out_proj_kernel(m_ref, w_ref, x_ref, gate_ref, g2_ref, shift_ref, scale_ref, wr_ref, x_out, h_out, lg_out):
    x = x_ref[...] + gate_ref[...] * jnp.dot(m_ref[...], w_ref[...], preferred_element_type=jnp.float32)
    x_out[...] = x
    h = x * lax.rsqrt(jnp.mean(x * x, axis=-1, keepdims=True) + RMS_EPS) * g2_ref[...]
    h = h * (1.0 + scale_ref[...]) + shift_ref[...]
    h_out[...] = h.astype(h_out.dtype)
    pair = jnp.dot(h.astype(jnp.bfloat16), wr_ref[...], preferred_element_type=jnp.float32)
    lg_out[...] = pair[:, :ROUTER_PAD] + pair[:, ROUTER_PAD:]


def out_project(m, w_out, x, gate, norm_g, shift, scale, w_router):
    M, D = x.shape
    tm = _pick(M, (256, 128, 64, 32, 16, 8))
    row = pl.BlockSpec((tm, D), lambda i: (i, 0))
    vec = pl.BlockSpec((1, D), lambda i: (0, 0))
    wr = jnp.pad(w_router.astype(jnp.float32), ((0, 0), (0, ROUTER_PAD - w_router.shape[1])))
    wr_hi = wr.astype(jnp.bfloat16)
    wr_lo = (wr - wr_hi.astype(jnp.float32)).astype(jnp.bfloat16)
    wr_pair = jnp.concatenate([wr_hi, wr_lo], axis=1)
    v2 = lambda a: a.reshape(1, D).astype(jnp.float32)
    return pl.pallas_call(
        _out_proj_kernel,
        grid=(M // tm,),
        in_specs=[row, _resident((D, D), lambda i: (0, 0)), row, vec, vec, vec, vec,
                  _resident((D, 2 * ROUTER_PAD), lambda i: (0, 0))],
        out_specs=[row, row, pl.BlockSpec((tm, ROUTER_PAD), lambda i: (i, 0))],
        out_shape=[jax.ShapeDtypeStruct((M, D), jnp.float32), jax.ShapeDtypeStruct((M, D), jnp.bfloat16),
                   jax.ShapeDtypeStruct((M, ROUTER_PAD), jnp.float32)],
        compiler_params=pltpu.CompilerParams(
            dimension_semantics=("parallel",), vmem_limit_bytes=VMEM_LIMIT_BYTES),
        name="out_project",
    )(m, w_out, x, v2(gate), v2(norm_g), v2(shift), v2(scale), wr_pair)


def _gate_up_kernel(x_ref, wg_ref, wu_ref, o_ref, wg_bf, wu_bf):
    @pl.when(pl.program_id(2) == 0)
    def _():
        wg_bf[...] = wg_ref[...].astype(jnp.bfloat16)
        wu_bf[...] = wu_ref[...].astype(jnp.bfloat16)

    x = x_ref[0]
    a = jnp.dot(x, wg_bf[...], preferred_element_type=jnp.float32)
    u = jnp.dot(x, wu_bf[...], preferred_element_type=jnp.float32)
    o_ref[0] = (a * _sigmoid(a) * u).astype(o_ref.dtype)


def _down_kernel(h_ref, wd_ref, gsel_ref, gate_ref, o_ref, wd_bf):
    @pl.when(pl.program_id(1) == 0)
    def _():
        wd_bf[...] = wd_ref[...].astype(jnp.bfloat16)

    y = jnp.dot(h_ref[0], wd_bf[...], preferred_element_type=jnp.float32)
    o_ref[0] = (y * gsel_ref[0] * gate_ref[...]).astype(o_ref.dtype)


def expert_ffn(xe, w_gate, w_up, w_down, layer, gsel, out_gate):
    E, cap, D = xe.shape
    F = w_gate.shape[3]
    tm = _pick(cap, (512, 256, 128, 64, 32, 16, 8))
    tn = _pick(F, (1024, 512, 256, 128))
    seq = pltpu.CompilerParams(dimension_semantics=("parallel", "parallel", "arbitrary"),
                               vmem_limit_bytes=VMEM_LIMIT_BYTES)
    h = pl.pallas_call(
        _gate_up_kernel,
        grid=(E, F // tn, cap // tm),
        in_specs=[pl.BlockSpec((1, tm, D), lambda e, j, i: (e, i, 0)),
                  pl.BlockSpec((None, None, D, tn), lambda e, j, i: (layer, e, 0, j)),
                  pl.BlockSpec((None, None, D, tn), lambda e, j, i: (layer, e, 0, j))],
        out_specs=pl.BlockSpec((1, tm, tn), lambda e, j, i: (e, i, j)),
        out_shape=jax.ShapeDtypeStruct((E, cap, F), jnp.bfloat16),
        scratch_shapes=[pltpu.VMEM((D, tn), jnp.bfloat16), pltpu.VMEM((D, tn), jnp.bfloat16)],
        compiler_params=seq,
        name="expert_gate_up",
    )(xe, w_gate, w_up)
    td = _pick(cap, (1024, 512, 256, 128, 64, 32, 16, 8))
    return pl.pallas_call(
        _down_kernel,
        grid=(E, cap // td),
        in_specs=[pl.BlockSpec((1, td, F), lambda e, i: (e, i, 0)),
                  pl.BlockSpec((None, None, F, D), lambda e, i: (layer, e, 0, 0)),
                  pl.BlockSpec((1, td, 1), lambda e, i: (e, i, 0)),
                  pl.BlockSpec((1, D), lambda e, i: (0, 0))],
        out_specs=pl.BlockSpec((1, td, D), lambda e, i: (e, i, 0)),
        out_shape=jax.ShapeDtypeStruct((E, cap, D), jnp.bfloat16),
        scratch_shapes=[pltpu.VMEM((F, D), jnp.bfloat16)],
        compiler_params=pltpu.CompilerParams(dimension_semantics=("parallel", "arbitrary"),
                                             vmem_limit_bytes=VMEM_LIMIT_BYTES),
        name="expert_down",
    )(h, w_down, gsel.reshape(E, cap, 1).astype(jnp.float32), out_gate.reshape(1, D).astype(jnp.float32))


SEG_ROWS = 512
SEG_WIN = 512
SEG_LANES = 256


def _segment_add_kernel(first_ref, last_ref, tok_ref, ye_ref, x_ref, o_ref, *, n_tokens, win):
    c = pl.program_id(1)

    @pl.when(c == 0)
    def _():
        o_ref[...] = x_ref[...]

    base0 = (first_ref[c] // 8) * 8
    n_win = (last_ref[c] - base0) // win + 1
    tok = tok_ref[0]
    rows = ye_ref[...]

    def window(w, carry):
        lo = base0 + w * win
        base = pl.multiple_of(jnp.minimum(lo, n_tokens - win), 8)
        rid = base + lax.broadcasted_iota(jnp.int32, (win, tok.shape[1]), 0)
        onehot = ((rid == tok) & (tok >= lo)).astype(jnp.bfloat16)
        o_ref[pl.ds(base, win), :] += jnp.dot(onehot, rows, preferred_element_type=jnp.float32)
        return carry

    lax.fori_loop(0, n_win, window, 0)


def segment_add(x, ye_sorted, tok_sorted):
    N, D = x.shape
    P = ye_sorted.shape[0]
    rows = min(SEG_ROWS, P)
    win = min(SEG_WIN, N)
    tc = min(SEG_LANES, D)
    n_chunks = P // rows
    tok3 = tok_sorted.reshape(n_chunks, 1, rows)
    return pl.pallas_call(
        partial(_segment_add_kernel, n_tokens=N, win=win),
        grid_spec=pltpu.PrefetchScalarGridSpec(
            num_scalar_prefetch=2,
            grid=(D // tc, n_chunks),
            in_specs=[pl.BlockSpec((1, 1, rows), lambda d, c, f, l: (c, 0, 0)),
                      pl.BlockSpec((rows, tc), lambda d, c, f, l: (c, d)),
                      pl.BlockSpec((N, tc), lambda d, c, f, l: (0, d), pipeline_mode=pl.Buffered(1))],
            out_specs=pl.BlockSpec((N, tc), lambda d, c, f, l: (0, d)),
        ),
        out_shape=jax.ShapeDtypeStruct((N, D), jnp.float32),
        compiler_params=pltpu.CompilerParams(
            dimension_semantics=("parallel", "arbitrary"), vmem_limit_bytes=VMEM_LIMIT_BYTES),
        name="segment_add",
    )(tok3[:, 0, 0], tok3[:, 0, rows - 1], tok3, ye_sorted, x)


def expert_choice_latent(x1, h2, logits, out_gate, w_gate, w_up, w_down, layer):
    N, D = x1.shape
    cap = EC_CAPACITY_FACTOR * N // N_EXPERTS
    aff = jax.nn.softmax(logits[:, :N_EXPERTS], axis=-1)
    gsel, idx = lax.top_k(aff.T, cap)
    ye = expert_ffn(h2[idx], w_gate, w_up, w_down, layer, gsel, out_gate)
    tok = idx.reshape(-1)
    order = jnp.argsort(tok)
    return segment_add(x1, ye.reshape(-1, D)[order], tok[order])


def _ctx_attn_kernel(sink_ref, q_ref, k_ref, v_ref, o_ref, *, use_sink):
    s = lax.dot_general(q_ref[...], k_ref[...], _NT, preferred_element_type=jnp.float32)
    m = jnp.max(s, axis=-1, keepdims=True)
    if use_sink:
        sk = sink_ref[pl.program_id(0)]
        m = jnp.maximum(m, sk)
    p = jnp.exp(s - m)
    denom = jnp.sum(p, axis=-1, keepdims=True)
    if use_sink:
        denom = denom + jnp.exp(sk - m)
    o = jnp.dot(p.astype(v_ref.dtype), v_ref[...], preferred_element_type=jnp.float32)
    o_ref[...] = (o / denom).astype(o_ref.dtype)


def context_attention(q, q_blk, kv, k_blk, v_blk, sink, n_heads, n_kv_heads):
    Lc = q.shape[0]
    group = n_heads // n_kv_heads
    use_sink = sink is not None
    sink = jnp.zeros((n_heads,), jnp.float32) if sink is None else sink.astype(jnp.float32)
    return pl.pallas_call(
        partial(_ctx_attn_kernel, use_sink=use_sink),
        grid_spec=pltpu.PrefetchScalarGridSpec(
            num_scalar_prefetch=1,
            grid=(n_heads,),
            in_specs=[pl.BlockSpec((Lc, HEAD_DIM), lambda h, s: (0, q_blk + h)),
                      pl.BlockSpec((Lc, HEAD_DIM), lambda h, s: (0, k_blk + h // group)),
                      pl.BlockSpec((Lc, HEAD_DIM), lambda h, s: (0, v_blk + h // group))],
            out_specs=pl.BlockSpec((Lc, HEAD_DIM), lambda h, s: (0, h)),
        ),
        out_shape=jax.ShapeDtypeStruct((Lc, n_heads * HEAD_DIM), jnp.bfloat16),
        compiler_params=pltpu.CompilerParams(
            dimension_semantics=("parallel",), vmem_limit_bytes=VMEM_LIMIT_BYTES),
        name="context_attention",
    )(sink, q, kv, kv)


def trunk_layer(x, ctx, c, c_ctx, p, stacked, layer, update_ctx, consts, consts_ctx):
    L = x.shape[0]
    bf = jnp.bfloat16
    w_in = stacked['w_in']
    cond = jnp.pad(jnp.concatenate([jax.nn.silu(c), jax.nn.silu(c_ctx)[None]], axis=0), ((0, 6), (0, 0)))
    mod = matmul(cond, stacked['w_mod'], layer=layer) + p['b_mod']
    mx = jnp.split(mod[0], N_MOD)
    mc = jnp.split(mod[1], N_MOD)
    qk_scale = HEAD_DIM ** -0.5
    nwq = WA_HEADS * HEAD_DIM
    nkv = WA_KV_HEADS * HEAD_DIM
    nna = NA_HEADS * HEAD_DIM
    proj = lambda a, *args: project(a, w_in, layer, *args)

    def mix_and_ffn(res, y_hy, y_wa, y_na, gates, m_vec):
        m = merge_gated(y_hy, y_wa, y_na, gates, p['w_branch'].astype(bf))
        r1, h2, logits = out_project(m, p['w_out'].astype(bf), res, m_vec[2], p['norm2_g'], m_vec[3], m_vec[4],
                                     p['w_router'])
        return expert_choice_latent(r1, h2, logits, m_vec[5], stacked['w_gate'], stacked['w_up'],
                                    stacked['w_down'], layer)

    ones = jnp.ones((HEAD_DIM,), jnp.float32)
    wa_k_gain = jnp.stack([p['wa_k_norm']] * WA_KV_HEADS + [ones] * WA_KV_HEADS)
    na_gain = jnp.stack([p['na_q_norm'] * qk_scale] * NA_HEADS + [p['na_k_norm']] * NA_HEADS + [ones] * NA_HEADS)

    hcb = norm_modulate(ctx, p['norm1_g'], mc[0], mc[1])
    kvc_wa = proj(hcb, OFF_WA_KV, 2 * nkv, "headnorm", bf, wa_k_gain, None, nkv)
    qkvc_na = proj(hcb, OFF_NA_Q, 3 * nna, "headnorm", bf, na_gain, None, 2 * nna)
    kvc_na = qkvc_na[:, nna:]

    hxb = norm_modulate(x, p['norm1_g'], mx[0], mx[1])
    rope = rope_lane_tables(L)
    y_hy = hyena_latent(proj(hxb, OFF_HY, OFF_WA_Q - OFF_HY), p, consts)
    q_wa = proj(hxb, OFF_WA_Q, nwq, "headnorm", bf, p['wa_q_norm'] * qk_scale, rope)
    kv_wa = proj(hxb, OFF_WA_KV, 2 * nkv, "headnorm", bf, wa_k_gain, rope, nkv)
    y_wa = windowed_attention(q_wa, kv_wa, kvc_wa, p['wa_sink'])
    qkv_na = proj(hxb, OFF_NA_Q, 3 * nna, "headnorm", bf, na_gain, None, 2 * nna)
    y_na = neighbourhood_attention(qkv_na, kvc_na, p['na_rpb'])
    gates = proj(hxb, OFF_GATE, N_BRANCH * D_MODEL, "sigmoid")
    x = mix_and_ffn(x, y_hy, y_wa, y_na, gates, mx)

    if update_ctx:
        yc_hy = hyena_latent(proj(hcb, OFF_HY, OFF_WA_Q - OFF_HY), p, consts_ctx)
        qc_wa = proj(hcb, OFF_WA_Q, nwq, "headnorm", bf, p['wa_q_norm'] * qk_scale)
        yc_wa = context_attention(qc_wa, 0, kvc_wa, 0, WA_KV_HEADS, p['wa_sink'], WA_HEADS, WA_KV_HEADS)
        yc_na = context_attention(qkvc_na, 0, qkvc_na, NA_HEADS, 2 * NA_HEADS, None, NA_HEADS, NA_HEADS)
        gates_c = proj(hcb, OFF_GATE, N_BRANCH * D_MODEL, "sigmoid")
        ctx = mix_and_ffn(ctx, yc_hy, yc_wa, yc_na, gates_c, mc)
    return x, ctx


def kernel(x, c, ctx, c_ctx, w_mod, b_mod, norm1_g, w_in, hy_conv_w, hy_conv_b, hy_fw1, hy_fb1, hy_fw2, hy_fb2, hy_fw3, hy_freq, hy_decay, hy_d, wa_q_norm, wa_k_norm, wa_sink, na_q_norm, na_k_norm, na_rpb, w_branch, w_out, norm2_g, w_router, w_gate, w_up, w_down):
    consts = dft_constants(x.shape[1])
    consts_ctx = dft_constants(ctx.shape[1])
    stacked = {'w_mod': w_mod, 'w_in': w_in, 'w_gate': w_gate, 'w_up': w_up, 'w_down': w_down}
    xs, cs = x[0], ctx[0]
    for l in range(DEPTH):
        p = {
            'b_mod': b_mod[l], 'norm1_g': norm1_g[l],
            'hy_conv_w': hy_conv_w[l], 'hy_conv_b': hy_conv_b[l], 'hy_fw1': hy_fw1[l], 'hy_fb1': hy_fb1[l],
            'hy_fw2': hy_fw2[l], 'hy_fb2': hy_fb2[l], 'hy_fw3': hy_fw3[l], 'hy_freq': hy_freq[l],
            'hy_decay': hy_decay[l], 'hy_d': hy_d[l], 'wa_q_norm': wa_q_norm[l], 'wa_k_norm': wa_k_norm[l],
            'wa_sink': wa_sink[l], 'na_q_norm': na_q_norm[l], 'na_k_norm': na_k_norm[l], 'na_rpb': na_rpb[l],
            'w_branch': w_branch[l], 'w_out': w_out[l], 'norm2_g': norm2_g[l], 'w_router': w_router[l],
        }
        xs, cs = trunk_layer(xs, cs, c, c_ctx, p, stacked, l, l < DEPTH - 1, consts, consts_ctx)
    return xs[None]
```

```python
import math
from functools import partial

import jax
import jax.numpy as jnp
from jax import lax
from jax.experimental import pallas as pl
from jax.experimental.pallas import tpu as pltpu

D_MODEL = 2048
SEQ = 16384
DEPTH = 2
CTX_LEN = 256
GRID_W = 64
HEAD_DIM = 128
BRANCH_WIDTH = 1024
N_BRANCH = 3
N_MOD = 6
RMS_EPS = 1e-6
NEG_INF = -1e30

HY_WIDTH = BRANCH_WIDTH
HY_ORDER = 2
HY_DIRS = 2
HY_BANDS = 16
HY_DECAY_SHIFT = 0.05

WA_HEADS = BRANCH_WIDTH // HEAD_DIM
WA_KV_HEADS = 2
WA_WINDOW = 128
WA_BLOCK = 128

NA_HEADS = BRANCH_WIDTH // HEAD_DIM
NA_WIN_ROWS = 8
NA_WIN_COLS = 16

ROPE_BASE = 10000.0

N_EXPERTS = 16
EC_CAPACITY_FACTOR = 2
D_EXPERT = 1024

OFF_HY = 0
OFF_WA_Q = OFF_HY + 3 * HY_WIDTH
OFF_WA_KV = OFF_WA_Q + WA_HEADS * HEAD_DIM
OFF_NA_Q = OFF_WA_KV + 2 * WA_KV_HEADS * HEAD_DIM
OFF_NA_KV = OFF_NA_Q + NA_HEADS * HEAD_DIM
OFF_GATE = OFF_NA_KV + 2 * NA_HEADS * HEAD_DIM
N_IN = OFF_GATE + N_BRANCH * D_MODEL

VMEM_LIMIT_BYTES = 56 * 1024 * 1024


def _mm_kernel(a_ref, b_ref, o_ref):
    o_ref[...] = jnp.dot(a_ref[...].astype(jnp.bfloat16), b_ref[...].astype(jnp.bfloat16),
                         preferred_element_type=jnp.float32).astype(o_ref.dtype)


def _pick(n, pref):
    for t in pref:
        if n % t == 0:
            return t
    return n


def matmul(a, b, out_dtype=jnp.float32, layer=None):
    M, K = a.shape
    N = b.shape[-1]
    tm = _pick(M, (512, 256, 128, 64, 32, 16, 8))
    tn = _pick(N, (1024, 512, 256, 128))
    if layer is None:
        b_spec = pl.BlockSpec((K, tn), lambda j, i: (0, j))
    else:
        b_spec = pl.BlockSpec((None, K, tn), lambda j, i: (layer, 0, j))
    return pl.pallas_call(
        _mm_kernel,
        grid=(N // tn, M // tm),
        in_specs=[pl.BlockSpec((tm, K), lambda j, i: (i, 0)), b_spec],
        out_specs=pl.BlockSpec((tm, tn), lambda j, i: (i, j)),
        out_shape=jax.ShapeDtypeStruct((M, N), out_dtype),
        compiler_params=pltpu.CompilerParams(
            dimension_semantics=("parallel", "parallel"), vmem_limit_bytes=VMEM_LIMIT_BYTES),
        name="matmul",
    )(a, b)


def _norm_mod_kernel(x_ref, g_ref, shift_ref, scale_ref, o_ref):
    x = x_ref[...]
    y = x * lax.rsqrt(jnp.mean(x * x, axis=-1, keepdims=True) + RMS_EPS) * g_ref[...]
    o_ref[...] = (y * (1.0 + scale_ref[...]) + shift_ref[...]).astype(o_ref.dtype)


def norm_modulate(x, g, shift, scale, out_dtype=jnp.bfloat16):
    M, D = x.shape
    tm = _pick(M, (512, 256, 128, 64, 32, 16, 8))
    vec = pl.BlockSpec((1, D), lambda i: (0, 0))
    return pl.pallas_call(
        _norm_mod_kernel,
        grid=(M // tm,),
        in_specs=[pl.BlockSpec((tm, D), lambda i: (i, 0)), vec, vec, vec],
        out_specs=pl.BlockSpec((tm, D), lambda i: (i, 0)),
        out_shape=jax.ShapeDtypeStruct((M, D), out_dtype),
        compiler_params=pltpu.CompilerParams(
            dimension_semantics=("parallel",), vmem_limit_bytes=VMEM_LIMIT_BYTES),
        name="norm_modulate",
    )(x, g.reshape(1, D), shift.reshape(1, D), scale.reshape(1, D))


def _swap_halves(x):
    lane = lax.broadcasted_iota(jnp.int32, x.shape, 1)
    return jnp.where((lane % 64) < 32, pltpu.roll(x, 96, 1), pltpu.roll(x, 32, 1))


HEADNORM_ROWS = 256
HEADNORM_COLS = 256


def _sigmoid(x):
    return 0.5 * (jnp.tanh(0.5 * x) + 1.0)


def _proj_kernel(*refs, mode, rope, norm_cols, n_tiles):
    if mode == "headnorm":
        if rope:
            a_ref, w_ref, gain_ref, cos_ref, sin_ref, o_ref, w_bf = refs
        else:
            a_ref, w_ref, gain_ref, o_ref, w_bf = refs
    else:
        a_ref, w_ref, o_ref, w_bf = refs

    @pl.when(pl.program_id(1) == 0)
    def _():
        w_bf[...] = w_ref[...].astype(jnp.bfloat16)

    if mode == "plain":
        o_ref[...] = jnp.dot(a_ref[...], w_bf[...], preferred_element_type=jnp.float32).astype(o_ref.dtype)
        return
    if mode == "sigmoid":
        acc = jnp.dot(a_ref[...], w_bf[...], preferred_element_type=jnp.float32)
        o_ref[...] = _sigmoid(acc).astype(o_ref.dtype)
        return

    tm, tn = o_ref.shape
    rows = min(tm, HEADNORM_ROWS) if rope else tm
    width = min(tn, HEADNORM_COLS) if rope else tn

    def epilogue(norm_heads):
        def chunk(r, carry):
            r0 = pl.multiple_of(r * rows, rows)
            a = a_ref[pl.ds(r0, rows), :]
            for c0 in range(0, tn, width):
                acc = jnp.dot(a, w_bf[:, c0:c0 + width], preferred_element_type=jnp.float32)
                for h in range(width // HEAD_DIM):
                    head = c0 // HEAD_DIM + h
                    y = acc[:, h * HEAD_DIM:(h + 1) * HEAD_DIM]
                    if head < norm_heads:
                        y = y * lax.rsqrt(jnp.mean(y * y, axis=-1, keepdims=True) + RMS_EPS) * gain_ref[head:head + 1, :]
                        if rope:
                            y = (y * cos_ref[pl.ds(r0, rows), :]
                                 + _swap_halves(y) * sin_ref[pl.ds(r0, rows), :])
                    o_ref[pl.ds(r0, rows), c0 + h * HEAD_DIM:c0 + (h + 1) * HEAD_DIM] = y.astype(o_ref.dtype)
            return carry

        lax.fori_loop(0, tm // rows, chunk, 0)

    heads = tn // HEAD_DIM
    if n_tiles == 1 or norm_cols >= n_tiles * tn:
        epilogue(min(heads, norm_cols // HEAD_DIM))
    else:
        norm_tiles = norm_cols // tn
        j = pl.program_id(0)
        pl.when(j < norm_tiles)(lambda: epilogue(heads))
        pl.when(j >= norm_tiles)(lambda: epilogue(0))


def project(a, w, layer, col_off, n_cols, mode="plain", out_dtype=jnp.float32, gain=None, rope=None,
            norm_cols=None):
    M, K = a.shape
    tm = _pick(M, (1024, 512, 256, 128, 64, 32, 16, 8))
    norm_cols = n_cols if norm_cols is None else norm_cols
    tn = next(t for t in (1024, 768, 512, 256, 128)
              if n_cols % t == 0 and col_off % t == 0 and (norm_cols % t == 0 or t == n_cols))
    off = col_off // tn
    n_tiles = n_cols // tn
    in_specs = [pl.BlockSpec((tm, K), lambda j, i: (i, 0)),
                pl.BlockSpec((None, K, tn), lambda j, i: (layer, 0, off + j))]
    args = [a, w]
    if mode == "headnorm":
        assert n_tiles == 1 or norm_cols % tn == 0
        heads = tn // HEAD_DIM
        gain = jnp.broadcast_to(gain.astype(jnp.float32).reshape(-1, HEAD_DIM), (n_cols // HEAD_DIM, HEAD_DIM))
        in_specs.append(pl.BlockSpec((None, heads, HEAD_DIM), lambda j, i: (j, 0, 0)))
        args.append(gain.reshape(n_tiles, heads, HEAD_DIM))
        if rope is not None:
            in_specs += [pl.BlockSpec((tm, HEAD_DIM), lambda j, i: (i, 0))] * 2
            args += list(rope)
    return pl.pallas_call(
        partial(_proj_kernel, mode=mode, rope=rope is not None, norm_cols=norm_cols, n_tiles=n_tiles),
        grid=(n_tiles, M // tm),
        in_specs=in_specs,
        out_specs=pl.BlockSpec((tm, tn), lambda j, i: (i, j)),
        out_shape=jax.ShapeDtypeStruct((M, n_cols), out_dtype),
        scratch_shapes=[pltpu.VMEM((K, tn), jnp.bfloat16)],
        compiler_params=pltpu.CompilerParams(
            dimension_semantics=("parallel", "arbitrary"), vmem_limit_bytes=VMEM_LIMIT_BYTES),
        name="project_" + mode,
    )(*args)


def rope_lane_tables(L):
    t = jnp.arange(L, dtype=jnp.int32)
    row = (t // GRID_W).astype(jnp.float32)
    col = (t % GRID_W).astype(jnp.float32)
    nf = HEAD_DIM // 4
    inv = ROPE_BASE ** (-jnp.arange(nf, dtype=jnp.float32) / nf)
    ar, ac = row[:, None] * inv, col[:, None] * inv
    cos = jnp.concatenate([jnp.cos(ar), jnp.cos(ar), jnp.cos(ac), jnp.cos(ac)], axis=-1)
    sin = jnp.concatenate([-jnp.sin(ar), jnp.sin(ar), -jnp.sin(ac), jnp.sin(ac)], axis=-1)
    return cos, sin


_NT = (((1,), (1,)), ((), ()))


def _wa_kernel(sink_ref, q_ref, k_ref, v_ref, kc_ref, vc_ref, o_ref, *, tq, seq):
    g = pl.program_id(0)
    i = pl.program_id(1)
    nwin = tq + 2 * WA_WINDOW
    ws = jnp.clip(i * tq - WA_WINDOW, 0, seq - nwin)
    start = pl.multiple_of(ws, WA_WINDOW)
    kwin = k_ref[pl.ds(start, nwin), :]
    vwin = v_ref[pl.ds(start, nwin), :]
    qpos = i * tq + lax.broadcasted_iota(jnp.int32, (tq, nwin), 0)
    kpos = ws + lax.broadcasted_iota(jnp.int32, (tq, nwin), 1)
    valid = jnp.abs(qpos - kpos) <= WA_WINDOW
    group = WA_HEADS // WA_KV_HEADS
    for hh in range(group):
        q = q_ref[:, hh * HEAD_DIM:(hh + 1) * HEAD_DIM]
        s = jnp.where(valid, lax.dot_general(q, kwin, _NT, preferred_element_type=jnp.float32), NEG_INF)
        sc = lax.dot_general(q, kc_ref[...], _NT, preferred_element_type=jnp.float32)
        sk = sink_ref[g * group + hh]
        m = jnp.maximum(jnp.maximum(jnp.max(s, axis=-1, keepdims=True), jnp.max(sc, axis=-1, keepdims=True)), sk)
        p = jnp.exp(s - m)
        pc = jnp.exp(sc - m)
        denom = jnp.sum(p, axis=-1, keepdims=True) + jnp.sum(pc, axis=-1, keepdims=True) + jnp.exp(sk - m)
        o = (jnp.dot(p.astype(vwin.dtype), vwin, preferred_element_type=jnp.float32)
             + jnp.dot(pc.astype(vwin.dtype), vc_ref[...], preferred_element_type=jnp.float32))
        o_ref[:, hh * HEAD_DIM:(hh + 1) * HEAD_DIM] = (o / denom).astype(o_ref.dtype)


def windowed_attention(q, kv, kvc, sink, tq=256):
    L = q.shape[0]
    Lc = kvc.shape[0]
    gw = (WA_HEADS // WA_KV_HEADS) * HEAD_DIM
    k_slab = pl.BlockSpec((L, HEAD_DIM), lambda g, i, s: (0, g))
    v_slab = pl.BlockSpec((L, HEAD_DIM), lambda g, i, s: (0, WA_KV_HEADS + g))
    kc_slab = pl.BlockSpec((Lc, HEAD_DIM), lambda g, i, s: (0, g))
    vc_slab = pl.BlockSpec((Lc, HEAD_DIM), lambda g, i, s: (0, WA_KV_HEADS + g))
    return pl.pallas_call(
        partial(_wa_kernel, tq=tq, seq=L),
        grid_spec=pltpu.PrefetchScalarGridSpec(
            num_scalar_prefetch=1,
            grid=(WA_KV_HEADS, L // tq),
            in_specs=[pl.BlockSpec((tq, gw), lambda g, i, s: (i, g)), k_slab, v_slab, kc_slab, vc_slab],
            out_specs=pl.BlockSpec((tq, gw), lambda g, i, s: (i, g)),
        ),
        out_shape=jax.ShapeDtypeStruct((L, WA_HEADS * HEAD_DIM), jnp.bfloat16),
        compiler_params=pltpu.CompilerParams(
            dimension_semantics=("parallel", "parallel"), vmem_limit_bytes=VMEM_LIMIT_BYTES),
        name="windowed_attention",
    )(sink.astype(jnp.float32), q, kv, kv, kvc, kvc)


NA_ROW_BLOCK = 4
NA_KEY_ROWS = NA_ROW_BLOCK + NA_WIN_ROWS - 1


NA_HEAD_GROUP = 4


def _na_kernel(q_ref, k_ref, v_ref, kc_ref, vc_ref, bias_ref, o_ref, *, rows):
    i = pl.program_id(1)
    ws = jnp.clip(i * NA_ROW_BLOCK - NA_WIN_ROWS // 2, 0, rows - NA_KEY_ROWS)
    start = pl.multiple_of(ws * GRID_W, GRID_W)
    nk = NA_KEY_ROWS * GRID_W
    for hh in range(NA_HEAD_GROUP):
        cols = slice(hh * HEAD_DIM, (hh + 1) * HEAD_DIM)
        kwin = k_ref[pl.ds(start, nk), cols]
        vwin = v_ref[pl.ds(start, nk), cols]
        q = q_ref[:, cols]
        s = lax.dot_general(q, kwin, _NT, preferred_element_type=jnp.float32) + bias_ref[0, hh]
        sc = lax.dot_general(q, kc_ref[:, cols], _NT, preferred_element_type=jnp.float32)
        m = jnp.maximum(jnp.max(s, axis=-1, keepdims=True), jnp.max(sc, axis=-1, keepdims=True))
        p = jnp.exp(s - m)
        pc = jnp.exp(sc - m)
        denom = jnp.sum(p, axis=-1, keepdims=True) + jnp.sum(pc, axis=-1, keepdims=True)
        o = (jnp.dot(p.astype(vwin.dtype), vwin, preferred_element_type=jnp.float32)
             + jnp.dot(pc.astype(vwin.dtype), vc_ref[:, cols], preferred_element_type=jnp.float32))
        o_ref[:, cols] = (o / denom).astype(o_ref.dtype)


def na_bias_tiles(rpb, rows):
    col = jnp.arange(GRID_W)
    cstart = jnp.clip(col - NA_WIN_COLS // 2, 0, GRID_W - NA_WIN_COLS)
    col_in = (col[None, :] >= cstart[:, None]) & (col[None, :] < cstart[:, None] + NA_WIN_COLS)
    dc_idx = jnp.clip(col[None, :] - col[:, None] + NA_WIN_COLS - 1, 0, 2 * NA_WIN_COLS - 2)
    exact = lax.Precision.HIGHEST
    by_col = jnp.einsum('hrd,qkd->hrqk', rpb.astype(jnp.float32),
                        jax.nn.one_hot(dc_idx, 2 * NA_WIN_COLS - 1, dtype=jnp.float32), precision=exact)
    tiles = []
    for blk in (0, 1, rows // NA_ROW_BLOCK - 1):
        r = blk * NA_ROW_BLOCK
        ws = min(max(r - NA_WIN_ROWS // 2, 0), rows - NA_KEY_ROWS)
        qr = r + jnp.arange(NA_ROW_BLOCK)
        kr = ws + jnp.arange(NA_KEY_ROWS)
        r0 = jnp.clip(qr - NA_WIN_ROWS // 2, 0, rows - NA_WIN_ROWS)
        row_in = (kr[None, :] >= r0[:, None]) & (kr[None, :] < r0[:, None] + NA_WIN_ROWS)
        dr_idx = jnp.clip(kr[None, :] - qr[:, None] + NA_WIN_ROWS - 1, 0, 2 * NA_WIN_ROWS - 2)
        b = jnp.einsum('hrqk,abr->haqbk', by_col,
                       jax.nn.one_hot(dr_idx, 2 * NA_WIN_ROWS - 1, dtype=jnp.float32), precision=exact)
        ok = row_in[:, None, :, None] & col_in[None, :, None, :]
        b = jnp.where(ok[None], b, NEG_INF)
        tiles.append(b.reshape(rpb.shape[0], NA_ROW_BLOCK * GRID_W, NA_KEY_ROWS * GRID_W))
    return jnp.stack(tiles)


def neighbourhood_attention(qkv, kvc, rpb):
    L = qkv.shape[0]
    Lc = kvc.shape[0]
    rows = L // GRID_W
    nblk = rows // NA_ROW_BLOCK
    tq = NA_ROW_BLOCK * GRID_W
    nk = NA_KEY_ROWS * GRID_W
    bias = na_bias_tiles(rpb, rows)
    gw = NA_HEAD_GROUP * HEAD_DIM
    ng = NA_HEADS // NA_HEAD_GROUP
    slab = lambda part: pl.BlockSpec((L, gw), lambda h, i: (0, part * ng + h), pipeline_mode=pl.Buffered(1))
    cslab = lambda part: pl.BlockSpec((Lc, gw), lambda h, i: (0, part * ng + h))
    variant = lambda h, i: (jnp.where(i == 0, 0, jnp.where(i == nblk - 1, 2, 1)), h, 0, 0)
    return pl.pallas_call(
        partial(_na_kernel, rows=rows),
        grid=(ng, nblk),
        in_specs=[pl.BlockSpec((tq, gw), lambda h, i: (i, h)), slab(1), slab(2), cslab(0), cslab(1),
                  pl.BlockSpec((1, NA_HEAD_GROUP, tq, nk), variant)],
        out_specs=pl.BlockSpec((tq, gw), lambda h, i: (i, h)),
        out_shape=jax.ShapeDtypeStruct((L, NA_HEADS * HEAD_DIM), jnp.bfloat16),
        compiler_params=pltpu.CompilerParams(
            dimension_semantics=("parallel", "parallel"), vmem_limit_bytes=VMEM_LIMIT_BYTES),
        name="neighbourhood_attention",
    )(qkv, qkv, qkv, kvc, kvc, bias)


DFT_N1 = 128
DFT_K1 = DFT_N1 // 2 + 1
DFT_K1_PAD = 72


def dft_constants(L):
    N = 2 * L
    N2 = N // DFT_N1
    f32, bf = jnp.float32, jnp.bfloat16
    k1 = jnp.arange(DFT_K1, dtype=jnp.int32)
    n1 = jnp.arange(DFT_N1, dtype=jnp.int32)
    th = ((k1[:, None] * n1[None, :]) % DFT_N1).astype(f32) * (2.0 * math.pi / DFT_N1)
    pad = ((0, DFT_K1_PAD - DFT_K1), (0, 0))
    rows_fwd = jnp.concatenate([jnp.pad(jnp.cos(th), pad), jnp.pad(-jnp.sin(th), pad)], axis=0)
    ck = jnp.where((k1 == 0) | (k1 == DFT_N1 // 2), 1.0, 2.0)[:, None]
    half = DFT_N1 // 2
    rows_inv = jnp.concatenate([jnp.pad(ck * jnp.cos(th[:, :half]), pad),
                                jnp.pad(-ck * jnp.sin(th[:, :half]), pad)], axis=0).T
    n2 = jnp.arange(N2, dtype=jnp.int32)
    alpha = ((k1[:, None] * n2[None, :]) % N).astype(f32) * (2.0 * math.pi / N)
    beta = ((n2[:, None] * n2[None, :]) % N2).astype(f32) * (2.0 * math.pi / N2)
    twr, twi = jnp.cos(alpha)[:, None, :], -jnp.sin(alpha)[:, None, :]
    fr, fi = jnp.cos(beta)[None], -jnp.sin(beta)[None]
    gr, gi = twr * fr - twi * fi, twr * fi + twi * fr
    mid_fwd = jnp.concatenate([jnp.concatenate([gr, -gi], axis=2),
                               jnp.concatenate([gi, gr], axis=2)], axis=1)
    return {
        "rows_fwd": rows_fwd.astype(bf), "rows_fwd_half": rows_fwd[:, :half].astype(bf),
        "rows_inv": rows_inv.astype(bf),
        "mid_fwd": mid_fwd.astype(bf),
    }


def _spectral_fwd_kernel(a_ref, g_ref, o_ref):
    n2 = a_ref.shape[2]
    x = (jnp.dot(g_ref[0, :, :n2], a_ref[0, 0], preferred_element_type=jnp.float32)
         + jnp.dot(g_ref[0, :, n2:], a_ref[1, 0], preferred_element_type=jnp.float32))
    o_ref[0, 0] = x[:n2].astype(o_ref.dtype)
    o_ref[1, 0] = x[n2:].astype(o_ref.dtype)


def spectral_fwd(a, mid_fwd, cb=2048):
    _, _, n2, C = a.shape
    cb = min(cb, C)
    return pl.pallas_call(
        _spectral_fwd_kernel,
        grid=(DFT_K1, C // cb),
        in_specs=[pl.BlockSpec((2, 1, n2, cb), lambda k, c: (0, k, 0, c)),
                  pl.BlockSpec((1, 2 * n2, 2 * n2), lambda k, c: (k, 0, 0))],
        out_specs=pl.BlockSpec((2, 1, n2, cb), lambda k, c: (0, k, 0, c)),
        out_shape=jax.ShapeDtypeStruct((2, DFT_K1, n2, C), jnp.bfloat16),
        compiler_params=pltpu.CompilerParams(
            dimension_semantics=("parallel", "parallel"), vmem_limit_bytes=VMEM_LIMIT_BYTES),
        name="spectral_fwd",
    )(a, mid_fwd)


_TN = (((0,), (0,)), ((), ()))


def _spectral_mid_kernel(a_ref, gf_ref, ks_ref, o_ref):
    n2 = a_ref.shape[2]
    k1 = pl.program_id(0)

    @pl.when(k1 < DFT_K1)
    def _():
        x = (jnp.dot(gf_ref[0, :, :n2], a_ref[0, 0], preferred_element_type=jnp.float32)
             + jnp.dot(gf_ref[0, :, n2:], a_ref[1, 0], preferred_element_type=jnp.float32))
        xr, xi = x[:n2], x[n2:]
        kr, ki = ks_ref[0, 0].astype(jnp.float32), ks_ref[1, 0].astype(jnp.float32)
        yr = (xr * kr - xi * ki).astype(jnp.bfloat16)
        yi = (xr * ki + xi * kr).astype(jnp.bfloat16)
        b = (lax.dot_general(gf_ref[0, :n2, :], yr, _TN, preferred_element_type=jnp.float32)
             + lax.dot_general(gf_ref[0, n2:, :], yi, _TN, preferred_element_type=jnp.float32))
        o_ref[0, 0] = b[:n2].astype(o_ref.dtype)
        o_ref[1, 0] = b[n2:].astype(o_ref.dtype)

    @pl.when(k1 >= DFT_K1)
    def _():
        o_ref[...] = jnp.zeros_like(o_ref)


def spectral_mid(a, consts, kspec, col_off, cb=1024):
    _, _, n2, C = a.shape
    cb = min(cb, C)
    off = col_off // cb
    kc = lambda k: jnp.minimum(k, DFT_K1 - 1)
    return pl.pallas_call(
        _spectral_mid_kernel,
        grid=(DFT_K1_PAD, C // cb),
        in_specs=[pl.BlockSpec((2, 1, n2, cb), lambda k, c: (0, k, 0, c)),
                  pl.BlockSpec((1, 2 * n2, 2 * n2), lambda k, c: (kc(k), 0, 0)),
                  pl.BlockSpec((2, 1, n2, cb), lambda k, c: (0, kc(k), 0, off + c))],
        out_specs=pl.BlockSpec((2, 1, n2, cb), lambda k, c: (0, k, 0, c)),
        out_shape=jax.ShapeDtypeStruct(a.shape, jnp.bfloat16),
        compiler_params=pltpu.CompilerParams(
            dimension_semantics=("parallel", "parallel"), vmem_limit_bytes=VMEM_LIMIT_BYTES),
        name="spectral_mid",
    )(a, consts["mid_fwd"], kspec)


DFT_ROW_GROUP = 16
DFT_ROW_LANES = 256


def _rows_fwd_kernel(f_ref, u_ref, o_ref):
    x = pltpu.einshape("abc->bac", u_ref[...].astype(jnp.float32))
    rs = [jnp.dot(f_ref[...], x[s].astype(jnp.bfloat16), preferred_element_type=jnp.float32)
          for s in range(x.shape[0])]
    o_ref[...] = pltpu.einshape("abc->bac", jnp.stack(rs)).astype(o_ref.dtype)


def dft_rows_fwd(rows_mat, u3):
    kn, n2, C = u3.shape
    grp = min(DFT_ROW_GROUP, n2)
    tc = min(DFT_ROW_LANES, C)
    m = rows_mat.shape[0]
    return pl.pallas_call(
        _rows_fwd_kernel,
        grid=(n2 // grp, C // tc),
        in_specs=[pl.BlockSpec((m, kn), lambda j, c: (0, 0)),
                  pl.BlockSpec((kn, grp, tc), lambda j, c: (0, j, c))],
        out_specs=pl.BlockSpec((m, grp, tc), lambda j, c: (0, j, c)),
        out_shape=jax.ShapeDtypeStruct((m, n2, C), jnp.bfloat16),
        compiler_params=pltpu.CompilerParams(
            dimension_semantics=("parallel", "parallel"), vmem_limit_bytes=VMEM_LIMIT_BYTES),
        name="dft_rows_fwd",
    )(rows_mat, u3)


def _rows_inv_kernel(f_ref, b_ref, u_ref, gate_ref, scale_ref, d_ref, o_ref):
    x = pltpu.einshape("abc->bac", b_ref[...].astype(jnp.float32))
    ys = [jnp.dot(f_ref[...], x[s].astype(jnp.bfloat16), preferred_element_type=jnp.float32)
          for s in range(x.shape[0])]
    y = pltpu.einshape("abc->bac", jnp.stack(ys))
    o_ref[...] = (gate_ref[...] * (y * scale_ref[...] + u_ref[...] * d_ref[...])).astype(o_ref.dtype)


def dft_rows_inv(rows_inv, b3, u3, gate3, scale, d_skip, out_dtype):
    nr, n2, C = u3.shape
    grp = min(DFT_ROW_GROUP, n2)
    tc = min(DFT_ROW_LANES, C)
    blk = pl.BlockSpec((nr, grp, tc), lambda j, c: (0, j, c))
    vec = pl.BlockSpec((1, 1, tc), lambda j, c: (0, 0, c))
    return pl.pallas_call(
        _rows_inv_kernel,
        grid=(n2 // grp, C // tc),
        in_specs=[pl.BlockSpec(rows_inv.shape, lambda j, c: (0, 0)),
                  pl.BlockSpec((b3.shape[0], grp, tc), lambda j, c: (0, j, c)), blk, blk, vec, vec],
        out_specs=blk,
        out_shape=jax.ShapeDtypeStruct((nr, n2, C), out_dtype),
        compiler_params=pltpu.CompilerParams(
            dimension_semantics=("parallel", "parallel"), vmem_limit_bytes=VMEM_LIMIT_BYTES),
        name="dft_rows_inv",
    )(rows_inv, b3, u3, gate3, scale.reshape(1, 1, C), d_skip.reshape(1, 1, C))


def _conv3_kernel(z_ref, prev_ref, next_ref, w_ref, b_ref, o_ref):
    i = pl.program_id(0)
    z = z_ref[...]
    tm = z.shape[0]
    row = lax.broadcasted_iota(jnp.int32, z.shape, 0)
    prev_row = jnp.where(i > 0, prev_ref[7:8, :], 0.0)
    next_row = jnp.where(i < pl.num_programs(0) - 1, next_ref[0:1, :], 0.0)
    zp = jnp.where(row == 0, prev_row, pltpu.roll(z, 1, 0))
    zn = jnp.where(row == tm - 1, next_row, pltpu.roll(z, tm - 1, 0))
    o_ref[...] = zp * w_ref[0:1, :] + z * w_ref[1:2, :] + zn * w_ref[2:3, :] + b_ref[...]


def short_conv3_part(z, w, b, part, width):
    L = z.shape[0]
    tm = _pick(L, (512, 256, 128, 64, 32, 16, 8))
    tc = _pick(width, (1024, 512, 256, 128))
    off = part * width // tc
    halo = 8
    return pl.pallas_call(
        _conv3_kernel,
        grid=(L // tm, width // tc),
        in_specs=[pl.BlockSpec((tm, tc), lambda i, j: (i, off + j)),
                  pl.BlockSpec((halo, tc), lambda i, j: (jnp.maximum(i * (tm // halo) - 1, 0), off + j)),
                  pl.BlockSpec((halo, tc), lambda i, j: (jnp.minimum((i + 1) * (tm // halo), L // halo - 1), off + j)),
                  pl.BlockSpec((3, tc), lambda i, j: (0, off + j)),
                  pl.BlockSpec((1, tc), lambda i, j: (0, off + j))],
        out_specs=pl.BlockSpec((tm, tc), lambda i, j: (i, j)),
        out_shape=jax.ShapeDtypeStruct((L, width), jnp.float32),
        compiler_params=pltpu.CompilerParams(
            dimension_semantics=("parallel", "parallel"), vmem_limit_bytes=VMEM_LIMIT_BYTES),
        name="short_conv3",
    )(z, z, z, w, b.reshape(1, -1))


FILTER_PAD = 128
FILTER_LANES = 512


def _filter_rows_kernel(feats_ref, feats0_ref, fw1_ref, fb1_ref, fw2_ref, fb2_ref, fr_ref, fw3_ref, rate_ref,
                        rows_ref, a_ref, asum_ref, hid_ref, hid0_ref):
    g = pl.program_id(0)
    exact = lax.Precision.HIGHEST
    n_dir, prow, nf2 = feats_ref.shape
    hid = fw3_ref.shape[1]
    half = rows_ref.shape[1] // 2
    grp = 2 * prow // half

    def hidden(feats):
        h = jnp.sin(fr_ref[0:1, :] * (jnp.dot(feats, fw1_ref[...], precision=exact,
                                              preferred_element_type=jnp.float32) + fb1_ref[...]))
        return jnp.sin(fr_ref[1:2, :] * (jnp.dot(h, fw2_ref[...], precision=exact,
                                                 preferred_element_type=jnp.float32) + fb2_ref[...]))

    @pl.when(pl.program_id(1) == 0)
    def _():
        for d in range(n_dir):
            hid_ref[d] = hidden(feats_ref[d])
        hid0_ref[...] = hidden(feats0_ref[...])

    def taps(h, t, d):
        k = jnp.dot(h.astype(jnp.bfloat16), fw3_ref[d].astype(jnp.bfloat16), preferred_element_type=jnp.float32)
        return k * (jnp.exp(-t * rate_ref[d]) + HY_DECAY_SHIFT)

    def direction(d):
        h2, f = hid_ref[d], feats_ref[d]
        return jnp.concatenate([taps(h2[:, :hid], f[:, 0:1], d),
                                taps(h2[:, hid:], f[:, nf2 // 2:nf2 // 2 + 1], d)], axis=0)

    k_fwd = direction(0)
    k_bwd = direction(1)
    lag0_back = taps(hid0_ref[:, :hid], feats0_ref[:, 0:1], 1)[0:1, :]
    first = (lax.broadcasted_iota(jnp.int32, k_fwd.shape, 0) == 0) & (g == 0)
    k_fwd = k_fwd + jnp.where(first, lag0_back, 0.0)
    k_bwd = jnp.where(first, 0.0, k_bwd)
    asum_ref[0] = jnp.sum(jnp.abs(k_fwd), axis=0, keepdims=True) + jnp.sum(jnp.abs(k_bwd), axis=0, keepdims=True)
    kf = k_fwd.astype(jnp.bfloat16)
    kb = k_bwd.astype(jnp.bfloat16)
    rs = [jnp.dot(rows_ref[:, :half], kf[s * half:(s + 1) * half], preferred_element_type=jnp.float32)
          + jnp.dot(rows_ref[:, half:], kb[s * half:(s + 1) * half], preferred_element_type=jnp.float32)
          for s in range(grp)]
    a_ref[...] = pltpu.einshape("abc->bac", jnp.stack(rs)).astype(a_ref.dtype)


def _block_diag2(w):
    z = jnp.zeros_like(w)
    return jnp.concatenate([jnp.concatenate([w, z], axis=1), jnp.concatenate([z, w], axis=1)], axis=0)


def hyena_filter_rows(L, p, rows_fwd):
    f32 = jnp.float32
    N = 2 * L
    n2 = N // DFT_N1
    half = DFT_N1 // 2
    C2 = HY_ORDER * HY_WIDTH
    grp = min(DFT_ROW_GROUP, n2)
    tc = min(FILTER_LANES, C2)
    prow = grp * half // 2
    n = (n2 * jnp.arange(DFT_N1, dtype=jnp.int32)[None, :] + jnp.arange(n2, dtype=jnp.int32)[:, None])
    pos = jnp.where(n < L, n, N - n).astype(f32)[..., None]
    bands = jnp.linspace(1e-4, HY_BANDS - 1, HY_BANDS, dtype=f32)
    ang = (2.0 * math.pi / L) * bands * pos
    feats = jnp.concatenate([pos / (L - 1), jnp.cos(ang), -jnp.sin(ang)], axis=-1)
    feats = jnp.pad(feats, ((0, 0), (0, 0), (0, FILTER_PAD - feats.shape[-1])))
    feats = feats.reshape(n2 // grp, grp, 2, half, FILTER_PAD).transpose(2, 0, 1, 3, 4)
    feats = feats.reshape(2, n2 // grp, 2, prow, FILTER_PAD).transpose(0, 1, 3, 2, 4)
    feats = feats.reshape(2, (n2 // grp) * prow, 2 * FILTER_PAD)
    feats0 = jnp.broadcast_to(feats[0, 0:1], (8, 2 * FILTER_PAD))
    hid = p['hy_fw2'].shape[0]
    fw1 = _block_diag2(jnp.pad(p['hy_fw1'].astype(f32), ((0, FILTER_PAD - p['hy_fw1'].shape[0]), (0, 0))))
    fb1 = jnp.tile(p['hy_fb1'].astype(f32), 2).reshape(1, -1)
    fw2 = _block_diag2(p['hy_fw2'].astype(f32))
    fb2 = jnp.tile(p['hy_fb2'].astype(f32), 2).reshape(1, -1)
    fr = jnp.tile(p['hy_freq'].astype(f32), (1, 2))
    fw3 = p['hy_fw3'].astype(f32).reshape(hid, HY_DIRS, C2).transpose(1, 0, 2)
    rate = jnp.abs(p['hy_decay'].astype(f32)).reshape(HY_DIRS, 1, C2)
    full = lambda shape: pl.BlockSpec(shape, lambda g, c: (0,) * len(shape))
    m = rows_fwd.shape[0]
    a, asum = pl.pallas_call(
        _filter_rows_kernel,
        grid=(n2 // grp, C2 // tc),
        in_specs=[pl.BlockSpec((2, prow, 2 * FILTER_PAD), lambda g, c: (0, g, 0)), full((8, 2 * FILTER_PAD)),
                  full((2 * FILTER_PAD, 2 * hid)), full((1, 2 * hid)),
                  full((2 * hid, 2 * hid)), full((1, 2 * hid)), full((2, 2 * hid)),
                  pl.BlockSpec((HY_DIRS, hid, tc), lambda g, c: (0, 0, c)),
                  pl.BlockSpec((HY_DIRS, 1, tc), lambda g, c: (0, 0, c)),
                  full(rows_fwd.shape)],
        out_specs=[pl.BlockSpec((m, grp, tc), lambda g, c: (0, g, c)),
                   pl.BlockSpec((1, 1, tc), lambda g, c: (g, 0, c))],
        out_shape=[jax.ShapeDtypeStruct((m, n2, C2), jnp.bfloat16),
                   jax.ShapeDtypeStruct((n2 // grp, 1, C2), f32)],
        scratch_shapes=[pltpu.VMEM((2, prow, 2 * hid), f32), pltpu.VMEM((8, 2 * hid), f32)],
        compiler_params=pltpu.CompilerParams(
            dimension_semantics=("parallel", "arbitrary"), vmem_limit_bytes=VMEM_LIMIT_BYTES),
        name="hyena_filter_rows",
    )(feats, feats0, fw1, fb1, fw2, fb2, fr, fw3, rate, rows_fwd)
    return a, jnp.sum(asum, axis=(0, 1))


def hyena_latent(z, p, consts):
    L = z.shape[0]
    C = HY_WIDTH
    N = 2 * L
    n2 = N // DFT_N1
    half = DFT_N1 // 2
    v, x1, x2 = (short_conv3_part(z, p['hy_conv_w'], p['hy_conv_b'], i, C) for i in range(3))
    ka, asum = hyena_filter_rows(L, p, consts["rows_fwd"])
    kspec = spectral_fwd(ka.reshape(2, DFT_K1_PAD, n2, HY_ORDER * C), consts["mid_fwd"])
    scale = 1.0 / (N * asum)

    def long_conv(u, gate, order, out_dtype):
        u3 = u.reshape(half, n2, C)
        ua = dft_rows_fwd(consts["rows_fwd_half"], u3)
        bm = spectral_mid(ua.reshape(2, DFT_K1_PAD, n2, C), consts, kspec, order * C)
        y = dft_rows_inv(consts["rows_inv"], bm.reshape(2 * DFT_K1_PAD, n2, C), u3, gate.reshape(half, n2, C),
                         scale[order * C:(order + 1) * C], p['hy_d'][order].astype(jnp.float32), out_dtype)
        return y.reshape(L, C)

    y1 = long_conv(v, x1, 0, jnp.float32)
    return long_conv(y1, x2, 1, jnp.bfloat16)


def _resident(shape, index_map):
    return pl.BlockSpec(shape, index_map, pipeline_mode=pl.Buffered(1))


def _merge_kernel(y0_ref, y1_ref, y2_ref, g_ref, wb_ref, o_ref):
    d = o_ref.shape[1]
    m = None
    for b, y_ref in enumerate((y0_ref, y1_ref, y2_ref)):
        t = g_ref[:, b * d:(b + 1) * d] * jnp.dot(y_ref[...], wb_ref[b], preferred_element_type=jnp.float32)
        m = t if m is None else m + t
    o_ref[...] = m.astype(o_ref.dtype)


def merge_gated(y_hy, y_wa, y_na, gates, w_branch):
    M, wbr = y_hy.shape
    D = w_branch.shape[2]
    tm = _pick(M, (256, 128, 64, 32, 16, 8))
    yspec = pl.BlockSpec((tm, wbr), lambda i: (i, 0))
    return pl.pallas_call(
        _merge_kernel,
        grid=(M // tm,),
        in_specs=[yspec, yspec, yspec, pl.BlockSpec((tm, N_BRANCH * D), lambda i: (i, 0)),
                  _resident((N_BRANCH, wbr, D), lambda i: (0, 0, 0))],
        out_specs=pl.BlockSpec((tm, D), lambda i: (i, 0)),
        out_shape=jax.ShapeDtypeStruct((M, D), jnp.bfloat16),
        compiler_params=pltpu.CompilerParams(
            dimension_semantics=("parallel",), vmem_limit_bytes=VMEM_LIMIT_BYTES),
        name="merge_gated",
    )(y_hy, y_wa, y_na, gates, w_branch)


ROUTER_PAD = 128


def _out_proj_kernel(m_ref, w_ref, x_ref, gate_ref, g2_ref, shift_ref, scale_ref, wr_ref, x_out, h_out, lg_out):
    x = x_ref[...] + gate_ref[...] * jnp.dot(m_ref[...], w_ref[...], preferred_element_type=jnp.float32)
    x_out[...] = x
    h = x * lax.rsqrt(jnp.mean(x * x, axis=-1, keepdims=True) + RMS_EPS) * g2_ref[...]
    h = h * (1.0 + scale_ref[...]) + shift_ref[...]
    h_out[...] = h.astype(h_out.dtype)
    pair = jnp.dot(h.astype(jnp.bfloat16), wr_ref[...], preferred_element_type=jnp.float32)
    lg_out[...] = pair[:, :ROUTER_PAD] + pair[:, ROUTER_PAD:]


def out_project(m, w_out, x, gate, norm_g, shift, scale, w_router):
    M, D = x.shape
    tm = _pick(M, (256, 128, 64, 32, 16, 8))
    row = pl.BlockSpec((tm, D), lambda i: (i, 0))
    vec = pl.BlockSpec((1, D), lambda i: (0, 0))
    wr = jnp.pad(w_router.astype(jnp.float32), ((0, 0), (0, ROUTER_PAD - w_router.shape[1])))
    wr_hi = wr.astype(jnp.bfloat16)
    wr_lo = (wr - wr_hi.astype(jnp.float32)).astype(jnp.bfloat16)
    wr_pair = jnp.concatenate([wr_hi, wr_lo], axis=1)
    v2 = lambda a: a.reshape(1, D).astype(jnp.float32)
    return pl.pallas_call(
        _out_proj_kernel,
        grid=(M // tm,),
        in_specs=[row, _resident((D, D), lambda i: (0, 0)), row, vec, vec, vec, vec,
                  _resident((D, 2 * ROUTER_PAD), lambda i: (0, 0))],
        out_specs=[row, row, pl.BlockSpec((tm, ROUTER_PAD), lambda i: (i, 0))],
        out_shape=[jax.ShapeDtypeStruct((M, D), jnp.float32), jax.ShapeDtypeStruct((M, D), jnp.bfloat16),
                   jax.ShapeDtypeStruct((M, ROUTER_PAD), jnp.float32)],
        compiler_params=pltpu.CompilerParams(
            dimension_semantics=("parallel",), vmem_limit_bytes=VMEM_LIMIT_BYTES),
        name="out_project",
    )(m, w_out, x, v2(gate), v2(norm_g), v2(shift), v2(scale), wr_pair)


def _mix_out_kernel(y0_ref, y1_ref, y2_ref, g_ref, wb_ref, w_ref, x_ref, gate_ref, g2_ref, shift_ref, scale_ref,
                    wr_ref, x_out, h_out, lg_out):
    d = x_ref.shape[1]
    m = None
    for b, y_ref in enumerate((y0_ref, y1_ref, y2_ref)):
        t = g_ref[:, b * d:(b + 1) * d] * jnp.dot(y_ref[...], wb_ref[b], preferred_element_type=jnp.float32)
        m = t if m is None else m + t
    x = x_ref[...] + gate_ref[...] * jnp.dot(m.astype(jnp.bfloat16), w_ref[...], preferred_element_type=jnp.float32)
    x_out[...] = x
    h = x * lax.rsqrt(jnp.mean(x * x, axis=-1, keepdims=True) + RMS_EPS) * g2_ref[...]
    h = h * (1.0 + scale_ref[...]) + shift_ref[...]
    h_out[...] = h.astype(h_out.dtype)
    pair = jnp.dot(h.astype(jnp.bfloat16), wr_ref[...], preferred_element_type=jnp.float32)
    lg_out[...] = pair[:, :ROUTER_PAD] + pair[:, ROUTER_PAD:]


def mix_out_project(y_hy, y_wa, y_na, gates, w_branch, w_out, x, gate, norm_g, shift, scale, w_router):
    M, D = x.shape
    wbr = y_hy.shape[1]
    tm = _pick(M, (256, 128, 64, 32, 16, 8))
    row = pl.BlockSpec((tm, D), lambda i: (i, 0))
    yspec = pl.BlockSpec((tm, wbr), lambda i: (i, 0))
    vec = pl.BlockSpec((1, D), lambda i: (0, 0))
    wr = jnp.pad(w_router.astype(jnp.float32), ((0, 0), (0, ROUTER_PAD - w_router.shape[1])))
    wr_hi = wr.astype(jnp.bfloat16)
    wr_pair = jnp.concatenate([wr_hi, (wr - wr_hi.astype(jnp.float32)).astype(jnp.bfloat16)], axis=1)
    v2 = lambda a: a.reshape(1, D).astype(jnp.float32)
    return pl.pallas_call(
        _mix_out_kernel,
        grid=(M // tm,),
        in_specs=[yspec, yspec, yspec, pl.BlockSpec((tm, N_BRANCH * D), lambda i: (i, 0)),
                  _resident((N_BRANCH, wbr, D), lambda i: (0, 0, 0)), _resident((D, D), lambda i: (0, 0)),
                  row, vec, vec, vec, vec, _resident((D, 2 * ROUTER_PAD), lambda i: (0, 0))],
        out_specs=[row, row, pl.BlockSpec((tm, ROUTER_PAD), lambda i: (i, 0))],
        out_shape=[jax.ShapeDtypeStruct((M, D), jnp.float32), jax.ShapeDtypeStruct((M, D), jnp.bfloat16),
                   jax.ShapeDtypeStruct((M, ROUTER_PAD), jnp.float32)],
        compiler_params=pltpu.CompilerParams(
            dimension_semantics=("parallel",), vmem_limit_bytes=VMEM_LIMIT_BYTES),
        name="mix_out_project",
    )(y_hy, y_wa, y_na, gates, w_branch, w_out, x, v2(gate), v2(norm_g), v2(shift), v2(scale), wr_pair)


def _gate_up_kernel(x_ref, wg_ref, wu_ref, o_ref, wg_bf, wu_bf):
    @pl.when(pl.program_id(2) == 0)
    def _():
        wg_bf[...] = wg_ref[...].astype(jnp.bfloat16)
        wu_bf[...] = wu_ref[...].astype(jnp.bfloat16)

    x = x_ref[0]
    a = jnp.dot(x, wg_bf[...], preferred_element_type=jnp.float32)
    u = jnp.dot(x, wu_bf[...], preferred_element_type=jnp.float32)
    o_ref[0] = (a * _sigmoid(a) * u).astype(o_ref.dtype)


def _down_kernel(h_ref, wd_ref, gsel_ref, gate_ref, o_ref, wd_bf):
    @pl.when(pl.program_id(1) == 0)
    def _():
        wd_bf[...] = wd_ref[...].astype(jnp.bfloat16)

    y = jnp.dot(h_ref[0], wd_bf[...], preferred_element_type=jnp.float32)
    o_ref[0] = (y * gsel_ref[0] * gate_ref[...]).astype(o_ref.dtype)


def expert_ffn(xe, w_gate, w_up, w_down, layer, gsel, out_gate):
    E, cap, D = xe.shape
    F = w_gate.shape[3]
    tm = _pick(cap, (512, 256, 128, 64, 32, 16, 8))
    tn = _pick(F, (1024, 512, 256, 128))
    seq = pltpu.CompilerParams(dimension_semantics=("parallel", "parallel", "arbitrary"),
                               vmem_limit_bytes=VMEM_LIMIT_BYTES)
    h = pl.pallas_call(
        _gate_up_kernel,
        grid=(E, F // tn, cap // tm),
        in_specs=[pl.BlockSpec((1, tm, D), lambda e, j, i: (e, i, 0)),
                  pl.BlockSpec((None, None, D, tn), lambda e, j, i: (layer, e, 0, j)),
                  pl.BlockSpec((None, None, D, tn), lambda e, j, i: (layer, e, 0, j))],
        out_specs=pl.BlockSpec((1, tm, tn), lambda e, j, i: (e, i, j)),
        out_shape=jax.ShapeDtypeStruct((E, cap, F), jnp.bfloat16),
        scratch_shapes=[pltpu.VMEM((D, tn), jnp.bfloat16), pltpu.VMEM((D, tn), jnp.bfloat16)],
        compiler_params=seq,
        name="expert_gate_up",
    )(xe, w_gate, w_up)
    td = _pick(cap, (1024, 512, 256, 128, 64, 32, 16, 8))
    return pl.pallas_call(
        _down_kernel,
        grid=(E, cap // td),
        in_specs=[pl.BlockSpec((1, td, F), lambda e, i: (e, i, 0)),
                  pl.BlockSpec((None, None, F, D), lambda e, i: (layer, e, 0, 0)),
                  pl.BlockSpec((1, td, 1), lambda e, i: (e, i, 0)),
                  pl.BlockSpec((1, D), lambda e, i: (0, 0))],
        out_specs=pl.BlockSpec((1, td, D), lambda e, i: (e, i, 0)),
        out_shape=jax.ShapeDtypeStruct((E, cap, D), jnp.bfloat16),
        scratch_shapes=[pltpu.VMEM((F, D), jnp.bfloat16)],
        compiler_params=pltpu.CompilerParams(dimension_semantics=("parallel", "arbitrary"),
                                             vmem_limit_bytes=VMEM_LIMIT_BYTES),
        name="expert_down",
    )(h, w_down, gsel.reshape(E, cap, 1).astype(jnp.float32), out_gate.reshape(1, D).astype(jnp.float32))


SEG_ROWS = 512
SEG_WIN = 512
SEG_LANES = 256


def _segment_add_kernel(first_ref, last_ref, tok_ref, ye_ref, x_ref, o_ref, *, n_tokens, win):
    c = pl.program_id(1)

    @pl.when(c == 0)
    def _():
        o_ref[...] = x_ref[...]

    base0 = (first_ref[c] // 8) * 8
    n_win = (last_ref[c] - base0) // win + 1
    tok = tok_ref[0]
    rows = ye_ref[...]

    def window(w, carry):
        lo = base0 + w * win
        base = pl.multiple_of(jnp.minimum(lo, n_tokens - win), 8)
        rid = base + lax.broadcasted_iota(jnp.int32, (win, tok.shape[1]), 0)
        onehot = ((rid == tok) & (tok >= lo)).astype(jnp.bfloat16)
        o_ref[pl.ds(base, win), :] += jnp.dot(onehot, rows, preferred_element_type=jnp.float32)
        return carry

    lax.fori_loop(0, n_win, window, 0)


def segment_add(x, ye_sorted, tok_sorted):
    N, D = x.shape
    P = ye_sorted.shape[0]
    rows = min(SEG_ROWS, P)
    win = min(SEG_WIN, N)
    tc = min(SEG_LANES, D)
    n_chunks = P // rows
    tok3 = tok_sorted.reshape(n_chunks, 1, rows)
    return pl.pallas_call(
        partial(_segment_add_kernel, n_tokens=N, win=win),
        grid_spec=pltpu.PrefetchScalarGridSpec(
            num_scalar_prefetch=2,
            grid=(D // tc, n_chunks),
            in_specs=[pl.BlockSpec((1, 1, rows), lambda d, c, f, l: (c, 0, 0)),
                      pl.BlockSpec((rows, tc), lambda d, c, f, l: (c, d)),
                      pl.BlockSpec((N, tc), lambda d, c, f, l: (0, d), pipeline_mode=pl.Buffered(1))],
            out_specs=pl.BlockSpec((N, tc), lambda d, c, f, l: (0, d)),
        ),
        out_shape=jax.ShapeDtypeStruct((N, D), jnp.float32),
        compiler_params=pltpu.CompilerParams(
            dimension_semantics=("parallel", "arbitrary"), vmem_limit_bytes=VMEM_LIMIT_BYTES),
        name="segment_add",
    )(tok3[:, 0, 0], tok3[:, 0, rows - 1], tok3, ye_sorted, x)


def expert_choice_latent(x1, h2, logits, out_gate, w_gate, w_up, w_down, layer):
    N, D = x1.shape
    cap = EC_CAPACITY_FACTOR * N // N_EXPERTS
    aff = jax.nn.softmax(logits[:, :N_EXPERTS], axis=-1)
    gsel, idx = lax.top_k(aff.T, cap)
    ye = expert_ffn(h2[idx], w_gate, w_up, w_down, layer, gsel, out_gate)
    tok = idx.reshape(-1)
    order = jnp.argsort(tok)
    return segment_add(x1, ye.reshape(-1, D)[order], tok[order])


def _ctx_attn_kernel(sink_ref, q_ref, k_ref, v_ref, o_ref, *, use_sink):
    s = lax.dot_general(q_ref[...], k_ref[...], _NT, preferred_element_type=jnp.float32)
    m = jnp.max(s, axis=-1, keepdims=True)
    if use_sink:
        sk = sink_ref[pl.program_id(0)]
        m = jnp.maximum(m, sk)
    p = jnp.exp(s - m)
    denom = jnp.sum(p, axis=-1, keepdims=True)
    if use_sink:
        denom = denom + jnp.exp(sk - m)
    o = jnp.dot(p.astype(v_ref.dtype), v_ref[...], preferred_element_type=jnp.float32)
    o_ref[...] = (o / denom).astype(o_ref.dtype)


def context_attention(q, q_blk, kv, k_blk, v_blk, sink, n_heads, n_kv_heads):
    Lc = q.shape[0]
    group = n_heads // n_kv_heads
    use_sink = sink is not None
    sink = jnp.zeros((n_heads,), jnp.float32) if sink is None else sink.astype(jnp.float32)
    return pl.pallas_call(
        partial(_ctx_attn_kernel, use_sink=use_sink),
        grid_spec=pltpu.PrefetchScalarGridSpec(
            num_scalar_prefetch=1,
            grid=(n_heads,),
            in_specs=[pl.BlockSpec((Lc, HEAD_DIM), lambda h, s: (0, q_blk + h)),
                      pl.BlockSpec((Lc, HEAD_DIM), lambda h, s: (0, k_blk + h // group)),
                      pl.BlockSpec((Lc, HEAD_DIM), lambda h, s: (0, v_blk + h // group))],
            out_specs=pl.BlockSpec((Lc, HEAD_DIM), lambda h, s: (0, h)),
        ),
        out_shape=jax.ShapeDtypeStruct((Lc, n_heads * HEAD_DIM), jnp.bfloat16),
        compiler_params=pltpu.CompilerParams(
            dimension_semantics=("parallel",), vmem_limit_bytes=VMEM_LIMIT_BYTES),
        name="context_attention",
    )(sink, q, kv, kv)


def trunk_layer(x, ctx, c, c_ctx, p, stacked, layer, update_ctx, consts, consts_ctx):
    L = x.shape[0]
    bf = jnp.bfloat16
    w_in = stacked['w_in']
    cond = jnp.pad(jnp.concatenate([jax.nn.silu(c), jax.nn.silu(c_ctx)[None]], axis=0), ((0, 6), (0, 0)))
    mod = matmul(cond, stacked['w_mod'], layer=layer) + p['b_mod']
    mx = jnp.split(mod[0], N_MOD)
    mc = jnp.split(mod[1], N_MOD)
    qk_scale = HEAD_DIM ** -0.5
    nwq = WA_HEADS * HEAD_DIM
    nkv = WA_KV_HEADS * HEAD_DIM
    nna = NA_HEADS * HEAD_DIM
    proj = lambda a, *args: project(a, w_in, layer, *args)

    def mix_and_ffn(res, y_hy, y_wa, y_na, gates, m_vec):
        r1, h2, logits = mix_out_project(y_hy, y_wa, y_na, gates, p['w_branch'].astype(bf), p['w_out'].astype(bf),
                                         res, m_vec[2], p['norm2_g'], m_vec[3], m_vec[4], p['w_router'])
        return expert_choice_latent(r1, h2, logits, m_vec[5], stacked['w_gate'], stacked['w_up'],
                                    stacked['w_down'], layer)

    ones = jnp.ones((HEAD_DIM,), jnp.float32)
    wa_k_gain = jnp.stack([p['wa_k_norm']] * WA_KV_HEADS + [ones] * WA_KV_HEADS)
    na_gain = jnp.stack([p['na_q_norm'] * qk_scale] * NA_HEADS + [p['na_k_norm']] * NA_HEADS + [ones] * NA_HEADS)

    hcb = norm_modulate(ctx, p['norm1_g'], mc[0], mc[1])
    kvc_wa = proj(hcb, OFF_WA_KV, 2 * nkv, "headnorm", bf, wa_k_gain, None, nkv)
    qkvc_na = proj(hcb, OFF_NA_Q, 3 * nna, "headnorm", bf, na_gain, None, 2 * nna)
    kvc_na = qkvc_na[:, nna:]

    hxb = norm_modulate(x, p['norm1_g'], mx[0], mx[1])
    rope = rope_lane_tables(L)
    y_hy = hyena_latent(proj(hxb, OFF_HY, OFF_WA_Q - OFF_HY), p, consts)
    q_wa = proj(hxb, OFF_WA_Q, nwq, "headnorm", bf, p['wa_q_norm'] * qk_scale, rope)
    kv_wa = proj(hxb, OFF_WA_KV, 2 * nkv, "headnorm", bf, wa_k_gain, rope, nkv)
    y_wa = windowed_attention(q_wa, kv_wa, kvc_wa, p['wa_sink'])
    qkv_na = proj(hxb, OFF_NA_Q, 3 * nna, "headnorm", bf, na_gain, None, 2 * nna)
    y_na = neighbourhood_attention(qkv_na, kvc_na, p['na_rpb'])
    gates = proj(hxb, OFF_GATE, N_BRANCH * D_MODEL, "sigmoid")
    x = mix_and_ffn(x, y_hy, y_wa, y_na, gates, mx)

    if update_ctx:
        yc_hy = hyena_latent(proj(hcb, OFF_HY, OFF_WA_Q - OFF_HY), p, consts_ctx)
        qc_wa = proj(hcb, OFF_WA_Q, nwq, "headnorm", bf, p['wa_q_norm'] * qk_scale)
        yc_wa = context_attention(qc_wa, 0, kvc_wa, 0, WA_KV_HEADS, p['wa_sink'], WA_HEADS, WA_KV_HEADS)
        yc_na = context_attention(qkvc_na, 0, qkvc_na, NA_HEADS, 2 * NA_HEADS, None, NA_HEADS, NA_HEADS)
        gates_c = proj(hcb, OFF_GATE, N_BRANCH * D_MODEL, "sigmoid")
        ctx = mix_and_ffn(ctx, yc_hy, yc_wa, yc_na, gates_c, mc)
    return x, ctx


def kernel(x, c, ctx, c_ctx, w_mod, b_mod, norm1_g, w_in, hy_conv_w, hy_conv_b, hy_fw1, hy_fb1, hy_fw2, hy_fb2, hy_fw3, hy_freq, hy_decay, hy_d, wa_q_norm, wa_k_norm, wa_sink, na_q_norm, na_k_norm, na_rpb, w_branch, w_out, norm2_g, w_router, w_gate, w_up, w_down):
    consts = dft_constants(x.shape[1])
    consts_ctx = dft_constants(ctx.shape[1])
    stacked = {'w_mod': w_mod, 'w_in': w_in, 'w_gate': w_gate, 'w_up': w_up, 'w_down': w_down}
    xs, cs = x[0], ctx[0]
    for l in range(DEPTH):
        p = {
            'b_mod': b_mod[l], 'norm1_g': norm1_g[l],
            'hy_conv_w': hy_conv_w[l], 'hy_conv_b': hy_conv_b[l], 'hy_fw1': hy_fw1[l], 'hy_fb1': hy_fb1[l],
            'hy_fw2': hy_fw2[l], 'hy_fb2': hy_fb2[l], 'hy_fw3': hy_fw3[l], 'hy_freq': hy_freq[l],
            'hy_decay': hy_decay[l], 'hy_d': hy_d[l], 'wa_q_norm': wa_q_norm[l], 'wa_k_norm': wa_k_norm[l],
            'wa_sink': wa_sink[l], 'na_q_norm': na_q_norm[l], 'na_k_norm': na_k_norm[l], 'na_rpb': na_rpb[l],
            'w_branch': w_branch[l], 'w_out': w_out[l], 'norm2_g': norm2_g[l], 'w_router': w_router[l],
        }
        xs, cs = trunk_layer(xs, cs, c, c_ctx, p, stacked, l, l < DEPTH - 1, consts, consts_ctx)
    return xs[None]
```
